```python
import numpy as np
import jax
import jax.numpy as jnp
from jax import lax

D_MODEL = 1024
BATCH = 16
SEQ = 4096
DEPTH = 2

PLE_DIM = 256
DN_ALPHA = (2 * DEPTH) ** 0.25
DN_BETA = (8 * DEPTH) ** -0.25
LN_EPS = 1e-5
ROPE_THETA = 10000.0

RW_WIDTH = D_MODEL // 2
RW_HEAD = 64
RW_HEADS = RW_WIDTH // RW_HEAD
RW_DECAY_LORA = 64
RW_AAA_LORA = 64
RW_GATE_LORA = 128
RW_GN_EPS = RW_HEAD * 1e-5
RW_SPLITS = (RW_WIDTH, RW_WIDTH, RW_WIDTH, RW_DECAY_LORA, RW_AAA_LORA, RW_GATE_LORA)
RW_COLS = sum(RW_SPLITS)

NSA_WIDTH = D_MODEL - RW_WIDTH
NSA_HEAD = 64
NSA_HEADS = NSA_WIDTH // NSA_HEAD
NSA_KV_HEADS = 2
NSA_GROUP = NSA_HEADS // NSA_KV_HEADS
NSA_KV = NSA_KV_HEADS * NSA_HEAD
CMP_LEN = 32
CMP_STRIDE = 16
CMP_HIDDEN = 128
SEL_LEN = 64
SEL_TOP = 16
WINDOW = 512
NSA_QBLOCK = 32
NSA_SPLITS = (NSA_WIDTH,) + (NSA_KV,) * 6 + (3 * NSA_HEADS,)
NSA_COLS = sum(NSA_SPLITS)
EV_COLS = RW_COLS + NSA_COLS

HG_HEAD = 128
HG_HEADS = D_MODEL // HG_HEAD
HG_CHUNK = 16
HG_SPLITS = (D_MODEL, D_MODEL, D_MODEL, D_MODEL)
HG_COLS = sum(HG_SPLITS)

MOE_GROUPS = 4
MOE_PER_GROUP = 8
MOE_EXPERTS = MOE_GROUPS * MOE_PER_GROUP
MOE_TOPK = 2
MOE_HIDDEN = 512
MOE_BLOCK = 256

N_EVEN = (DEPTH + 1) // 2
N_ODD = DEPTH // 2

kernel_name = 'hybrid_rwkv7_nsa_hgrn2_hmoe'


def _split(z, sizes):
    cuts = [int(c) for c in np.cumsum(sizes)[:-1]]
    return jnp.split(z, cuts, axis=-1)


def _layer_norm(x, g, b):
    xf = x.astype(jnp.float32)
    xc = xf - jnp.mean(xf, -1, keepdims=True)
    var = jnp.mean(xc * xc, -1, keepdims=True)
    return (xc * lax.rsqrt(var + LN_EPS) * g + b).astype(x.dtype)


def _rope_tables(positions, dim):
    inv = (1.0 / (ROPE_THETA ** (np.arange(0, dim, 2, dtype=np.float32) / dim))).astype(np.float32)
    ang = positions.astype(jnp.float32)[..., None] * inv
    return jnp.cos(ang)[:, :, None, :], jnp.sin(ang)[:, :, None, :]


def _apply_rope(x, cos, sin):
    half = x.shape[-1] // 2
    xf = x.astype(jnp.float32)
    x1, x2 = xf[..., :half], xf[..., half:]
    return jnp.concatenate([x1 * cos - x2 * sin, x2 * cos + x1 * sin], -1).astype(x.dtype)


def _masked_softmax(s, mask):
    s = jnp.where(mask, s.astype(jnp.float32), -jnp.inf)
    m = jnp.max(s, axis=-1, keepdims=True)
    m = jnp.where(jnp.isfinite(m), m, 0.0)
    e = jnp.exp(s - m)
    return e / jnp.maximum(jnp.sum(e, -1, keepdims=True), 1e-30)


def _rwkv7_scan(r, w, k, v, kk, a):
    B, T, H, N = r.shape

    def step(S, inp):
        r_t, w_t, k_t, v_t, kk_t, a_t = inp
        sa = jnp.einsum('bhvk,bhk->bhv', S, kk_t)
        S = S * w_t[:, :, None, :] - sa[..., None] * (kk_t * a_t)[:, :, None, :] + v_t[..., None] * k_t[:, :, None, :]
        return S, jnp.einsum('bhvk,bhk->bhv', S, r_t)

    xs = tuple(jnp.moveaxis(t, 1, 0) for t in (r, w, k, v, kk, a))
    _, o = lax.scan(step, jnp.zeros((B, H, N, N), jnp.float32), xs)
    return jnp.moveaxis(o, 0, 1)


def _rwkv7_group(z, mu, w0, w2, a0, a2, g2, k_k, k_a, r_k, gn_g, gn_b):
    B, T, _ = z.shape
    z = z.astype(jnp.float32)
    z_prev = jnp.pad(z, ((0, 0), (1, 0), (0, 0)))[:, :-1]
    z = z + mu * (z_prev - z)
    r, k, v, wd, ad, gd = _split(z, RW_SPLITS)
    w = -jax.nn.softplus(-(w0 + jnp.tanh(wd) @ w2)) - 0.5
    decay = jnp.exp(-jnp.exp(w))
    a = jax.nn.sigmoid(a0 + ad @ a2)
    g = jax.nn.sigmoid(gd) @ g2
    heads = lambda t: t.reshape(B, T, RW_HEADS, RW_HEAD)
    kk = heads(k * k_k)
    kk = kk / jnp.maximum(jnp.sqrt(jnp.sum(kk * kk, -1, keepdims=True)), 1e-12)
    k = k * (1.0 + (a - 1.0) * k_a)
    r, k, v, a = heads(r), heads(k), heads(v), heads(a)
    o = _rwkv7_scan(r, heads(decay), k, v, kk, a)
    oc = o - jnp.mean(o, -1, keepdims=True)
    o = oc * lax.rsqrt(jnp.mean(oc * oc, -1, keepdims=True) + RW_GN_EPS)
    o = o.reshape(B, T, RW_WIDTH) * gn_g + gn_b
    bonus = (jnp.sum(r * k * r_k, -1, keepdims=True) * v).reshape(B, T, RW_WIDTH)
    return (o + bonus) * g


def _cmp_sel_overlap(n_cmp, n_sel):
    cs = np.arange(n_cmp)[:, None] * CMP_STRIDE
    ss = np.arange(n_sel)[None, :] * SEL_LEN
    ov = np.clip(np.minimum(cs + CMP_LEN, ss + SEL_LEN) - np.maximum(cs, ss), 0, None)
    return (ov / CMP_LEN).astype(np.float32)


def _nsa_group(z, cos, sin, cmp_pos, cmp_w1, cmp_w2):
    B, T, _ = z.shape
    q, kc, vc, ksl, vsl, kwn, vwn, gl = _split(z, NSA_SPLITS)
    kv_heads = lambda t: t.reshape(B, T, NSA_KV_HEADS, NSA_HEAD)
    q = q.reshape(B, T, NSA_HEADS, NSA_HEAD)
    q_rot = _apply_rope(q, cos, sin).reshape(B, T, NSA_KV_HEADS, NSA_GROUP, NSA_HEAD)
    q_raw = q.reshape(B, T, NSA_KV_HEADS, NSA_GROUP, NSA_HEAD)
    ksl = _apply_rope(kv_heads(ksl), cos, sin)
    kwn = _apply_rope(kv_heads(kwn), cos, sin)
    vsl, vwn = kv_heads(vsl), kv_heads(vwn)
    gates = jax.nn.sigmoid(gl.astype(jnp.float32)).reshape(B, T, NSA_KV_HEADS, NSA_GROUP, 3)

    n_cmp = (T - CMP_LEN) // CMP_STRIDE + 1
    blk_idx = np.arange(n_cmp)[:, None] * CMP_STRIDE + np.arange(CMP_LEN)[None, :]
    blocks = jnp.stack([kv_heads(kc), kv_heads(vc)])[:, :, blk_idx]
    blocks = blocks + cmp_pos[:, None, None, :, None, :]
    blocks = jnp.moveaxis(blocks, 3, 4).reshape(2, B, n_cmp, NSA_KV_HEADS, CMP_LEN * NSA_HEAD)
    hid = jax.nn.gelu(jnp.einsum('zbnhi,zio->zbnho', blocks, cmp_w1))
    kv_cmp = jnp.einsum('zbnho,zod->zbnhd', hid, cmp_w2)
    k_cmp, v_cmp = kv_cmp[0], kv_cmp[1]
    cmp_last = np.arange(n_cmp) * CMP_STRIDE + CMP_LEN - 1

    n_sel = T // SEL_LEN
    n_top = min(SEL_TOP, n_sel)
    overlap = jnp.asarray(_cmp_sel_overlap(n_cmp, n_sel))
    sel_blocks = lambda t: jnp.moveaxis(t.reshape(B, n_sel, SEL_LEN, NSA_KV_HEADS, NSA_HEAD), 3, 1)
    k_blk, v_blk = sel_blocks(ksl), sel_blocks(vsl)
    gather = jax.vmap(jax.vmap(lambda blk, ix: blk[ix]))
    blk_ids = np.arange(n_sel)
    in_blk = np.arange(SEL_LEN)

    pad = ((0, 0), (WINDOW, 0), (0, 0), (0, 0))
    k_win, v_win = jnp.pad(kwn, pad), jnp.pad(vwn, pad)
    win_off = np.arange(WINDOW + NSA_QBLOCK) - WINDOW
    scale = NSA_HEAD ** -0.5

    def q_block(c):
        t0 = c * NSA_QBLOCK
        tq = t0 + jnp.arange(NSA_QBLOCK)
        sl = lambda t: lax.dynamic_slice_in_dim(t, t0, NSA_QBLOCK, axis=1)
        qc, qr, gc = sl(q_raw) * scale, sl(q_rot) * scale, sl(gates)
        s_c = jnp.einsum('bqhgd,bnhd->bhgqn', qc, k_cmp)
        p_c = _masked_softmax(s_c, cmp_last[None, :] <= tq[:, None])
        o_c = jnp.einsum('bhgqn,bnhd->bqhgd', p_c, v_cmp)
        imp = jnp.einsum('bhgqn,ns->bhqs', p_c, overlap)
        cur = (tq // SEL_LEN)[:, None]
        forced = (blk_ids[None, :] == 0) | (blk_ids[None, :] == cur) | (blk_ids[None, :] == cur - 1)
        imp = jnp.where(forced, jnp.inf, jnp.where(blk_ids[None, :] > cur, -jnp.inf, imp))
        _, sel = lax.top_k(imp, n_top)
        k_g = gather(k_blk, sel).reshape(B, NSA_KV_HEADS, NSA_QBLOCK, n_top * SEL_LEN, NSA_HEAD)
        v_g = gather(v_blk, sel).reshape(B, NSA_KV_HEADS, NSA_QBLOCK, n_top * SEL_LEN, NSA_HEAD)
        pos_s = (sel[..., None] * SEL_LEN + in_blk).reshape(B, NSA_KV_HEADS, NSA_QBLOCK, n_top * SEL_LEN)
        s_s = jnp.einsum('bqhgd,bhqkd->bhgqk', qr, k_g)
        p_s = _masked_softmax(s_s, (pos_s <= tq[:, None])[:, :, None])
        o_s = jnp.einsum('bhgqk,bhqkd->bqhgd', p_s, v_g)
        kw = lax.dynamic_slice_in_dim(k_win, t0, WINDOW + NSA_QBLOCK, axis=1)
        vw = lax.dynamic_slice_in_dim(v_win, t0, WINDOW + NSA_QBLOCK, axis=1)
        pos_w = t0 + win_off
        valid_w = (pos_w[None, :] >= 0) & (pos_w[None, :] <= tq[:, None]) & (pos_w[None, :] > tq[:, None] - WINDOW)
        s_w = jnp.einsum('bqhgd,bkhd->bhgqk', qr, kw)
        p_w = _masked_softmax(s_w, valid_w)
        o_w = jnp.einsum('bhgqk,bkhd->bqhgd', p_w, vw)
        return gc[..., 0:1] * o_c + gc[..., 1:2] * o_s + gc[..., 2:3] * o_w

    out = lax.map(q_block, jnp.arange(T // NSA_QBLOCK))
    return jnp.moveaxis(out, 0, 1).reshape(B, T, NSA_WIDTH)


def _hgrn2_chunkwise(q, k, v, logf):
    B, T, H, dk = q.shape
    dv = v.shape[-1]
    C = HG_CHUNK
    N = T // C
    to_chunks = lambda t: t.reshape(B, N, C, H, t.shape[-1]).transpose(0, 3, 1, 2, 4)
    q, k, v, logf = to_chunks(q), to_chunks(k), to_chunks(v), to_chunks(logf)
    b = jnp.cumsum(logf, axis=3)
    b_last = b[:, :, :, -1:, :]
    q_e = q * jnp.exp(b)
    k_e = k * jnp.exp(-b)
    k_tail = k * jnp.exp(b_last - b)
    causal = np.tril(np.ones((C, C), dtype=bool))
    A = jnp.where(causal, jnp.einsum('bhncd,bhnsd->bhncs', q_e, k_e), 0.0)
    o_intra = jnp.einsum('bhncs,bhnsv->bhncv', A, v)

    def step(S, inp):
        qn, kn, vn, dn = inp
        o = jnp.einsum('bhcd,bhdv->bhcv', qn, S)
        S = S * dn[..., None] + jnp.einsum('bhcd,bhcv->bhdv', kn, vn)
        return S, o

    xs = (jnp.moveaxis(q_e, 2, 0), jnp.moveaxis(k_tail, 2, 0), jnp.moveaxis(v, 2, 0),
          jnp.moveaxis(jnp.exp(b_last[:, :, :, 0]), 2, 0))
    _, o_inter = lax.scan(step, jnp.zeros((B, H, dk, dv), jnp.float32), xs)
    o = o_intra + jnp.moveaxis(o_inter, 0, 2)
    return o.transpose(0, 2, 3, 1, 4).reshape(B, T, H, dv)


def _hgrn2_mixer(x, w_in, w_out, lb, norm_g):
    B, T, _ = x.shape
    q, f, i, g = _split(x @ w_in, HG_SPLITS)
    heads = lambda t: t.astype(jnp.float32).reshape(B, T, HG_HEADS, HG_HEAD)
    lb = lb.reshape(HG_HEADS, HG_HEAD)
    f = heads(f)
    forget = lb + (1.0 - lb) * jax.nn.sigmoid(f)
    k = (1.0 - lb) * jax.nn.sigmoid(-f)
    o = _hgrn2_chunkwise(jax.nn.silu(heads(q)), k, heads(i), jnp.log(forget))
    o = o * lax.rsqrt(jnp.mean(o * o, -1, keepdims=True) + LN_EPS) * norm_g
    o = o.reshape(B, T, D_MODEL) * jax.nn.silu(g.astype(jnp.float32))
    return o.astype(x.dtype) @ w_out


def _hier_moe(h, w_rg, b_rg, w_re, b_re, w1, w3, w2):
    B, T, D = h.shape
    M = B * T
    A = M * MOE_TOPK
    xt = h.reshape(M, D)
    lg = (xt @ w_rg + b_rg).astype(jnp.float32)
    g_sel = jnp.argmax(lg, -1)
    p_grp = jnp.max(jax.nn.softmax(lg, -1), -1)
    le = (xt @ w_re + b_re).astype(jnp.float32).reshape(M, MOE_GROUPS, MOE_PER_GROUP)
    le = le[jnp.arange(M), g_sel]
    top_v, top_i = lax.top_k(le, MOE_TOPK)
    gate = (p_grp[:, None] * jax.nn.softmax(top_v, -1)).reshape(A)
    eid = (g_sel[:, None] * MOE_PER_GROUP + top_i).reshape(A)
    tok = jnp.repeat(jnp.arange(M), MOE_TOPK)
    order = jnp.argsort(eid)
    e_s, tok_s, gate_s = eid[order], tok[order], gate[order]
    counts = jnp.bincount(eid, length=MOE_EXPERTS)
    padded = (counts + MOE_BLOCK - 1) // MOE_BLOCK * MOE_BLOCK
    ends = jnp.cumsum(padded)
    start = ends - padded
    off = jnp.cumsum(counts) - counts
    dest = start[e_s] + jnp.arange(A) - off[e_s]
    P = A + MOE_EXPERTS * MOE_BLOCK
    n_blk = P // MOE_BLOCK
    xbuf = jnp.zeros((P, D), h.dtype).at[dest].set(xt[tok_s])
    blk_e = jnp.minimum(jnp.searchsorted(ends, jnp.arange(n_blk) * MOE_BLOCK, side='right'), MOE_EXPERTS - 1)

    def expert_block(args):
        xb, e = args
        return (jax.nn.silu(xb @ w1[e]) * (xb @ w3[e])) @ w2[e]

    ybuf = lax.map(expert_block, (xbuf.reshape(n_blk, MOE_BLOCK, D), blk_e)).reshape(P, D)
    y = jnp.zeros((M, D), jnp.float32).at[tok_s].add(gate_s[:, None] * ybuf[dest].astype(jnp.float32))
    return y.reshape(B, T, D).astype(h.dtype)


def setup_inputs(seed: int = 0) -> dict:
    key = jax.random.key(seed)
    keys = iter(jax.random.split(key, 40))

    def nrm(shape, scale):
        return jax.random.normal(next(keys), shape, jnp.float32) * scale

    def col_scale(spec):
        return jnp.asarray(np.concatenate([np.full(n, s, np.float32) for n, s in spec]))

    b = DN_BETA
    ev_spec = [(RW_WIDTH, 1.0), (RW_WIDTH, 1.0), (RW_WIDTH, b), (RW_DECAY_LORA, 1.0), (RW_AAA_LORA, 1.0),
               (RW_GATE_LORA, 1.0), (NSA_WIDTH, 1.0), (NSA_KV, 1.0), (NSA_KV, b), (NSA_KV, 1.0), (NSA_KV, b),
               (NSA_KV, 1.0), (NSA_KV, b), (3 * NSA_HEADS, 1.0)]
    hg_spec = [(D_MODEL, 1.0), (D_MODEL, 1.0), (D_MODEL, b), (D_MODEL, 1.0)]
    d_inv = D_MODEL ** -0.5
    return {
        'x': nrm((BATCH, SEQ, D_MODEL), 1.0),
        'p': nrm((DEPTH, BATCH, SEQ, PLE_DIM), 1.0),
        'positions': jnp.broadcast_to(jnp.arange(SEQ, dtype=jnp.int32)[None, :], (BATCH, SEQ)),
        'ev_w_in': nrm((N_EVEN, D_MODEL, EV_COLS), d_inv) * col_scale(ev_spec),
        'ev_w_out': nrm((N_EVEN, D_MODEL, D_MODEL), d_inv * b),
        'rw_mu': jax.random.uniform(next(keys), (N_EVEN, RW_COLS), jnp.float32, 0.0, 1.0),
        'rw_w0': jax.random.uniform(next(keys), (N_EVEN, RW_WIDTH), jnp.float32, -3.0, 1.0),
        'rw_w2': nrm((N_EVEN, RW_DECAY_LORA, RW_WIDTH), 0.5 * RW_DECAY_LORA ** -0.5),
        'rw_a0': nrm((N_EVEN, RW_WIDTH), 0.5),
        'rw_a2': nrm((N_EVEN, RW_AAA_LORA, RW_WIDTH), 0.5 * RW_AAA_LORA ** -0.5),
        'rw_g2': nrm((N_EVEN, RW_GATE_LORA, RW_WIDTH), RW_GATE_LORA ** -0.5),
        'rw_k_k': 0.85 + nrm((N_EVEN, RW_WIDTH), 0.05),
        'rw_k_a': 1.0 + nrm((N_EVEN, RW_WIDTH), 0.05),
        'rw_r_k': nrm((N_EVEN, RW_HEADS, RW_HEAD), 0.1),
        'rw_gn_g': 1.0 + nrm((N_EVEN, RW_WIDTH), 0.02),
        'rw_gn_b': nrm((N_EVEN, RW_WIDTH), 0.02),
        'nsa_cmp_pos': nrm((N_EVEN, 2, CMP_LEN, NSA_HEAD), 0.02),
        'nsa_cmp_w1': nrm((N_EVEN, 2, CMP_LEN * NSA_HEAD, CMP_HIDDEN), (CMP_LEN * NSA_HEAD) ** -0.5),
        'nsa_cmp_w2': nrm((N_EVEN, 2, CMP_HIDDEN, NSA_HEAD), CMP_HIDDEN ** -0.5),
        'od_w_in': nrm((N_ODD, D_MODEL, HG_COLS), d_inv) * col_scale(hg_spec),
        'od_w_out': nrm((N_ODD, D_MODEL, D_MODEL), d_inv * b),
        'hg_lb': nrm((DEPTH, D_MODEL), 0.1),
        'hg_norm_g': 1.0 + nrm((N_ODD, HG_HEAD), 0.02),
        'moe_w_rg': nrm((DEPTH, D_MODEL, MOE_GROUPS), d_inv),
        'moe_b_rg': nrm((DEPTH, MOE_GROUPS), 0.01),
        'moe_w_re': nrm((DEPTH, D_MODEL, MOE_EXPERTS), d_inv),
        'moe_b_re': nrm((DEPTH, MOE_EXPERTS), 0.01),
        'moe_w1': nrm((DEPTH, MOE_EXPERTS, D_MODEL, MOE_HIDDEN), d_inv),
        'moe_w3': nrm((DEPTH, MOE_EXPERTS, D_MODEL, MOE_HIDDEN), d_inv),
        'moe_w2': nrm((DEPTH, MOE_EXPERTS, MOE_HIDDEN, D_MODEL), MOE_HIDDEN ** -0.5 * b),
        'ln_g': 1.0 + nrm((DEPTH, 2, D_MODEL), 0.02),
        'ln_b': nrm((DEPTH, 2, D_MODEL), 0.02),
        'ple_w': nrm((DEPTH, PLE_DIM, D_MODEL), PLE_DIM ** -0.5),
        'ple_gate_w': nrm((DEPTH, D_MODEL, D_MODEL), d_inv),
    }


def reference(x, p, positions, ev_w_in, ev_w_out, rw_mu, rw_w0, rw_w2, rw_a0, rw_a2, rw_g2, rw_k_k, rw_k_a,
              rw_r_k, rw_gn_g, rw_gn_b, nsa_cmp_pos, nsa_cmp_w1, nsa_cmp_w2, od_w_in, od_w_out, hg_lb, hg_norm_g,
              moe_w_rg, moe_b_rg, moe_w_re, moe_b_re, moe_w1, moe_w3, moe_w2, ln_g, ln_b, ple_w, ple_gate_w):
    cos, sin = _rope_tables(positions, NSA_HEAD)
    lb_soft = jax.nn.softmax(hg_lb.astype(jnp.float32), axis=0)
    lb_all = jnp.cumsum(lb_soft, axis=0) - lb_soft[0:1]
    for li in range(DEPTH):
        j = li // 2
        if li % 2 == 0:
            z_rw, z_nsa = jnp.split(x @ ev_w_in[j], [RW_COLS], axis=-1)
            y_rw = _rwkv7_group(z_rw, rw_mu[j], rw_w0[j], rw_w2[j], rw_a0[j], rw_a2[j], rw_g2[j], rw_k_k[j],
                                rw_k_a[j], rw_r_k[j], rw_gn_g[j], rw_gn_b[j])
            y_nsa = _nsa_group(z_nsa, cos, sin, nsa_cmp_pos[j], nsa_cmp_w1[j], nsa_cmp_w2[j])
            mix = jnp.concatenate([y_rw.astype(x.dtype), y_nsa.astype(x.dtype)], -1) @ ev_w_out[j]
        else:
            mix = _hgrn2_mixer(x, od_w_in[j], od_w_out[j], lb_all[li], hg_norm_g[j])
        h = _layer_norm(DN_ALPHA * x + mix, ln_g[li, 0], ln_b[li, 0])
        ffn = _hier_moe(h, moe_w_rg[li], moe_b_rg[li], moe_w_re[li], moe_b_re[li], moe_w1[li], moe_w3[li], moe_w2[li])
        h = _layer_norm(DN_ALPHA * h + ffn, ln_g[li, 1], ln_b[li, 1])
        x = h + jax.nn.sigmoid(h @ ple_gate_w[li]) * (p[li] @ ple_w[li])
    return x
```

```python
import functools

import numpy as np
import jax
import jax.numpy as jnp
from jax import lax
from jax.experimental import pallas as pl
from jax.experimental.pallas import tpu as pltpu

D_MODEL = 1024
DEPTH = 2
PLE_DIM = 256
DN_ALPHA = (2 * DEPTH) ** 0.25
LN_EPS = 1e-5
ROPE_THETA = 10000.0

RW_WIDTH = D_MODEL // 2
RW_HEAD = 64
RW_HEADS = RW_WIDTH // RW_HEAD
RW_DECAY_LORA = 64
RW_AAA_LORA = 64
RW_GATE_LORA = 128
RW_GN_EPS = RW_HEAD * 1e-5
RW_SPLITS = (RW_WIDTH, RW_WIDTH, RW_WIDTH, RW_DECAY_LORA, RW_AAA_LORA, RW_GATE_LORA)
RW_COLS = sum(RW_SPLITS)

NSA_WIDTH = D_MODEL - RW_WIDTH
NSA_HEAD = 64
NSA_HEADS = NSA_WIDTH // NSA_HEAD
NSA_KV_HEADS = 2
NSA_GROUP = NSA_HEADS // NSA_KV_HEADS
NSA_KV = NSA_KV_HEADS * NSA_HEAD
CMP_LEN = 32
CMP_STRIDE = 16
CMP_HIDDEN = 128
SEL_LEN = 64
SEL_TOP = 16
WINDOW = 512
NSA_QBLOCK = 32
NSA_SPLITS = (NSA_WIDTH,) + (NSA_KV,) * 6 + (3 * NSA_HEADS,)
NSA_COLS = sum(NSA_SPLITS)
EV_COLS = RW_COLS + NSA_COLS

HG_HEAD = 128
HG_HEADS = D_MODEL // HG_HEAD
HG_CHUNK = 16
HG_SPLITS = (D_MODEL, D_MODEL, D_MODEL, D_MODEL)

MOE_GROUPS = 4
MOE_PER_GROUP = 8
MOE_EXPERTS = MOE_GROUPS * MOE_PER_GROUP
MOE_TOPK = 2
MOE_HIDDEN = 512
MOE_BLOCK = 256

LANES = 128
VMEM_LIMIT_BYTES = 56 * 1024 * 1024


def _round_up(n, m):
    return (n + m - 1) // m * m


def _dense_kernel(x_ref, w_ref, o_ref):
    o_ref[...] = jnp.dot(x_ref[...].astype(jnp.bfloat16), w_ref[...], preferred_element_type=jnp.float32)


def _dense(x2d, w, tm=512):
    m, k = x2d.shape
    n = w.shape[1]
    n_pad = _round_up(n, LANES)
    wb = w.astype(jnp.bfloat16)
    if n_pad != n:
        wb = jnp.pad(wb, ((0, 0), (0, n_pad - n)))
    tm = min(tm, m)
    assert m % tm == 0
    out = pl.pallas_call(
        _dense_kernel,
        grid=(m // tm,),
        in_specs=[pl.BlockSpec((tm, k), lambda i: (i, 0)), pl.BlockSpec((k, n_pad), lambda i: (0, 0))],
        out_specs=pl.BlockSpec((tm, n_pad), lambda i: (i, 0)),
        out_shape=jax.ShapeDtypeStruct((m, n_pad), jnp.float32),
        compiler_params=pltpu.CompilerParams(dimension_semantics=("arbitrary",), vmem_limit_bytes=VMEM_LIMIT_BYTES),
        name="dense",
    )(x2d, wb)
    return out[:, :n] if n_pad != n else out


def _split(z, sizes):
    cuts = [int(c) for c in np.cumsum(sizes)[:-1]]
    return jnp.split(z, cuts, axis=-1)


def _layer_norm(x, g, b):
    xc = x - jnp.mean(x, -1, keepdims=True)
    var = jnp.mean(xc * xc, -1, keepdims=True)
    return xc * lax.rsqrt(var + LN_EPS) * g + b


def _rope_tables(positions, dim):
    inv = (1.0 / (ROPE_THETA ** (np.arange(0, dim, 2, dtype=np.float32) / dim))).astype(np.float32)
    ang = positions.astype(jnp.float32)[..., None] * inv
    return jnp.cos(ang)[:, :, None, :], jnp.sin(ang)[:, :, None, :]


def _apply_rope(x, cos, sin):
    half = x.shape[-1] // 2
    x1, x2 = x[..., :half], x[..., half:]
    return jnp.concatenate([x1 * cos - x2 * sin, x2 * cos + x1 * sin], -1)


def _masked_softmax(s, mask):
    s = jnp.where(mask, s, -jnp.inf)
    m = jnp.max(s, axis=-1, keepdims=True)
    m = jnp.where(jnp.isfinite(m), m, 0.0)
    e = jnp.exp(s - m)
    return e / jnp.maximum(jnp.sum(e, -1, keepdims=True), 1e-30)


def _rwkv7_scan(r, w, k, v, kk, a):
    B, T, H, N = r.shape

    def step(S, inp):
        r_t, w_t, k_t, v_t, kk_t, a_t = inp
        sa = jnp.einsum('bhvk,bhk->bhv', S, kk_t)
        S = S * w_t[:, :, None, :] - sa[..., None] * (kk_t * a_t)[:, :, None, :] + v_t[..., None] * k_t[:, :, None, :]
        return S, jnp.einsum('bhvk,bhk->bhv', S, r_t)

    xs = tuple(jnp.moveaxis(t, 1, 0) for t in (r, w, k, v, kk, a))
    _, o = lax.scan(step, jnp.zeros((B, H, N, N), jnp.float32), xs)
    return jnp.moveaxis(o, 0, 1)


def _rwkv7_group(z, mu, w0, w2, a0, a2, g2, k_k, k_a, r_k, gn_g, gn_b):
    B, T, _ = z.shape
    z_prev = jnp.pad(z, ((0, 0), (1, 0), (0, 0)))[:, :-1]
    z = z + mu * (z_prev - z)
    r, k, v, wd, ad, gd = _split(z, RW_SPLITS)
    w = -jax.nn.softplus(-(w0 + jnp.tanh(wd) @ w2)) - 0.5
    decay = jnp.exp(-jnp.exp(w))
    a = jax.nn.sigmoid(a0 + ad @ a2)
    g = jax.nn.sigmoid(gd) @ g2
    heads = lambda t: t.reshape(B, T, RW_HEADS, RW_HEAD)
    kk = heads(k * k_k)
    kk = kk / jnp.maximum(jnp.sqrt(jnp.sum(kk * kk, -1, keepdims=True)), 1e-12)
    k = k * (1.0 + (a - 1.0) * k_a)
    r, k, v, a = heads(r), heads(k), heads(v), heads(a)
    o = _rwkv7_scan(r, heads(decay), k, v, kk, a)
    oc = o - jnp.mean(o, -1, keepdims=True)
    o = oc * lax.rsqrt(jnp.mean(oc * oc, -1, keepdims=True) + RW_GN_EPS)
    o = o.reshape(B, T, RW_WIDTH) * gn_g + gn_b
    bonus = (jnp.sum(r * k * r_k, -1, keepdims=True) * v).reshape(B, T, RW_WIDTH)
    return (o + bonus) * g


def _cmp_sel_overlap(n_cmp, n_sel):
    cs = np.arange(n_cmp)[:, None] * CMP_STRIDE
    ss = np.arange(n_sel)[None, :] * SEL_LEN
    ov = np.clip(np.minimum(cs + CMP_LEN, ss + SEL_LEN) - np.maximum(cs, ss), 0, None)
    return (ov / CMP_LEN).astype(np.float32)


def _nsa_group(z, cos, sin, cmp_pos, cmp_w1, cmp_w2):
    B, T, _ = z.shape
    q, kc, vc, ksl, vsl, kwn, vwn, gl = _split(z, NSA_SPLITS)
    kv_heads = lambda t: t.reshape(B, T, NSA_KV_HEADS, NSA_HEAD)
    q = q.reshape(B, T, NSA_HEADS, NSA_HEAD)
    q_rot = _apply_rope(q, cos, sin).reshape(B, T, NSA_KV_HEADS, NSA_GROUP, NSA_HEAD)
    q_raw = q.reshape(B, T, NSA_KV_HEADS, NSA_GROUP, NSA_HEAD)
    ksl = _apply_rope(kv_heads(ksl), cos, sin)
    kwn = _apply_rope(kv_heads(kwn), cos, sin)
    vsl, vwn = kv_heads(vsl), kv_heads(vwn)
    gates = jax.nn.sigmoid(gl).reshape(B, T, NSA_KV_HEADS, NSA_GROUP, 3)

    n_cmp = (T - CMP_LEN) // CMP_STRIDE + 1
    blk_idx = np.arange(n_cmp)[:, None] * CMP_STRIDE + np.arange(CMP_LEN)[None, :]
    blocks = jnp.stack([kv_heads(kc), kv_heads(vc)])[:, :, blk_idx]
    blocks = blocks + cmp_pos[:, None, None, :, None, :]
    blocks = jnp.moveaxis(blocks, 3, 4).reshape(2, B, n_cmp, NSA_KV_HEADS, CMP_LEN * NSA_HEAD)
    hid = jax.nn.gelu(jnp.einsum('zbnhi,zio->zbnho', blocks, cmp_w1))
    kv_cmp = jnp.einsum('zbnho,zod->zbnhd', hid, cmp_w2)
    k_cmp, v_cmp = kv_cmp[0], kv_cmp[1]
    cmp_last = np.arange(n_cmp) * CMP_STRIDE + CMP_LEN - 1

    n_sel = T // SEL_LEN
    n_top = min(SEL_TOP, n_sel)
    overlap = jnp.asarray(_cmp_sel_overlap(n_cmp, n_sel))
    sel_blocks = lambda t: jnp.moveaxis(t.reshape(B, n_sel, SEL_LEN, NSA_KV_HEADS, NSA_HEAD), 3, 1)
    k_blk, v_blk = sel_blocks(ksl), sel_blocks(vsl)
    gather = jax.vmap(jax.vmap(lambda blk, ix: blk[ix]))
    blk_ids = np.arange(n_sel)
    in_blk = np.arange(SEL_LEN)

    pad = ((0, 0), (WINDOW, 0), (0, 0), (0, 0))
    k_win, v_win = jnp.pad(kwn, pad), jnp.pad(vwn, pad)
    win_off = np.arange(WINDOW + NSA_QBLOCK) - WINDOW
    scale = NSA_HEAD ** -0.5

    def q_block(c):
        t0 = c * NSA_QBLOCK
        tq = t0 + jnp.arange(NSA_QBLOCK)
        sl = lambda t: lax.dynamic_slice_in_dim(t, t0, NSA_QBLOCK, axis=1)
        qc, qr, gc = sl(q_raw) * scale, sl(q_rot) * scale, sl(gates)
        s_c = jnp.einsum('bqhgd,bnhd->bhgqn', qc, k_cmp)
        p_c = _masked_softmax(s_c, cmp_last[None, :] <= tq[:, None])
        o_c = jnp.einsum('bhgqn,bnhd->bqhgd', p_c, v_cmp)
        imp = jnp.einsum('bhgqn,ns->bhqs', p_c, overlap)
        cur = (tq // SEL_LEN)[:, None]
        forced = (blk_ids[None, :] == 0) | (blk_ids[None, :] == cur) | (blk_ids[None, :] == cur - 1)
        imp = jnp.where(forced, jnp.inf, jnp.where(blk_ids[None, :] > cur, -jnp.inf, imp))
        _, sel = lax.top_k(imp, n_top)
        k_g = gather(k_blk, sel).reshape(B, NSA_KV_HEADS, NSA_QBLOCK, n_top * SEL_LEN, NSA_HEAD)
        v_g = gather(v_blk, sel).reshape(B, NSA_KV_HEADS, NSA_QBLOCK, n_top * SEL_LEN, NSA_HEAD)
        pos_s = (sel[..., None] * SEL_LEN + in_blk).reshape(B, NSA_KV_HEADS, NSA_QBLOCK, n_top * SEL_LEN)
        s_s = jnp.einsum('bqhgd,bhqkd->bhgqk', qr, k_g)
        p_s = _masked_softmax(s_s, (pos_s <= tq[:, None])[:, :, None])
        o_s = jnp.einsum('bhgqk,bhqkd->bqhgd', p_s, v_g)
        kw = lax.dynamic_slice_in_dim(k_win, t0, WINDOW + NSA_QBLOCK, axis=1)
        vw = lax.dynamic_slice_in_dim(v_win, t0, WINDOW + NSA_QBLOCK, axis=1)
        pos_w = t0 + win_off
        valid_w = (pos_w[None, :] >= 0) & (pos_w[None, :] <= tq[:, None]) & (pos_w[None, :] > tq[:, None] - WINDOW)
        s_w = jnp.einsum('bqhgd,bkhd->bhgqk', qr, kw)
        p_w = _masked_softmax(s_w, valid_w)
        o_w = jnp.einsum('bhgqk,bkhd->bqhgd', p_w, vw)
        return gc[..., 0:1] * o_c + gc[..., 1:2] * o_s + gc[..., 2:3] * o_w

    out = lax.map(q_block, jnp.arange(T // NSA_QBLOCK))
    return jnp.moveaxis(out, 0, 1).reshape(B, T, NSA_WIDTH)


def _hgrn2_chunkwise(q, k, v, logf):
    B, T, H, dk = q.shape
    dv = v.shape[-1]
    C = HG_CHUNK
    N = T // C
    to_chunks = lambda t: t.reshape(B, N, C, H, t.shape[-1]).transpose(0, 3, 1, 2, 4)
    q, k, v, logf = to_chunks(q), to_chunks(k), to_chunks(v), to_chunks(logf)
    b = jnp.cumsum(logf, axis=3)
    b_last = b[:, :, :, -1:, :]
    q_e = q * jnp.exp(b)
    k_e = k * jnp.exp(-b)
    k_tail = k * jnp.exp(b_last - b)
    causal = np.tril(np.ones((C, C), dtype=bool))
    A = jnp.where(causal, jnp.einsum('bhncd,bhnsd->bhncs', q_e, k_e), 0.0)
    o_intra = jnp.einsum('bhncs,bhnsv->bhncv', A, v)

    def step(S, inp):
        qn, kn, vn, dn = inp
        o = jnp.einsum('bhcd,bhdv->bhcv', qn, S)
        S = S * dn[..., None] + jnp.einsum('bhcd,bhcv->bhdv', kn, vn)
        return S, o

    xs = (jnp.moveaxis(q_e, 2, 0), jnp.moveaxis(k_tail, 2, 0), jnp.moveaxis(v, 2, 0),
          jnp.moveaxis(jnp.exp(b_last[:, :, :, 0]), 2, 0))
    _, o_inter = lax.scan(step, jnp.zeros((B, H, dk, dv), jnp.float32), xs)
    o = o_intra + jnp.moveaxis(o_inter, 0, 2)
    return o.transpose(0, 2, 3, 1, 4).reshape(B, T, H, dv)


def _hgrn2_mixer(x, w_in, w_out, lb, norm_g):
    B, T, D = x.shape
    z = _dense(x.reshape(B * T, D), w_in).reshape(B, T, -1)
    q, f, i, g = _split(z, HG_SPLITS)
    heads = lambda t: t.reshape(B, T, HG_HEADS, HG_HEAD)
    lb = lb.reshape(HG_HEADS, HG_HEAD)
    f = heads(f)
    forget = lb + (1.0 - lb) * jax.nn.sigmoid(f)
    k = (1.0 - lb) * jax.nn.sigmoid(-f)
    o = _hgrn2_chunkwise(jax.nn.silu(heads(q)), k, heads(i), jnp.log(forget))
    o = o * lax.rsqrt(jnp.mean(o * o, -1, keepdims=True) + LN_EPS) * norm_g
    o = o.reshape(B, T, D_MODEL) * jax.nn.silu(g)
    return _dense(o.reshape(B * T, D), w_out).reshape(B, T, D)


def _hier_moe(h, w_rg, b_rg, w_re, b_re, w1, w3, w2):
    B, T, D = h.shape
    M = B * T
    A = M * MOE_TOPK
    xt = h.reshape(M, D)
    lg = xt @ w_rg + b_rg
    g_sel = jnp.argmax(lg, -1)
    p_grp = jnp.max(jax.nn.softmax(lg, -1), -1)
    le = (xt @ w_re + b_re).reshape(M, MOE_GROUPS, MOE_PER_GROUP)
    le = le[jnp.arange(M), g_sel]
    top_v, top_i = lax.top_k(le, MOE_TOPK)
    gate = (p_grp[:, None] * jax.nn.softmax(top_v, -1)).reshape(A)
    eid = (g_sel[:, None] * MOE_PER_GROUP + top_i).reshape(A)
    tok = jnp.repeat(jnp.arange(M), MOE_TOPK)
    order = jnp.argsort(eid)
    e_s, tok_s, gate_s = eid[order], tok[order], gate[order]
    counts = jnp.bincount(eid, length=MOE_EXPERTS)
    padded = (counts + MOE_BLOCK - 1) // MOE_BLOCK * MOE_BLOCK
    ends = jnp.cumsum(padded)
    start = ends - padded
    off = jnp.cumsum(counts) - counts
    dest = start[e_s] + jnp.arange(A) - off[e_s]
    P = A + MOE_EXPERTS * MOE_BLOCK
    n_blk = P // MOE_BLOCK
    xbuf = jnp.zeros((P, D), h.dtype).at[dest].set(xt[tok_s])
    blk_e = jnp.minimum(jnp.searchsorted(ends, jnp.arange(n_blk) * MOE_BLOCK, side='right'), MOE_EXPERTS - 1)

    def expert_block(args):
        xb, e = args
        return (jax.nn.silu(xb @ w1[e]) * (xb @ w3[e])) @ w2[e]

    ybuf = lax.map(expert_block, (xbuf.reshape(n_blk, MOE_BLOCK, D), blk_e)).reshape(P, D)
    y = jnp.zeros((M, D), jnp.float32).at[tok_s].add(gate_s[:, None] * ybuf[dest])
    return y.reshape(B, T, D)


def kernel(x, p, positions, ev_w_in, ev_w_out, rw_mu, rw_w0, rw_w2, rw_a0, rw_a2, rw_g2, rw_k_k, rw_k_a,
           rw_r_k, rw_gn_g, rw_gn_b, nsa_cmp_pos, nsa_cmp_w1, nsa_cmp_w2, od_w_in, od_w_out, hg_lb, hg_norm_g,
           moe_w_rg, moe_b_rg, moe_w_re, moe_b_re, moe_w1, moe_w3, moe_w2, ln_g, ln_b, ple_w, ple_gate_w):
    B, T, D = x.shape
    M = B * T
    cos, sin = _rope_tables(positions, NSA_HEAD)
    lb_soft = jax.nn.softmax(hg_lb, axis=0)
    lb_all = jnp.cumsum(lb_soft, axis=0) - lb_soft[0:1]
    for li in range(DEPTH):
        j = li // 2
        if li % 2 == 0:
            z = _dense(x.reshape(M, D), ev_w_in[j]).reshape(B, T, -1)
            z_rw, z_nsa = jnp.split(z, [RW_COLS], axis=-1)
            y_rw = _rwkv7_group(z_rw, rw_mu[j], rw_w0[j], rw_w2[j], rw_a0[j], rw_a2[j], rw_g2[j], rw_k_k[j],
                                rw_k_a[j], rw_r_k[j], rw_gn_g[j], rw_gn_b[j])
            y_nsa = _nsa_group(z_nsa, cos, sin, nsa_cmp_pos[j], nsa_cmp_w1[j], nsa_cmp_w2[j])
            mix = _dense(jnp.concatenate([y_rw, y_nsa], -1).reshape(M, D), ev_w_out[j]).reshape(B, T, D)
        else:
            mix = _hgrn2_mixer(x, od_w_in[j], od_w_out[j], lb_all[li], hg_norm_g[j])
        h = _layer_norm(DN_ALPHA * x + mix, ln_g[li, 0], ln_b[li, 0])
        ffn = _hier_moe(h, moe_w_rg[li], moe_b_rg[li], moe_w_re[li], moe_b_re[li], moe_w1[li], moe_w3[li], moe_w2[li])
        h = _layer_norm(DN_ALPHA * h + ffn, ln_g[li, 1], ln_b[li, 1])
        gate = _dense(h.reshape(M, D), ple_gate_w[li]).reshape(B, T, D)
        pp = _dense(p[li].reshape(M, PLE_DIM), ple_w[li]).reshape(B, T, D)
        x = h + jax.nn.sigmoid(gate) * pp
    return x
```

```python
import functools

import numpy as np
import jax
import jax.numpy as jnp
from jax import lax
from jax.experimental import pallas as pl
from jax.experimental.pallas import tpu as pltpu

D_MODEL = 1024
DEPTH = 2
PLE_DIM = 256
DN_ALPHA = (2 * DEPTH) ** 0.25
LN_EPS = 1e-5
ROPE_THETA = 10000.0

RW_WIDTH = D_MODEL // 2
RW_HEAD = 64
RW_HEADS = RW_WIDTH // RW_HEAD
RW_DECAY_LORA = 64
RW_AAA_LORA = 64
RW_GATE_LORA = 128
RW_GN_EPS = RW_HEAD * 1e-5
RW_SPLITS = (RW_WIDTH, RW_WIDTH, RW_WIDTH, RW_DECAY_LORA, RW_AAA_LORA, RW_GATE_LORA)
RW_COLS = sum(RW_SPLITS)

NSA_WIDTH = D_MODEL - RW_WIDTH
NSA_HEAD = 64
NSA_HEADS = NSA_WIDTH // NSA_HEAD
NSA_KV_HEADS = 2
NSA_GROUP = NSA_HEADS // NSA_KV_HEADS
NSA_KV = NSA_KV_HEADS * NSA_HEAD
CMP_LEN = 32
CMP_STRIDE = 16
CMP_HIDDEN = 128
SEL_LEN = 64
SEL_TOP = 16
WINDOW = 512
NSA_QBLOCK = 32
NSA_SPLITS = (NSA_WIDTH,) + (NSA_KV,) * 6 + (3 * NSA_HEADS,)
NSA_COLS = sum(NSA_SPLITS)
EV_COLS = RW_COLS + NSA_COLS

HG_HEAD = 128
HG_HEADS = D_MODEL // HG_HEAD
HG_CHUNK = 16
HG_SPLITS = (D_MODEL, D_MODEL, D_MODEL, D_MODEL)

MOE_GROUPS = 4
MOE_PER_GROUP = 8
MOE_EXPERTS = MOE_GROUPS * MOE_PER_GROUP
MOE_TOPK = 2
MOE_HIDDEN = 512
MOE_BLOCK = 256

LANES = 128
VMEM_LIMIT_BYTES = 56 * 1024 * 1024


def _round_up(n, m):
    return (n + m - 1) // m * m


def _dense_kernel(x_ref, w_ref, o_ref):
    o_ref[...] = jnp.dot(x_ref[...].astype(jnp.bfloat16), w_ref[...], preferred_element_type=jnp.float32)


def _dense(x2d, w, tm=512):
    m, k = x2d.shape
    n = w.shape[1]
    n_pad = _round_up(n, LANES)
    wb = w.astype(jnp.bfloat16)
    if n_pad != n:
        wb = jnp.pad(wb, ((0, 0), (0, n_pad - n)))
    tm = min(tm, m)
    assert m % tm == 0
    out = pl.pallas_call(
        _dense_kernel,
        grid=(m // tm,),
        in_specs=[pl.BlockSpec((tm, k), lambda i: (i, 0)), pl.BlockSpec((k, n_pad), lambda i: (0, 0))],
        out_specs=pl.BlockSpec((tm, n_pad), lambda i: (i, 0)),
        out_shape=jax.ShapeDtypeStruct((m, n_pad), jnp.float32),
        compiler_params=pltpu.CompilerParams(dimension_semantics=("arbitrary",), vmem_limit_bytes=VMEM_LIMIT_BYTES),
        name="dense",
    )(x2d, wb)
    return out[:, :n] if n_pad != n else out


def _split(z, sizes):
    cuts = [int(c) for c in np.cumsum(sizes)[:-1]]
    return jnp.split(z, cuts, axis=-1)


def _layer_norm(x, g, b):
    xc = x - jnp.mean(x, -1, keepdims=True)
    var = jnp.mean(xc * xc, -1, keepdims=True)
    return xc * lax.rsqrt(var + LN_EPS) * g + b


def _rope_tables(positions, dim):
    inv = (1.0 / (ROPE_THETA ** (np.arange(0, dim, 2, dtype=np.float32) / dim))).astype(np.float32)
    ang = positions.astype(jnp.float32)[..., None] * inv
    return jnp.cos(ang)[:, :, None, :], jnp.sin(ang)[:, :, None, :]


def _apply_rope(x, cos, sin):
    half = x.shape[-1] // 2
    x1, x2 = x[..., :half], x[..., half:]
    return jnp.concatenate([x1 * cos - x2 * sin, x2 * cos + x1 * sin], -1)


def _masked_softmax(s, mask):
    s = jnp.where(mask, s, -jnp.inf)
    m = jnp.max(s, axis=-1, keepdims=True)
    m = jnp.where(jnp.isfinite(m), m, 0.0)
    e = jnp.exp(s - m)
    return e / jnp.maximum(jnp.sum(e, -1, keepdims=True), 1e-30)


def _rwkv7_scan(r, w, k, v, kk, a):
    B, T, H, N = r.shape

    def step(S, inp):
        r_t, w_t, k_t, v_t, kk_t, a_t = inp
        sa = jnp.einsum('bhvk,bhk->bhv', S, kk_t)
        S = S * w_t[:, :, None, :] - sa[..., None] * (kk_t * a_t)[:, :, None, :] + v_t[..., None] * k_t[:, :, None, :]
        return S, jnp.einsum('bhvk,bhk->bhv', S, r_t)

    xs = tuple(jnp.moveaxis(t, 1, 0) for t in (r, w, k, v, kk, a))
    _, o = lax.scan(step, jnp.zeros((B, H, N, N), jnp.float32), xs)
    return jnp.moveaxis(o, 0, 1)


def _rwkv7_group(z, mu, w0, w2, a0, a2, g2, k_k, k_a, r_k, gn_g, gn_b):
    B, T, _ = z.shape
    z_prev = jnp.pad(z, ((0, 0), (1, 0), (0, 0)))[:, :-1]
    z = z + mu * (z_prev - z)
    r, k, v, wd, ad, gd = _split(z, RW_SPLITS)
    w = -jax.nn.softplus(-(w0 + jnp.tanh(wd) @ w2)) - 0.5
    decay = jnp.exp(-jnp.exp(w))
    a = jax.nn.sigmoid(a0 + ad @ a2)
    g = jax.nn.sigmoid(gd) @ g2
    heads = lambda t: t.reshape(B, T, RW_HEADS, RW_HEAD)
    kk = heads(k * k_k)
    kk = kk / jnp.maximum(jnp.sqrt(jnp.sum(kk * kk, -1, keepdims=True)), 1e-12)
    k = k * (1.0 + (a - 1.0) * k_a)
    r, k, v, a = heads(r), heads(k), heads(v), heads(a)
    o = _rwkv7_scan(r, heads(decay), k, v, kk, a)
    oc = o - jnp.mean(o, -1, keepdims=True)
    o = oc * lax.rsqrt(jnp.mean(oc * oc, -1, keepdims=True) + RW_GN_EPS)
    o = o.reshape(B, T, RW_WIDTH) * gn_g + gn_b
    bonus = (jnp.sum(r * k * r_k, -1, keepdims=True) * v).reshape(B, T, RW_WIDTH)
    return (o + bonus) * g


def _cmp_sel_overlap(n_cmp, n_sel):
    cs = np.arange(n_cmp)[:, None] * CMP_STRIDE
    ss = np.arange(n_sel)[None, :] * SEL_LEN
    ov = np.clip(np.minimum(cs + CMP_LEN, ss + SEL_LEN) - np.maximum(cs, ss), 0, None)
    return (ov / CMP_LEN).astype(np.float32)


NSA_TQ = 128
NSA_TK_SEL = 512
NSA_TK_WIN = 256
NEG_INIT = -1e30


def _nsa_attn_kernel(qraw_ref, qrot_ref, kcmp_ref, vcmpT_ref, ovT_ref, ksel_ref, vselT_ref, kwin_ref, vwinT_ref,
                     gl_ref, o_ref, lim_ref, *, tq, n_top):
    f32, bf16 = jnp.float32, jnp.bfloat16
    G = NSA_GROUP
    R = G * tq
    t0 = pl.program_id(2) * tq
    ncp = kcmp_ref.shape[2]
    n_sel = ovT_ref.shape[0]

    t_row = t0 + (lax.broadcasted_iota(jnp.int32, (1, R), 1) & (tq - 1))
    qraw = jnp.concatenate([qraw_ref[0, g] for g in range(G)], axis=1)
    qrot = jnp.concatenate([qrot_ref[0, g] for g in range(G)], axis=1)

    s_c = jnp.dot(kcmp_ref[0, 0], qraw, precision=lax.Precision.HIGHEST, preferred_element_type=f32)
    cmp_last = lax.broadcasted_iota(jnp.int32, (ncp, R), 0) * CMP_STRIDE + (CMP_LEN - 1)
    s_c = jnp.where(cmp_last <= t_row, s_c, -jnp.inf)
    m_c = jnp.max(s_c, axis=0, keepdims=True)
    m_c = jnp.where(m_c == -jnp.inf, 0.0, m_c)
    e_c = jnp.exp(s_c - m_c)
    p_c = e_c / jnp.maximum(jnp.sum(e_c, axis=0, keepdims=True), 1e-30)
    o_c = jnp.dot(vcmpT_ref[0, 0], p_c.astype(bf16), preferred_element_type=f32)

    p_sum = p_c[:, 0:tq]
    for g in range(1, G):
        p_sum = p_sum + p_c[:, g * tq:(g + 1) * tq]
    imp = jnp.dot(ovT_ref[...], p_sum, precision=lax.Precision.HIGHEST, preferred_element_type=f32)
    j_iota = lax.broadcasted_iota(jnp.int32, (n_sel, tq), 0)
    t_tok = t0 + lax.broadcasted_iota(jnp.int32, (1, tq), 1)
    cur = lax.shift_right_logical(t_tok, int(np.log2(SEL_LEN)))
    forced = (j_iota == 0) | (j_iota == cur) | (j_iota == cur - 1)
    cand = (j_iota >= 1) & (j_iota <= cur - 2)
    quota = n_top - 1 - jnp.minimum(cur, 2)
    key = jnp.where(cand, pltpu.bitcast(imp, jnp.int32), -1)
    rank = jnp.zeros((n_sel, tq), jnp.int32)
    for i in range(n_sel):
        row = key[i:i + 1, :]
        rank = rank + jnp.where(row + jnp.where(j_iota > i, 1, 0) > key, 1, 0)
    sel = forced | (cand & (rank < quota))
    lim_ref[...] = jnp.where(sel, t_tok, -1)

    def online_update(carry, s, vT):
        m, l, acc = carry
        m_new = jnp.maximum(m, jnp.max(s, axis=0, keepdims=True))
        alpha = jnp.exp(m - m_new)
        p = jnp.exp(s - m_new)
        l = alpha * l + jnp.sum(p, axis=0, keepdims=True)
        acc = alpha * acc + jnp.dot(vT, p.astype(bf16), preferred_element_type=f32)
        return m_new, l, acc

    init = (jnp.full((1, R), NEG_INIT, f32), jnp.zeros((1, R), f32), jnp.zeros((NSA_HEAD, R), f32))

    tk = min(NSA_TK_SEL, ksel_ref.shape[2])
    nb = tk // SEL_LEN
    key_iota = lax.broadcasted_iota(jnp.int32, (tk, R), 0)

    def sel_body(kt, carry):
        k0 = pl.multiple_of(kt * tk, tk)
        s = jnp.dot(ksel_ref[0, 0, pl.ds(k0, tk), :], qrot, preferred_element_type=f32)
        limb = lim_ref[pl.ds(pl.multiple_of(kt * nb, nb), nb), :] - k0
        lim_t = jnp.concatenate([jnp.broadcast_to(limb[jb:jb + 1, :], (SEL_LEN, tq)) for jb in range(nb)], axis=0)
        lim_t = jnp.concatenate([lim_t] * G, axis=1)
        s = jnp.where(key_iota <= lim_t, s, -jnp.inf)
        return online_update(carry, s, vselT_ref[0, 0, :, pl.ds(k0, tk)])

    n_kt = lax.div(t0 + tq - 1, tk) + 1
    _, l_s, acc_s = lax.fori_loop(0, n_kt, sel_body, init)
    o_s = acc_s / jnp.maximum(l_s, 1e-30)

    tkw = min(NSA_TK_WIN, kwin_ref.shape[2])
    key_iota_w = lax.broadcasted_iota(jnp.int32, (tkw, R), 0)

    def win_body(kt, carry):
        k0 = pl.multiple_of(kt * tkw, tkw)
        s = jnp.dot(kwin_ref[0, 0, pl.ds(k0, tkw), :], qrot, preferred_element_type=f32)
        rel = t_row - k0
        s = jnp.where((key_iota_w <= rel) & (key_iota_w > rel - WINDOW), s, -jnp.inf)
        return online_update(carry, s, vwinT_ref[0, 0, :, pl.ds(k0, tkw)])

    kt_lo = lax.div(jnp.maximum(t0 - (WINDOW - 1), 0), tkw)
    kt_hi = lax.div(t0 + tq - 1, tkw) + 1
    _, l_w, acc_w = lax.fori_loop(kt_lo, kt_hi, win_body, init)
    o_w = acc_w / jnp.maximum(l_w, 1e-30)

    gl = jnp.concatenate([gl_ref[0, g] for g in range(G)], axis=1)
    gates = jax.nn.sigmoid(gl)
    out = gates[0:1, :] * o_c + gates[1:2, :] * o_s + gates[2:3, :] * o_w
    for g in range(G):
        o_ref[0, g] = out[:, g * tq:(g + 1) * tq]


def _nsa_group(z, cos, sin, cmp_pos, cmp_w1, cmp_w2):
    B, T, _ = z.shape
    f32, bf16 = jnp.float32, jnp.bfloat16
    q, kc, vc, ksl, vsl, kwn, vwn, gl = _split(z, NSA_SPLITS)
    kv_heads = lambda t: t.reshape(B, T, NSA_KV_HEADS, NSA_HEAD)
    scale = NSA_HEAD ** -0.5
    q = q.reshape(B, T, NSA_HEADS, NSA_HEAD)
    qraw_T = (q * scale).transpose(0, 2, 3, 1)
    qrot_T = (_apply_rope(q, cos, sin) * scale).transpose(0, 2, 3, 1).astype(bf16)
    ksel = _apply_rope(kv_heads(ksl), cos, sin).transpose(0, 2, 1, 3).astype(bf16)
    kwin = _apply_rope(kv_heads(kwn), cos, sin).transpose(0, 2, 1, 3).astype(bf16)
    vselT = kv_heads(vsl).transpose(0, 2, 3, 1).astype(bf16)
    vwinT = kv_heads(vwn).transpose(0, 2, 3, 1).astype(bf16)
    gl_T = gl.reshape(B, T, NSA_HEADS, 3).transpose(0, 2, 3, 1)

    n_grp = T // CMP_STRIDE
    n_cmp = n_grp - 1
    half = CMP_STRIDE * NSA_HEAD
    grp = jnp.stack([kv_heads(kc), kv_heads(vc)]).transpose(0, 1, 3, 2, 4).reshape(2, B * NSA_KV_HEADS * n_grp, half)
    cmp_kv = []
    for zi in range(2):
        w1ab = jnp.concatenate([cmp_w1[zi, :half], cmp_w1[zi, half:]], axis=1)
        ab = _dense(grp[zi], w1ab).reshape(B, NSA_KV_HEADS, n_grp, 2 * CMP_HIDDEN)
        c = cmp_pos[zi].reshape(1, CMP_LEN * NSA_HEAD) @ cmp_w1[zi]
        hid = jax.nn.gelu(ab[:, :, :-1, :CMP_HIDDEN] + ab[:, :, 1:, CMP_HIDDEN:] + c)
        cmp_kv.append(jnp.pad(hid @ cmp_w2[zi], ((0, 0), (0, 0), (0, 1), (0, 0))))
    k_cmp = cmp_kv[0]
    v_cmpT = cmp_kv[1].transpose(0, 1, 3, 2).astype(bf16)

    n_sel = T // SEL_LEN
    n_top = min(SEL_TOP, n_sel)
    ovT = jnp.asarray(np.pad(_cmp_sel_overlap(n_cmp, n_sel), ((0, 1), (0, 0))).T)
    tq = min(NSA_TQ, T)
    G = NSA_GROUP
    q_spec = pl.BlockSpec((1, G, NSA_HEAD, tq), lambda b, h, i: (b, h, 0, i))
    kv_spec = lambda shape: pl.BlockSpec((1, 1) + shape, lambda b, h, i: (b, h, 0, 0))
    out = pl.pallas_call(
        functools.partial(_nsa_attn_kernel, tq=tq, n_top=n_top),
        grid=(B, NSA_KV_HEADS, T // tq),
        in_specs=[q_spec, q_spec, kv_spec((n_grp, NSA_HEAD)), kv_spec((NSA_HEAD, n_grp)),
                  pl.BlockSpec((n_sel, n_grp), lambda b, h, i: (0, 0)),
                  kv_spec((T, NSA_HEAD)), kv_spec((NSA_HEAD, T)), kv_spec((T, NSA_HEAD)), kv_spec((NSA_HEAD, T)),
                  pl.BlockSpec((1, G, 3, tq), lambda b, h, i: (b, h, 0, i))],
        out_specs=q_spec,
        out_shape=jax.ShapeDtypeStruct((B, NSA_HEADS, NSA_HEAD, T), f32),
        scratch_shapes=[pltpu.VMEM((n_sel, tq), jnp.int32)],
        compiler_params=pltpu.CompilerParams(dimension_semantics=("arbitrary", "arbitrary", "arbitrary"),
                                             vmem_limit_bytes=VMEM_LIMIT_BYTES),
        name="nsa_attn",
    )(qraw_T, qrot_T, k_cmp, v_cmpT, ovT, ksel, vselT, kwin, vwinT, gl_T)
    return out.transpose(0, 3, 1, 2).reshape(B, T, NSA_WIDTH)


def _hgrn2_chunkwise(q, k, v, logf):
    B, T, H, dk = q.shape
    dv = v.shape[-1]
    C = HG_CHUNK
    N = T // C
    to_chunks = lambda t: t.reshape(B, N, C, H, t.shape[-1]).transpose(0, 3, 1, 2, 4)
    q, k, v, logf = to_chunks(q), to_chunks(k), to_chunks(v), to_chunks(logf)
    b = jnp.cumsum(logf, axis=3)
    b_last = b[:, :, :, -1:, :]
    q_e = q * jnp.exp(b)
    k_e = k * jnp.exp(-b)
    k_tail = k * jnp.exp(b_last - b)
    causal = np.tril(np.ones((C, C), dtype=bool))
    A = jnp.where(causal, jnp.einsum('bhncd,bhnsd->bhncs', q_e, k_e), 0.0)
    o_intra = jnp.einsum('bhncs,bhnsv->bhncv', A, v)

    def step(S, inp):
        qn, kn, vn, dn = inp
        o = jnp.einsum('bhcd,bhdv->bhcv', qn, S)
        S = S * dn[..., None] + jnp.einsum('bhcd,bhcv->bhdv', kn, vn)
        return S, o

    xs = (jnp.moveaxis(q_e, 2, 0), jnp.moveaxis(k_tail, 2, 0), jnp.moveaxis(v, 2, 0),
          jnp.moveaxis(jnp.exp(b_last[:, :, :, 0]), 2, 0))
    _, o_inter = lax.scan(step, jnp.zeros((B, H, dk, dv), jnp.float32), xs)
    o = o_intra + jnp.moveaxis(o_inter, 0, 2)
    return o.transpose(0, 2, 3, 1, 4).reshape(B, T, H, dv)


def _hgrn2_mixer(x, w_in, w_out, lb, norm_g):
    B, T, D = x.shape
    z = _dense(x.reshape(B * T, D), w_in).reshape(B, T, -1)
    q, f, i, g = _split(z, HG_SPLITS)
    heads = lambda t: t.reshape(B, T, HG_HEADS, HG_HEAD)
    lb = lb.reshape(HG_HEADS, HG_HEAD)
    f = heads(f)
    forget = lb + (1.0 - lb) * jax.nn.sigmoid(f)
    k = (1.0 - lb) * jax.nn.sigmoid(-f)
    o = _hgrn2_chunkwise(jax.nn.silu(heads(q)), k, heads(i), jnp.log(forget))
    o = o * lax.rsqrt(jnp.mean(o * o, -1, keepdims=True) + LN_EPS) * norm_g
    o = o.reshape(B, T, D_MODEL) * jax.nn.silu(g)
    return _dense(o.reshape(B * T, D), w_out).reshape(B, T, D)


def _hier_moe(h, w_rg, b_rg, w_re, b_re, w1, w3, w2):
    B, T, D = h.shape
    M = B * T
    A = M * MOE_TOPK
    xt = h.reshape(M, D)
    lg = xt @ w_rg + b_rg
    g_sel = jnp.argmax(lg, -1)
    p_grp = jnp.max(jax.nn.softmax(lg, -1), -1)
    le = (xt @ w_re + b_re).reshape(M, MOE_GROUPS, MOE_PER_GROUP)
    le = le[jnp.arange(M), g_sel]
    top_v, top_i = lax.top_k(le, MOE_TOPK)
    gate = (p_grp[:, None] * jax.nn.softmax(top_v, -1)).reshape(A)
    eid = (g_sel[:, None] * MOE_PER_GROUP + top_i).reshape(A)
    tok = jnp.repeat(jnp.arange(M), MOE_TOPK)
    order = jnp.argsort(eid)
    e_s, tok_s, gate_s = eid[order], tok[order], gate[order]
    counts = jnp.bincount(eid, length=MOE_EXPERTS)
    padded = (counts + MOE_BLOCK - 1) // MOE_BLOCK * MOE_BLOCK
    ends = jnp.cumsum(padded)
    start = ends - padded
    off = jnp.cumsum(counts) - counts
    dest = start[e_s] + jnp.arange(A) - off[e_s]
    P = A + MOE_EXPERTS * MOE_BLOCK
    n_blk = P // MOE_BLOCK
    xbuf = jnp.zeros((P, D), h.dtype).at[dest].set(xt[tok_s])
    blk_e = jnp.minimum(jnp.searchsorted(ends, jnp.arange(n_blk) * MOE_BLOCK, side='right'), MOE_EXPERTS - 1)

    def expert_block(args):
        xb, e = args
        return (jax.nn.silu(xb @ w1[e]) * (xb @ w3[e])) @ w2[e]

    ybuf = lax.map(expert_block, (xbuf.reshape(n_blk, MOE_BLOCK, D), blk_e)).reshape(P, D)
    y = jnp.zeros((M, D), jnp.float32).at[tok_s].add(gate_s[:, None] * ybuf[dest])
    return y.reshape(B, T, D)


def kernel(x, p, positions, ev_w_in, ev_w_out, rw_mu, rw_w0, rw_w2, rw_a0, rw_a2, rw_g2, rw_k_k, rw_k_a,
           rw_r_k, rw_gn_g, rw_gn_b, nsa_cmp_pos, nsa_cmp_w1, nsa_cmp_w2, od_w_in, od_w_out, hg_lb, hg_norm_g,
           moe_w_rg, moe_b_rg, moe_w_re, moe_b_re, moe_w1, moe_w3, moe_w2, ln_g, ln_b, ple_w, ple_gate_w):
    B, T, D = x.shape
    M = B * T
    cos, sin = _rope_tables(positions, NSA_HEAD)
    lb_soft = jax.nn.softmax(hg_lb, axis=0)
    lb_all = jnp.cumsum(lb_soft, axis=0) - lb_soft[0:1]
    for li in range(DEPTH):
        j = li // 2
        if li % 2 == 0:
            z = _dense(x.reshape(M, D), ev_w_in[j]).reshape(B, T, -1)
            z_rw, z_nsa = jnp.split(z, [RW_COLS], axis=-1)
            y_rw = _rwkv7_group(z_rw, rw_mu[j], rw_w0[j], rw_w2[j], rw_a0[j], rw_a2[j], rw_g2[j], rw_k_k[j],
                                rw_k_a[j], rw_r_k[j], rw_gn_g[j], rw_gn_b[j])
            y_nsa = _nsa_group(z_nsa, cos, sin, nsa_cmp_pos[j], nsa_cmp_w1[j], nsa_cmp_w2[j])
            mix = _dense(jnp.concatenate([y_rw, y_nsa], -1).reshape(M, D), ev_w_out[j]).reshape(B, T, D)
        else:
            mix = _hgrn2_mixer(x, od_w_in[j], od_w_out[j], lb_all[li], hg_norm_g[j])
        h = _layer_norm(DN_ALPHA * x + mix, ln_g[li, 0], ln_b[li, 0])
        ffn = _hier_moe(h, moe_w_rg[li], moe_b_rg[li], moe_w_re[li], moe_b_re[li], moe_w1[li], moe_w3[li], moe_w2[li])
        h = _layer_norm(DN_ALPHA * h + ffn, ln_g[li, 1], ln_b[li, 1])
        gate = _dense(h.reshape(M, D), ple_gate_w[li]).reshape(B, T, D)
        pp = _dense(p[li].reshape(M, PLE_DIM), ple_w[li]).reshape(B, T, D)
        x = h + jax.nn.sigmoid(gate) * pp
    return x
```

```python
import functools

import numpy as np
import jax
import jax.numpy as jnp
from jax import lax
from jax.experimental import pallas as pl
from jax.experimental.pallas import tpu as pltpu

D_MODEL = 1024
DEPTH = 2
PLE_DIM = 256
DN_ALPHA = (2 * DEPTH) ** 0.25
LN_EPS = 1e-5
ROPE_THETA = 10000.0

RW_WIDTH = D_MODEL // 2
RW_HEAD = 64
RW_HEADS = RW_WIDTH // RW_HEAD
RW_DECAY_LORA = 64
RW_AAA_LORA = 64
RW_GATE_LORA = 128
RW_GN_EPS = RW_HEAD * 1e-5
RW_SPLITS = (RW_WIDTH, RW_WIDTH, RW_WIDTH, RW_DECAY_LORA, RW_AAA_LORA, RW_GATE_LORA)
RW_COLS = sum(RW_SPLITS)

NSA_WIDTH = D_MODEL - RW_WIDTH
NSA_HEAD = 64
NSA_HEADS = NSA_WIDTH // NSA_HEAD
NSA_KV_HEADS = 2
NSA_GROUP = NSA_HEADS // NSA_KV_HEADS
NSA_KV = NSA_KV_HEADS * NSA_HEAD
CMP_LEN = 32
CMP_STRIDE = 16
CMP_HIDDEN = 128
SEL_LEN = 64
SEL_TOP = 16
WINDOW = 512
NSA_QBLOCK = 32
NSA_SPLITS = (NSA_WIDTH,) + (NSA_KV,) * 6 + (3 * NSA_HEADS,)
NSA_COLS = sum(NSA_SPLITS)
EV_COLS = RW_COLS + NSA_COLS

HG_HEAD = 128
HG_HEADS = D_MODEL // HG_HEAD
HG_CHUNK = 16
HG_SPLITS = (D_MODEL, D_MODEL, D_MODEL, D_MODEL)

MOE_GROUPS = 4
MOE_PER_GROUP = 8
MOE_EXPERTS = MOE_GROUPS * MOE_PER_GROUP
MOE_TOPK = 2
MOE_HIDDEN = 512
MOE_BLOCK = 256

LANES = 128
VMEM_LIMIT_BYTES = 56 * 1024 * 1024


def _round_up(n, m):
    return (n + m - 1) // m * m


def _dense_kernel(x_ref, w_ref, o_ref):
    o_ref[...] = jnp.dot(x_ref[...].astype(jnp.bfloat16), w_ref[...], preferred_element_type=jnp.float32)


def _dense(x2d, w, tm=512):
    m, k = x2d.shape
    n = w.shape[1]
    n_pad = _round_up(n, LANES)
    wb = w.astype(jnp.bfloat16)
    if n_pad != n:
        wb = jnp.pad(wb, ((0, 0), (0, n_pad - n)))
    tm = min(tm, m)
    assert m % tm == 0
    out = pl.pallas_call(
        _dense_kernel,
        grid=(m // tm,),
        in_specs=[pl.BlockSpec((tm, k), lambda i: (i, 0)), pl.BlockSpec((k, n_pad), lambda i: (0, 0))],
        out_specs=pl.BlockSpec((tm, n_pad), lambda i: (i, 0)),
        out_shape=jax.ShapeDtypeStruct((m, n_pad), jnp.float32),
        compiler_params=pltpu.CompilerParams(dimension_semantics=("arbitrary",), vmem_limit_bytes=VMEM_LIMIT_BYTES),
        name="dense",
    )(x2d, wb)
    return out[:, :n] if n_pad != n else out


def _split(z, sizes):
    cuts = [int(c) for c in np.cumsum(sizes)[:-1]]
    return jnp.split(z, cuts, axis=-1)


def _layer_norm(x, g, b):
    xc = x - jnp.mean(x, -1, keepdims=True)
    var = jnp.mean(xc * xc, -1, keepdims=True)
    return xc * lax.rsqrt(var + LN_EPS) * g + b


def _rope_tables(positions, dim):
    inv = (1.0 / (ROPE_THETA ** (np.arange(0, dim, 2, dtype=np.float32) / dim))).astype(np.float32)
    ang = positions.astype(jnp.float32)[..., None] * inv
    return jnp.cos(ang)[:, :, None, :], jnp.sin(ang)[:, :, None, :]


def _apply_rope(x, cos, sin):
    half = x.shape[-1] // 2
    x1, x2 = x[..., :half], x[..., half:]
    return jnp.concatenate([x1 * cos - x2 * sin, x2 * cos + x1 * sin], -1)


def _masked_softmax(s, mask):
    s = jnp.where(mask, s, -jnp.inf)
    m = jnp.max(s, axis=-1, keepdims=True)
    m = jnp.where(jnp.isfinite(m), m, 0.0)
    e = jnp.exp(s - m)
    return e / jnp.maximum(jnp.sum(e, -1, keepdims=True), 1e-30)


RW_CHUNK = 64
RW_STEP = 128


def _bdot(a, b):
    return jnp.dot(a.astype(jnp.bfloat16), b.astype(jnp.bfloat16), preferred_element_type=jnp.float32)


def _exact_dot_01(tri, x):
    bf16, f32 = jnp.bfloat16, jnp.float32
    h1 = x.astype(bf16)
    r1 = x - h1.astype(f32)
    h2 = r1.astype(bf16)
    h3 = (r1 - h2.astype(f32)).astype(bf16)
    d = lambda h: jnp.dot(tri, h, preferred_element_type=f32)
    return d(h1) + d(h2) + d(h3)


def _rwkv7_chunk_kernel(r_ref, lw_ref, k_ref, v_ref, kk_ref, b_ref, lwT_ref, kT_ref, bT_ref, o_ref, h_ref):
    f32, bf16 = jnp.float32, jnp.bfloat16
    n_heads, S, N = r_ref.shape[1], r_ref.shape[2], r_ref.shape[3]
    C = RW_CHUNK
    n_sub = S // C

    @pl.when(pl.program_id(1) == 0)
    def _():
        h_ref[...] = jnp.zeros_like(h_ref)

    row = lax.broadcasted_iota(jnp.int32, (S, S), 0)
    col = lax.broadcasted_iota(jnp.int32, (S, S), 1)
    same = lax.shift_right_logical(row, int(np.log2(C))) == lax.shift_right_logical(col, int(np.log2(C)))
    incl = same & (row >= col)
    strict = same & (row > col)
    tri = jnp.where(incl, 1.0, 0.0).astype(bf16)
    triT = jnp.where(same & (col >= row), 1.0, 0.0).astype(bf16)
    eye_s = jnp.where(row == col, 1.0, 0.0)
    rn = lax.broadcasted_iota(jnp.int32, (N, N), 0)
    cn = lax.broadcasted_iota(jnp.int32, (N, N), 1)
    eye_n = jnp.where(rn == cn, 1.0, 0.0)
    lane_chunk = lax.shift_right_logical(lax.broadcasted_iota(jnp.int32, (N, S), 1), int(np.log2(C)))

    for h in range(n_heads):
        lw = lw_ref[0, h]
        cs = _exact_dot_01(tri, lw)
        lwT = lwT_ref[0, h]
        h1 = lwT.astype(bf16)
        r1 = lwT - h1.astype(f32)
        h2 = r1.astype(bf16)
        h3 = (r1 - h2.astype(f32)).astype(bf16)
        dT = lambda x: jnp.dot(x, triT, preferred_element_type=f32)
        csT = dT(h1) + dT(h2) + dT(h3)
        kks = kk_ref[0, h] * jnp.exp(cs - lw)
        rs = r_ref[0, h] * jnp.exp(cs)
        e_negT = jnp.exp(-csT)
        bsT = bT_ref[0, h] * e_negT
        ksT = kT_ref[0, h] * e_negT
        bsT_b, ksT_b = bsT.astype(bf16), ksT.astype(bf16)
        v = v_ref[0, h].astype(bf16)

        lhs = jnp.concatenate([kks, rs], axis=0).astype(bf16)
        mb = jnp.dot(lhs, bsT_b, preferred_element_type=f32)
        mk = jnp.dot(lhs, ksT_b, preferred_element_type=f32)
        a_b = jnp.where(strict, mb[:S], 0.0)
        a_k = jnp.where(strict, mk[:S], 0.0)
        r_b = jnp.where(incl, mb[S:], 0.0)
        r_k = jnp.where(incl, mk[S:], 0.0)

        t_inv = eye_s - a_b
        pw = a_b
        for _ in range(int(np.log2(C)) - 1):
            pw = _bdot(pw, pw)
            t_inv = t_inv + _bdot(t_inv, pw)
        t_inv = t_inv.astype(bf16)

        w_mat = jnp.dot(t_inv, kks.astype(bf16), preferred_element_type=f32)
        u_loc = -jnp.dot(t_inv, _bdot(a_k, v).astype(bf16), preferred_element_type=f32)
        q_eff = rs - _bdot(r_b, w_mat)
        o_loc = _bdot(r_b, u_loc) + _bdot(r_k, v)
        w_b, u_b = w_mat.astype(bf16), u_loc.astype(bf16)

        hc = h_ref[h]
        for c in range(n_sub):
            in_c = lane_chunk == c
            bs_c = jnp.where(in_c, bsT, 0.0).astype(bf16)
            ks_c = jnp.where(in_c, ksT, 0.0).astype(bf16)
            g_end = jnp.exp(csT[:, (c + 1) * C - 1:(c + 1) * C])
            g_mat = g_end * (eye_n - jnp.dot(bs_c, w_b, preferred_element_type=f32))
            h_loc = g_end * (jnp.dot(bs_c, u_b, preferred_element_type=f32) + jnp.dot(ks_c, v, preferred_element_type=f32))
            sl = slice(c * C, (c + 1) * C)
            o_ref[0, h, sl, :] = _bdot(q_eff[sl], hc) + o_loc[sl]
            hc = _bdot(g_mat, hc) + h_loc
        h_ref[h] = hc


def _rwkv7_scan(r, lw, k, v, kk, b):
    B, T, H, N = r.shape
    C = min(RW_STEP, T)
    assert C % RW_CHUNK == 0 and T % C == 0
    hm = lambda t: t.transpose(0, 2, 1, 3)
    hmT = lambda t: t.transpose(0, 2, 3, 1)
    spec = pl.BlockSpec((1, H, C, N), lambda bi, ci: (bi, 0, ci, 0))
    specT = pl.BlockSpec((1, H, N, C), lambda bi, ci: (bi, 0, 0, ci))
    o = pl.pallas_call(
        _rwkv7_chunk_kernel,
        grid=(B, T // C),
        in_specs=[spec] * 6 + [specT] * 3,
        out_specs=spec,
        out_shape=jax.ShapeDtypeStruct((B, H, T, N), jnp.float32),
        scratch_shapes=[pltpu.VMEM((H, N, N), jnp.float32)],
        compiler_params=pltpu.CompilerParams(dimension_semantics=("arbitrary", "arbitrary"),
                                             vmem_limit_bytes=VMEM_LIMIT_BYTES),
        name="rwkv7_chunk",
    )(hm(r), hm(lw), hm(k), hm(v), hm(kk), hm(b), hmT(lw), hmT(k), hmT(b))
    return o.transpose(0, 2, 1, 3)


def _rwkv7_group(z, mu, w0, w2, a0, a2, g2, k_k, k_a, r_k, gn_g, gn_b):
    B, T, _ = z.shape
    z_prev = jnp.pad(z, ((0, 0), (1, 0), (0, 0)))[:, :-1]
    z = z + mu * (z_prev - z)
    r, k, v, wd, ad, gd = _split(z, RW_SPLITS)
    w = -jax.nn.softplus(-(w0 + jnp.tanh(wd) @ w2)) - 0.5
    log_decay = -jnp.exp(w)
    a = jax.nn.sigmoid(a0 + ad @ a2)
    g = jax.nn.sigmoid(gd) @ g2
    heads = lambda t: t.reshape(B, T, RW_HEADS, RW_HEAD)
    kk = heads(k * k_k)
    kk = kk / jnp.maximum(jnp.sqrt(jnp.sum(kk * kk, -1, keepdims=True)), 1e-12)
    k = k * (1.0 + (a - 1.0) * k_a)
    r, k, v, a = heads(r), heads(k), heads(v), heads(a)
    o = _rwkv7_scan(r, heads(log_decay), k, v, kk, a * kk)
    oc = o - jnp.mean(o, -1, keepdims=True)
    o = oc * lax.rsqrt(jnp.mean(oc * oc, -1, keepdims=True) + RW_GN_EPS)
    o = o.reshape(B, T, RW_WIDTH) * gn_g + gn_b
    bonus = (jnp.sum(r * k * r_k, -1, keepdims=True) * v).reshape(B, T, RW_WIDTH)
    return (o + bonus) * g


def _cmp_sel_overlap(n_cmp, n_sel):
    cs = np.arange(n_cmp)[:, None] * CMP_STRIDE
    ss = np.arange(n_sel)[None, :] * SEL_LEN
    ov = np.clip(np.minimum(cs + CMP_LEN, ss + SEL_LEN) - np.maximum(cs, ss), 0, None)
    return (ov / CMP_LEN).astype(np.float32)


NSA_TQ = 128
NSA_TK_SEL = 512
NSA_TK_WIN = 256
NEG_INIT = -1e30


def _nsa_attn_kernel(qraw_ref, qrot_ref, kcmp_ref, vcmpT_ref, ovT_ref, ksel_ref, vselT_ref, kwin_ref, vwinT_ref,
                     gl_ref, o_ref, lim_ref, *, tq, n_top):
    f32, bf16 = jnp.float32, jnp.bfloat16
    G = NSA_GROUP
    R = G * tq
    t0 = pl.program_id(2) * tq
    ncp = kcmp_ref.shape[2]
    n_sel = ovT_ref.shape[0]

    t_row = t0 + (lax.broadcasted_iota(jnp.int32, (1, R), 1) & (tq - 1))
    qraw = jnp.concatenate([qraw_ref[0, g] for g in range(G)], axis=1)
    qrot = jnp.concatenate([qrot_ref[0, g] for g in range(G)], axis=1)

    s_c = jnp.dot(kcmp_ref[0, 0], qraw, precision=lax.Precision.HIGHEST, preferred_element_type=f32)
    cmp_last = lax.broadcasted_iota(jnp.int32, (ncp, R), 0) * CMP_STRIDE + (CMP_LEN - 1)
    s_c = jnp.where(cmp_last <= t_row, s_c, -jnp.inf)
    m_c = jnp.max(s_c, axis=0, keepdims=True)
    m_c = jnp.where(m_c == -jnp.inf, 0.0, m_c)
    e_c = jnp.exp(s_c - m_c)
    p_c = e_c / jnp.maximum(jnp.sum(e_c, axis=0, keepdims=True), 1e-30)
    o_c = jnp.dot(vcmpT_ref[0, 0], p_c.astype(bf16), preferred_element_type=f32)

    p_sum = p_c[:, 0:tq]
    for g in range(1, G):
        p_sum = p_sum + p_c[:, g * tq:(g + 1) * tq]
    imp = jnp.dot(ovT_ref[...], p_sum, precision=lax.Precision.HIGHEST, preferred_element_type=f32)
    j_iota = lax.broadcasted_iota(jnp.int32, (n_sel, tq), 0)
    t_tok = t0 + lax.broadcasted_iota(jnp.int32, (1, tq), 1)
    cur = lax.shift_right_logical(t_tok, int(np.log2(SEL_LEN)))
    forced = (j_iota == 0) | (j_iota == cur) | (j_iota == cur - 1)
    cand = (j_iota >= 1) & (j_iota <= cur - 2)
    quota = n_top - 1 - jnp.minimum(cur, 2)
    key = jnp.where(cand, pltpu.bitcast(imp, jnp.int32), -1)
    rank = jnp.zeros((n_sel, tq), jnp.int32)
    for i in range(n_sel):
        row = key[i:i + 1, :]
        rank = rank + jnp.where(row + jnp.where(j_iota > i, 1, 0) > key, 1, 0)
    sel = forced | (cand & (rank < quota))
    lim_ref[...] = jnp.where(sel, t_tok, -1)

    def online_update(carry, s, vT):
        m, l, acc = carry
        m_new = jnp.maximum(m, jnp.max(s, axis=0, keepdims=True))
        alpha = jnp.exp(m - m_new)
        p = jnp.exp(s - m_new)
        l = alpha * l + jnp.sum(p, axis=0, keepdims=True)
        acc = alpha * acc + jnp.dot(vT, p.astype(bf16), preferred_element_type=f32)
        return m_new, l, acc

    init = (jnp.full((1, R), NEG_INIT, f32), jnp.zeros((1, R), f32), jnp.zeros((NSA_HEAD, R), f32))

    tk = min(NSA_TK_SEL, ksel_ref.shape[2])
    nb = tk // SEL_LEN
    key_iota = lax.broadcasted_iota(jnp.int32, (tk, R), 0)

    def sel_body(kt, carry):
        k0 = pl.multiple_of(kt * tk, tk)
        s = jnp.dot(ksel_ref[0, 0, pl.ds(k0, tk), :], qrot, preferred_element_type=f32)
        limb = lim_ref[pl.ds(pl.multiple_of(kt * nb, nb), nb), :] - k0
        lim_t = jnp.concatenate([jnp.broadcast_to(limb[jb:jb + 1, :], (SEL_LEN, tq)) for jb in range(nb)], axis=0)
        lim_t = jnp.concatenate([lim_t] * G, axis=1)
        s = jnp.where(key_iota <= lim_t, s, -jnp.inf)
        return online_update(carry, s, vselT_ref[0, 0, :, pl.ds(k0, tk)])

    n_kt = lax.div(t0 + tq - 1, tk) + 1
    _, l_s, acc_s = lax.fori_loop(0, n_kt, sel_body, init)
    o_s = acc_s / jnp.maximum(l_s, 1e-30)

    tkw = min(NSA_TK_WIN, kwin_ref.shape[2])
    key_iota_w = lax.broadcasted_iota(jnp.int32, (tkw, R), 0)

    def win_body(kt, carry):
        k0 = pl.multiple_of(kt * tkw, tkw)
        s = jnp.dot(kwin_ref[0, 0, pl.ds(k0, tkw), :], qrot, preferred_element_type=f32)
        rel = t_row - k0
        s = jnp.where((key_iota_w <= rel) & (key_iota_w > rel - WINDOW), s, -jnp.inf)
        return online_update(carry, s, vwinT_ref[0, 0, :, pl.ds(k0, tkw)])

    kt_lo = lax.div(jnp.maximum(t0 - (WINDOW - 1), 0), tkw)
    kt_hi = lax.div(t0 + tq - 1, tkw) + 1
    _, l_w, acc_w = lax.fori_loop(kt_lo, kt_hi, win_body, init)
    o_w = acc_w / jnp.maximum(l_w, 1e-30)

    gl = jnp.concatenate([gl_ref[0, g] for g in range(G)], axis=1)
    gates = jax.nn.sigmoid(gl)
    out = gates[0:1, :] * o_c + gates[1:2, :] * o_s + gates[2:3, :] * o_w
    for g in range(G):
        o_ref[0, g] = out[:, g * tq:(g + 1) * tq]


def _nsa_group(z, cos, sin, cmp_pos, cmp_w1, cmp_w2):
    B, T, _ = z.shape
    f32, bf16 = jnp.float32, jnp.bfloat16
    q, kc, vc, ksl, vsl, kwn, vwn, gl = _split(z, NSA_SPLITS)
    kv_heads = lambda t: t.reshape(B, T, NSA_KV_HEADS, NSA_HEAD)
    scale = NSA_HEAD ** -0.5
    q = q.reshape(B, T, NSA_HEADS, NSA_HEAD)
    qraw_T = (q * scale).transpose(0, 2, 3, 1)
    qrot_T = (_apply_rope(q, cos, sin) * scale).transpose(0, 2, 3, 1).astype(bf16)
    ksel = _apply_rope(kv_heads(ksl), cos, sin).transpose(0, 2, 1, 3).astype(bf16)
    kwin = _apply_rope(kv_heads(kwn), cos, sin).transpose(0, 2, 1, 3).astype(bf16)
    vselT = kv_heads(vsl).transpose(0, 2, 3, 1).astype(bf16)
    vwinT = kv_heads(vwn).transpose(0, 2, 3, 1).astype(bf16)
    gl_T = gl.reshape(B, T, NSA_HEADS, 3).transpose(0, 2, 3, 1)

    n_grp = T // CMP_STRIDE
    n_cmp = n_grp - 1
    half = CMP_STRIDE * NSA_HEAD
    grp = jnp.stack([kv_heads(kc), kv_heads(vc)]).transpose(0, 1, 3, 2, 4).reshape(2, B * NSA_KV_HEADS * n_grp, half)
    cmp_kv = []
    for zi in range(2):
        w1ab = jnp.concatenate([cmp_w1[zi, :half], cmp_w1[zi, half:]], axis=1)
        ab = _dense(grp[zi], w1ab).reshape(B, NSA_KV_HEADS, n_grp, 2 * CMP_HIDDEN)
        c = cmp_pos[zi].reshape(1, CMP_LEN * NSA_HEAD) @ cmp_w1[zi]
        hid = jax.nn.gelu(ab[:, :, :-1, :CMP_HIDDEN] + ab[:, :, 1:, CMP_HIDDEN:] + c)
        cmp_kv.append(jnp.pad(hid @ cmp_w2[zi], ((0, 0), (0, 0), (0, 1), (0, 0))))
    k_cmp = cmp_kv[0]
    v_cmpT = cmp_kv[1].transpose(0, 1, 3, 2).astype(bf16)

    n_sel = T // SEL_LEN
    n_top = min(SEL_TOP, n_sel)
    ovT = jnp.asarray(np.pad(_cmp_sel_overlap(n_cmp, n_sel), ((0, 1), (0, 0))).T)
    tq = min(NSA_TQ, T)
    G = NSA_GROUP
    q_spec = pl.BlockSpec((1, G, NSA_HEAD, tq), lambda b, h, i: (b, h, 0, i))
    kv_spec = lambda shape: pl.BlockSpec((1, 1) + shape, lambda b, h, i: (b, h, 0, 0))
    out = pl.pallas_call(
        functools.partial(_nsa_attn_kernel, tq=tq, n_top=n_top),
        grid=(B, NSA_KV_HEADS, T // tq),
        in_specs=[q_spec, q_spec, kv_spec((n_grp, NSA_HEAD)), kv_spec((NSA_HEAD, n_grp)),
                  pl.BlockSpec((n_sel, n_grp), lambda b, h, i: (0, 0)),
                  kv_spec((T, NSA_HEAD)), kv_spec((NSA_HEAD, T)), kv_spec((T, NSA_HEAD)), kv_spec((NSA_HEAD, T)),
                  pl.BlockSpec((1, G, 3, tq), lambda b, h, i: (b, h, 0, i))],
        out_specs=q_spec,
        out_shape=jax.ShapeDtypeStruct((B, NSA_HEADS, NSA_HEAD, T), f32),
        scratch_shapes=[pltpu.VMEM((n_sel, tq), jnp.int32)],
        compiler_params=pltpu.CompilerParams(dimension_semantics=("arbitrary", "arbitrary", "arbitrary"),
                                             vmem_limit_bytes=VMEM_LIMIT_BYTES),
        name="nsa_attn",
    )(qraw_T, qrot_T, k_cmp, v_cmpT, ovT, ksel, vselT, kwin, vwinT, gl_T)
    return out.transpose(0, 3, 1, 2).reshape(B, T, NSA_WIDTH)


HG_STEP = 256


def _hgrn2_kernel(q_ref, f_ref, i_ref, g_ref, lb_ref, ng_ref, o_ref, st_ref):
    f32, bf16 = jnp.float32, jnp.bfloat16
    S, d = q_ref.shape[1], q_ref.shape[2]
    C = HG_CHUNK
    n_sub = S // C
    shift = int(np.log2(C))

    @pl.when(pl.program_id(2) == 0)
    def _():
        st_ref[...] = jnp.zeros_like(st_ref)

    lb = lb_ref[0]
    f = f_ref[0]
    logf = jnp.log(lb + (1.0 - lb) * jax.nn.sigmoid(f))
    k = (1.0 - lb) * jax.nn.sigmoid(-f)
    qs = jax.nn.silu(q_ref[0])
    v = i_ref[0]

    row = lax.broadcasted_iota(jnp.int32, (S, S), 0)
    col = lax.broadcasted_iota(jnp.int32, (S, S), 1)
    same = lax.shift_right_logical(row, shift) == lax.shift_right_logical(col, shift)
    causal = same & (row >= col)
    b = _exact_dot_01(jnp.where(causal, 1.0, 0.0).astype(bf16), logf)
    tail = _exact_dot_01(jnp.where(same & (col > row), 1.0, 0.0).astype(bf16), logf)
    q_e = (qs * jnp.exp(b)).astype(bf16)
    k_e = (k * jnp.exp(-b)).astype(bf16)
    k_tail = k * jnp.exp(tail)
    vb = v.astype(bf16)

    a = lax.dot_general(q_e, k_e, (((1,), (1,)), ((), ())), preferred_element_type=f32)
    o_intra = jnp.dot(jnp.where(causal, a, 0.0).astype(bf16), vb, preferred_element_type=f32)

    vT = v.T.astype(bf16)
    row_chunk = lax.shift_right_logical(lax.broadcasted_iota(jnp.int32, (S, d), 0), shift)
    st = st_ref[...]
    o_inter = []
    for c in range(n_sub):
        sl = slice(c * C, (c + 1) * C)
        o_inter.append(lax.dot_general(q_e[sl], st.astype(bf16), (((1,), (1,)), ((), ())), preferred_element_type=f32))
        k_c = jnp.where(row_chunk == c, k_tail, 0.0).astype(bf16)
        d_c = jnp.exp(b[(c + 1) * C - 1:(c + 1) * C, :])
        st = st * d_c + jnp.dot(vT, k_c, preferred_element_type=f32)
    st_ref[...] = st
    o = o_intra + jnp.concatenate(o_inter, axis=0)
    o = o * lax.rsqrt(jnp.mean(o * o, axis=-1, keepdims=True) + LN_EPS) * ng_ref[0]
    o_ref[0] = o * jax.nn.silu(g_ref[0])


def _hgrn2_mixer(x, w_in, w_out, lb, norm_g):
    B, T, D = x.shape
    z = _dense(x.reshape(B * T, D), w_in).reshape(B, T, -1)
    S = min(HG_STEP, T)
    H, d = HG_HEADS, HG_HEAD
    col = lambda j: pl.BlockSpec((1, S, d), lambda b, h, t: (b, t, j * H + h))
    vec = pl.BlockSpec((1, 1, d), lambda b, h, t: (h, 0, 0))
    o = pl.pallas_call(
        _hgrn2_kernel,
        grid=(B, H, T // S),
        in_specs=[col(0), col(1), col(2), col(3), vec, pl.BlockSpec((1, 1, d), lambda b, h, t: (0, 0, 0))],
        out_specs=pl.BlockSpec((1, S, d), lambda b, h, t: (b, t, h)),
        out_shape=jax.ShapeDtypeStruct((B, T, D), jnp.float32),
        scratch_shapes=[pltpu.VMEM((d, d), jnp.float32)],
        compiler_params=pltpu.CompilerParams(dimension_semantics=("arbitrary", "arbitrary", "arbitrary"),
                                             vmem_limit_bytes=VMEM_LIMIT_BYTES),
        name="hgrn2",
    )(z, z, z, z, lb.reshape(H, 1, d), norm_g.reshape(1, 1, d))
    return _dense(o.reshape(B * T, D), w_out).reshape(B, T, D)


def _moe_expert_kernel(blk_e_ref, x_ref, w1_ref, w3_ref, w2_ref, o_ref):
    del blk_e_ref
    f32, bf16 = jnp.float32, jnp.bfloat16
    x = x_ref[...].astype(bf16)
    a = jnp.dot(x, w1_ref[0], preferred_element_type=f32)
    b = jnp.dot(x, w3_ref[0], preferred_element_type=f32)
    hid = (jax.nn.silu(a) * b).astype(bf16)
    o_ref[...] = jnp.dot(hid, w2_ref[0], preferred_element_type=f32)


def _moe_experts(xbuf, blk_e, w1, w3, w2):
    P, D = xbuf.shape
    hid = w1.shape[-1]
    bf16 = jnp.bfloat16
    w_spec = lambda shape: pl.BlockSpec((1,) + shape, lambda i, be: (be[i], 0, 0))
    x_spec = pl.BlockSpec((MOE_BLOCK, D), lambda i, be: (i, 0))
    return pl.pallas_call(
        _moe_expert_kernel,
        grid_spec=pltpu.PrefetchScalarGridSpec(
            num_scalar_prefetch=1,
            grid=(P // MOE_BLOCK,),
            in_specs=[x_spec, w_spec((D, hid)), w_spec((D, hid)), w_spec((hid, D))],
            out_specs=x_spec,
        ),
        out_shape=jax.ShapeDtypeStruct((P, D), jnp.float32),
        compiler_params=pltpu.CompilerParams(dimension_semantics=("arbitrary",), vmem_limit_bytes=VMEM_LIMIT_BYTES),
        name="moe_experts",
    )(blk_e, xbuf, w1.astype(bf16), w3.astype(bf16), w2.astype(bf16))


def _hier_moe(h, w_rg, b_rg, w_re, b_re, w1, w3, w2):
    B, T, D = h.shape
    M = B * T
    A = M * MOE_TOPK
    xt = h.reshape(M, D)
    lg = xt @ w_rg + b_rg
    g_sel = jnp.argmax(lg, -1)
    p_grp = jnp.max(jax.nn.softmax(lg, -1), -1)
    le = (xt @ w_re + b_re).reshape(M, MOE_GROUPS, MOE_PER_GROUP)
    le = le[jnp.arange(M), g_sel]
    top_v, top_i = lax.top_k(le, MOE_TOPK)
    gate = (p_grp[:, None] * jax.nn.softmax(top_v, -1)).reshape(A)
    eid = (g_sel[:, None] * MOE_PER_GROUP + top_i).reshape(A)
    tok = jnp.repeat(jnp.arange(M), MOE_TOPK)
    order = jnp.argsort(eid)
    e_s, tok_s, gate_s = eid[order], tok[order], gate[order]
    counts = jnp.bincount(eid, length=MOE_EXPERTS)
    padded = (counts + MOE_BLOCK - 1) // MOE_BLOCK * MOE_BLOCK
    ends = jnp.cumsum(padded)
    start = ends - padded
    off = jnp.cumsum(counts) - counts
    dest = start[e_s] + jnp.arange(A) - off[e_s]
    P = A + MOE_EXPERTS * MOE_BLOCK
    n_blk = P // MOE_BLOCK
    xbuf = jnp.zeros((P, D), h.dtype).at[dest].set(xt[tok_s])
    blk_e = jnp.minimum(jnp.searchsorted(ends, jnp.arange(n_blk) * MOE_BLOCK, side='right'), MOE_EXPERTS - 1)

    ybuf = _moe_experts(xbuf, blk_e.astype(jnp.int32), w1, w3, w2)
    y = jnp.zeros((M, D), jnp.float32).at[tok_s].add(gate_s[:, None] * ybuf[dest])
    return y.reshape(B, T, D)


def kernel(x, p, positions, ev_w_in, ev_w_out, rw_mu, rw_w0, rw_w2, rw_a0, rw_a2, rw_g2, rw_k_k, rw_k_a,
           rw_r_k, rw_gn_g, rw_gn_b, nsa_cmp_pos, nsa_cmp_w1, nsa_cmp_w2, od_w_in, od_w_out, hg_lb, hg_norm_g,
           moe_w_rg, moe_b_rg, moe_w_re, moe_b_re, moe_w1, moe_w3, moe_w2, ln_g, ln_b, ple_w, ple_gate_w):
    B, T, D = x.shape
    M = B * T
    cos, sin = _rope_tables(positions, NSA_HEAD)
    lb_soft = jax.nn.softmax(hg_lb, axis=0)
    lb_all = jnp.cumsum(lb_soft, axis=0) - lb_soft[0:1]
    for li in range(DEPTH):
        j = li // 2
        if li % 2 == 0:
            z = _dense(x.reshape(M, D), ev_w_in[j]).reshape(B, T, -1)
            z_rw, z_nsa = jnp.split(z, [RW_COLS], axis=-1)
            y_rw = _rwkv7_group(z_rw, rw_mu[j], rw_w0[j], rw_w2[j], rw_a0[j], rw_a2[j], rw_g2[j], rw_k_k[j],
                                rw_k_a[j], rw_r_k[j], rw_gn_g[j], rw_gn_b[j])
            y_nsa = _nsa_group(z_nsa, cos, sin, nsa_cmp_pos[j], nsa_cmp_w1[j], nsa_cmp_w2[j])
            mix = _dense(jnp.concatenate([y_rw, y_nsa], -1).reshape(M, D), ev_w_out[j]).reshape(B, T, D)
        else:
            mix = _hgrn2_mixer(x, od_w_in[j], od_w_out[j], lb_all[li], hg_norm_g[j])
        h = _layer_norm(DN_ALPHA * x + mix, ln_g[li, 0], ln_b[li, 0])
        ffn = _hier_moe(h, moe_w_rg[li], moe_b_rg[li], moe_w_re[li], moe_b_re[li], moe_w1[li], moe_w3[li], moe_w2[li])
        h = _layer_norm(DN_ALPHA * h + ffn, ln_g[li, 1], ln_b[li, 1])
        gate = _dense(h.reshape(M, D), ple_gate_w[li]).reshape(B, T, D)
        pp = _dense(p[li].reshape(M, PLE_DIM), ple_w[li]).reshape(B, T, D)
        x = h + jax.nn.sigmoid(gate) * pp
    return x
```

```python
import functools

import numpy as np
import jax
import jax.numpy as jnp
from jax import lax
from jax.experimental import pallas as pl
from jax.experimental.pallas import tpu as pltpu

D_MODEL = 1024
DEPTH = 2
PLE_DIM = 256
DN_ALPHA = (2 * DEPTH) ** 0.25
LN_EPS = 1e-5
ROPE_THETA = 10000.0

RW_WIDTH = D_MODEL // 2
RW_HEAD = 64
RW_HEADS = RW_WIDTH // RW_HEAD
RW_DECAY_LORA = 64
RW_AAA_LORA = 64
RW_GATE_LORA = 128
RW_GN_EPS = RW_HEAD * 1e-5
RW_SPLITS = (RW_WIDTH, RW_WIDTH, RW_WIDTH, RW_DECAY_LORA, RW_AAA_LORA, RW_GATE_LORA)
RW_COLS = sum(RW_SPLITS)

NSA_WIDTH = D_MODEL - RW_WIDTH
NSA_HEAD = 64
NSA_HEADS = NSA_WIDTH // NSA_HEAD
NSA_KV_HEADS = 2
NSA_GROUP = NSA_HEADS // NSA_KV_HEADS
NSA_KV = NSA_KV_HEADS * NSA_HEAD
CMP_LEN = 32
CMP_STRIDE = 16
CMP_HIDDEN = 128
SEL_LEN = 64
SEL_TOP = 16
WINDOW = 512
NSA_QBLOCK = 32
NSA_SPLITS = (NSA_WIDTH,) + (NSA_KV,) * 6 + (3 * NSA_HEADS,)
NSA_COLS = sum(NSA_SPLITS)
EV_COLS = RW_COLS + NSA_COLS

HG_HEAD = 128
HG_HEADS = D_MODEL // HG_HEAD
HG_CHUNK = 16
HG_SPLITS = (D_MODEL, D_MODEL, D_MODEL, D_MODEL)

MOE_GROUPS = 4
MOE_PER_GROUP = 8
MOE_EXPERTS = MOE_GROUPS * MOE_PER_GROUP
MOE_TOPK = 2
MOE_HIDDEN = 512
MOE_BLOCK = 256

LANES = 128
VMEM_LIMIT_BYTES = 56 * 1024 * 1024


def _round_up(n, m):
    return (n + m - 1) // m * m


def _dense_kernel(x_ref, w_ref, o_ref):
    o_ref[...] = jnp.dot(x_ref[...].astype(jnp.bfloat16), w_ref[...], preferred_element_type=jnp.float32)


def _dense(x2d, w, tm=512):
    m, k = x2d.shape
    n = w.shape[1]
    n_pad = _round_up(n, LANES)
    wb = w.astype(jnp.bfloat16)
    if n_pad != n:
        wb = jnp.pad(wb, ((0, 0), (0, n_pad - n)))
    tm = min(tm, m)
    assert m % tm == 0
    out = pl.pallas_call(
        _dense_kernel,
        grid=(m // tm,),
        in_specs=[pl.BlockSpec((tm, k), lambda i: (i, 0)), pl.BlockSpec((k, n_pad), lambda i: (0, 0))],
        out_specs=pl.BlockSpec((tm, n_pad), lambda i: (i, 0)),
        out_shape=jax.ShapeDtypeStruct((m, n_pad), jnp.float32),
        compiler_params=pltpu.CompilerParams(dimension_semantics=("arbitrary",), vmem_limit_bytes=VMEM_LIMIT_BYTES),
        name="dense",
    )(x2d, wb)
    return out[:, :n] if n_pad != n else out


def _split(z, sizes):
    cuts = [int(c) for c in np.cumsum(sizes)[:-1]]
    return jnp.split(z, cuts, axis=-1)


def _layer_norm(x, g, b):
    xc = x - jnp.mean(x, -1, keepdims=True)
    var = jnp.mean(xc * xc, -1, keepdims=True)
    return xc * lax.rsqrt(var + LN_EPS) * g + b


def _rope_tables(positions, dim):
    inv = (1.0 / (ROPE_THETA ** (np.arange(0, dim, 2, dtype=np.float32) / dim))).astype(np.float32)
    ang = positions.astype(jnp.float32)[..., None] * inv
    return jnp.cos(ang)[:, :, None, :], jnp.sin(ang)[:, :, None, :]


def _apply_rope(x, cos, sin):
    half = x.shape[-1] // 2
    x1, x2 = x[..., :half], x[..., half:]
    return jnp.concatenate([x1 * cos - x2 * sin, x2 * cos + x1 * sin], -1)


def _masked_softmax(s, mask):
    s = jnp.where(mask, s, -jnp.inf)
    m = jnp.max(s, axis=-1, keepdims=True)
    m = jnp.where(jnp.isfinite(m), m, 0.0)
    e = jnp.exp(s - m)
    return e / jnp.maximum(jnp.sum(e, -1, keepdims=True), 1e-30)


RW_CHUNK = 64
RW_STEP = 128


def _bdot(a, b):
    return jnp.dot(a.astype(jnp.bfloat16), b.astype(jnp.bfloat16), preferred_element_type=jnp.float32)


def _exact_dot_01(tri, x):
    bf16, f32 = jnp.bfloat16, jnp.float32
    h1 = x.astype(bf16)
    r1 = x - h1.astype(f32)
    h2 = r1.astype(bf16)
    h3 = (r1 - h2.astype(f32)).astype(bf16)
    d = lambda h: jnp.dot(tri, h, preferred_element_type=f32)
    return d(h1) + d(h2) + d(h3)


def _rwkv7_chunk_kernel(r_ref, lw_ref, k_ref, v_ref, kk_ref, b_ref, lwT_ref, kT_ref, bT_ref, o_ref, h_ref):
    f32, bf16 = jnp.float32, jnp.bfloat16
    n_heads, S, N = r_ref.shape[1], r_ref.shape[2], r_ref.shape[3]
    C = RW_CHUNK
    n_sub = S // C

    @pl.when(pl.program_id(1) == 0)
    def _():
        h_ref[...] = jnp.zeros_like(h_ref)

    row = lax.broadcasted_iota(jnp.int32, (S, S), 0)
    col = lax.broadcasted_iota(jnp.int32, (S, S), 1)
    same = lax.shift_right_logical(row, int(np.log2(C))) == lax.shift_right_logical(col, int(np.log2(C)))
    incl = same & (row >= col)
    strict = same & (row > col)
    tri = jnp.where(incl, 1.0, 0.0).astype(bf16)
    triT = jnp.where(same & (col >= row), 1.0, 0.0).astype(bf16)
    eye_s = jnp.where(row == col, 1.0, 0.0)
    rn = lax.broadcasted_iota(jnp.int32, (N, N), 0)
    cn = lax.broadcasted_iota(jnp.int32, (N, N), 1)
    eye_n = jnp.where(rn == cn, 1.0, 0.0)
    lane_chunk = lax.shift_right_logical(lax.broadcasted_iota(jnp.int32, (N, S), 1), int(np.log2(C)))

    for h in range(n_heads):
        lw = lw_ref[0, h]
        cs = _exact_dot_01(tri, lw)
        lwT = lwT_ref[0, h]
        h1 = lwT.astype(bf16)
        r1 = lwT - h1.astype(f32)
        h2 = r1.astype(bf16)
        h3 = (r1 - h2.astype(f32)).astype(bf16)
        dT = lambda x: jnp.dot(x, triT, preferred_element_type=f32)
        csT = dT(h1) + dT(h2) + dT(h3)
        kks = kk_ref[0, h] * jnp.exp(cs - lw)
        rs = r_ref[0, h] * jnp.exp(cs)
        e_negT = jnp.exp(-csT)
        bsT = bT_ref[0, h] * e_negT
        ksT = kT_ref[0, h] * e_negT
        bsT_b, ksT_b = bsT.astype(bf16), ksT.astype(bf16)
        v = v_ref[0, h].astype(bf16)

        lhs = jnp.concatenate([kks, rs], axis=0).astype(bf16)
        mb = jnp.dot(lhs, bsT_b, preferred_element_type=f32)
        mk = jnp.dot(lhs, ksT_b, preferred_element_type=f32)
        a_b = jnp.where(strict, mb[:S], 0.0)
        a_k = jnp.where(strict, mk[:S], 0.0)
        r_b = jnp.where(incl, mb[S:], 0.0)
        r_k = jnp.where(incl, mk[S:], 0.0)

        t_inv = eye_s - a_b
        pw = a_b
        for _ in range(int(np.log2(C)) - 1):
            pw = _bdot(pw, pw)
            t_inv = t_inv + _bdot(t_inv, pw)
        t_inv = t_inv.astype(bf16)

        w_mat = jnp.dot(t_inv, kks.astype(bf16), preferred_element_type=f32)
        u_loc = -jnp.dot(t_inv, _bdot(a_k, v).astype(bf16), preferred_element_type=f32)
        q_eff = rs - _bdot(r_b, w_mat)
        o_loc = _bdot(r_b, u_loc) + _bdot(r_k, v)
        w_b, u_b = w_mat.astype(bf16), u_loc.astype(bf16)

        hc = h_ref[h]
        for c in range(n_sub):
            in_c = lane_chunk == c
            bs_c = jnp.where(in_c, bsT, 0.0).astype(bf16)
            ks_c = jnp.where(in_c, ksT, 0.0).astype(bf16)
            g_end = jnp.exp(csT[:, (c + 1) * C - 1:(c + 1) * C])
            g_mat = g_end * (eye_n - jnp.dot(bs_c, w_b, preferred_element_type=f32))
            h_loc = g_end * (jnp.dot(bs_c, u_b, preferred_element_type=f32) + jnp.dot(ks_c, v, preferred_element_type=f32))
            sl = slice(c * C, (c + 1) * C)
            o_ref[0, h, sl, :] = _bdot(q_eff[sl], hc) + o_loc[sl]
            hc = _bdot(g_mat, hc) + h_loc
        h_ref[h] = hc


def _rwkv7_scan(r, lw, k, v, kk, b):
    B, T, H, N = r.shape
    C = min(RW_STEP, T)
    assert C % RW_CHUNK == 0 and T % C == 0
    hm = lambda t: t.transpose(0, 2, 1, 3)
    hmT = lambda t: t.transpose(0, 2, 3, 1)
    spec = pl.BlockSpec((1, H, C, N), lambda bi, ci: (bi, 0, ci, 0))
    specT = pl.BlockSpec((1, H, N, C), lambda bi, ci: (bi, 0, 0, ci))
    o = pl.pallas_call(
        _rwkv7_chunk_kernel,
        grid=(B, T // C),
        in_specs=[spec] * 6 + [specT] * 3,
        out_specs=spec,
        out_shape=jax.ShapeDtypeStruct((B, H, T, N), jnp.float32),
        scratch_shapes=[pltpu.VMEM((H, N, N), jnp.float32)],
        compiler_params=pltpu.CompilerParams(dimension_semantics=("arbitrary", "arbitrary"),
                                             vmem_limit_bytes=VMEM_LIMIT_BYTES),
        name="rwkv7_chunk",
    )(hm(r), hm(lw), hm(k), hm(v), hm(kk), hm(b), hmT(lw), hmT(k), hmT(b))
    return o.transpose(0, 2, 1, 3)


def _rwkv7_group(z, mu, w0, w2, a0, a2, g2, k_k, k_a, r_k, gn_g, gn_b):
    B, T, _ = z.shape
    z_prev = jnp.pad(z, ((0, 0), (1, 0), (0, 0)))[:, :-1]
    z = z + mu * (z_prev - z)
    r, k, v, wd, ad, gd = _split(z, RW_SPLITS)
    w = -jax.nn.softplus(-(w0 + jnp.tanh(wd) @ w2)) - 0.5
    log_decay = -jnp.exp(w)
    a = jax.nn.sigmoid(a0 + ad @ a2)
    g = jax.nn.sigmoid(gd) @ g2
    heads = lambda t: t.reshape(B, T, RW_HEADS, RW_HEAD)
    kk = heads(k * k_k)
    kk = kk / jnp.maximum(jnp.sqrt(jnp.sum(kk * kk, -1, keepdims=True)), 1e-12)
    k = k * (1.0 + (a - 1.0) * k_a)
    r, k, v, a = heads(r), heads(k), heads(v), heads(a)
    o = _rwkv7_scan(r, heads(log_decay), k, v, kk, a * kk)
    oc = o - jnp.mean(o, -1, keepdims=True)
    o = oc * lax.rsqrt(jnp.mean(oc * oc, -1, keepdims=True) + RW_GN_EPS)
    o = o.reshape(B, T, RW_WIDTH) * gn_g + gn_b
    bonus = (jnp.sum(r * k * r_k, -1, keepdims=True) * v).reshape(B, T, RW_WIDTH)
    return (o + bonus) * g


def _cmp_sel_overlap(n_cmp, n_sel):
    cs = np.arange(n_cmp)[:, None] * CMP_STRIDE
    ss = np.arange(n_sel)[None, :] * SEL_LEN
    ov = np.clip(np.minimum(cs + CMP_LEN, ss + SEL_LEN) - np.maximum(cs, ss), 0, None)
    return (ov / CMP_LEN).astype(np.float32)


NSA_TQ = 128
NSA_TK_SEL = 512
NSA_TK_WIN = 256
NEG_INIT = -1e30


def _nsa_attn_kernel(qraw_ref, qrot_ref, kcmp_ref, vcmpT_ref, ovT_ref, ksel_ref, vselT_ref, kwin_ref, vwinT_ref,
                     gl_ref, o_ref, lim_ref, *, tq, n_top):
    f32, bf16 = jnp.float32, jnp.bfloat16
    G = NSA_GROUP
    R = G * tq
    t0 = pl.program_id(2) * tq
    ncp = kcmp_ref.shape[2]
    n_sel = ovT_ref.shape[0]

    t_row = t0 + (lax.broadcasted_iota(jnp.int32, (1, R), 1) & (tq - 1))
    qraw = jnp.concatenate([qraw_ref[0, g] for g in range(G)], axis=1)
    qrot = jnp.concatenate([qrot_ref[0, g] for g in range(G)], axis=1)

    s_c = jnp.dot(kcmp_ref[0, 0], qraw, precision=lax.Precision.HIGHEST, preferred_element_type=f32)
    cmp_last = lax.broadcasted_iota(jnp.int32, (ncp, R), 0) * CMP_STRIDE + (CMP_LEN - 1)
    s_c = jnp.where(cmp_last <= t_row, s_c, -jnp.inf)
    m_c = jnp.max(s_c, axis=0, keepdims=True)
    m_c = jnp.where(m_c == -jnp.inf, 0.0, m_c)
    e_c = jnp.exp(s_c - m_c)
    p_c = e_c / jnp.maximum(jnp.sum(e_c, axis=0, keepdims=True), 1e-30)
    o_c = jnp.dot(vcmpT_ref[0, 0], p_c.astype(bf16), preferred_element_type=f32)

    p_sum = p_c[:, 0:tq]
    for g in range(1, G):
        p_sum = p_sum + p_c[:, g * tq:(g + 1) * tq]
    imp = jnp.dot(ovT_ref[...], p_sum, precision=lax.Precision.HIGHEST, preferred_element_type=f32)
    j_iota = lax.broadcasted_iota(jnp.int32, (n_sel, tq), 0)
    t_tok = t0 + lax.broadcasted_iota(jnp.int32, (1, tq), 1)
    cur = lax.shift_right_logical(t_tok, int(np.log2(SEL_LEN)))
    forced = (j_iota == 0) | (j_iota == cur) | (j_iota == cur - 1)
    cand = (j_iota >= 1) & (j_iota <= cur - 2)
    quota = n_top - 1 - jnp.minimum(cur, 2)
    key = jnp.where(cand, pltpu.bitcast(imp, jnp.int32), -1)
    rank = jnp.zeros((n_sel, tq), jnp.int32)
    for i in range(n_sel):
        row = key[i:i + 1, :]
        rank = rank + jnp.where(row + jnp.where(j_iota > i, 1, 0) > key, 1, 0)
    sel = forced | (cand & (rank < quota))
    lim_ref[...] = jnp.where(sel, t_tok, -1)

    def online_update(carry, s, vT):
        m, l, acc = carry
        m_new = jnp.maximum(m, jnp.max(s, axis=0, keepdims=True))
        alpha = jnp.exp(m - m_new)
        p = jnp.exp(s - m_new)
        l = alpha * l + jnp.sum(p, axis=0, keepdims=True)
        acc = alpha * acc + jnp.dot(vT, p.astype(bf16), preferred_element_type=f32)
        return m_new, l, acc

    init = (jnp.full((1, R), NEG_INIT, f32), jnp.zeros((1, R), f32), jnp.zeros((NSA_HEAD, R), f32))

    tk = min(NSA_TK_SEL, ksel_ref.shape[2])
    nb = tk // SEL_LEN
    key_iota = lax.broadcasted_iota(jnp.int32, (tk, R), 0)

    def sel_body(kt, carry):
        k0 = pl.multiple_of(kt * tk, tk)
        s = jnp.dot(ksel_ref[0, 0, pl.ds(k0, tk), :], qrot, preferred_element_type=f32)
        limb = lim_ref[pl.ds(pl.multiple_of(kt * nb, nb), nb), :] - k0
        lim_t = jnp.concatenate([jnp.broadcast_to(limb[jb:jb + 1, :], (SEL_LEN, tq)) for jb in range(nb)], axis=0)
        lim_t = jnp.concatenate([lim_t] * G, axis=1)
        s = jnp.where(key_iota <= lim_t, s, -jnp.inf)
        return online_update(carry, s, vselT_ref[0, 0, :, pl.ds(k0, tk)])

    n_kt = lax.div(t0 + tq - 1, tk) + 1
    _, l_s, acc_s = lax.fori_loop(0, n_kt, sel_body, init)
    o_s = acc_s / jnp.maximum(l_s, 1e-30)

    tkw = min(NSA_TK_WIN, kwin_ref.shape[2])
    key_iota_w = lax.broadcasted_iota(jnp.int32, (tkw, R), 0)

    def win_body(kt, carry):
        k0 = pl.multiple_of(kt * tkw, tkw)
        s = jnp.dot(kwin_ref[0, 0, pl.ds(k0, tkw), :], qrot, preferred_element_type=f32)
        rel = t_row - k0
        s = jnp.where((key_iota_w <= rel) & (key_iota_w > rel - WINDOW), s, -jnp.inf)
        return online_update(carry, s, vwinT_ref[0, 0, :, pl.ds(k0, tkw)])

    kt_lo = lax.div(jnp.maximum(t0 - (WINDOW - 1), 0), tkw)
    kt_hi = lax.div(t0 + tq - 1, tkw) + 1
    _, l_w, acc_w = lax.fori_loop(kt_lo, kt_hi, win_body, init)
    o_w = acc_w / jnp.maximum(l_w, 1e-30)

    gl = jnp.concatenate([gl_ref[0, g] for g in range(G)], axis=1)
    gates = jax.nn.sigmoid(gl)
    out = gates[0:1, :] * o_c + gates[1:2, :] * o_s + gates[2:3, :] * o_w
    for g in range(G):
        o_ref[0, g] = out[:, g * tq:(g + 1) * tq]


def _nsa_group(z, cos, sin, cmp_pos, cmp_w1, cmp_w2):
    B, T, _ = z.shape
    f32, bf16 = jnp.float32, jnp.bfloat16
    q, kc, vc, ksl, vsl, kwn, vwn, gl = _split(z, NSA_SPLITS)
    kv_heads = lambda t: t.reshape(B, T, NSA_KV_HEADS, NSA_HEAD)
    scale = NSA_HEAD ** -0.5
    q = q.reshape(B, T, NSA_HEADS, NSA_HEAD)
    qraw_T = (q * scale).transpose(0, 2, 3, 1)
    qrot_T = (_apply_rope(q, cos, sin) * scale).transpose(0, 2, 3, 1).astype(bf16)
    ksel = _apply_rope(kv_heads(ksl), cos, sin).transpose(0, 2, 1, 3).astype(bf16)
    kwin = _apply_rope(kv_heads(kwn), cos, sin).transpose(0, 2, 1, 3).astype(bf16)
    vselT = kv_heads(vsl).transpose(0, 2, 3, 1).astype(bf16)
    vwinT = kv_heads(vwn).transpose(0, 2, 3, 1).astype(bf16)
    gl_T = gl.reshape(B, T, NSA_HEADS, 3).transpose(0, 2, 3, 1)

    n_grp = T // CMP_STRIDE
    n_cmp = n_grp - 1
    half = CMP_STRIDE * NSA_HEAD
    grp = jnp.stack([kv_heads(kc), kv_heads(vc)]).transpose(0, 1, 3, 2, 4).reshape(2, B * NSA_KV_HEADS * n_grp, half)
    cmp_kv = []
    for zi in range(2):
        w1ab = jnp.concatenate([cmp_w1[zi, :half], cmp_w1[zi, half:]], axis=1)
        ab = _dense(grp[zi], w1ab).reshape(B, NSA_KV_HEADS, n_grp, 2 * CMP_HIDDEN)
        c = cmp_pos[zi].reshape(1, CMP_LEN * NSA_HEAD) @ cmp_w1[zi]
        hid = jax.nn.gelu(ab[:, :, :-1, :CMP_HIDDEN] + ab[:, :, 1:, CMP_HIDDEN:] + c)
        cmp_kv.append(jnp.pad(hid @ cmp_w2[zi], ((0, 0), (0, 0), (0, 1), (0, 0))))
    k_cmp = cmp_kv[0]
    v_cmpT = cmp_kv[1].transpose(0, 1, 3, 2).astype(bf16)

    n_sel = T // SEL_LEN
    n_top = min(SEL_TOP, n_sel)
    ovT = jnp.asarray(np.pad(_cmp_sel_overlap(n_cmp, n_sel), ((0, 1), (0, 0))).T)
    tq = min(NSA_TQ, T)
    G = NSA_GROUP
    q_spec = pl.BlockSpec((1, G, NSA_HEAD, tq), lambda b, h, i: (b, h, 0, i))
    kv_spec = lambda shape: pl.BlockSpec((1, 1) + shape, lambda b, h, i: (b, h, 0, 0))
    out = pl.pallas_call(
        functools.partial(_nsa_attn_kernel, tq=tq, n_top=n_top),
        grid=(B, NSA_KV_HEADS, T // tq),
        in_specs=[q_spec, q_spec, kv_spec((n_grp, NSA_HEAD)), kv_spec((NSA_HEAD, n_grp)),
                  pl.BlockSpec((n_sel, n_grp), lambda b, h, i: (0, 0)),
                  kv_spec((T, NSA_HEAD)), kv_spec((NSA_HEAD, T)), kv_spec((T, NSA_HEAD)), kv_spec((NSA_HEAD, T)),
                  pl.BlockSpec((1, G, 3, tq), lambda b, h, i: (b, h, 0, i))],
        out_specs=q_spec,
        out_shape=jax.ShapeDtypeStruct((B, NSA_HEADS, NSA_HEAD, T), f32),
        scratch_shapes=[pltpu.VMEM((n_sel, tq), jnp.int32)],
        compiler_params=pltpu.CompilerParams(dimension_semantics=("arbitrary", "arbitrary", "arbitrary"),
                                             vmem_limit_bytes=VMEM_LIMIT_BYTES),
        name="nsa_attn",
    )(qraw_T, qrot_T, k_cmp, v_cmpT, ovT, ksel, vselT, kwin, vwinT, gl_T)
    return out.transpose(0, 3, 1, 2).reshape(B, T, NSA_WIDTH)


HG_STEP = 256


def _hgrn2_kernel(q_ref, f_ref, i_ref, g_ref, lb_ref, ng_ref, o_ref, st_ref):
    f32, bf16 = jnp.float32, jnp.bfloat16
    S, d = q_ref.shape[1], q_ref.shape[2]
    C = HG_CHUNK
    n_sub = S // C
    shift = int(np.log2(C))

    @pl.when(pl.program_id(2) == 0)
    def _():
        st_ref[...] = jnp.zeros_like(st_ref)

    lb = lb_ref[0]
    f = f_ref[0]
    logf = jnp.log(lb + (1.0 - lb) * jax.nn.sigmoid(f))
    k = (1.0 - lb) * jax.nn.sigmoid(-f)
    qs = jax.nn.silu(q_ref[0])
    v = i_ref[0]

    row = lax.broadcasted_iota(jnp.int32, (S, S), 0)
    col = lax.broadcasted_iota(jnp.int32, (S, S), 1)
    same = lax.shift_right_logical(row, shift) == lax.shift_right_logical(col, shift)
    causal = same & (row >= col)
    b = _exact_dot_01(jnp.where(causal, 1.0, 0.0).astype(bf16), logf)
    tail = _exact_dot_01(jnp.where(same & (col > row), 1.0, 0.0).astype(bf16), logf)
    q_e = (qs * jnp.exp(b)).astype(bf16)
    k_e = (k * jnp.exp(-b)).astype(bf16)
    k_tail = k * jnp.exp(tail)
    vb = v.astype(bf16)

    a = lax.dot_general(q_e, k_e, (((1,), (1,)), ((), ())), preferred_element_type=f32)
    o_intra = jnp.dot(jnp.where(causal, a, 0.0).astype(bf16), vb, preferred_element_type=f32)

    vT = v.T.astype(bf16)
    row_chunk = lax.shift_right_logical(lax.broadcasted_iota(jnp.int32, (S, d), 0), shift)
    st = st_ref[...]
    o_inter = []
    for c in range(n_sub):
        sl = slice(c * C, (c + 1) * C)
        o_inter.append(lax.dot_general(q_e[sl], st.astype(bf16), (((1,), (1,)), ((), ())), preferred_element_type=f32))
        k_c = jnp.where(row_chunk == c, k_tail, 0.0).astype(bf16)
        d_c = jnp.exp(b[(c + 1) * C - 1:(c + 1) * C, :])
        st = st * d_c + jnp.dot(vT, k_c, preferred_element_type=f32)
    st_ref[...] = st
    o = o_intra + jnp.concatenate(o_inter, axis=0)
    o = o * lax.rsqrt(jnp.mean(o * o, axis=-1, keepdims=True) + LN_EPS) * ng_ref[0]
    o_ref[0] = o * jax.nn.silu(g_ref[0])


def _hgrn2_mixer(x, w_in, lb, norm_g):
    B, T, D = x.shape
    z = _dense(x.reshape(B * T, D), w_in).reshape(B, T, -1)
    S = min(HG_STEP, T)
    H, d = HG_HEADS, HG_HEAD
    col = lambda j: pl.BlockSpec((1, S, d), lambda b, h, t: (b, t, j * H + h))
    vec = pl.BlockSpec((1, 1, d), lambda b, h, t: (h, 0, 0))
    o = pl.pallas_call(
        _hgrn2_kernel,
        grid=(B, H, T // S),
        in_specs=[col(0), col(1), col(2), col(3), vec, pl.BlockSpec((1, 1, d), lambda b, h, t: (0, 0, 0))],
        out_specs=pl.BlockSpec((1, S, d), lambda b, h, t: (b, t, h)),
        out_shape=jax.ShapeDtypeStruct((B, T, D), jnp.float32),
        scratch_shapes=[pltpu.VMEM((d, d), jnp.float32)],
        compiler_params=pltpu.CompilerParams(dimension_semantics=("arbitrary", "arbitrary", "arbitrary"),
                                             vmem_limit_bytes=VMEM_LIMIT_BYTES),
        name="hgrn2",
    )(z, z, z, z, lb.reshape(H, 1, d), norm_g.reshape(1, 1, d))
    return o.reshape(B * T, D)


def _moe_expert_kernel(blk_e_ref, x_ref, w1_ref, w3_ref, w2_ref, o_ref):
    del blk_e_ref
    f32, bf16 = jnp.float32, jnp.bfloat16
    x = x_ref[...].astype(bf16)
    a = jnp.dot(x, w1_ref[0], preferred_element_type=f32)
    b = jnp.dot(x, w3_ref[0], preferred_element_type=f32)
    hid = (jax.nn.silu(a) * b).astype(bf16)
    o_ref[...] = jnp.dot(hid, w2_ref[0], preferred_element_type=f32)


def _moe_experts(xbuf, blk_e, w1, w3, w2):
    P, D = xbuf.shape
    hid = w1.shape[-1]
    bf16 = jnp.bfloat16
    w_spec = lambda shape: pl.BlockSpec((1,) + shape, lambda i, be: (be[i], 0, 0))
    x_spec = pl.BlockSpec((MOE_BLOCK, D), lambda i, be: (i, 0))
    return pl.pallas_call(
        _moe_expert_kernel,
        grid_spec=pltpu.PrefetchScalarGridSpec(
            num_scalar_prefetch=1,
            grid=(P // MOE_BLOCK,),
            in_specs=[x_spec, w_spec((D, hid)), w_spec((D, hid)), w_spec((hid, D))],
            out_specs=x_spec,
        ),
        out_shape=jax.ShapeDtypeStruct((P, D), jnp.float32),
        compiler_params=pltpu.CompilerParams(dimension_semantics=("arbitrary",), vmem_limit_bytes=VMEM_LIMIT_BYTES),
        name="moe_experts",
    )(blk_e, xbuf, w1.astype(bf16), w3.astype(bf16), w2.astype(bf16))


MOE_ROUTE_TM = 512
MOE_ROW_TM = 256
MOE_ROUTE_ROWS = 40


def _moe_route_kernel(h_ref, wr_ref, br_ref, eid_ref, gate_ref, pos_ref, cnt_ref, run_ref):
    f32 = jnp.float32
    tm = h_ref.shape[0]
    E, PG, NG = MOE_EXPERTS, MOE_PER_GROUP, MOE_GROUPS

    @pl.when(pl.program_id(0) == 0)
    def _():
        run_ref[...] = jnp.zeros_like(run_ref)

    lg = lax.dot_general(wr_ref[...], h_ref[...], (((1,), (1,)), ((), ())), precision=lax.Precision.HIGHEST,
                         preferred_element_type=f32) + br_ref[...]
    grp = lg[E:E + NG]
    g_iota = lax.broadcasted_iota(jnp.int32, (NG, tm), 0)
    g_max = jnp.max(grp, axis=0, keepdims=True)
    g_sel = jnp.min(jnp.where(grp == g_max, g_iota, NG), axis=0, keepdims=True)
    p_grp = 1.0 / jnp.sum(jnp.exp(grp - g_max), axis=0, keepdims=True)
    le = lg[0:PG]
    for g in range(1, NG):
        le = jnp.where(g_sel == g, lg[g * PG:(g + 1) * PG], le)
    e_iota = lax.broadcasted_iota(jnp.int32, (PG, tm), 0)
    m1 = jnp.max(le, axis=0, keepdims=True)
    i1 = jnp.min(jnp.where(le == m1, e_iota, PG), axis=0, keepdims=True)
    le2 = jnp.where(e_iota == i1, -jnp.inf, le)
    m2 = jnp.max(le2, axis=0, keepdims=True)
    i2 = jnp.min(jnp.where(le2 == m2, e_iota, PG), axis=0, keepdims=True)
    e2 = jnp.exp(m2 - m1)
    den = 1.0 + e2
    eid1 = g_sel * PG + i1
    eid2 = g_sel * PG + i2
    eid_ref[...] = jnp.concatenate([eid1, eid2], axis=0)
    gate_ref[...] = jnp.concatenate([p_grp / den, p_grp * e2 / den], axis=0)

    x_iota = lax.broadcasted_iota(jnp.int32, (E, tm), 0)
    oh1 = jnp.where(x_iota == eid1, 1.0, 0.0)
    oh2 = jnp.where(x_iota == eid2, 1.0, 0.0)
    before = jnp.where(lax.broadcasted_iota(jnp.int32, (tm, tm), 0) < lax.broadcasted_iota(jnp.int32, (tm, tm), 1),
                       1.0, 0.0).astype(jnp.bfloat16)
    cum1 = jnp.dot(oh1.astype(jnp.bfloat16), before, preferred_element_type=f32)
    cum2 = jnp.dot(oh2.astype(jnp.bfloat16), before, preferred_element_type=f32)
    tot1 = jnp.sum(oh1, axis=1, keepdims=True)
    tot2 = jnp.sum(oh2, axis=1, keepdims=True)
    base = run_ref[:, 0:1]
    pos1 = jnp.sum(oh1 * (cum1 + base), axis=0, keepdims=True)
    pos2 = jnp.sum(oh2 * (cum2 + base + tot1), axis=0, keepdims=True)
    pos_ref[...] = jnp.concatenate([pos1, pos2], axis=0).astype(jnp.int32)
    new = jnp.broadcast_to(base + tot1 + tot2, run_ref.shape)
    run_ref[...] = new
    cnt_ref[...] = new


def _moe_scatter_kernel(dest_ref, h_ref, xin_ref, xbuf_ref, sem):
    del xin_ref
    tm = h_ref.shape[0]

    def row_copy(r, j):
        return pltpu.make_async_copy(h_ref.at[pl.ds(r, 1)], xbuf_ref.at[pl.ds(dest_ref[j, r], 1)], sem)

    def issue(r, c):
        row_copy(r, 0).start()
        row_copy(r, 1).start()
        return c

    lax.fori_loop(0, tm, issue, 0, unroll=8)

    def drain(r, c):
        row_copy(r, 0).wait()
        row_copy(r, 1).wait()
        return c

    lax.fori_loop(0, tm, drain, 0, unroll=8)


def _moe_combine_ln_kernel(dest_ref, gate_ref, h_ref, lng_ref, lnb_ref, ybuf_ref, o_ref, buf_ref, sem):
    tm = h_ref.shape[0]

    def row_copy(r, j):
        return pltpu.make_async_copy(ybuf_ref.at[pl.ds(dest_ref[j, r], 1)], buf_ref.at[j, pl.ds(r, 1)], sem)

    def issue(r, c):
        row_copy(r, 0).start()
        row_copy(r, 1).start()
        return c

    lax.fori_loop(0, tm, issue, 0, unroll=8)

    def drain(r, c):
        row_copy(r, 0).wait()
        row_copy(r, 1).wait()
        return c

    lax.fori_loop(0, tm, drain, 0, unroll=8)
    gate = gate_ref[...]
    ffn = gate[:, 0:1] * buf_ref[0] + gate[:, 1:2] * buf_ref[1]
    o_ref[...] = _layer_norm(DN_ALPHA * h_ref[...] + ffn, lng_ref[...], lnb_ref[...])


def _hier_moe_ln(h, w_rg, b_rg, w_re, b_re, w1, w3, w2, ln_g, ln_b):
    M, D = h.shape
    f32, i32 = jnp.float32, jnp.int32
    E = MOE_EXPERTS
    pad_rows = MOE_ROUTE_ROWS - E - MOE_GROUPS
    wr = jnp.concatenate([w_re.T, w_rg.T, jnp.zeros((pad_rows, D), f32)], axis=0)
    br = jnp.concatenate([b_re, b_rg, jnp.zeros((pad_rows,), f32)]).reshape(MOE_ROUTE_ROWS, 1)
    tm = min(MOE_ROUTE_TM, M)
    slot_spec = pl.BlockSpec((MOE_TOPK, tm), lambda i: (0, i))
    eid, gate, pos, cnt = pl.pallas_call(
        _moe_route_kernel,
        grid=(M // tm,),
        in_specs=[pl.BlockSpec((tm, D), lambda i: (i, 0)), pl.BlockSpec((MOE_ROUTE_ROWS, D), lambda i: (0, 0)),
                  pl.BlockSpec((MOE_ROUTE_ROWS, 1), lambda i: (0, 0))],
        out_specs=[slot_spec, slot_spec, slot_spec, pl.BlockSpec((E, LANES), lambda i: (0, 0))],
        out_shape=[jax.ShapeDtypeStruct((MOE_TOPK, M), i32), jax.ShapeDtypeStruct((MOE_TOPK, M), f32),
                   jax.ShapeDtypeStruct((MOE_TOPK, M), i32), jax.ShapeDtypeStruct((E, LANES), f32)],
        scratch_shapes=[pltpu.VMEM((E, LANES), f32)],
        compiler_params=pltpu.CompilerParams(dimension_semantics=("arbitrary",), vmem_limit_bytes=VMEM_LIMIT_BYTES),
        name="moe_route",
    )(h, wr, br)

    counts = cnt[:, 0].astype(i32)
    padded = (counts + MOE_BLOCK - 1) // MOE_BLOCK * MOE_BLOCK
    ends = jnp.cumsum(padded)
    start = ends - padded
    P = M * MOE_TOPK + E * MOE_BLOCK
    n_blk = P // MOE_BLOCK
    blk_e = jnp.minimum(jnp.sum(ends[None, :] <= (jnp.arange(n_blk) * MOE_BLOCK)[:, None], axis=1), E - 1).astype(i32)
    dest = pos + jnp.sum(jnp.where(eid[:, :, None] == jnp.arange(E), start, 0), axis=-1)

    rt = min(MOE_ROW_TM, M)
    dest_spec = pl.BlockSpec((MOE_TOPK, rt), lambda i: (0, i), memory_space=pltpu.SMEM)
    row_spec = pl.BlockSpec((rt, D), lambda i: (i, 0))
    any_spec = pl.BlockSpec(memory_space=pl.ANY)
    xbuf = pl.pallas_call(
        _moe_scatter_kernel,
        grid=(M // rt,),
        in_specs=[dest_spec, row_spec, any_spec],
        out_specs=any_spec,
        out_shape=jax.ShapeDtypeStruct((P, D), f32),
        scratch_shapes=[pltpu.SemaphoreType.DMA(())],
        input_output_aliases={2: 0},
        compiler_params=pltpu.CompilerParams(dimension_semantics=("arbitrary",)),
        name="moe_scatter",
    )(dest, h, jnp.zeros((P, D), f32))

    ybuf = _moe_experts(xbuf, blk_e, w1, w3, w2)

    vec_spec = pl.BlockSpec((1, D), lambda i: (0, 0))
    return pl.pallas_call(
        _moe_combine_ln_kernel,
        grid=(M // rt,),
        in_specs=[dest_spec, pl.BlockSpec((rt, MOE_TOPK), lambda i: (i, 0)), row_spec, vec_spec, vec_spec, any_spec],
        out_specs=row_spec,
        out_shape=jax.ShapeDtypeStruct((M, D), f32),
        scratch_shapes=[pltpu.VMEM((MOE_TOPK, rt, D), f32), pltpu.SemaphoreType.DMA(())],
        compiler_params=pltpu.CompilerParams(dimension_semantics=("arbitrary",), vmem_limit_bytes=VMEM_LIMIT_BYTES),
        name="moe_combine_ln",
    )(dest, gate.T, h, ln_g.reshape(1, D), ln_b.reshape(1, D), ybuf)


def _dense_res_ln_kernel(y_ref, w_ref, x_ref, g_ref, b_ref, o_ref):
    mix = jnp.dot(y_ref[...].astype(jnp.bfloat16), w_ref[...], preferred_element_type=jnp.float32)
    o_ref[...] = _layer_norm(DN_ALPHA * x_ref[...] + mix, g_ref[...], b_ref[...])


def _dense_res_ln(y, w, x, ln_g, ln_b, tm=512):
    m, k = y.shape
    d = w.shape[1]
    tm = min(tm, m)
    vec_spec = pl.BlockSpec((1, d), lambda i: (0, 0))
    return pl.pallas_call(
        _dense_res_ln_kernel,
        grid=(m // tm,),
        in_specs=[pl.BlockSpec((tm, k), lambda i: (i, 0)), pl.BlockSpec((k, d), lambda i: (0, 0)),
                  pl.BlockSpec((tm, d), lambda i: (i, 0)), vec_spec, vec_spec],
        out_specs=pl.BlockSpec((tm, d), lambda i: (i, 0)),
        out_shape=jax.ShapeDtypeStruct((m, d), jnp.float32),
        compiler_params=pltpu.CompilerParams(dimension_semantics=("arbitrary",), vmem_limit_bytes=VMEM_LIMIT_BYTES),
        name="dense_res_ln",
    )(y, w.astype(jnp.bfloat16), x, ln_g.reshape(1, d), ln_b.reshape(1, d))


def _ple_kernel(h_ref, p_ref, wg_ref, wp_ref, o_ref):
    bf16, f32 = jnp.bfloat16, jnp.float32
    h = h_ref[...]
    gate = jax.nn.sigmoid(jnp.dot(h.astype(bf16), wg_ref[...], preferred_element_type=f32))
    o_ref[...] = h + gate * jnp.dot(p_ref[...].astype(bf16), wp_ref[...], preferred_element_type=f32)


def _ple(h, p, wg, wp, tm=512):
    m, d = h.shape
    kp = p.shape[1]
    tm = min(tm, m)
    return pl.pallas_call(
        _ple_kernel,
        grid=(m // tm,),
        in_specs=[pl.BlockSpec((tm, d), lambda i: (i, 0)), pl.BlockSpec((tm, kp), lambda i: (i, 0)),
                  pl.BlockSpec((d, d), lambda i: (0, 0)), pl.BlockSpec((kp, d), lambda i: (0, 0))],
        out_specs=pl.BlockSpec((tm, d), lambda i: (i, 0)),
        out_shape=jax.ShapeDtypeStruct((m, d), jnp.float32),
        compiler_params=pltpu.CompilerParams(dimension_semantics=("arbitrary",), vmem_limit_bytes=VMEM_LIMIT_BYTES),
        name="ple",
    )(h, p, wg.astype(jnp.bfloat16), wp.astype(jnp.bfloat16))


def kernel(x, p, positions, ev_w_in, ev_w_out, rw_mu, rw_w0, rw_w2, rw_a0, rw_a2, rw_g2, rw_k_k, rw_k_a,
           rw_r_k, rw_gn_g, rw_gn_b, nsa_cmp_pos, nsa_cmp_w1, nsa_cmp_w2, od_w_in, od_w_out, hg_lb, hg_norm_g,
           moe_w_rg, moe_b_rg, moe_w_re, moe_b_re, moe_w1, moe_w3, moe_w2, ln_g, ln_b, ple_w, ple_gate_w):
    B, T, D = x.shape
    M = B * T
    cos, sin = _rope_tables(positions, NSA_HEAD)
    lb_soft = jax.nn.softmax(hg_lb, axis=0)
    lb_all = jnp.cumsum(lb_soft, axis=0) - lb_soft[0:1]
    for li in range(DEPTH):
        j = li // 2
        if li % 2 == 0:
            z = _dense(x.reshape(M, D), ev_w_in[j]).reshape(B, T, -1)
            z_rw, z_nsa = jnp.split(z, [RW_COLS], axis=-1)
            y_rw = _rwkv7_group(z_rw, rw_mu[j], rw_w0[j], rw_w2[j], rw_a0[j], rw_a2[j], rw_g2[j], rw_k_k[j],
                                rw_k_a[j], rw_r_k[j], rw_gn_g[j], rw_gn_b[j])
            y_nsa = _nsa_group(z_nsa, cos, sin, nsa_cmp_pos[j], nsa_cmp_w1[j], nsa_cmp_w2[j])
            y, w_out = jnp.concatenate([y_rw, y_nsa], -1).reshape(M, D), ev_w_out[j]
        else:
            y, w_out = _hgrn2_mixer(x, od_w_in[j], lb_all[li], hg_norm_g[j]), od_w_out[j]
        h = _dense_res_ln(y, w_out, x.reshape(M, D), ln_g[li, 0], ln_b[li, 0])
        h = _hier_moe_ln(h, moe_w_rg[li], moe_b_rg[li], moe_w_re[li], moe_b_re[li], moe_w1[li], moe_w3[li],
                         moe_w2[li], ln_g[li, 1], ln_b[li, 1])
        x = _ple(h, p[li].reshape(M, PLE_DIM), ple_gate_w[li], ple_w[li]).reshape(B, T, D)
    return x
```

```python
import functools

import numpy as np
import jax
import jax.numpy as jnp
from jax import lax
from jax.experimental import pallas as pl
from jax.experimental.pallas import tpu as pltpu

D_MODEL = 1024
DEPTH = 2
PLE_DIM = 256
DN_ALPHA = (2 * DEPTH) ** 0.25
LN_EPS = 1e-5
ROPE_THETA = 10000.0

RW_WIDTH = D_MODEL // 2
RW_HEAD = 64
RW_HEADS = RW_WIDTH // RW_HEAD
RW_DECAY_LORA = 64
RW_AAA_LORA = 64
RW_GATE_LORA = 128
RW_GN_EPS = RW_HEAD * 1e-5
RW_SPLITS = (RW_WIDTH, RW_WIDTH, RW_WIDTH, RW_DECAY_LORA, RW_AAA_LORA, RW_GATE_LORA)
RW_COLS = sum(RW_SPLITS)

NSA_WIDTH = D_MODEL - RW_WIDTH
NSA_HEAD = 64
NSA_HEADS = NSA_WIDTH // NSA_HEAD
NSA_KV_HEADS = 2
NSA_GROUP = NSA_HEADS // NSA_KV_HEADS
NSA_KV = NSA_KV_HEADS * NSA_HEAD
CMP_LEN = 32
CMP_STRIDE = 16
CMP_HIDDEN = 128
SEL_LEN = 64
SEL_TOP = 16
WINDOW = 512
NSA_QBLOCK = 32
NSA_SPLITS = (NSA_WIDTH,) + (NSA_KV,) * 6 + (3 * NSA_HEADS,)
NSA_COLS = sum(NSA_SPLITS)
EV_COLS = RW_COLS + NSA_COLS

HG_HEAD = 128
HG_HEADS = D_MODEL // HG_HEAD
HG_CHUNK = 16
HG_SPLITS = (D_MODEL, D_MODEL, D_MODEL, D_MODEL)

MOE_GROUPS = 4
MOE_PER_GROUP = 8
MOE_EXPERTS = MOE_GROUPS * MOE_PER_GROUP
MOE_TOPK = 2
MOE_HIDDEN = 512
MOE_BLOCK = 256

LANES = 128
VMEM_LIMIT_BYTES = 56 * 1024 * 1024


def _round_up(n, m):
    return (n + m - 1) // m * m


def _dense_kernel(x_ref, w_ref, o_ref):
    o_ref[...] = jnp.dot(x_ref[...].astype(jnp.bfloat16), w_ref[...], preferred_element_type=jnp.float32)


def _dense(x2d, w, tm=512):
    m, k = x2d.shape
    n = w.shape[1]
    n_pad = _round_up(n, LANES)
    wb = w.astype(jnp.bfloat16)
    if n_pad != n:
        wb = jnp.pad(wb, ((0, 0), (0, n_pad - n)))
    tm = min(tm, m)
    assert m % tm == 0
    out = pl.pallas_call(
        _dense_kernel,
        grid=(m // tm,),
        in_specs=[pl.BlockSpec((tm, k), lambda i: (i, 0)), pl.BlockSpec((k, n_pad), lambda i: (0, 0))],
        out_specs=pl.BlockSpec((tm, n_pad), lambda i: (i, 0)),
        out_shape=jax.ShapeDtypeStruct((m, n_pad), jnp.float32),
        compiler_params=pltpu.CompilerParams(dimension_semantics=("arbitrary",), vmem_limit_bytes=VMEM_LIMIT_BYTES),
        name="dense",
    )(x2d, wb)
    return out[:, :n] if n_pad != n else out


def _split(z, sizes):
    cuts = [int(c) for c in np.cumsum(sizes)[:-1]]
    return jnp.split(z, cuts, axis=-1)


def _layer_norm(x, g, b):
    xc = x - jnp.mean(x, -1, keepdims=True)
    var = jnp.mean(xc * xc, -1, keepdims=True)
    return xc * lax.rsqrt(var + LN_EPS) * g + b


def _rope_tables(positions, dim):
    inv = (1.0 / (ROPE_THETA ** (np.arange(0, dim, 2, dtype=np.float32) / dim))).astype(np.float32)
    ang = positions.astype(jnp.float32)[..., None] * inv
    return jnp.cos(ang)[:, :, None, :], jnp.sin(ang)[:, :, None, :]


def _apply_rope(x, cos, sin):
    half = x.shape[-1] // 2
    x1, x2 = x[..., :half], x[..., half:]
    return jnp.concatenate([x1 * cos - x2 * sin, x2 * cos + x1 * sin], -1)


def _masked_softmax(s, mask):
    s = jnp.where(mask, s, -jnp.inf)
    m = jnp.max(s, axis=-1, keepdims=True)
    m = jnp.where(jnp.isfinite(m), m, 0.0)
    e = jnp.exp(s - m)
    return e / jnp.maximum(jnp.sum(e, -1, keepdims=True), 1e-30)


RW_CHUNK = 64
RW_STEP = 128


def _bdot(a, b):
    return jnp.dot(a.astype(jnp.bfloat16), b.astype(jnp.bfloat16), preferred_element_type=jnp.float32)


def _exact_dot_01(tri, x):
    bf16, f32 = jnp.bfloat16, jnp.float32
    h1 = x.astype(bf16)
    r1 = x - h1.astype(f32)
    h2 = r1.astype(bf16)
    h3 = (r1 - h2.astype(f32)).astype(bf16)
    d = lambda h: jnp.dot(tri, h, preferred_element_type=f32)
    return d(h1) + d(h2) + d(h3)


def _rwkv7_chunk_kernel(r_ref, lw_ref, k_ref, v_ref, kk_ref, b_ref, lwT_ref, kT_ref, bT_ref, o_ref, h_ref):
    f32, bf16 = jnp.float32, jnp.bfloat16
    n_heads, S, N = r_ref.shape[1], r_ref.shape[2], r_ref.shape[3]
    C = RW_CHUNK
    n_sub = S // C

    @pl.when(pl.program_id(1) == 0)
    def _():
        h_ref[...] = jnp.zeros_like(h_ref)

    row = lax.broadcasted_iota(jnp.int32, (S, S), 0)
    col = lax.broadcasted_iota(jnp.int32, (S, S), 1)
    same = lax.shift_right_logical(row, int(np.log2(C))) == lax.shift_right_logical(col, int(np.log2(C)))
    incl = same & (row >= col)
    strict = same & (row > col)
    tri = jnp.where(incl, 1.0, 0.0).astype(bf16)
    triT = jnp.where(same & (col >= row), 1.0, 0.0).astype(bf16)
    eye_s = jnp.where(row == col, 1.0, 0.0)
    rn = lax.broadcasted_iota(jnp.int32, (N, N), 0)
    cn = lax.broadcasted_iota(jnp.int32, (N, N), 1)
    eye_n = jnp.where(rn == cn, 1.0, 0.0)
    lane_chunk = lax.shift_right_logical(lax.broadcasted_iota(jnp.int32, (N, S), 1), int(np.log2(C)))

    def bmm(a, b):
        return jnp.einsum('hij,hjk->hik', a.astype(bf16), b.astype(bf16), preferred_element_type=f32)

    def split3(x):
        h1 = x.astype(bf16)
        r1 = x - h1.astype(f32)
        h2 = r1.astype(bf16)
        return h1, h2, (r1 - h2.astype(f32)).astype(bf16)

    tri_h = jnp.broadcast_to(tri, (n_heads, S, S))
    triT_h = jnp.broadcast_to(triT, (n_heads, S, S))
    lw = lw_ref[0]
    cs = sum(bmm(tri_h, t) for t in split3(lw))
    csT = sum(bmm(t, triT_h) for t in split3(lwT_ref[0]))
    kks = kk_ref[0] * jnp.exp(cs - lw)
    rs = r_ref[0] * jnp.exp(cs)
    e_negT = jnp.exp(-csT)
    bsT = bT_ref[0] * e_negT
    ksT = kT_ref[0] * e_negT
    bsT_b, ksT_b = bsT.astype(bf16), ksT.astype(bf16)
    v = v_ref[0].astype(bf16)

    lhs = jnp.concatenate([kks, rs], axis=1).astype(bf16)
    mb = bmm(lhs, bsT_b)
    mk = bmm(lhs, ksT_b)
    a_b = jnp.where(strict, mb[:, :S], 0.0)
    a_k = jnp.where(strict, mk[:, :S], 0.0)
    r_b = jnp.where(incl, mb[:, S:], 0.0)
    r_k = jnp.where(incl, mk[:, S:], 0.0)

    t_inv = eye_s - a_b
    pw = a_b
    for _ in range(int(np.log2(C)) - 1):
        pw = bmm(pw, pw)
        t_inv = t_inv + bmm(t_inv, pw)

    w_mat = bmm(t_inv, kks)
    u_loc = -bmm(t_inv, bmm(a_k, v))
    q_eff = rs - bmm(r_b, w_mat)
    o_loc = bmm(r_b, u_loc) + bmm(r_k, v)

    hc = h_ref[...]
    for c in range(n_sub):
        in_c = lane_chunk == c
        bs_c = jnp.where(in_c, bsT, 0.0)
        ks_c = jnp.where(in_c, ksT, 0.0)
        g_end = jnp.exp(csT[:, :, (c + 1) * C - 1:(c + 1) * C])
        g_mat = g_end * (eye_n - bmm(bs_c, w_mat))
        h_loc = g_end * (bmm(bs_c, u_loc) + bmm(ks_c, v))
        sl = slice(c * C, (c + 1) * C)
        o_ref[0, :, sl, :] = bmm(q_eff[:, sl], hc) + o_loc[:, sl]
        hc = bmm(g_mat, hc) + h_loc
    h_ref[...] = hc


def _rwkv7_scan(r, lw, k, v, kk, b):
    B, T, H, N = r.shape
    C = min(RW_STEP, T)
    assert C % RW_CHUNK == 0 and T % C == 0
    hm = lambda t: t.transpose(0, 2, 1, 3)
    hmT = lambda t: t.transpose(0, 2, 3, 1)
    spec = pl.BlockSpec((1, H, C, N), lambda bi, ci: (bi, 0, ci, 0))
    specT = pl.BlockSpec((1, H, N, C), lambda bi, ci: (bi, 0, 0, ci))
    o = pl.pallas_call(
        _rwkv7_chunk_kernel,
        grid=(B, T // C),
        in_specs=[spec] * 6 + [specT] * 3,
        out_specs=spec,
        out_shape=jax.ShapeDtypeStruct((B, H, T, N), jnp.float32),
        scratch_shapes=[pltpu.VMEM((H, N, N), jnp.float32)],
        compiler_params=pltpu.CompilerParams(dimension_semantics=("arbitrary", "arbitrary"),
                                             vmem_limit_bytes=VMEM_LIMIT_BYTES),
        name="rwkv7_chunk",
    )(hm(r), hm(lw), hm(k), hm(v), hm(kk), hm(b), hmT(lw), hmT(k), hmT(b))
    return o.transpose(0, 2, 1, 3)


def _rwkv7_group(z, mu, w0, w2, a0, a2, g2, k_k, k_a, r_k, gn_g, gn_b):
    B, T, _ = z.shape
    z_prev = jnp.pad(z, ((0, 0), (1, 0), (0, 0)))[:, :-1]
    z = z + mu * (z_prev - z)
    r, k, v, wd, ad, gd = _split(z, RW_SPLITS)
    w = -jax.nn.softplus(-(w0 + jnp.tanh(wd) @ w2)) - 0.5
    log_decay = -jnp.exp(w)
    a = jax.nn.sigmoid(a0 + ad @ a2)
    g = jax.nn.sigmoid(gd) @ g2
    heads = lambda t: t.reshape(B, T, RW_HEADS, RW_HEAD)
    kk = heads(k * k_k)
    kk = kk / jnp.maximum(jnp.sqrt(jnp.sum(kk * kk, -1, keepdims=True)), 1e-12)
    k = k * (1.0 + (a - 1.0) * k_a)
    r, k, v, a = heads(r), heads(k), heads(v), heads(a)
    o = _rwkv7_scan(r, heads(log_decay), k, v, kk, a * kk)
    oc = o - jnp.mean(o, -1, keepdims=True)
    o = oc * lax.rsqrt(jnp.mean(oc * oc, -1, keepdims=True) + RW_GN_EPS)
    o = o.reshape(B, T, RW_WIDTH) * gn_g + gn_b
    bonus = (jnp.sum(r * k * r_k, -1, keepdims=True) * v).reshape(B, T, RW_WIDTH)
    return (o + bonus) * g


def _cmp_sel_overlap(n_cmp, n_sel):
    cs = np.arange(n_cmp)[:, None] * CMP_STRIDE
    ss = np.arange(n_sel)[None, :] * SEL_LEN
    ov = np.clip(np.minimum(cs + CMP_LEN, ss + SEL_LEN) - np.maximum(cs, ss), 0, None)
    return (ov / CMP_LEN).astype(np.float32)


NSA_TQ = 128
NSA_TK_SEL = 512
NSA_TK_WIN = 256
NEG_INIT = -1e30


def _nsa_attn_kernel(qraw_ref, qrot_ref, kcmp_ref, vcmpT_ref, ovT_ref, ksel_ref, vselT_ref, kwin_ref, vwinT_ref,
                     gl_ref, o_ref, lim_ref, *, tq, n_top):
    f32, bf16 = jnp.float32, jnp.bfloat16
    G = NSA_GROUP
    R = G * tq
    t0 = pl.program_id(2) * tq
    ncp = kcmp_ref.shape[2]
    n_sel = ovT_ref.shape[0]

    t_row = t0 + (lax.broadcasted_iota(jnp.int32, (1, R), 1) & (tq - 1))
    qraw = jnp.concatenate([qraw_ref[0, g] for g in range(G)], axis=1)
    qrot = jnp.concatenate([qrot_ref[0, g] for g in range(G)], axis=1)

    s_c = jnp.dot(kcmp_ref[0, 0], qraw, precision=lax.Precision.HIGHEST, preferred_element_type=f32)
    cmp_last = lax.broadcasted_iota(jnp.int32, (ncp, R), 0) * CMP_STRIDE + (CMP_LEN - 1)
    s_c = jnp.where(cmp_last <= t_row, s_c, -jnp.inf)
    m_c = jnp.max(s_c, axis=0, keepdims=True)
    m_c = jnp.where(m_c == -jnp.inf, 0.0, m_c)
    e_c = jnp.exp(s_c - m_c)
    p_c = e_c / jnp.maximum(jnp.sum(e_c, axis=0, keepdims=True), 1e-30)
    o_c = jnp.dot(vcmpT_ref[0, 0], p_c.astype(bf16), preferred_element_type=f32)

    p_sum = p_c[:, 0:tq]
    for g in range(1, G):
        p_sum = p_sum + p_c[:, g * tq:(g + 1) * tq]
    imp = jnp.dot(ovT_ref[...], p_sum, precision=lax.Precision.HIGHEST, preferred_element_type=f32)
    j_iota = lax.broadcasted_iota(jnp.int32, (n_sel, tq), 0)
    t_tok = t0 + lax.broadcasted_iota(jnp.int32, (1, tq), 1)
    cur = lax.shift_right_logical(t_tok, int(np.log2(SEL_LEN)))
    forced = (j_iota == 0) | (j_iota == cur) | (j_iota == cur - 1)
    cand = (j_iota >= 1) & (j_iota <= cur - 2)
    quota = n_top - 1 - jnp.minimum(cur, 2)
    key = jnp.where(cand, pltpu.bitcast(imp, jnp.int32), -1)
    rank = jnp.zeros((n_sel, tq), jnp.int32)
    for i in range(n_sel):
        row = key[i:i + 1, :]
        rank = rank + jnp.where(row + jnp.where(j_iota > i, 1, 0) > key, 1, 0)
    sel = forced | (cand & (rank < quota))
    lim_ref[...] = jnp.where(sel, t_tok, -1)

    def online_update(carry, s, vT):
        m, l, acc = carry
        m_new = jnp.maximum(m, jnp.max(s, axis=0, keepdims=True))
        alpha = jnp.exp(m - m_new)
        p = jnp.exp(s - m_new)
        l = alpha * l + jnp.sum(p, axis=0, keepdims=True)
        acc = alpha * acc + jnp.dot(vT, p.astype(bf16), preferred_element_type=f32)
        return m_new, l, acc

    init = (jnp.full((1, R), NEG_INIT, f32), jnp.zeros((1, R), f32), jnp.zeros((NSA_HEAD, R), f32))

    tk = min(NSA_TK_SEL, ksel_ref.shape[2])
    nb = tk // SEL_LEN
    key_iota = lax.broadcasted_iota(jnp.int32, (tk, R), 0)

    def sel_body(kt, carry):
        k0 = pl.multiple_of(kt * tk, tk)
        s = jnp.dot(ksel_ref[0, 0, pl.ds(k0, tk), :], qrot, preferred_element_type=f32)
        limb = lim_ref[pl.ds(pl.multiple_of(kt * nb, nb), nb), :] - k0
        lim_t = jnp.concatenate([jnp.broadcast_to(limb[jb:jb + 1, :], (SEL_LEN, tq)) for jb in range(nb)], axis=0)
        lim_t = jnp.concatenate([lim_t] * G, axis=1)
        s = jnp.where(key_iota <= lim_t, s, -jnp.inf)
        return online_update(carry, s, vselT_ref[0, 0, :, pl.ds(k0, tk)])

    n_kt = lax.div(t0 + tq - 1, tk) + 1
    _, l_s, acc_s = lax.fori_loop(0, n_kt, sel_body, init)
    o_s = acc_s / jnp.maximum(l_s, 1e-30)

    tkw = min(NSA_TK_WIN, kwin_ref.shape[2])
    key_iota_w = lax.broadcasted_iota(jnp.int32, (tkw, R), 0)

    def win_body(kt, carry):
        k0 = pl.multiple_of(kt * tkw, tkw)
        s = jnp.dot(kwin_ref[0, 0, pl.ds(k0, tkw), :], qrot, preferred_element_type=f32)
        rel = t_row - k0
        s = jnp.where((key_iota_w <= rel) & (key_iota_w > rel - WINDOW), s, -jnp.inf)
        return online_update(carry, s, vwinT_ref[0, 0, :, pl.ds(k0, tkw)])

    kt_lo = lax.div(jnp.maximum(t0 - (WINDOW - 1), 0), tkw)
    kt_hi = lax.div(t0 + tq - 1, tkw) + 1
    _, l_w, acc_w = lax.fori_loop(kt_lo, kt_hi, win_body, init)
    o_w = acc_w / jnp.maximum(l_w, 1e-30)

    gl = jnp.concatenate([gl_ref[0, g] for g in range(G)], axis=1)
    gates = jax.nn.sigmoid(gl)
    out = gates[0:1, :] * o_c + gates[1:2, :] * o_s + gates[2:3, :] * o_w
    for g in range(G):
        o_ref[0, g] = out[:, g * tq:(g + 1) * tq]


def _nsa_group(z, cos, sin, cmp_pos, cmp_w1, cmp_w2):
    B, T, _ = z.shape
    f32, bf16 = jnp.float32, jnp.bfloat16
    q, kc, vc, ksl, vsl, kwn, vwn, gl = _split(z, NSA_SPLITS)
    kv_heads = lambda t: t.reshape(B, T, NSA_KV_HEADS, NSA_HEAD)
    scale = NSA_HEAD ** -0.5
    q = q.reshape(B, T, NSA_HEADS, NSA_HEAD)
    qraw_T = (q * scale).transpose(0, 2, 3, 1)
    qrot_T = (_apply_rope(q, cos, sin) * scale).transpose(0, 2, 3, 1).astype(bf16)
    ksel = _apply_rope(kv_heads(ksl), cos, sin).transpose(0, 2, 1, 3).astype(bf16)
    kwin = _apply_rope(kv_heads(kwn), cos, sin).transpose(0, 2, 1, 3).astype(bf16)
    vselT = kv_heads(vsl).transpose(0, 2, 3, 1).astype(bf16)
    vwinT = kv_heads(vwn).transpose(0, 2, 3, 1).astype(bf16)
    gl_T = gl.reshape(B, T, NSA_HEADS, 3).transpose(0, 2, 3, 1)

    n_grp = T // CMP_STRIDE
    n_cmp = n_grp - 1
    half = CMP_STRIDE * NSA_HEAD
    grp = jnp.stack([kv_heads(kc), kv_heads(vc)]).transpose(0, 1, 3, 2, 4).reshape(2, B * NSA_KV_HEADS * n_grp, half)
    cmp_kv = []
    for zi in range(2):
        w1ab = jnp.concatenate([cmp_w1[zi, :half], cmp_w1[zi, half:]], axis=1)
        ab = _dense(grp[zi], w1ab).reshape(B, NSA_KV_HEADS, n_grp, 2 * CMP_HIDDEN)
        c = cmp_pos[zi].reshape(1, CMP_LEN * NSA_HEAD) @ cmp_w1[zi]
        hid = jax.nn.gelu(ab[:, :, :-1, :CMP_HIDDEN] + ab[:, :, 1:, CMP_HIDDEN:] + c)
        cmp_kv.append(jnp.pad(hid @ cmp_w2[zi], ((0, 0), (0, 0), (0, 1), (0, 0))))
    k_cmp = cmp_kv[0]
    v_cmpT = cmp_kv[1].transpose(0, 1, 3, 2).astype(bf16)

    n_sel = T // SEL_LEN
    n_top = min(SEL_TOP, n_sel)
    ovT = jnp.asarray(np.pad(_cmp_sel_overlap(n_cmp, n_sel), ((0, 1), (0, 0))).T)
    tq = min(NSA_TQ, T)
    G = NSA_GROUP
    q_spec = pl.BlockSpec((1, G, NSA_HEAD, tq), lambda b, h, i: (b, h, 0, i))
    kv_spec = lambda shape: pl.BlockSpec((1, 1) + shape, lambda b, h, i: (b, h, 0, 0))
    out = pl.pallas_call(
        functools.partial(_nsa_attn_kernel, tq=tq, n_top=n_top),
        grid=(B, NSA_KV_HEADS, T // tq),
        in_specs=[q_spec, q_spec, kv_spec((n_grp, NSA_HEAD)), kv_spec((NSA_HEAD, n_grp)),
                  pl.BlockSpec((n_sel, n_grp), lambda b, h, i: (0, 0)),
                  kv_spec((T, NSA_HEAD)), kv_spec((NSA_HEAD, T)), kv_spec((T, NSA_HEAD)), kv_spec((NSA_HEAD, T)),
                  pl.BlockSpec((1, G, 3, tq), lambda b, h, i: (b, h, 0, i))],
        out_specs=q_spec,
        out_shape=jax.ShapeDtypeStruct((B, NSA_HEADS, NSA_HEAD, T), f32),
        scratch_shapes=[pltpu.VMEM((n_sel, tq), jnp.int32)],
        compiler_params=pltpu.CompilerParams(dimension_semantics=("arbitrary", "arbitrary", "arbitrary"),
                                             vmem_limit_bytes=VMEM_LIMIT_BYTES),
        name="nsa_attn",
    )(qraw_T, qrot_T, k_cmp, v_cmpT, ovT, ksel, vselT, kwin, vwinT, gl_T)
    return out.transpose(0, 3, 1, 2).reshape(B, T, NSA_WIDTH)


HG_STEP = 256
HG_HEADS_PER_STEP = 8


def _hgrn2_kernel(q_ref, f_ref, i_ref, g_ref, lb_ref, ng_ref, o_ref, st_ref):
    f32, bf16 = jnp.float32, jnp.bfloat16
    S, d = q_ref.shape[1], HG_HEAD
    hb = q_ref.shape[2] // d
    C = HG_CHUNK
    n_sub = S // C
    shift = int(np.log2(C))

    @pl.when(pl.program_id(2) == 0)
    def _():
        st_ref[...] = jnp.zeros_like(st_ref)

    heads = lambda ref: jnp.stack([ref[0][:, h * d:(h + 1) * d] for h in range(hb)])
    lb = lb_ref[...]
    f = heads(f_ref)
    logf = jnp.log(lb + (1.0 - lb) * jax.nn.sigmoid(f))
    k = (1.0 - lb) * jax.nn.sigmoid(-f)
    qs = jax.nn.silu(heads(q_ref))
    v = heads(i_ref)

    row = lax.broadcasted_iota(jnp.int32, (S, S), 0)
    col = lax.broadcasted_iota(jnp.int32, (S, S), 1)
    same = lax.shift_right_logical(row, shift) == lax.shift_right_logical(col, shift)
    causal = same & (row >= col)

    def cumsum01(mask, x):
        m = jnp.broadcast_to(jnp.where(mask, 1.0, 0.0).astype(bf16), (hb, S, S))
        h1 = x.astype(bf16)
        r1 = x - h1.astype(f32)
        h2 = r1.astype(bf16)
        h3 = (r1 - h2.astype(f32)).astype(bf16)
        return sum(jnp.einsum('hij,hjk->hik', m, t, preferred_element_type=f32) for t in (h1, h2, h3))

    b = cumsum01(causal, logf)
    tail = cumsum01(same & (col > row), logf)
    q_e = (qs * jnp.exp(b)).astype(bf16)
    k_e = (k * jnp.exp(-b)).astype(bf16)
    k_tail = k * jnp.exp(tail)
    vb = v.astype(bf16)

    a = jnp.einsum('hsd,htd->hst', q_e, k_e, preferred_element_type=f32)
    o_intra = jnp.einsum('hst,htv->hsv', jnp.where(causal, a, 0.0).astype(bf16), vb, preferred_element_type=f32)

    vT = jnp.stack([v[h].T for h in range(hb)]).astype(bf16)
    row_chunk = lax.shift_right_logical(lax.broadcasted_iota(jnp.int32, (S, d), 0), shift)
    st = st_ref[...]
    o_inter = []
    for c in range(n_sub):
        sl = slice(c * C, (c + 1) * C)
        o_inter.append(jnp.einsum('hcd,hvd->hcv', q_e[:, sl], st.astype(bf16), preferred_element_type=f32))
        k_c = jnp.where(row_chunk == c, k_tail, 0.0).astype(bf16)
        d_c = jnp.exp(b[:, (c + 1) * C - 1:(c + 1) * C, :])
        st = st * d_c + jnp.einsum('hvs,hsk->hvk', vT, k_c, preferred_element_type=f32)
    st_ref[...] = st
    o = o_intra + jnp.concatenate(o_inter, axis=1)
    o = o * lax.rsqrt(jnp.mean(o * o, axis=-1, keepdims=True) + LN_EPS) * ng_ref[...]
    g = heads(g_ref)
    o = o * jax.nn.silu(g)
    for h in range(hb):
        o_ref[0, :, h * d:(h + 1) * d] = o[h]


def _hgrn2_mixer(x, w_in, lb, norm_g):
    B, T, D = x.shape
    z = _dense(x.reshape(B * T, D), w_in).reshape(B, T, -1)
    S = min(HG_STEP, T)
    H, d = HG_HEADS, HG_HEAD
    hb = HG_HEADS_PER_STEP
    ng = H // hb
    col = lambda j: pl.BlockSpec((1, S, hb * d), lambda b, h, t: (b, t, j * ng + h))
    vec = pl.BlockSpec((hb, 1, d), lambda b, h, t: (h, 0, 0))
    o = pl.pallas_call(
        _hgrn2_kernel,
        grid=(B, ng, T // S),
        in_specs=[col(0), col(1), col(2), col(3), vec, pl.BlockSpec((1, 1, d), lambda b, h, t: (0, 0, 0))],
        out_specs=pl.BlockSpec((1, S, hb * d), lambda b, h, t: (b, t, h)),
        out_shape=jax.ShapeDtypeStruct((B, T, D), jnp.float32),
        scratch_shapes=[pltpu.VMEM((hb, d, d), jnp.float32)],
        compiler_params=pltpu.CompilerParams(dimension_semantics=("arbitrary", "arbitrary", "arbitrary"),
                                             vmem_limit_bytes=VMEM_LIMIT_BYTES),
        name="hgrn2",
    )(z, z, z, z, lb.reshape(H, 1, d), norm_g.reshape(1, 1, d))
    return o.reshape(B * T, D)


def _moe_expert_kernel(blk_e_ref, x_ref, w1_ref, w3_ref, w2_ref, o_ref):
    del blk_e_ref
    f32, bf16 = jnp.float32, jnp.bfloat16
    x = x_ref[...].astype(bf16)
    a = jnp.dot(x, w1_ref[0], preferred_element_type=f32)
    b = jnp.dot(x, w3_ref[0], preferred_element_type=f32)
    hid = (jax.nn.silu(a) * b).astype(bf16)
    o_ref[...] = jnp.dot(hid, w2_ref[0], preferred_element_type=f32)


def _moe_experts(xbuf, blk_e, w1, w3, w2):
    P, D = xbuf.shape
    hid = w1.shape[-1]
    bf16 = jnp.bfloat16
    w_spec = lambda shape: pl.BlockSpec((1,) + shape, lambda i, be: (be[i], 0, 0))
    x_spec = pl.BlockSpec((MOE_BLOCK, D), lambda i, be: (i, 0))
    return pl.pallas_call(
        _moe_expert_kernel,
        grid_spec=pltpu.PrefetchScalarGridSpec(
            num_scalar_prefetch=1,
            grid=(P // MOE_BLOCK,),
            in_specs=[x_spec, w_spec((D, hid)), w_spec((D, hid)), w_spec((hid, D))],
            out_specs=x_spec,
        ),
        out_shape=jax.ShapeDtypeStruct((P, D), jnp.float32),
        compiler_params=pltpu.CompilerParams(dimension_semantics=("arbitrary",), vmem_limit_bytes=VMEM_LIMIT_BYTES),
        name="moe_experts",
    )(blk_e, xbuf, w1.astype(bf16), w3.astype(bf16), w2.astype(bf16))


MOE_ROUTE_TM = 512
MOE_ROW_TM = 256
MOE_ROUTE_ROWS = 40


def _moe_route_kernel(h_ref, wr_ref, br_ref, eid_ref, gate_ref, pos_ref, cnt_ref, run_ref):
    f32 = jnp.float32
    tm = h_ref.shape[0]
    E, PG, NG = MOE_EXPERTS, MOE_PER_GROUP, MOE_GROUPS

    @pl.when(pl.program_id(0) == 0)
    def _():
        run_ref[...] = jnp.zeros_like(run_ref)

    lg = lax.dot_general(wr_ref[...], h_ref[...], (((1,), (1,)), ((), ())), precision=lax.Precision.HIGHEST,
                         preferred_element_type=f32) + br_ref[...]
    grp = lg[E:E + NG]
    g_iota = lax.broadcasted_iota(jnp.int32, (NG, tm), 0)
    g_max = jnp.max(grp, axis=0, keepdims=True)
    g_sel = jnp.min(jnp.where(grp == g_max, g_iota, NG), axis=0, keepdims=True)
    p_grp = 1.0 / jnp.sum(jnp.exp(grp - g_max), axis=0, keepdims=True)
    le = lg[0:PG]
    for g in range(1, NG):
        le = jnp.where(g_sel == g, lg[g * PG:(g + 1) * PG], le)
    e_iota = lax.broadcasted_iota(jnp.int32, (PG, tm), 0)
    m1 = jnp.max(le, axis=0, keepdims=True)
    i1 = jnp.min(jnp.where(le == m1, e_iota, PG), axis=0, keepdims=True)
    le2 = jnp.where(e_iota == i1, -jnp.inf, le)
    m2 = jnp.max(le2, axis=0, keepdims=True)
    i2 = jnp.min(jnp.where(le2 == m2, e_iota, PG), axis=0, keepdims=True)
    e2 = jnp.exp(m2 - m1)
    den = 1.0 + e2
    eid1 = g_sel * PG + i1
    eid2 = g_sel * PG + i2
    eid_ref[...] = jnp.concatenate([eid1, eid2], axis=0)
    gate_ref[...] = jnp.concatenate([p_grp / den, p_grp * e2 / den], axis=0)

    x_iota = lax.broadcasted_iota(jnp.int32, (E, tm), 0)
    oh1 = jnp.where(x_iota == eid1, 1.0, 0.0)
    oh2 = jnp.where(x_iota == eid2, 1.0, 0.0)
    before = jnp.where(lax.broadcasted_iota(jnp.int32, (tm, tm), 0) < lax.broadcasted_iota(jnp.int32, (tm, tm), 1),
                       1.0, 0.0).astype(jnp.bfloat16)
    cum1 = jnp.dot(oh1.astype(jnp.bfloat16), before, preferred_element_type=f32)
    cum2 = jnp.dot(oh2.astype(jnp.bfloat16), before, preferred_element_type=f32)
    tot1 = jnp.sum(oh1, axis=1, keepdims=True)
    tot2 = jnp.sum(oh2, axis=1, keepdims=True)
    base = run_ref[:, 0:1]
    pos1 = jnp.sum(oh1 * (cum1 + base), axis=0, keepdims=True)
    pos2 = jnp.sum(oh2 * (cum2 + base + tot1), axis=0, keepdims=True)
    pos_ref[...] = jnp.concatenate([pos1, pos2], axis=0).astype(jnp.int32)
    new = jnp.broadcast_to(base + tot1 + tot2, run_ref.shape)
    run_ref[...] = new
    cnt_ref[...] = new


def _moe_scatter_kernel(dest_ref, h_ref, xin_ref, xbuf_ref, sem):
    del xin_ref
    tm = h_ref.shape[0]

    def row_copy(r, j):
        return pltpu.make_async_copy(h_ref.at[pl.ds(r, 1)], xbuf_ref.at[pl.ds(dest_ref[j, r], 1)], sem)

    def issue(r, c):
        row_copy(r, 0).start()
        row_copy(r, 1).start()
        return c

    lax.fori_loop(0, tm, issue, 0, unroll=8)

    def drain(r, c):
        row_copy(r, 0).wait()
        row_copy(r, 1).wait()
        return c

    lax.fori_loop(0, tm, drain, 0, unroll=8)


def _moe_combine_ln_kernel(dest_ref, gate_ref, h_ref, lng_ref, lnb_ref, ybuf_ref, o_ref, buf_ref, sem):
    tm = h_ref.shape[0]

    def row_copy(r, j):
        return pltpu.make_async_copy(ybuf_ref.at[pl.ds(dest_ref[j, r], 1)], buf_ref.at[j, pl.ds(r, 1)], sem)

    def issue(r, c):
        row_copy(r, 0).start()
        row_copy(r, 1).start()
        return c

    lax.fori_loop(0, tm, issue, 0, unroll=8)

    def drain(r, c):
        row_copy(r, 0).wait()
        row_copy(r, 1).wait()
        return c

    lax.fori_loop(0, tm, drain, 0, unroll=8)
    gate = gate_ref[...]
    ffn = gate[:, 0:1] * buf_ref[0] + gate[:, 1:2] * buf_ref[1]
    o_ref[...] = _layer_norm(DN_ALPHA * h_ref[...] + ffn, lng_ref[...], lnb_ref[...])


def _hier_moe_ln(h, w_rg, b_rg, w_re, b_re, w1, w3, w2, ln_g, ln_b):
    M, D = h.shape
    f32, i32 = jnp.float32, jnp.int32
    E = MOE_EXPERTS
    pad_rows = MOE_ROUTE_ROWS - E - MOE_GROUPS
    wr = jnp.concatenate([w_re.T, w_rg.T, jnp.zeros((pad_rows, D), f32)], axis=0)
    br = jnp.concatenate([b_re, b_rg, jnp.zeros((pad_rows,), f32)]).reshape(MOE_ROUTE_ROWS, 1)
    tm = min(MOE_ROUTE_TM, M)
    slot_spec = pl.BlockSpec((MOE_TOPK, tm), lambda i: (0, i))
    eid, gate, pos, cnt = pl.pallas_call(
        _moe_route_kernel,
        grid=(M // tm,),
        in_specs=[pl.BlockSpec((tm, D), lambda i: (i, 0)), pl.BlockSpec((MOE_ROUTE_ROWS, D), lambda i: (0, 0)),
                  pl.BlockSpec((MOE_ROUTE_ROWS, 1), lambda i: (0, 0))],
        out_specs=[slot_spec, slot_spec, slot_spec, pl.BlockSpec((E, LANES), lambda i: (0, 0))],
        out_shape=[jax.ShapeDtypeStruct((MOE_TOPK, M), i32), jax.ShapeDtypeStruct((MOE_TOPK, M), f32),
                   jax.ShapeDtypeStruct((MOE_TOPK, M), i32), jax.ShapeDtypeStruct((E, LANES), f32)],
        scratch_shapes=[pltpu.VMEM((E, LANES), f32)],
        compiler_params=pltpu.CompilerParams(dimension_semantics=("arbitrary",), vmem_limit_bytes=VMEM_LIMIT_BYTES),
        name="moe_route",
    )(h, wr, br)

    counts = cnt[:, 0].astype(i32)
    padded = (counts + MOE_BLOCK - 1) // MOE_BLOCK * MOE_BLOCK
    ends = jnp.cumsum(padded)
    start = ends - padded
    P = M * MOE_TOPK + E * MOE_BLOCK
    n_blk = P // MOE_BLOCK
    blk_e = jnp.minimum(jnp.sum(ends[None, :] <= (jnp.arange(n_blk) * MOE_BLOCK)[:, None], axis=1), E - 1).astype(i32)
    dest = pos + jnp.sum(jnp.where(eid[:, :, None] == jnp.arange(E), start, 0), axis=-1)

    rt = min(MOE_ROW_TM, M)
    dest_spec = pl.BlockSpec((MOE_TOPK, rt), lambda i: (0, i), memory_space=pltpu.SMEM)
    row_spec = pl.BlockSpec((rt, D), lambda i: (i, 0))
    any_spec = pl.BlockSpec(memory_space=pl.ANY)
    xbuf = pl.pallas_call(
        _moe_scatter_kernel,
        grid=(M // rt,),
        in_specs=[dest_spec, row_spec, any_spec],
        out_specs=any_spec,
        out_shape=jax.ShapeDtypeStruct((P, D), f32),
        scratch_shapes=[pltpu.SemaphoreType.DMA(())],
        input_output_aliases={2: 0},
        compiler_params=pltpu.CompilerParams(dimension_semantics=("arbitrary",)),
        name="moe_scatter",
    )(dest, h, jnp.zeros((P, D), f32))

    ybuf = _moe_experts(xbuf, blk_e, w1, w3, w2)

    vec_spec = pl.BlockSpec((1, D), lambda i: (0, 0))
    return pl.pallas_call(
        _moe_combine_ln_kernel,
        grid=(M // rt,),
        in_specs=[dest_spec, pl.BlockSpec((rt, MOE_TOPK), lambda i: (i, 0)), row_spec, vec_spec, vec_spec, any_spec],
        out_specs=row_spec,
        out_shape=jax.ShapeDtypeStruct((M, D), f32),
        scratch_shapes=[pltpu.VMEM((MOE_TOPK, rt, D), f32), pltpu.SemaphoreType.DMA(())],
        compiler_params=pltpu.CompilerParams(dimension_semantics=("arbitrary",), vmem_limit_bytes=VMEM_LIMIT_BYTES),
        name="moe_combine_ln",
    )(dest, gate.T, h, ln_g.reshape(1, D), ln_b.reshape(1, D), ybuf)


def _dense_res_ln_kernel(y_ref, w_ref, x_ref, g_ref, b_ref, o_ref):
    mix = jnp.dot(y_ref[...].astype(jnp.bfloat16), w_ref[...], preferred_element_type=jnp.float32)
    o_ref[...] = _layer_norm(DN_ALPHA * x_ref[...] + mix, g_ref[...], b_ref[...])


def _dense_res_ln(y, w, x, ln_g, ln_b, tm=512):
    m, k = y.shape
    d = w.shape[1]
    tm = min(tm, m)
    vec_spec = pl.BlockSpec((1, d), lambda i: (0, 0))
    return pl.pallas_call(
        _dense_res_ln_kernel,
        grid=(m // tm,),
        in_specs=[pl.BlockSpec((tm, k), lambda i: (i, 0)), pl.BlockSpec((k, d), lambda i: (0, 0)),
                  pl.BlockSpec((tm, d), lambda i: (i, 0)), vec_spec, vec_spec],
        out_specs=pl.BlockSpec((tm, d), lambda i: (i, 0)),
        out_shape=jax.ShapeDtypeStruct((m, d), jnp.float32),
        compiler_params=pltpu.CompilerParams(dimension_semantics=("arbitrary",), vmem_limit_bytes=VMEM_LIMIT_BYTES),
        name="dense_res_ln",
    )(y, w.astype(jnp.bfloat16), x, ln_g.reshape(1, d), ln_b.reshape(1, d))


def _ple_kernel(h_ref, p_ref, wg_ref, wp_ref, o_ref):
    bf16, f32 = jnp.bfloat16, jnp.float32
    h = h_ref[...]
    gate = jax.nn.sigmoid(jnp.dot(h.astype(bf16), wg_ref[...], preferred_element_type=f32))
    o_ref[...] = h + gate * jnp.dot(p_ref[...].astype(bf16), wp_ref[...], preferred_element_type=f32)


def _ple(h, p, wg, wp, tm=512):
    m, d = h.shape
    kp = p.shape[1]
    tm = min(tm, m)
    return pl.pallas_call(
        _ple_kernel,
        grid=(m // tm,),
        in_specs=[pl.BlockSpec((tm, d), lambda i: (i, 0)), pl.BlockSpec((tm, kp), lambda i: (i, 0)),
                  pl.BlockSpec((d, d), lambda i: (0, 0)), pl.BlockSpec((kp, d), lambda i: (0, 0))],
        out_specs=pl.BlockSpec((tm, d), lambda i: (i, 0)),
        out_shape=jax.ShapeDtypeStruct((m, d), jnp.float32),
        compiler_params=pltpu.CompilerParams(dimension_semantics=("arbitrary",), vmem_limit_bytes=VMEM_LIMIT_BYTES),
        name="ple",
    )(h, p, wg.astype(jnp.bfloat16), wp.astype(jnp.bfloat16))


def kernel(x, p, positions, ev_w_in, ev_w_out, rw_mu, rw_w0, rw_w2, rw_a0, rw_a2, rw_g2, rw_k_k, rw_k_a,
           rw_r_k, rw_gn_g, rw_gn_b, nsa_cmp_pos, nsa_cmp_w1, nsa_cmp_w2, od_w_in, od_w_out, hg_lb, hg_norm_g,
           moe_w_rg, moe_b_rg, moe_w_re, moe_b_re, moe_w1, moe_w3, moe_w2, ln_g, ln_b, ple_w, ple_gate_w):
    B, T, D = x.shape
    M = B * T
    cos, sin = _rope_tables(positions, NSA_HEAD)
    lb_soft = jax.nn.softmax(hg_lb, axis=0)
    lb_all = jnp.cumsum(lb_soft, axis=0) - lb_soft[0:1]
    for li in range(DEPTH):
        j = li // 2
        if li % 2 == 0:
            z = _dense(x.reshape(M, D), ev_w_in[j]).reshape(B, T, -1)
            z_rw, z_nsa = jnp.split(z, [RW_COLS], axis=-1)
            y_rw = _rwkv7_group(z_rw, rw_mu[j], rw_w0[j], rw_w2[j], rw_a0[j], rw_a2[j], rw_g2[j], rw_k_k[j],
                                rw_k_a[j], rw_r_k[j], rw_gn_g[j], rw_gn_b[j])
            y_nsa = _nsa_group(z_nsa, cos, sin, nsa_cmp_pos[j], nsa_cmp_w1[j], nsa_cmp_w2[j])
            y, w_out = jnp.concatenate([y_rw, y_nsa], -1).reshape(M, D), ev_w_out[j]
        else:
            y, w_out = _hgrn2_mixer(x, od_w_in[j], lb_all[li], hg_norm_g[j]), od_w_out[j]
        h = _dense_res_ln(y, w_out, x.reshape(M, D), ln_g[li, 0], ln_b[li, 0])
        h = _hier_moe_ln(h, moe_w_rg[li], moe_b_rg[li], moe_w_re[li], moe_b_re[li], moe_w1[li], moe_w3[li],
                         moe_w2[li], ln_g[li, 1], ln_b[li, 1])
        x = _ple(h, p[li].reshape(M, PLE_DIM), ple_gate_w[li], ple_w[li]).reshape(B, T, D)
    return x
```

```python
import functools

import numpy as np
import jax
import jax.numpy as jnp
from jax import lax
from jax.experimental import pallas as pl
from jax.experimental.pallas import tpu as pltpu

D_MODEL = 1024
DEPTH = 2
PLE_DIM = 256
DN_ALPHA = (2 * DEPTH) ** 0.25
LN_EPS = 1e-5
ROPE_THETA = 10000.0

RW_WIDTH = D_MODEL // 2
RW_HEAD = 64
RW_HEADS = RW_WIDTH // RW_HEAD
RW_DECAY_LORA = 64
RW_AAA_LORA = 64
RW_GATE_LORA = 128
RW_GN_EPS = RW_HEAD * 1e-5
RW_SPLITS = (RW_WIDTH, RW_WIDTH, RW_WIDTH, RW_DECAY_LORA, RW_AAA_LORA, RW_GATE_LORA)
RW_COLS = sum(RW_SPLITS)

NSA_WIDTH = D_MODEL - RW_WIDTH
NSA_HEAD = 64
NSA_HEADS = NSA_WIDTH // NSA_HEAD
NSA_KV_HEADS = 2
NSA_GROUP = NSA_HEADS // NSA_KV_HEADS
NSA_KV = NSA_KV_HEADS * NSA_HEAD
CMP_LEN = 32
CMP_STRIDE = 16
CMP_HIDDEN = 128
SEL_LEN = 64
SEL_TOP = 16
WINDOW = 512
NSA_QBLOCK = 32
NSA_SPLITS = (NSA_WIDTH,) + (NSA_KV,) * 6 + (3 * NSA_HEADS,)
NSA_COLS = sum(NSA_SPLITS)
EV_COLS = RW_COLS + NSA_COLS

HG_HEAD = 128
HG_HEADS = D_MODEL // HG_HEAD
HG_CHUNK = 16
HG_SPLITS = (D_MODEL, D_MODEL, D_MODEL, D_MODEL)

MOE_GROUPS = 4
MOE_PER_GROUP = 8
MOE_EXPERTS = MOE_GROUPS * MOE_PER_GROUP
MOE_TOPK = 2
MOE_HIDDEN = 512
MOE_BLOCK = 256

LANES = 128
VMEM_LIMIT_BYTES = 56 * 1024 * 1024


def _round_up(n, m):
    return (n + m - 1) // m * m


def _dense_kernel(x_ref, w_ref, o_ref):
    o_ref[...] = jnp.dot(x_ref[...].astype(jnp.bfloat16), w_ref[...], preferred_element_type=jnp.float32)


def _dense(x2d, w, tm=512, keep_pad=False):
    m, k = x2d.shape
    n = w.shape[1]
    n_pad = _round_up(n, LANES)
    wb = w.astype(jnp.bfloat16)
    if n_pad != n:
        wb = jnp.pad(wb, ((0, 0), (0, n_pad - n)))
    tm = min(tm, m)
    assert m % tm == 0
    out = pl.pallas_call(
        _dense_kernel,
        grid=(m // tm,),
        in_specs=[pl.BlockSpec((tm, k), lambda i: (i, 0)), pl.BlockSpec((k, n_pad), lambda i: (0, 0))],
        out_specs=pl.BlockSpec((tm, n_pad), lambda i: (i, 0)),
        out_shape=jax.ShapeDtypeStruct((m, n_pad), jnp.float32),
        compiler_params=pltpu.CompilerParams(dimension_semantics=("arbitrary",), vmem_limit_bytes=VMEM_LIMIT_BYTES),
        name="dense",
    )(x2d, wb)
    return out if keep_pad or n_pad == n else out[:, :n]


def _layer_norm(x, g, b):
    xc = x - jnp.mean(x, -1, keepdims=True)
    var = jnp.mean(xc * xc, -1, keepdims=True)
    return xc * lax.rsqrt(var + LN_EPS) * g + b


def _rope_tables(positions, dim):
    inv = (1.0 / (ROPE_THETA ** (np.arange(0, dim, 2, dtype=np.float32) / dim))).astype(np.float32)
    ang = positions.astype(jnp.float32)[..., None] * inv
    return jnp.cos(ang)[:, :, None, :], jnp.sin(ang)[:, :, None, :]


RW_CHUNK = 64
RW_STEP = 128


def _split3(x):
    bf16, f32 = jnp.bfloat16, jnp.float32
    h1 = x.astype(bf16)
    r1 = x - h1.astype(f32)
    h2 = r1.astype(bf16)
    return h1, h2, (r1 - h2.astype(f32)).astype(bf16)


def _rwkv7_kernel(z_ref, mu_ref, vec_ref, w2_ref, a2_ref, g2_ref, bd_ref, o_ref, carry_ref, h_ref):
    f32, bf16 = jnp.float32, jnp.bfloat16
    S = z_ref.shape[1]
    W, N, H, C = RW_WIDTH, RW_HEAD, RW_HEADS, RW_CHUNK
    n_sub = S // C
    lora_w = RW_DECAY_LORA + RW_AAA_LORA

    @pl.when(pl.program_id(1) == 0)
    def _():
        carry_ref[...] = jnp.zeros_like(carry_ref)
        h_ref[...] = jnp.zeros_like(h_ref)

    dotf = lambda a, b: jnp.dot(a, b, preferred_element_type=f32)
    bd = bd_ref[...]
    head_sum = lambda x: sum(dotf(t, bd) for t in _split3(x))

    z = z_ref[0]
    z_prev = jnp.concatenate([carry_ref[0:1, :], z[:S - 1, :]], axis=0)
    carry_ref[0:1, :] = z[S - 1:S, :]
    zs = z + mu_ref[...] * (z_prev - z)
    r, k, v = zs[:, 0:W], zs[:, W:2 * W], zs[:, 2 * W:3 * W]
    lora = zs[:, 3 * W:3 * W + lora_w]
    gd = zs[:, 3 * W + lora_w:]
    w0, a0, k_k, k_a, r_k, gn_g, gn_b = (vec_ref[i:i + 1, :] for i in range(7))
    w_pre = -(w0 + dotf(jnp.tanh(lora).astype(bf16), w2_ref[...]))
    softplus = jnp.maximum(w_pre, 0.0) + jnp.log(1.0 + jnp.exp(-jnp.abs(w_pre)))
    lw = -jnp.exp(-softplus - 0.5)
    a = jax.nn.sigmoid(a0 + dotf(lora.astype(bf16), a2_ref[...]))
    g = dotf(jax.nn.sigmoid(gd).astype(bf16), g2_ref[...])
    kk = k * k_k
    kk = kk / jnp.maximum(jnp.sqrt(head_sum(kk * kk)), 1e-12)
    k = k * (1.0 + (a - 1.0) * k_a)
    b = a * kk

    row = lax.broadcasted_iota(jnp.int32, (S, S), 0)
    col = lax.broadcasted_iota(jnp.int32, (S, S), 1)
    same = lax.shift_right_logical(row, int(np.log2(C))) == lax.shift_right_logical(col, int(np.log2(C)))
    incl = same & (row >= col)
    strict = same & (row > col)
    tri = jnp.where(incl, 1.0, 0.0).astype(bf16)
    cs = sum(dotf(tri, t) for t in _split3(lw))
    e_neg = jnp.exp(-cs)

    stack = lambda x: jnp.stack([x[:, h * N:(h + 1) * N] for h in range(H)])

    def stack_t(x):
        parts = []
        for j in range(W // LANES):
            t = x[:, j * LANES:(j + 1) * LANES].T
            parts += [t[i * N:(i + 1) * N] for i in range(LANES // N)]
        return jnp.stack(parts)

    kks = stack(kk * jnp.exp(cs - lw))
    rs = stack(r * jnp.exp(cs))
    vh = stack(v).astype(bf16)
    bsT = stack_t(b * e_neg)
    ksT = stack_t(k * e_neg)
    csT = stack_t(cs)

    def bmm(x, y):
        return jnp.einsum('hij,hjk->hik', x.astype(bf16), y.astype(bf16), preferred_element_type=f32)

    lhs = jnp.concatenate([kks, rs], axis=1)
    mb = bmm(lhs, bsT)
    mk = bmm(lhs, ksT)
    a_b = jnp.where(strict, mb[:, :S], 0.0)
    a_k = jnp.where(strict, mk[:, :S], 0.0)
    q_b = jnp.where(incl, mb[:, S:], 0.0)
    q_k = jnp.where(incl, mk[:, S:], 0.0)

    t_inv = jnp.where(row == col, 1.0, 0.0) - a_b
    pw = a_b
    for _ in range(int(np.log2(C)) - 1):
        pw = bmm(pw, pw)
        t_inv = t_inv + bmm(t_inv, pw)

    w_mat = bmm(t_inv, kks)
    u_loc = -bmm(t_inv, bmm(a_k, vh))
    q_eff = rs - bmm(q_b, w_mat)
    o_loc = bmm(q_b, u_loc) + bmm(q_k, vh)

    eye_n = jnp.where(lax.broadcasted_iota(jnp.int32, (N, N), 0) == lax.broadcasted_iota(jnp.int32, (N, N), 1), 1.0, 0.0)
    lane_chunk = lax.shift_right_logical(lax.broadcasted_iota(jnp.int32, (N, S), 1), int(np.log2(C)))
    hc = h_ref[...]
    outs = []
    for c in range(n_sub):
        in_c = lane_chunk == c
        bs_c = jnp.where(in_c, bsT, 0.0)
        ks_c = jnp.where(in_c, ksT, 0.0)
        g_end = jnp.exp(csT[:, :, (c + 1) * C - 1:(c + 1) * C])
        g_mat = g_end * (eye_n - bmm(bs_c, w_mat))
        h_loc = g_end * (bmm(bs_c, u_loc) + bmm(ks_c, vh))
        sl = slice(c * C, (c + 1) * C)
        outs.append(bmm(q_eff[:, sl], hc) + o_loc[:, sl])
        hc = bmm(g_mat, hc) + h_loc
    h_ref[...] = hc

    o = jnp.concatenate(outs, axis=1)
    oc = o - jnp.mean(o, axis=-1, keepdims=True)
    o = oc * lax.rsqrt(jnp.mean(oc * oc, axis=-1, keepdims=True) + RW_GN_EPS)
    o = jnp.concatenate([o[h] for h in range(H)], axis=1)
    bonus = head_sum(r * k * r_k) * v
    o_ref[0] = (o * gn_g + gn_b + bonus) * g


def _rwkv7_branch(z, mu, w0, w2, a0, a2, g2, k_k, k_a, r_k, gn_g, gn_b):
    B, T, _ = z.shape
    f32, bf16 = jnp.float32, jnp.bfloat16
    S = min(RW_STEP, T)
    assert S % RW_CHUNK == 0 and T % S == 0 and RW_DECAY_LORA + RW_AAA_LORA == LANES == RW_GATE_LORA
    W = RW_WIDTH
    vec = jnp.stack([w0, a0, k_k, k_a, r_k.reshape(W), gn_g, gn_b, jnp.zeros((W,), f32)])
    w2p = jnp.concatenate([w2, jnp.zeros((RW_AAA_LORA, W), f32)]).astype(bf16)
    a2p = jnp.concatenate([jnp.zeros((RW_DECAY_LORA, W), f32), a2]).astype(bf16)
    head_of = np.arange(W) // RW_HEAD
    bd = jnp.asarray(head_of[:, None] == head_of[None, :], dtype=bf16)
    full = lambda shape: pl.BlockSpec(shape, lambda bi, ci: (0,) * len(shape))
    return pl.pallas_call(
        _rwkv7_kernel,
        grid=(B, T // S),
        in_specs=[pl.BlockSpec((1, S, RW_COLS), lambda bi, ci: (bi, ci, 0)), full((1, RW_COLS)), full((8, W)),
                  full((LANES, W)), full((LANES, W)), full((LANES, W)), full((W, W))],
        out_specs=pl.BlockSpec((1, S, W), lambda bi, ci: (bi, ci, 0)),
        out_shape=jax.ShapeDtypeStruct((B, T, W), f32),
        scratch_shapes=[pltpu.VMEM((8, RW_COLS), f32), pltpu.VMEM((RW_HEADS, RW_HEAD, RW_HEAD), f32)],
        compiler_params=pltpu.CompilerParams(dimension_semantics=("arbitrary", "arbitrary"),
                                             vmem_limit_bytes=VMEM_LIMIT_BYTES),
        name="rwkv7",
    )(z, mu.reshape(1, RW_COLS), vec, w2p, a2p, g2.astype(bf16), bd)


def _cmp_sel_overlap(n_cmp, n_sel):
    cs = np.arange(n_cmp)[:, None] * CMP_STRIDE
    ss = np.arange(n_sel)[None, :] * SEL_LEN
    ov = np.clip(np.minimum(cs + CMP_LEN, ss + SEL_LEN) - np.maximum(cs, ss), 0, None)
    return (ov / CMP_LEN).astype(np.float32)


NSA_TQ = 256
NSA_TK_SEL = 512
NSA_TK_WIN = 256
NEG_INIT = -1e30
LOG2E = float(np.log2(np.e))


def _dot_bf16x3(a, b):
    bf16, f32 = jnp.bfloat16, jnp.float32
    ah, bh = a.astype(bf16), b.astype(bf16)
    al, bl = (a - ah.astype(f32)).astype(bf16), (b - bh.astype(f32)).astype(bf16)
    d = lambda x, y: jnp.dot(x, y, preferred_element_type=f32)
    return d(ah, bh) + d(ah, bl) + d(al, bh)


def _nsa_attn_kernel(qraw_ref, qrot_ref, kcmp_ref, vcmpT_ref, ovT_ref, ksel_ref, vselT_ref, kwin_ref, vwinT_ref,
                     gl_ref, o_ref, lim_ref, key_ref, *, tq, n_top):
    f32, bf16 = jnp.float32, jnp.bfloat16
    G = NSA_GROUP
    R = G * tq
    t0 = pl.program_id(2) * tq
    ncp = kcmp_ref.shape[2]
    n_sel = ovT_ref.shape[0]

    qraw = jnp.concatenate([qraw_ref[0, g] for g in range(G)], axis=1)
    qrot = jnp.concatenate([qrot_ref[0, g] for g in range(G)], axis=1)

    s_c = _dot_bf16x3(kcmp_ref[0, 0], qraw)
    t_tok = t0 + lax.broadcasted_iota(jnp.int32, (1, tq), 1)
    cmp_last = lax.broadcasted_iota(jnp.int32, (ncp, tq), 0) * CMP_STRIDE + (CMP_LEN - 1)
    s_c = s_c + jnp.concatenate([jnp.where(cmp_last <= t_tok, 0.0, -jnp.inf)] * G, axis=1)
    m_c = jnp.max(s_c, axis=0, keepdims=True)
    m_c = jnp.where(m_c == -jnp.inf, 0.0, m_c)
    e_c = jnp.exp(s_c - m_c)
    p_c = e_c / jnp.maximum(jnp.sum(e_c, axis=0, keepdims=True), 1e-30)
    o_c = jnp.dot(vcmpT_ref[0, 0], p_c.astype(bf16), preferred_element_type=f32)

    p_sum = p_c[:, 0:tq]
    for g in range(1, G):
        p_sum = p_sum + p_c[:, g * tq:(g + 1) * tq]
    imp = _dot_bf16x3(ovT_ref[...], p_sum)
    j_iota = lax.broadcasted_iota(jnp.int32, (n_sel, tq), 0)
    cur = lax.shift_right_logical(t_tok, int(np.log2(SEL_LEN)))
    forced = (j_iota == 0) | (j_iota == cur) | (j_iota == cur - 1)
    cand = (j_iota >= 1) & (j_iota <= cur - 2)
    quota = n_top - 1 - jnp.minimum(cur, 2)
    key = jnp.where(cand, pltpu.bitcast(imp, jnp.int32), -1)
    key_ref[...] = key

    def rank_body(i, rank):
        row = key_ref[pl.ds(i, 1), :]
        return rank + jnp.where(row + jnp.where(j_iota > i, 1, 0) > key, 1, 0)

    i_end = jnp.maximum(lax.shift_right_logical(t0 + tq - 1, int(np.log2(SEL_LEN))) - 1, 1)
    rank = lax.fori_loop(1, i_end, rank_body, jnp.zeros((n_sel, tq), jnp.int32))
    sel = forced | (cand & (rank < quota))
    lim_ref[...] = jnp.where(sel, t_tok, -1)

    def online_update(carry, s, mask_bias, vT):
        m, l, acc = carry
        s = s + jnp.concatenate([mask_bias] * G, axis=1)
        m_new = jnp.maximum(m, jnp.max(s, axis=0, keepdims=True))
        alpha = jnp.exp2(m - m_new)
        p = jnp.exp2(s - m_new)
        l = alpha * l + jnp.sum(p, axis=0, keepdims=True)
        acc = alpha * acc + jnp.dot(vT, p.astype(bf16), preferred_element_type=f32)
        return m_new, l, acc

    init = (jnp.full((1, R), NEG_INIT, f32), jnp.zeros((1, R), f32), jnp.zeros((NSA_HEAD, R), f32))

    tk = min(NSA_TK_SEL, ksel_ref.shape[2])
    nb = tk // SEL_LEN
    key_iota = lax.broadcasted_iota(jnp.int32, (tk, tq), 0)

    def sel_body(kt, carry):
        k0 = pl.multiple_of(kt * tk, tk)
        s = jnp.dot(ksel_ref[0, 0, pl.ds(k0, tk), :], qrot, preferred_element_type=f32)
        limb = lim_ref[pl.ds(pl.multiple_of(kt * nb, nb), nb), :] - k0
        lim_t = jnp.concatenate([jnp.broadcast_to(limb[jb:jb + 1, :], (SEL_LEN, tq)) for jb in range(nb)], axis=0)
        bias = jnp.where(key_iota <= lim_t, 0.0, -jnp.inf)
        return online_update(carry, s, bias, vselT_ref[0, 0, :, pl.ds(k0, tk)])

    n_kt = lax.div(t0 + tq - 1, tk) + 1
    _, l_s, acc_s = lax.fori_loop(0, n_kt, sel_body, init)
    o_s = acc_s / jnp.maximum(l_s, 1e-30)

    tkw = min(NSA_TK_WIN, kwin_ref.shape[2])
    key_iota_w = lax.broadcasted_iota(jnp.int32, (tkw, tq), 0)

    def win_body(kt, carry):
        k0 = pl.multiple_of(kt * tkw, tkw)
        s = jnp.dot(kwin_ref[0, 0, pl.ds(k0, tkw), :], qrot, preferred_element_type=f32)
        rel = t_tok - k0
        bias = jnp.where((key_iota_w <= rel) & (key_iota_w > rel - WINDOW), 0.0, -jnp.inf)
        return online_update(carry, s, bias, vwinT_ref[0, 0, :, pl.ds(k0, tkw)])

    kt_lo = lax.div(jnp.maximum(t0 - (WINDOW - 1), 0), tkw)
    kt_hi = lax.div(t0 + tq - 1, tkw) + 1
    _, l_w, acc_w = lax.fori_loop(kt_lo, kt_hi, win_body, init)
    o_w = acc_w / jnp.maximum(l_w, 1e-30)

    gl = jnp.concatenate([gl_ref[0, g] for g in range(G)], axis=1)
    gates = jax.nn.sigmoid(gl)
    out = gates[0:1, :] * o_c + gates[1:2, :] * o_s + gates[2:3, :] * o_w
    for g in range(G):
        o_ref[0, g] = out[:, g * tq:(g + 1) * tq]


NSA_PREP_T = 256
NSA_COL0 = RW_COLS


def _nsa_prep_kernel(qa_ref, qb_ref, ks_ref, kw_ref, gl_ref, cos_ref, sin_ref,
                     qraw_ref, qrot_ref, ksel_ref, vselT_ref, kwin_ref, vwinT_ref, glT_ref):
    bf16 = jnp.bfloat16
    d, half = NSA_HEAD, NSA_HEAD // 2
    scale = NSA_HEAD ** -0.5
    cos, sin = cos_ref[0], sin_ref[0]
    first = (lax.broadcasted_iota(jnp.int32, cos.shape, 1) & (d - 1)) < half

    def rope(x):
        partner = jnp.where(first, pltpu.roll(x, LANES - half, 1), pltpu.roll(x, half, 1))
        return x * cos + jnp.where(first, -partner, partner) * sin

    def put_heads_t(ref, first_head, x_t, dtype):
        for i in range(LANES // d):
            ref[0, first_head + i] = x_t[i * d:(i + 1) * d].astype(dtype)

    for j in range(NSA_HEADS * d // LANES):
        src = qa_ref if j < 2 else qb_ref
        piece = src[0][:, (j % 2) * LANES:(j % 2 + 1) * LANES]
        put_heads_t(qraw_ref, 2 * j, (piece * scale).T, jnp.float32)
        put_heads_t(qrot_ref, 2 * j, (rope(piece) * (scale * LOG2E)).T, bf16)
    for src, k_out, vT_out in ((ks_ref, ksel_ref, vselT_ref), (kw_ref, kwin_ref, vwinT_ref)):
        kv = src[0]
        kr = rope(kv[:, :LANES])
        for h in range(NSA_KV_HEADS):
            k_out[0, h] = kr[:, h * d:(h + 1) * d].astype(bf16)
        put_heads_t(vT_out, 0, kv[:, LANES:].T, bf16)
    gl_t = gl_ref[0].T
    for h in range(NSA_HEADS):
        glT_ref[0, h] = gl_t[3 * h:3 * h + 3]


def _nsa_prep(z, cos, sin):
    B, T, _ = z.shape
    f32, bf16 = jnp.float32, jnp.bfloat16
    t = min(NSA_PREP_T, T)
    H, Hkv, d = NSA_HEADS, NSA_KV_HEADS, NSA_HEAD
    assert NSA_COL0 % 256 == 0 and NSA_WIDTH == 512 and NSA_KV == LANES and (NSA_COL0 + NSA_WIDTH + 6 * NSA_KV) % LANES == 0
    c0 = NSA_COL0 // 256
    wide = lambda j: pl.BlockSpec((1, t, 256), lambda b, i: (b, i, c0 + j))
    gl_col = (NSA_COL0 + NSA_WIDTH + 6 * NSA_KV) // LANES
    tab = pl.BlockSpec((1, t, LANES), lambda b, i: (b, i, 0))
    tile4 = lambda c: jnp.tile(c.reshape(B, T, d // 2), (1, 1, LANES // (d // 2)))
    q_out = pl.BlockSpec((1, H, d, t), lambda b, i: (b, 0, 0, i))
    k_out = pl.BlockSpec((1, Hkv, t, d), lambda b, i: (b, 0, i, 0))
    vT_out = pl.BlockSpec((1, Hkv, d, t), lambda b, i: (b, 0, 0, i))
    return pl.pallas_call(
        _nsa_prep_kernel,
        grid=(B, T // t),
        in_specs=[wide(0), wide(1), wide(3), wide(4), pl.BlockSpec((1, t, LANES), lambda b, i: (b, i, gl_col)), tab, tab],
        out_specs=[q_out, q_out, k_out, vT_out, k_out, vT_out, pl.BlockSpec((1, H, 3, t), lambda b, i: (b, 0, 0, i))],
        out_shape=[jax.ShapeDtypeStruct((B, H, d, T), f32), jax.ShapeDtypeStruct((B, H, d, T), bf16),
                   jax.ShapeDtypeStruct((B, Hkv, T, d), bf16), jax.ShapeDtypeStruct((B, Hkv, d, T), bf16),
                   jax.ShapeDtypeStruct((B, Hkv, T, d), bf16), jax.ShapeDtypeStruct((B, Hkv, d, T), bf16),
                   jax.ShapeDtypeStruct((B, H, 3, T), f32)],
        compiler_params=pltpu.CompilerParams(dimension_semantics=("arbitrary", "arbitrary"),
                                             vmem_limit_bytes=VMEM_LIMIT_BYTES),
        name="nsa_prep",
    )(z, z, z, z, z, tile4(cos), tile4(sin))


def _nsa_group(z, cos, sin, cmp_pos, cmp_w1, cmp_w2):
    B, T, _ = z.shape
    f32, bf16 = jnp.float32, jnp.bfloat16
    qraw_T, qrot_T, ksel, vselT, kwin, vwinT, gl_T = _nsa_prep(z, cos, sin)
    kc = z[..., NSA_COL0 + NSA_WIDTH:NSA_COL0 + NSA_WIDTH + NSA_KV]
    vc = z[..., NSA_COL0 + NSA_WIDTH + NSA_KV:NSA_COL0 + NSA_WIDTH + 2 * NSA_KV]
    kv_heads = lambda t: t.reshape(B, T, NSA_KV_HEADS, NSA_HEAD)

    n_grp = T // CMP_STRIDE
    n_cmp = n_grp - 1
    half = CMP_STRIDE * NSA_HEAD
    grp = jnp.stack([kv_heads(kc), kv_heads(vc)]).transpose(0, 1, 3, 2, 4).reshape(2, B * NSA_KV_HEADS * n_grp, half)
    cmp_kv = []
    for zi in range(2):
        w1ab = jnp.concatenate([cmp_w1[zi, :half], cmp_w1[zi, half:]], axis=1)
        ab = _dense(grp[zi], w1ab).reshape(B, NSA_KV_HEADS, n_grp, 2 * CMP_HIDDEN)
        c = cmp_pos[zi].reshape(1, CMP_LEN * NSA_HEAD) @ cmp_w1[zi]
        hid = jax.nn.gelu(ab[:, :, :-1, :CMP_HIDDEN] + ab[:, :, 1:, CMP_HIDDEN:] + c)
        cmp_kv.append(jnp.pad(hid @ cmp_w2[zi], ((0, 0), (0, 0), (0, 1), (0, 0))))
    k_cmp = cmp_kv[0]
    v_cmpT = cmp_kv[1].transpose(0, 1, 3, 2).astype(bf16)

    n_sel = T // SEL_LEN
    n_top = min(SEL_TOP, n_sel)
    ovT = jnp.asarray(np.pad(_cmp_sel_overlap(n_cmp, n_sel), ((0, 1), (0, 0))).T)
    tq = min(NSA_TQ, T)
    G = NSA_GROUP
    q_spec = pl.BlockSpec((1, G, NSA_HEAD, tq), lambda b, h, i: (b, h, 0, i))
    kv_spec = lambda shape: pl.BlockSpec((1, 1) + shape, lambda b, h, i: (b, h, 0, 0))
    out = pl.pallas_call(
        functools.partial(_nsa_attn_kernel, tq=tq, n_top=n_top),
        grid=(B, NSA_KV_HEADS, T // tq),
        in_specs=[q_spec, q_spec, kv_spec((n_grp, NSA_HEAD)), kv_spec((NSA_HEAD, n_grp)),
                  pl.BlockSpec((n_sel, n_grp), lambda b, h, i: (0, 0)),
                  kv_spec((T, NSA_HEAD)), kv_spec((NSA_HEAD, T)), kv_spec((T, NSA_HEAD)), kv_spec((NSA_HEAD, T)),
                  pl.BlockSpec((1, G, 3, tq), lambda b, h, i: (b, h, 0, i))],
        out_specs=q_spec,
        out_shape=jax.ShapeDtypeStruct((B, NSA_HEADS, NSA_HEAD, T), f32),
        scratch_shapes=[pltpu.VMEM((n_sel, tq), jnp.int32), pltpu.VMEM((n_sel, tq), jnp.int32)],
        compiler_params=pltpu.CompilerParams(dimension_semantics=("arbitrary", "arbitrary", "arbitrary"),
                                             vmem_limit_bytes=VMEM_LIMIT_BYTES),
        name="nsa_attn",
    )(qraw_T, qrot_T, k_cmp, v_cmpT, ovT, ksel, vselT, kwin, vwinT, gl_T)
    return out.transpose(0, 3, 1, 2).reshape(B, T, NSA_WIDTH)


HG_STEP = 256
HG_HEADS_PER_STEP = 8


def _hgrn2_kernel(q_ref, f_ref, i_ref, g_ref, lb_ref, ng_ref, o_ref, st_ref):
    f32, bf16 = jnp.float32, jnp.bfloat16
    S, d = q_ref.shape[1], HG_HEAD
    hb = q_ref.shape[2] // d
    C = HG_CHUNK
    n_sub = S // C
    shift = int(np.log2(C))

    @pl.when(pl.program_id(2) == 0)
    def _():
        st_ref[...] = jnp.zeros_like(st_ref)

    heads = lambda ref: jnp.stack([ref[0][:, h * d:(h + 1) * d] for h in range(hb)])
    lb = lb_ref[...]
    f = heads(f_ref)
    logf = jnp.log(lb + (1.0 - lb) * jax.nn.sigmoid(f))
    k = (1.0 - lb) * jax.nn.sigmoid(-f)
    qs = jax.nn.silu(heads(q_ref))
    v = heads(i_ref)

    row = lax.broadcasted_iota(jnp.int32, (S, S), 0)
    col = lax.broadcasted_iota(jnp.int32, (S, S), 1)
    same = lax.shift_right_logical(row, shift) == lax.shift_right_logical(col, shift)
    causal = same & (row >= col)

    def cumsum01(mask, x):
        m = jnp.broadcast_to(jnp.where(mask, 1.0, 0.0).astype(bf16), (hb, S, S))
        h1 = x.astype(bf16)
        r1 = x - h1.astype(f32)
        h2 = r1.astype(bf16)
        h3 = (r1 - h2.astype(f32)).astype(bf16)
        return sum(jnp.einsum('hij,hjk->hik', m, t, preferred_element_type=f32) for t in (h1, h2, h3))

    b = cumsum01(causal, logf)
    tail = cumsum01(same & (col > row), logf)
    q_e = (qs * jnp.exp(b)).astype(bf16)
    k_e = (k * jnp.exp(-b)).astype(bf16)
    k_tail = k * jnp.exp(tail)
    vb = v.astype(bf16)

    a = jnp.einsum('hsd,htd->hst', q_e, k_e, preferred_element_type=f32)
    o_intra = jnp.einsum('hst,htv->hsv', jnp.where(causal, a, 0.0).astype(bf16), vb, preferred_element_type=f32)

    vT = jnp.stack([v[h].T for h in range(hb)]).astype(bf16)
    row_chunk = lax.shift_right_logical(lax.broadcasted_iota(jnp.int32, (S, d), 0), shift)
    st = st_ref[...]
    o_inter = []
    for c in range(n_sub):
        sl = slice(c * C, (c + 1) * C)
        o_inter.append(jnp.einsum('hcd,hvd->hcv', q_e[:, sl], st.astype(bf16), preferred_element_type=f32))
        k_c = jnp.where(row_chunk == c, k_tail, 0.0).astype(bf16)
        d_c = jnp.exp(b[:, (c + 1) * C - 1:(c + 1) * C, :])
        st = st * d_c + jnp.einsum('hvs,hsk->hvk', vT, k_c, preferred_element_type=f32)
    st_ref[...] = st
    o = o_intra + jnp.concatenate(o_inter, axis=1)
    o = o * lax.rsqrt(jnp.mean(o * o, axis=-1, keepdims=True) + LN_EPS) * ng_ref[...]
    g = heads(g_ref)
    o = o * jax.nn.silu(g)
    for h in range(hb):
        o_ref[0, :, h * d:(h + 1) * d] = o[h]


def _hgrn2_mixer(x, w_in, lb, norm_g):
    B, T, D = x.shape
    z = _dense(x.reshape(B * T, D), w_in).reshape(B, T, -1)
    S = min(HG_STEP, T)
    H, d = HG_HEADS, HG_HEAD
    hb = HG_HEADS_PER_STEP
    ng = H // hb
    col = lambda j: pl.BlockSpec((1, S, hb * d), lambda b, h, t: (b, t, j * ng + h))
    vec = pl.BlockSpec((hb, 1, d), lambda b, h, t: (h, 0, 0))
    o = pl.pallas_call(
        _hgrn2_kernel,
        grid=(B, ng, T // S),
        in_specs=[col(0), col(1), col(2), col(3), vec, pl.BlockSpec((1, 1, d), lambda b, h, t: (0, 0, 0))],
        out_specs=pl.BlockSpec((1, S, hb * d), lambda b, h, t: (b, t, h)),
        out_shape=jax.ShapeDtypeStruct((B, T, D), jnp.float32),
        scratch_shapes=[pltpu.VMEM((hb, d, d), jnp.float32)],
        compiler_params=pltpu.CompilerParams(dimension_semantics=("arbitrary", "arbitrary", "arbitrary"),
                                             vmem_limit_bytes=VMEM_LIMIT_BYTES),
        name="hgrn2",
    )(z, z, z, z, lb.reshape(H, 1, d), norm_g.reshape(1, 1, d))
    return o.reshape(B * T, D)


def _moe_expert_kernel(blk_e_ref, x_ref, w1_ref, w3_ref, w2_ref, o_ref):
    del blk_e_ref
    f32, bf16 = jnp.float32, jnp.bfloat16
    x = x_ref[...].astype(bf16)
    a = jnp.dot(x, w1_ref[0], preferred_element_type=f32)
    b = jnp.dot(x, w3_ref[0], preferred_element_type=f32)
    hid = (jax.nn.silu(a) * b).astype(bf16)
    o_ref[...] = jnp.dot(hid, w2_ref[0], preferred_element_type=f32)


def _moe_experts(xbuf, blk_e, w1, w3, w2):
    P, D = xbuf.shape
    hid = w1.shape[-1]
    bf16 = jnp.bfloat16
    w_spec = lambda shape: pl.BlockSpec((1,) + shape, lambda i, be: (be[i], 0, 0))
    x_spec = pl.BlockSpec((MOE_BLOCK, D), lambda i, be: (i, 0))
    return pl.pallas_call(
        _moe_expert_kernel,
        grid_spec=pltpu.PrefetchScalarGridSpec(
            num_scalar_prefetch=1,
            grid=(P // MOE_BLOCK,),
            in_specs=[x_spec, w_spec((D, hid)), w_spec((D, hid)), w_spec((hid, D))],
            out_specs=x_spec,
        ),
        out_shape=jax.ShapeDtypeStruct((P, D), jnp.float32),
        compiler_params=pltpu.CompilerParams(dimension_semantics=("arbitrary",), vmem_limit_bytes=VMEM_LIMIT_BYTES),
        name="moe_experts",
    )(blk_e, xbuf, w1.astype(bf16), w3.astype(bf16), w2.astype(bf16))


MOE_ROUTE_TM = 512
MOE_ROW_TM = 256
MOE_ROUTE_ROWS = 40


def _moe_route_kernel(h_ref, wr_ref, br_ref, eid_ref, gate_ref, pos_ref, cnt_ref, run_ref):
    f32 = jnp.float32
    tm = h_ref.shape[0]
    E, PG, NG = MOE_EXPERTS, MOE_PER_GROUP, MOE_GROUPS

    @pl.when(pl.program_id(0) == 0)
    def _():
        run_ref[...] = jnp.zeros_like(run_ref)

    lg = lax.dot_general(wr_ref[...], h_ref[...], (((1,), (1,)), ((), ())), precision=lax.Precision.HIGHEST,
                         preferred_element_type=f32) + br_ref[...]
    grp = lg[E:E + NG]
    g_iota = lax.broadcasted_iota(jnp.int32, (NG, tm), 0)
    g_max = jnp.max(grp, axis=0, keepdims=True)
    g_sel = jnp.min(jnp.where(grp == g_max, g_iota, NG), axis=0, keepdims=True)
    p_grp = 1.0 / jnp.sum(jnp.exp(grp - g_max), axis=0, keepdims=True)
    le = lg[0:PG]
    for g in range(1, NG):
        le = jnp.where(g_sel == g, lg[g * PG:(g + 1) * PG], le)
    e_iota = lax.broadcasted_iota(jnp.int32, (PG, tm), 0)
    m1 = jnp.max(le, axis=0, keepdims=True)
    i1 = jnp.min(jnp.where(le == m1, e_iota, PG), axis=0, keepdims=True)
    le2 = jnp.where(e_iota == i1, -jnp.inf, le)
    m2 = jnp.max(le2, axis=0, keepdims=True)
    i2 = jnp.min(jnp.where(le2 == m2, e_iota, PG), axis=0, keepdims=True)
    e2 = jnp.exp(m2 - m1)
    den = 1.0 + e2
    eid1 = g_sel * PG + i1
    eid2 = g_sel * PG + i2
    eid_ref[...] = jnp.concatenate([eid1, eid2], axis=0)
    gate_ref[...] = jnp.concatenate([p_grp / den, p_grp * e2 / den], axis=0)

    x_iota = lax.broadcasted_iota(jnp.int32, (E, tm), 0)
    oh1 = jnp.where(x_iota == eid1, 1.0, 0.0)
    oh2 = jnp.where(x_iota == eid2, 1.0, 0.0)
    before = jnp.where(lax.broadcasted_iota(jnp.int32, (tm, tm), 0) < lax.broadcasted_iota(jnp.int32, (tm, tm), 1),
                       1.0, 0.0).astype(jnp.bfloat16)
    cum1 = jnp.dot(oh1.astype(jnp.bfloat16), before, preferred_element_type=f32)
    cum2 = jnp.dot(oh2.astype(jnp.bfloat16), before, preferred_element_type=f32)
    tot1 = jnp.sum(oh1, axis=1, keepdims=True)
    tot2 = jnp.sum(oh2, axis=1, keepdims=True)
    base = run_ref[:, 0:1]
    pos1 = jnp.sum(oh1 * (cum1 + base), axis=0, keepdims=True)
    pos2 = jnp.sum(oh2 * (cum2 + base + tot1), axis=0, keepdims=True)
    pos_ref[...] = jnp.concatenate([pos1, pos2], axis=0).astype(jnp.int32)
    new = jnp.broadcast_to(base + tot1 + tot2, run_ref.shape)
    run_ref[...] = new
    cnt_ref[...] = new


def _moe_scatter_kernel(dest_ref, h_ref, xin_ref, xbuf_ref, sem):
    del xin_ref
    tm = h_ref.shape[0]

    def row_copy(r, j):
        return pltpu.make_async_copy(h_ref.at[pl.ds(r, 1)], xbuf_ref.at[pl.ds(dest_ref[j, r], 1)], sem)

    def issue(r, c):
        row_copy(r, 0).start()
        row_copy(r, 1).start()
        return c

    lax.fori_loop(0, tm, issue, 0, unroll=8)

    def drain(r, c):
        row_copy(r, 0).wait()
        row_copy(r, 1).wait()
        return c

    lax.fori_loop(0, tm, drain, 0, unroll=8)


def _moe_combine_ln_kernel(dest_ref, gate_ref, h_ref, lng_ref, lnb_ref, ybuf_ref, o_ref, buf_ref, sem):
    tm = h_ref.shape[0]

    def row_copy(r, j):
        return pltpu.make_async_copy(ybuf_ref.at[pl.ds(dest_ref[j, r], 1)], buf_ref.at[j, pl.ds(r, 1)], sem)

    def issue(r, c):
        row_copy(r, 0).start()
        row_copy(r, 1).start()
        return c

    lax.fori_loop(0, tm, issue, 0, unroll=8)

    def drain(r, c):
        row_copy(r, 0).wait()
        row_copy(r, 1).wait()
        return c

    lax.fori_loop(0, tm, drain, 0, unroll=8)
    gate = gate_ref[...]
    ffn = gate[:, 0:1] * buf_ref[0] + gate[:, 1:2] * buf_ref[1]
    o_ref[...] = _layer_norm(DN_ALPHA * h_ref[...] + ffn, lng_ref[...], lnb_ref[...])


def _hier_moe_ln(h, w_rg, b_rg, w_re, b_re, w1, w3, w2, ln_g, ln_b):
    M, D = h.shape
    f32, i32 = jnp.float32, jnp.int32
    E = MOE_EXPERTS
    pad_rows = MOE_ROUTE_ROWS - E - MOE_GROUPS
    wr = jnp.concatenate([w_re.T, w_rg.T, jnp.zeros((pad_rows, D), f32)], axis=0)
    br = jnp.concatenate([b_re, b_rg, jnp.zeros((pad_rows,), f32)]).reshape(MOE_ROUTE_ROWS, 1)
    tm = min(MOE_ROUTE_TM, M)
    slot_spec = pl.BlockSpec((MOE_TOPK, tm), lambda i: (0, i))
    eid, gate, pos, cnt = pl.pallas_call(
        _moe_route_kernel,
        grid=(M // tm,),
        in_specs=[pl.BlockSpec((tm, D), lambda i: (i, 0)), pl.BlockSpec((MOE_ROUTE_ROWS, D), lambda i: (0, 0)),
                  pl.BlockSpec((MOE_ROUTE_ROWS, 1), lambda i: (0, 0))],
        out_specs=[slot_spec, slot_spec, slot_spec, pl.BlockSpec((E, LANES), lambda i: (0, 0))],
        out_shape=[jax.ShapeDtypeStruct((MOE_TOPK, M), i32), jax.ShapeDtypeStruct((MOE_TOPK, M), f32),
                   jax.ShapeDtypeStruct((MOE_TOPK, M), i32), jax.ShapeDtypeStruct((E, LANES), f32)],
        scratch_shapes=[pltpu.VMEM((E, LANES), f32)],
        compiler_params=pltpu.CompilerParams(dimension_semantics=("arbitrary",), vmem_limit_bytes=VMEM_LIMIT_BYTES),
        name="moe_route",
    )(h, wr, br)

    counts = cnt[:, 0].astype(i32)
    padded = (counts + MOE_BLOCK - 1) // MOE_BLOCK * MOE_BLOCK
    ends = jnp.cumsum(padded)
    start = ends - padded
    P = M * MOE_TOPK + E * MOE_BLOCK
    n_blk = P // MOE_BLOCK
    blk_e = jnp.minimum(jnp.sum(ends[None, :] <= (jnp.arange(n_blk) * MOE_BLOCK)[:, None], axis=1), E - 1).astype(i32)
    dest = pos + jnp.sum(jnp.where(eid[:, :, None] == jnp.arange(E), start, 0), axis=-1)

    rt = min(MOE_ROW_TM, M)
    dest_spec = pl.BlockSpec((MOE_TOPK, rt), lambda i: (0, i), memory_space=pltpu.SMEM)
    row_spec = pl.BlockSpec((rt, D), lambda i: (i, 0))
    any_spec = pl.BlockSpec(memory_space=pl.ANY)
    xbuf = pl.pallas_call(
        _moe_scatter_kernel,
        grid=(M // rt,),
        in_specs=[dest_spec, row_spec, any_spec],
        out_specs=any_spec,
        out_shape=jax.ShapeDtypeStruct((P, D), f32),
        scratch_shapes=[pltpu.SemaphoreType.DMA(())],
        input_output_aliases={2: 0},
        compiler_params=pltpu.CompilerParams(dimension_semantics=("arbitrary",)),
        name="moe_scatter",
    )(dest, h, jnp.zeros((P, D), f32))

    ybuf = _moe_experts(xbuf, blk_e, w1, w3, w2)

    vec_spec = pl.BlockSpec((1, D), lambda i: (0, 0))
    return pl.pallas_call(
        _moe_combine_ln_kernel,
        grid=(M // rt,),
        in_specs=[dest_spec, pl.BlockSpec((rt, MOE_TOPK), lambda i: (i, 0)), row_spec, vec_spec, vec_spec, any_spec],
        out_specs=row_spec,
        out_shape=jax.ShapeDtypeStruct((M, D), f32),
        scratch_shapes=[pltpu.VMEM((MOE_TOPK, rt, D), f32), pltpu.SemaphoreType.DMA(())],
        compiler_params=pltpu.CompilerParams(dimension_semantics=("arbitrary",), vmem_limit_bytes=VMEM_LIMIT_BYTES),
        name="moe_combine_ln",
    )(dest, gate.T, h, ln_g.reshape(1, D), ln_b.reshape(1, D), ybuf)


def _dense_res_ln_kernel(*refs):
    n = (len(refs) - 4) // 2
    x_ref, g_ref, b_ref, o_ref = refs[2 * n:]
    mix = DN_ALPHA * x_ref[...]
    for y_ref, w_ref in zip(refs[:n], refs[n:2 * n]):
        mix = mix + jnp.dot(y_ref[...].astype(jnp.bfloat16), w_ref[...], preferred_element_type=jnp.float32)
    o_ref[...] = _layer_norm(mix, g_ref[...], b_ref[...])


def _dense_res_ln(ys, w, x, ln_g, ln_b, tm=512):
    m, d = x.shape
    tm = min(tm, m)
    cuts = np.cumsum([0] + [y.shape[1] for y in ys])
    ws = [w[int(a):int(b)].astype(jnp.bfloat16) for a, b in zip(cuts[:-1], cuts[1:])]
    row_spec = lambda k: pl.BlockSpec((tm, k), lambda i: (i, 0))
    vec_spec = pl.BlockSpec((1, d), lambda i: (0, 0))
    return pl.pallas_call(
        _dense_res_ln_kernel,
        grid=(m // tm,),
        in_specs=[row_spec(y.shape[1]) for y in ys] + [pl.BlockSpec(wi.shape, lambda i: (0, 0)) for wi in ws]
        + [row_spec(d), vec_spec, vec_spec],
        out_specs=row_spec(d),
        out_shape=jax.ShapeDtypeStruct((m, d), jnp.float32),
        compiler_params=pltpu.CompilerParams(dimension_semantics=("arbitrary",), vmem_limit_bytes=VMEM_LIMIT_BYTES),
        name="dense_res_ln",
    )(*ys, *ws, x, ln_g.reshape(1, d), ln_b.reshape(1, d))


def _ple_kernel(h_ref, p_ref, wg_ref, wp_ref, o_ref):
    bf16, f32 = jnp.bfloat16, jnp.float32
    h = h_ref[...]
    gate = jax.nn.sigmoid(jnp.dot(h.astype(bf16), wg_ref[...], preferred_element_type=f32))
    o_ref[...] = h + gate * jnp.dot(p_ref[...].astype(bf16), wp_ref[...], preferred_element_type=f32)


def _ple(h, p, wg, wp, tm=512):
    m, d = h.shape
    kp = p.shape[1]
    tm = min(tm, m)
    return pl.pallas_call(
        _ple_kernel,
        grid=(m // tm,),
        in_specs=[pl.BlockSpec((tm, d), lambda i: (i, 0)), pl.BlockSpec((tm, kp), lambda i: (i, 0)),
                  pl.BlockSpec((d, d), lambda i: (0, 0)), pl.BlockSpec((kp, d), lambda i: (0, 0))],
        out_specs=pl.BlockSpec((tm, d), lambda i: (i, 0)),
        out_shape=jax.ShapeDtypeStruct((m, d), jnp.float32),
        compiler_params=pltpu.CompilerParams(dimension_semantics=("arbitrary",), vmem_limit_bytes=VMEM_LIMIT_BYTES),
        name="ple",
    )(h, p, wg.astype(jnp.bfloat16), wp.astype(jnp.bfloat16))


def kernel(x, p, positions, ev_w_in, ev_w_out, rw_mu, rw_w0, rw_w2, rw_a0, rw_a2, rw_g2, rw_k_k, rw_k_a,
           rw_r_k, rw_gn_g, rw_gn_b, nsa_cmp_pos, nsa_cmp_w1, nsa_cmp_w2, od_w_in, od_w_out, hg_lb, hg_norm_g,
           moe_w_rg, moe_b_rg, moe_w_re, moe_b_re, moe_w1, moe_w3, moe_w2, ln_g, ln_b, ple_w, ple_gate_w):
    B, T, D = x.shape
    M = B * T
    cos, sin = _rope_tables(positions, NSA_HEAD)
    lb_soft = jax.nn.softmax(hg_lb, axis=0)
    lb_all = jnp.cumsum(lb_soft, axis=0) - lb_soft[0:1]
    for li in range(DEPTH):
        j = li // 2
        if li % 2 == 0:
            z = _dense(x.reshape(M, D), ev_w_in[j], keep_pad=True).reshape(B, T, -1)
            y_rw = _rwkv7_branch(z, rw_mu[j], rw_w0[j], rw_w2[j], rw_a0[j], rw_a2[j], rw_g2[j], rw_k_k[j],
                                 rw_k_a[j], rw_r_k[j], rw_gn_g[j], rw_gn_b[j])
            y_nsa = _nsa_group(z, cos, sin, nsa_cmp_pos[j], nsa_cmp_w1[j], nsa_cmp_w2[j])
            ys, w_out = [y_rw.reshape(M, RW_WIDTH), y_nsa.reshape(M, NSA_WIDTH)], ev_w_out[j]
        else:
            ys, w_out = [_hgrn2_mixer(x, od_w_in[j], lb_all[li], hg_norm_g[j])], od_w_out[j]
        h = _dense_res_ln(ys, w_out, x.reshape(M, D), ln_g[li, 0], ln_b[li, 0])
        h = _hier_moe_ln(h, moe_w_rg[li], moe_b_rg[li], moe_w_re[li], moe_b_re[li], moe_w1[li], moe_w3[li],
                         moe_w2[li], ln_g[li, 1], ln_b[li, 1])
        x = _ple(h, p[li].reshape(M, PLE_DIM), ple_gate_w[li], ple_w[li]).reshape(B, T, D)
    return x
```

```python
import functools

import numpy as np
import jax
import jax.numpy as jnp
from jax import lax
from jax.experimental import pallas as pl
from jax.experimental.pallas import tpu as pltpu

D_MODEL = 1024
DEPTH = 2
PLE_DIM = 256
DN_ALPHA = (2 * DEPTH) ** 0.25
LN_EPS = 1e-5
ROPE_THETA = 10000.0

RW_WIDTH = D_MODEL // 2
RW_HEAD = 64
RW_HEADS = RW_WIDTH // RW_HEAD
RW_DECAY_LORA = 64
RW_AAA_LORA = 64
RW_GATE_LORA = 128
RW_GN_EPS = RW_HEAD * 1e-5
RW_SPLITS = (RW_WIDTH, RW_WIDTH, RW_WIDTH, RW_DECAY_LORA, RW_AAA_LORA, RW_GATE_LORA)
RW_COLS = sum(RW_SPLITS)

NSA_WIDTH = D_MODEL - RW_WIDTH
NSA_HEAD = 64
NSA_HEADS = NSA_WIDTH // NSA_HEAD
NSA_KV_HEADS = 2
NSA_GROUP = NSA_HEADS // NSA_KV_HEADS
NSA_KV = NSA_KV_HEADS * NSA_HEAD
CMP_LEN = 32
CMP_STRIDE = 16
CMP_HIDDEN = 128
SEL_LEN = 64
SEL_TOP = 16
WINDOW = 512
NSA_QBLOCK = 32
NSA_SPLITS = (NSA_WIDTH,) + (NSA_KV,) * 6 + (3 * NSA_HEADS,)
NSA_COLS = sum(NSA_SPLITS)
EV_COLS = RW_COLS + NSA_COLS

HG_HEAD = 128
HG_HEADS = D_MODEL // HG_HEAD
HG_CHUNK = 16
HG_SPLITS = (D_MODEL, D_MODEL, D_MODEL, D_MODEL)

MOE_GROUPS = 4
MOE_PER_GROUP = 8
MOE_EXPERTS = MOE_GROUPS * MOE_PER_GROUP
MOE_TOPK = 2
MOE_HIDDEN = 512
MOE_BLOCK = 512

LANES = 128
VMEM_LIMIT_BYTES = 56 * 1024 * 1024


def _round_up(n, m):
    return (n + m - 1) // m * m


def _dense_kernel(x_ref, w_ref, o_ref):
    o_ref[...] = jnp.dot(x_ref[...].astype(jnp.bfloat16), w_ref[...], preferred_element_type=jnp.float32)


def _dense(x2d, w, tm=512, keep_pad=False):
    m, k = x2d.shape
    n = w.shape[1]
    n_pad = _round_up(n, LANES)
    wb = w.astype(jnp.bfloat16)
    if n_pad != n:
        wb = jnp.pad(wb, ((0, 0), (0, n_pad - n)))
    tm = min(tm, m)
    assert m % tm == 0
    out = pl.pallas_call(
        _dense_kernel,
        grid=(m // tm,),
        in_specs=[pl.BlockSpec((tm, k), lambda i: (i, 0)), pl.BlockSpec((k, n_pad), lambda i: (0, 0))],
        out_specs=pl.BlockSpec((tm, n_pad), lambda i: (i, 0)),
        out_shape=jax.ShapeDtypeStruct((m, n_pad), jnp.float32),
        compiler_params=pltpu.CompilerParams(dimension_semantics=("arbitrary",), vmem_limit_bytes=VMEM_LIMIT_BYTES),
        name="dense",
    )(x2d, wb)
    return out if keep_pad or n_pad == n else out[:, :n]


def _sigmoid(x):
    return 1.0 / (1.0 + jnp.exp(-x))


def _layer_norm(x, g, b):
    xc = x - jnp.mean(x, -1, keepdims=True)
    var = jnp.mean(xc * xc, -1, keepdims=True)
    return xc * lax.rsqrt(var + LN_EPS) * g + b


def _rope_tables(positions, dim):
    inv = (1.0 / (ROPE_THETA ** (np.arange(0, dim, 2, dtype=np.float32) / dim))).astype(np.float32)
    ang = positions.astype(jnp.float32)[..., None] * inv
    return jnp.cos(ang)[:, :, None, :], jnp.sin(ang)[:, :, None, :]


RW_CHUNK = 64
RW_STEP = 128


def _split3(x):
    bf16, f32 = jnp.bfloat16, jnp.float32
    h1 = x.astype(bf16)
    r1 = x - h1.astype(f32)
    h2 = r1.astype(bf16)
    return h1, h2, (r1 - h2.astype(f32)).astype(bf16)


def _rwkv7_kernel(z_ref, mu_ref, vec_ref, w2_ref, a2_ref, g2_ref, bd_ref, o_ref, carry_ref, h_ref):
    f32, bf16 = jnp.float32, jnp.bfloat16
    S = z_ref.shape[1]
    W, N, H, C = RW_WIDTH, RW_HEAD, RW_HEADS, RW_CHUNK
    n_sub = S // C
    lora_w = RW_DECAY_LORA + RW_AAA_LORA

    @pl.when(pl.program_id(1) == 0)
    def _():
        carry_ref[...] = jnp.zeros_like(carry_ref)
        h_ref[...] = jnp.zeros_like(h_ref)

    dotf = lambda a, b: jnp.dot(a, b, preferred_element_type=f32)
    bd = bd_ref[...]
    head_sum = lambda x: sum(dotf(t, bd) for t in _split3(x))

    z = z_ref[0]
    z_prev = jnp.concatenate([carry_ref[0:1, :], z[:S - 1, :]], axis=0)
    carry_ref[0:1, :] = z[S - 1:S, :]
    zs = z + mu_ref[...] * (z_prev - z)
    r, k, v = zs[:, 0:W], zs[:, W:2 * W], zs[:, 2 * W:3 * W]
    lora = zs[:, 3 * W:3 * W + lora_w]
    gd = zs[:, 3 * W + lora_w:]
    w0, a0, k_k, k_a, r_k, gn_g, gn_b = (vec_ref[i:i + 1, :] for i in range(7))
    w_pre = -(w0 + dotf(jnp.tanh(lora).astype(bf16), w2_ref[...]))
    softplus = jnp.maximum(w_pre, 0.0) + jnp.log(1.0 + jnp.exp(-jnp.abs(w_pre)))
    lw = -jnp.exp(-softplus - 0.5)
    a = jax.nn.sigmoid(a0 + dotf(lora.astype(bf16), a2_ref[...]))
    g = dotf(jax.nn.sigmoid(gd).astype(bf16), g2_ref[...])
    kk = k * k_k
    kk = kk / jnp.maximum(jnp.sqrt(head_sum(kk * kk)), 1e-12)
    k = k * (1.0 + (a - 1.0) * k_a)
    b = a * kk

    row = lax.broadcasted_iota(jnp.int32, (S, S), 0)
    col = lax.broadcasted_iota(jnp.int32, (S, S), 1)
    same = lax.shift_right_logical(row, int(np.log2(C))) == lax.shift_right_logical(col, int(np.log2(C)))
    incl = same & (row >= col)
    strict = same & (row > col)
    tri = jnp.where(incl, 1.0, 0.0).astype(bf16)
    cs = sum(dotf(tri, t) for t in _split3(lw))
    e_neg = jnp.exp(-cs)

    stack = lambda x: jnp.stack([x[:, h * N:(h + 1) * N] for h in range(H)])

    def stack_t(x):
        parts = []
        for j in range(W // LANES):
            t = x[:, j * LANES:(j + 1) * LANES].T
            parts += [t[i * N:(i + 1) * N] for i in range(LANES // N)]
        return jnp.stack(parts)

    kks = stack(kk * jnp.exp(cs - lw))
    rs = stack(r * jnp.exp(cs))
    vh = stack(v).astype(bf16)
    bsT = stack_t(b * e_neg)
    ksT = stack_t(k * e_neg)
    csT = stack_t(cs)

    def bmm(x, y):
        return jnp.einsum('hij,hjk->hik', x.astype(bf16), y.astype(bf16), preferred_element_type=f32)

    lhs = jnp.concatenate([kks, rs], axis=1)
    mb = bmm(lhs, bsT)
    mk = bmm(lhs, ksT)
    a_b = jnp.where(strict, mb[:, :S], 0.0)
    a_k = jnp.where(strict, mk[:, :S], 0.0)
    q_b = jnp.where(incl, mb[:, S:], 0.0)
    q_k = jnp.where(incl, mk[:, S:], 0.0)

    t_inv = jnp.where(row == col, 1.0, 0.0) - a_b
    pw = a_b
    for _ in range(int(np.log2(C)) - 1):
        pw = bmm(pw, pw)
        t_inv = t_inv + bmm(t_inv, pw)

    w_mat = bmm(t_inv, kks)
    u_loc = -bmm(t_inv, bmm(a_k, vh))
    q_eff = rs - bmm(q_b, w_mat)
    o_loc = bmm(q_b, u_loc) + bmm(q_k, vh)

    eye_n = jnp.where(lax.broadcasted_iota(jnp.int32, (N, N), 0) == lax.broadcasted_iota(jnp.int32, (N, N), 1), 1.0, 0.0)
    lane_chunk = lax.shift_right_logical(lax.broadcasted_iota(jnp.int32, (N, S), 1), int(np.log2(C)))
    hc = h_ref[...]
    outs = []
    for c in range(n_sub):
        in_c = lane_chunk == c
        bs_c = jnp.where(in_c, bsT, 0.0)
        ks_c = jnp.where(in_c, ksT, 0.0)
        g_end = jnp.exp(csT[:, :, (c + 1) * C - 1:(c + 1) * C])
        g_mat = g_end * (eye_n - bmm(bs_c, w_mat))
        h_loc = g_end * (bmm(bs_c, u_loc) + bmm(ks_c, vh))
        sl = slice(c * C, (c + 1) * C)
        outs.append(bmm(q_eff[:, sl], hc) + o_loc[:, sl])
        hc = bmm(g_mat, hc) + h_loc
    h_ref[...] = hc

    o = jnp.concatenate(outs, axis=1)
    oc = o - jnp.mean(o, axis=-1, keepdims=True)
    o = oc * lax.rsqrt(jnp.mean(oc * oc, axis=-1, keepdims=True) + RW_GN_EPS)
    o = jnp.concatenate([o[h] for h in range(H)], axis=1)
    bonus = head_sum(r * k * r_k) * v
    o_ref[0] = (o * gn_g + gn_b + bonus) * g


def _rwkv7_branch(z, mu, w0, w2, a0, a2, g2, k_k, k_a, r_k, gn_g, gn_b):
    B, T, _ = z.shape
    f32, bf16 = jnp.float32, jnp.bfloat16
    S = min(RW_STEP, T)
    assert S % RW_CHUNK == 0 and T % S == 0 and RW_DECAY_LORA + RW_AAA_LORA == LANES == RW_GATE_LORA
    W = RW_WIDTH
    vec = jnp.stack([w0, a0, k_k, k_a, r_k.reshape(W), gn_g, gn_b, jnp.zeros((W,), f32)])
    w2p = jnp.concatenate([w2, jnp.zeros((RW_AAA_LORA, W), f32)]).astype(bf16)
    a2p = jnp.concatenate([jnp.zeros((RW_DECAY_LORA, W), f32), a2]).astype(bf16)
    head_of = np.arange(W) // RW_HEAD
    bd = jnp.asarray(head_of[:, None] == head_of[None, :], dtype=bf16)
    full = lambda shape: pl.BlockSpec(shape, lambda bi, ci: (0,) * len(shape))
    return pl.pallas_call(
        _rwkv7_kernel,
        grid=(B, T // S),
        in_specs=[pl.BlockSpec((1, S, RW_COLS), lambda bi, ci: (bi, ci, 0)), full((1, RW_COLS)), full((8, W)),
                  full((LANES, W)), full((LANES, W)), full((LANES, W)), full((W, W))],
        out_specs=pl.BlockSpec((1, S, W), lambda bi, ci: (bi, ci, 0)),
        out_shape=jax.ShapeDtypeStruct((B, T, W), f32),
        scratch_shapes=[pltpu.VMEM((8, RW_COLS), f32), pltpu.VMEM((RW_HEADS, RW_HEAD, RW_HEAD), f32)],
        compiler_params=pltpu.CompilerParams(dimension_semantics=("arbitrary", "arbitrary"),
                                             vmem_limit_bytes=VMEM_LIMIT_BYTES),
        name="rwkv7",
    )(z, mu.reshape(1, RW_COLS), vec, w2p, a2p, g2.astype(bf16), bd)


def _cmp_sel_overlap(n_cmp, n_sel):
    cs = np.arange(n_cmp)[:, None] * CMP_STRIDE
    ss = np.arange(n_sel)[None, :] * SEL_LEN
    ov = np.clip(np.minimum(cs + CMP_LEN, ss + SEL_LEN) - np.maximum(cs, ss), 0, None)
    return (ov / CMP_LEN).astype(np.float32)


NSA_TQ = 256
NSA_TK_SEL = 512
NSA_TK_WIN = 256
NEG_INIT = -1e30
LOG2E = float(np.log2(np.e))


def _dot_bf16x3(a, b):
    bf16, f32 = jnp.bfloat16, jnp.float32
    ah, bh = a.astype(bf16), b.astype(bf16)
    al, bl = (a - ah.astype(f32)).astype(bf16), (b - bh.astype(f32)).astype(bf16)
    d = lambda x, y: jnp.dot(x, y, preferred_element_type=f32)
    return d(ah, bh) + d(ah, bl) + d(al, bh)


def _nsa_attn_kernel(qraw_ref, qrot_ref, kcmp_ref, vcmpT_ref, ovT_ref, ksel_ref, vselT_ref, kwin_ref, vwinT_ref,
                     gl_ref, o_ref, lim_ref, key_ref, ssa_ref, ssb_ref, *, tq, n_top):
    f32, bf16 = jnp.float32, jnp.bfloat16
    G = NSA_GROUP
    R = G * tq
    t0 = pl.program_id(2) * tq
    ncp = kcmp_ref.shape[2]
    n_sel = ovT_ref.shape[0]

    qraw = jnp.concatenate([qraw_ref[0, g] for g in range(G)], axis=1)
    qrot = jnp.concatenate([qrot_ref[0, g] for g in range(G)], axis=1)

    s_c = _dot_bf16x3(kcmp_ref[0, 0], qraw)
    t_tok = t0 + lax.broadcasted_iota(jnp.int32, (1, tq), 1)
    cmp_last = lax.broadcasted_iota(jnp.int32, (ncp, tq), 0) * CMP_STRIDE + (CMP_LEN - 1)
    s_c = s_c + jnp.concatenate([jnp.where(cmp_last <= t_tok, 0.0, -jnp.inf)] * G, axis=1)
    m_c = jnp.max(s_c, axis=0, keepdims=True)
    m_c = jnp.where(m_c == -jnp.inf, 0.0, m_c)
    e_c = jnp.exp(s_c - m_c)
    p_c = e_c / jnp.maximum(jnp.sum(e_c, axis=0, keepdims=True), 1e-30)
    o_c = jnp.dot(vcmpT_ref[0, 0], p_c.astype(bf16), preferred_element_type=f32)

    p_sum = p_c[:, 0:tq]
    for g in range(1, G):
        p_sum = p_sum + p_c[:, g * tq:(g + 1) * tq]
    imp = _dot_bf16x3(ovT_ref[...], p_sum)
    j_iota = lax.broadcasted_iota(jnp.int32, (n_sel, tq), 0)
    cur = lax.shift_right_logical(t_tok, int(np.log2(SEL_LEN)))
    forced = (j_iota == 0) | (j_iota == cur) | (j_iota == cur - 1)
    cand = (j_iota >= 1) & (j_iota <= cur - 2)
    quota = n_top - 1 - jnp.minimum(cur, 2)
    key = jnp.where(cand, pltpu.bitcast(imp, jnp.int32), -1)
    key_ref[...] = key

    def rank_body(i, rank):
        row = key_ref[pl.ds(i, 1), :]
        return rank + jnp.where(row + jnp.where(j_iota > i, 1, 0) > key, 1, 0)

    i_end = jnp.maximum(lax.shift_right_logical(t0 + tq - 1, int(np.log2(SEL_LEN))) - 1, 1)
    rank = lax.fori_loop(1, i_end, rank_body, jnp.zeros((n_sel, tq), jnp.int32))
    sel = forced | (cand & (rank < quota))
    lim_ref[...] = jnp.where(sel, t_tok, -1)

    softmax_init = (jnp.full((1, R), NEG_INIT, f32), jnp.zeros((1, R), f32), jnp.zeros((NSA_HEAD, R), f32))

    def softmax_step(carry, s, vT):
        m, l, acc = carry
        m_new = jnp.maximum(m, jnp.max(s, axis=0, keepdims=True))
        alpha = jnp.exp2(m - m_new)
        p = jnp.exp2(s - m_new)
        l = alpha * l + jnp.sum(p, axis=0, keepdims=True)
        acc = alpha * acc + jnp.dot(vT, p.astype(bf16), preferred_element_type=f32)
        return m_new, l, acc

    def attend(score_fn, vT_fn, lo, hi, sa_ref, sb_ref):
        def scores(kt):
            pen = jnp.where(kt < hi, 0.0, -jnp.inf)
            return score_fn(jnp.minimum(kt, hi - 1), pen)

        def update(carry, s_ref, kt):
            return softmax_step(carry, s_ref[...], vT_fn(jnp.minimum(kt, hi - 1)))

        def body(i, carry):
            kt = lo + 2 * i
            sb_ref[...] = scores(kt + 1)
            carry = update(carry, sa_ref, kt)
            sa_ref[...] = scores(kt + 2)
            return update(carry, sb_ref, kt + 1)

        sa_ref[...] = scores(lo)
        _, l, acc = lax.fori_loop(0, lax.div(hi - lo + 1, 2), body, softmax_init)
        return acc / jnp.maximum(l, 1e-30)

    tile_g = lambda bias: jnp.concatenate([bias] * G, axis=1)

    tk = min(NSA_TK_SEL, ksel_ref.shape[2])
    nb = tk // SEL_LEN
    key_iota = lax.broadcasted_iota(jnp.int32, (tk, tq), 0)

    def sel_scores(kt, pen):
        k0 = pl.multiple_of(kt * tk, tk)
        s = jnp.dot(ksel_ref[0, 0, pl.ds(k0, tk), :], qrot, preferred_element_type=f32)
        limb = lim_ref[pl.ds(pl.multiple_of(kt * nb, nb), nb), :] - k0
        lim_t = jnp.concatenate([jnp.broadcast_to(limb[jb:jb + 1, :], (SEL_LEN, tq)) for jb in range(nb)], axis=0)
        return s + tile_g(jnp.where(key_iota <= lim_t, pen, -jnp.inf))

    o_s = attend(sel_scores, lambda kt: vselT_ref[0, 0, :, pl.ds(pl.multiple_of(kt * tk, tk), tk)],
                 0, lax.div(t0 + tq - 1, tk) + 1, ssa_ref, ssb_ref)

    tkw = min(NSA_TK_WIN, kwin_ref.shape[2])
    key_iota_w = lax.broadcasted_iota(jnp.int32, (tkw, tq), 0)

    assert WINDOW % tkw == 0 and tq % tkw == 0
    carry = softmax_init
    for i in range((WINDOW + tq) // tkw):
        kt = lax.div(t0, tkw) - WINDOW // tkw + i
        pen = jnp.where(kt >= 0, 0.0, -jnp.inf)
        k0 = pl.multiple_of(jnp.maximum(kt, 0) * tkw, tkw)
        s = jnp.dot(kwin_ref[0, 0, pl.ds(k0, tkw), :], qrot, preferred_element_type=f32)
        rel = t_tok - k0
        s = s + tile_g(jnp.where((key_iota_w <= rel) & (key_iota_w > rel - WINDOW), pen, -jnp.inf))
        carry = softmax_step(carry, s, vwinT_ref[0, 0, :, pl.ds(k0, tkw)])
    o_w = carry[2] / jnp.maximum(carry[1], 1e-30)

    gl = jnp.concatenate([gl_ref[0, g] for g in range(G)], axis=1)
    gates = jax.nn.sigmoid(gl)
    out = gates[0:1, :] * o_c + gates[1:2, :] * o_s + gates[2:3, :] * o_w
    for g in range(G):
        o_ref[0, g] = out[:, g * tq:(g + 1) * tq]


NSA_PREP_T = 256
NSA_COL0 = RW_COLS


def _nsa_prep_kernel(qa_ref, qb_ref, ks_ref, kw_ref, gl_ref, cos_ref, sin_ref,
                     qraw_ref, qrot_ref, ksel_ref, vselT_ref, kwin_ref, vwinT_ref, glT_ref):
    bf16 = jnp.bfloat16
    d, half = NSA_HEAD, NSA_HEAD // 2
    scale = NSA_HEAD ** -0.5
    cos, sin = cos_ref[0], sin_ref[0]
    first = (lax.broadcasted_iota(jnp.int32, cos.shape, 1) & (d - 1)) < half

    def rope(x):
        partner = jnp.where(first, pltpu.roll(x, LANES - half, 1), pltpu.roll(x, half, 1))
        return x * cos + jnp.where(first, -partner, partner) * sin

    def put_heads_t(ref, first_head, x_t, dtype):
        for i in range(LANES // d):
            ref[0, first_head + i] = x_t[i * d:(i + 1) * d].astype(dtype)

    for j in range(NSA_HEADS * d // LANES):
        src = qa_ref if j < 2 else qb_ref
        piece = src[0][:, (j % 2) * LANES:(j % 2 + 1) * LANES]
        put_heads_t(qraw_ref, 2 * j, (piece * scale).T, jnp.float32)
        put_heads_t(qrot_ref, 2 * j, (rope(piece) * (scale * LOG2E)).T, bf16)
    for src, k_out, vT_out in ((ks_ref, ksel_ref, vselT_ref), (kw_ref, kwin_ref, vwinT_ref)):
        kv = src[0]
        kr = rope(kv[:, :LANES])
        for h in range(NSA_KV_HEADS):
            k_out[0, h] = kr[:, h * d:(h + 1) * d].astype(bf16)
        put_heads_t(vT_out, 0, kv[:, LANES:].T, bf16)
    gl_t = gl_ref[0].T
    for h in range(NSA_HEADS):
        glT_ref[0, h] = gl_t[3 * h:3 * h + 3]


def _nsa_prep(z, cos, sin):
    B, T, _ = z.shape
    f32, bf16 = jnp.float32, jnp.bfloat16
    t = min(NSA_PREP_T, T)
    H, Hkv, d = NSA_HEADS, NSA_KV_HEADS, NSA_HEAD
    assert NSA_COL0 % 256 == 0 and NSA_WIDTH == 512 and NSA_KV == LANES and (NSA_COL0 + NSA_WIDTH + 6 * NSA_KV) % LANES == 0
    c0 = NSA_COL0 // 256
    wide = lambda j: pl.BlockSpec((1, t, 256), lambda b, i: (b, i, c0 + j))
    gl_col = (NSA_COL0 + NSA_WIDTH + 6 * NSA_KV) // LANES
    tab = pl.BlockSpec((1, t, LANES), lambda b, i: (b, i, 0))
    tile4 = lambda c: jnp.tile(c.reshape(B, T, d // 2), (1, 1, LANES // (d // 2)))
    q_out = pl.BlockSpec((1, H, d, t), lambda b, i: (b, 0, 0, i))
    k_out = pl.BlockSpec((1, Hkv, t, d), lambda b, i: (b, 0, i, 0))
    vT_out = pl.BlockSpec((1, Hkv, d, t), lambda b, i: (b, 0, 0, i))
    return pl.pallas_call(
        _nsa_prep_kernel,
        grid=(B, T // t),
        in_specs=[wide(0), wide(1), wide(3), wide(4), pl.BlockSpec((1, t, LANES), lambda b, i: (b, i, gl_col)), tab, tab],
        out_specs=[q_out, q_out, k_out, vT_out, k_out, vT_out, pl.BlockSpec((1, H, 3, t), lambda b, i: (b, 0, 0, i))],
        out_shape=[jax.ShapeDtypeStruct((B, H, d, T), f32), jax.ShapeDtypeStruct((B, H, d, T), bf16),
                   jax.ShapeDtypeStruct((B, Hkv, T, d), bf16), jax.ShapeDtypeStruct((B, Hkv, d, T), bf16),
                   jax.ShapeDtypeStruct((B, Hkv, T, d), bf16), jax.ShapeDtypeStruct((B, Hkv, d, T), bf16),
                   jax.ShapeDtypeStruct((B, H, 3, T), f32)],
        compiler_params=pltpu.CompilerParams(dimension_semantics=("arbitrary", "arbitrary"),
                                             vmem_limit_bytes=VMEM_LIMIT_BYTES),
        name="nsa_prep",
    )(z, z, z, z, z, tile4(cos), tile4(sin))


def _nsa_group(z, cos, sin, cmp_pos, cmp_w1, cmp_w2):
    B, T, _ = z.shape
    f32, bf16 = jnp.float32, jnp.bfloat16
    qraw_T, qrot_T, ksel, vselT, kwin, vwinT, gl_T = _nsa_prep(z, cos, sin)
    kc = z[..., NSA_COL0 + NSA_WIDTH:NSA_COL0 + NSA_WIDTH + NSA_KV]
    vc = z[..., NSA_COL0 + NSA_WIDTH + NSA_KV:NSA_COL0 + NSA_WIDTH + 2 * NSA_KV]
    kv_heads = lambda t: t.reshape(B, T, NSA_KV_HEADS, NSA_HEAD)

    n_grp = T // CMP_STRIDE
    n_cmp = n_grp - 1
    half = CMP_STRIDE * NSA_HEAD
    grp = jnp.stack([kv_heads(kc), kv_heads(vc)]).transpose(0, 1, 3, 2, 4).reshape(2, B * NSA_KV_HEADS * n_grp, half)
    cmp_kv = []
    for zi in range(2):
        w1ab = jnp.concatenate([cmp_w1[zi, :half], cmp_w1[zi, half:]], axis=1)
        ab = _dense(grp[zi], w1ab).reshape(B, NSA_KV_HEADS, n_grp, 2 * CMP_HIDDEN)
        c = cmp_pos[zi].reshape(1, CMP_LEN * NSA_HEAD) @ cmp_w1[zi]
        hid = jax.nn.gelu(ab[:, :, :-1, :CMP_HIDDEN] + ab[:, :, 1:, CMP_HIDDEN:] + c)
        cmp_kv.append(jnp.pad(hid @ cmp_w2[zi], ((0, 0), (0, 0), (0, 1), (0, 0))))
    k_cmp = cmp_kv[0]
    v_cmpT = cmp_kv[1].transpose(0, 1, 3, 2).astype(bf16)

    n_sel = T // SEL_LEN
    n_top = min(SEL_TOP, n_sel)
    ovT = jnp.asarray(np.pad(_cmp_sel_overlap(n_cmp, n_sel), ((0, 1), (0, 0))).T)
    tq = min(NSA_TQ, T)
    G = NSA_GROUP
    q_spec = pl.BlockSpec((1, G, NSA_HEAD, tq), lambda b, h, i: (b, h, 0, i))
    kv_spec = lambda shape: pl.BlockSpec((1, 1) + shape, lambda b, h, i: (b, h, 0, 0))
    out = pl.pallas_call(
        functools.partial(_nsa_attn_kernel, tq=tq, n_top=n_top),
        grid=(B, NSA_KV_HEADS, T // tq),
        in_specs=[q_spec, q_spec, kv_spec((n_grp, NSA_HEAD)), kv_spec((NSA_HEAD, n_grp)),
                  pl.BlockSpec((n_sel, n_grp), lambda b, h, i: (0, 0)),
                  kv_spec((T, NSA_HEAD)), kv_spec((NSA_HEAD, T)), kv_spec((T, NSA_HEAD)), kv_spec((NSA_HEAD, T)),
                  pl.BlockSpec((1, G, 3, tq), lambda b, h, i: (b, h, 0, i))],
        out_specs=q_spec,
        out_shape=jax.ShapeDtypeStruct((B, NSA_HEADS, NSA_HEAD, T), f32),
        scratch_shapes=[pltpu.VMEM((n_sel, tq), jnp.int32), pltpu.VMEM((n_sel, tq), jnp.int32)]
        + [pltpu.VMEM((min(NSA_TK_SEL, T), G * tq), f32)] * 2,
        compiler_params=pltpu.CompilerParams(dimension_semantics=("arbitrary", "arbitrary", "arbitrary"),
                                             vmem_limit_bytes=VMEM_LIMIT_BYTES),
        name="nsa_attn",
    )(qraw_T, qrot_T, k_cmp, v_cmpT, ovT, ksel, vselT, kwin, vwinT, gl_T)
    return out.transpose(0, 3, 1, 2).reshape(B, T, NSA_WIDTH)


HG_STEP = 256
HG_HEADS_PER_STEP = 8


def _hgrn2_kernel(q_ref, f_ref, i_ref, g_ref, lb_ref, ng_ref, o_ref, st_ref):
    f32, bf16 = jnp.float32, jnp.bfloat16
    S, d = q_ref.shape[1], HG_HEAD
    hb = q_ref.shape[2] // d
    C = HG_CHUNK
    n_sub = S // C
    shift = int(np.log2(C))

    @pl.when(pl.program_id(2) == 0)
    def _():
        st_ref[...] = jnp.zeros_like(st_ref)

    heads = lambda ref: jnp.stack([ref[0][:, h * d:(h + 1) * d] for h in range(hb)])
    lb = lb_ref[...]
    forget = lb + (1.0 - lb) * _sigmoid(heads(f_ref))
    logf = jnp.log(forget)
    k = 1.0 - forget
    q = heads(q_ref)
    qs = q * _sigmoid(q)
    v = heads(i_ref)

    pn = min(LANES, S)
    n_pan = S // pn
    panels = lambda x: x.reshape(hb * n_pan, pn, d)
    row = lax.broadcasted_iota(jnp.int32, (pn, pn), 0)
    col = lax.broadcasted_iota(jnp.int32, (pn, pn), 1)
    same = lax.shift_right_logical(row, shift) == lax.shift_right_logical(col, shift)
    causal = same & (row >= col)

    def cumsum01(mask, x):
        m = jnp.broadcast_to(jnp.where(mask, 1.0, 0.0).astype(bf16), (hb * n_pan, pn, pn))
        return sum(jnp.einsum('hij,hjk->hik', m, t, preferred_element_type=f32) for t in _split3(x))

    logf_p = panels(logf)
    b = cumsum01(causal, logf_p)
    tail = cumsum01(same & (col > row), logf_p)
    q_e = (panels(qs) * jnp.exp(b)).astype(bf16)
    k_e = (panels(k) * jnp.exp(-b)).astype(bf16)
    k_tail = (panels(k) * jnp.exp(tail)).reshape(hb, n_pan, pn, d)
    v_p = panels(v)

    a = jnp.einsum('hsd,htd->hst', q_e, k_e, preferred_element_type=f32)
    o_intra = jnp.einsum('hst,htv->hsv', jnp.where(causal, a, 0.0).astype(bf16), v_p.astype(bf16),
                         preferred_element_type=f32).reshape(hb, S, d)

    vT = jnp.stack([v_p[i].T for i in range(hb * n_pan)]).astype(bf16).reshape(hb, n_pan, d, pn)
    q_e = q_e.reshape(hb, S, d)
    b = b.reshape(hb, S, d)
    row_chunk = lax.shift_right_logical(lax.broadcasted_iota(jnp.int32, (pn, d), 0), shift)
    st = st_ref[...]
    o_inter = []
    for c in range(n_sub):
        sl = slice(c * C, (c + 1) * C)
        pan, c_in = divmod(c, pn // C)
        o_inter.append(jnp.einsum('hcd,hvd->hcv', q_e[:, sl], st.astype(bf16), preferred_element_type=f32))
        k_c = jnp.where(row_chunk == c_in, k_tail[:, pan], 0.0).astype(bf16)
        d_c = jnp.exp(b[:, (c + 1) * C - 1:(c + 1) * C, :])
        st = st * d_c + jnp.einsum('hvs,hsk->hvk', vT[:, pan], k_c, preferred_element_type=f32)
    st_ref[...] = st
    o = o_intra + jnp.concatenate(o_inter, axis=1)
    o = o * lax.rsqrt(jnp.mean(o * o, axis=-1, keepdims=True) + LN_EPS) * ng_ref[...]
    g = heads(g_ref)
    o = o * (g * _sigmoid(g))
    for h in range(hb):
        o_ref[0, :, h * d:(h + 1) * d] = o[h]


def _hgrn2_mixer(x, w_in, lb, norm_g):
    B, T, D = x.shape
    z = _dense(x.reshape(B * T, D), w_in).reshape(B, T, -1)
    S = min(HG_STEP, T)
    H, d = HG_HEADS, HG_HEAD
    hb = HG_HEADS_PER_STEP
    ng = H // hb
    col = lambda j: pl.BlockSpec((1, S, hb * d), lambda b, h, t: (b, t, j * ng + h))
    vec = pl.BlockSpec((hb, 1, d), lambda b, h, t: (h, 0, 0))
    o = pl.pallas_call(
        _hgrn2_kernel,
        grid=(B, ng, T // S),
        in_specs=[col(0), col(1), col(2), col(3), vec, pl.BlockSpec((1, 1, d), lambda b, h, t: (0, 0, 0))],
        out_specs=pl.BlockSpec((1, S, hb * d), lambda b, h, t: (b, t, h)),
        out_shape=jax.ShapeDtypeStruct((B, T, D), jnp.float32),
        scratch_shapes=[pltpu.VMEM((hb, d, d), jnp.float32)],
        compiler_params=pltpu.CompilerParams(dimension_semantics=("arbitrary", "arbitrary", "arbitrary"),
                                             vmem_limit_bytes=VMEM_LIMIT_BYTES),
        name="hgrn2",
    )(z, z, z, z, lb.reshape(H, 1, d), norm_g.reshape(1, 1, d))
    return o.reshape(B * T, D)


def _moe_expert_kernel(blk_e_ref, x_ref, w1_ref, w3_ref, w2_ref, o_ref):
    del blk_e_ref
    f32, bf16 = jnp.float32, jnp.bfloat16
    x = x_ref[...].astype(bf16)
    a = jnp.dot(x, w1_ref[0], preferred_element_type=f32)
    b = jnp.dot(x, w3_ref[0], preferred_element_type=f32)
    hid = (jax.nn.silu(a) * b).astype(bf16)
    o_ref[...] = jnp.dot(hid, w2_ref[0], preferred_element_type=f32)


def _moe_experts(xbuf, blk_e, w1, w3, w2):
    P, D = xbuf.shape
    hid = w1.shape[-1]
    bf16 = jnp.bfloat16
    w_spec = lambda shape: pl.BlockSpec((1,) + shape, lambda i, be: (be[i], 0, 0))
    x_spec = pl.BlockSpec((MOE_BLOCK, D), lambda i, be: (i, 0))
    return pl.pallas_call(
        _moe_expert_kernel,
        grid_spec=pltpu.PrefetchScalarGridSpec(
            num_scalar_prefetch=1,
            grid=(P // MOE_BLOCK,),
            in_specs=[x_spec, w_spec((D, hid)), w_spec((D, hid)), w_spec((hid, D))],
            out_specs=x_spec,
        ),
        out_shape=jax.ShapeDtypeStruct((P, D), jnp.float32),
        compiler_params=pltpu.CompilerParams(dimension_semantics=("arbitrary",), vmem_limit_bytes=VMEM_LIMIT_BYTES),
        name="moe_experts",
    )(blk_e, xbuf, w1.astype(bf16), w3.astype(bf16), w2.astype(bf16))


MOE_ROUTE_TM = 512
MOE_ROW_TM = 256
MOE_ROUTE_ROWS = 40


def _moe_route_kernel(h_ref, wr_ref, br_ref, eid_ref, gate_ref, pos_ref, cnt_ref, run_ref):
    f32 = jnp.float32
    tm = h_ref.shape[0]
    E, PG, NG = MOE_EXPERTS, MOE_PER_GROUP, MOE_GROUPS

    @pl.when(pl.program_id(0) == 0)
    def _():
        run_ref[...] = jnp.zeros_like(run_ref)

    lg = lax.dot_general(wr_ref[...], h_ref[...], (((1,), (1,)), ((), ())), precision=lax.Precision.HIGHEST,
                         preferred_element_type=f32) + br_ref[...]
    grp = lg[E:E + NG]
    g_iota = lax.broadcasted_iota(jnp.int32, (NG, tm), 0)
    g_max = jnp.max(grp, axis=0, keepdims=True)
    g_sel = jnp.min(jnp.where(grp == g_max, g_iota, NG), axis=0, keepdims=True)
    p_grp = 1.0 / jnp.sum(jnp.exp(grp - g_max), axis=0, keepdims=True)
    le = lg[0:PG]
    for g in range(1, NG):
        le = jnp.where(g_sel == g, lg[g * PG:(g + 1) * PG], le)
    e_iota = lax.broadcasted_iota(jnp.int32, (PG, tm), 0)
    m1 = jnp.max(le, axis=0, keepdims=True)
    i1 = jnp.min(jnp.where(le == m1, e_iota, PG), axis=0, keepdims=True)
    le2 = jnp.where(e_iota == i1, -jnp.inf, le)
    m2 = jnp.max(le2, axis=0, keepdims=True)
    i2 = jnp.min(jnp.where(le2 == m2, e_iota, PG), axis=0, keepdims=True)
    e2 = jnp.exp(m2 - m1)
    den = 1.0 + e2
    eid1 = g_sel * PG + i1
    eid2 = g_sel * PG + i2
    eid_ref[...] = jnp.concatenate([eid1, eid2], axis=0)
    gate_ref[...] = jnp.concatenate([p_grp / den, p_grp * e2 / den], axis=0)

    x_iota = lax.broadcasted_iota(jnp.int32, (E, tm), 0)
    oh1 = jnp.where(x_iota == eid1, 1.0, 0.0)
    oh2 = jnp.where(x_iota == eid2, 1.0, 0.0)
    before = jnp.where(lax.broadcasted_iota(jnp.int32, (tm, tm), 0) < lax.broadcasted_iota(jnp.int32, (tm, tm), 1),
                       1.0, 0.0).astype(jnp.bfloat16)
    cum1 = jnp.dot(oh1.astype(jnp.bfloat16), before, preferred_element_type=f32)
    cum2 = jnp.dot(oh2.astype(jnp.bfloat16), before, preferred_element_type=f32)
    tot1 = jnp.sum(oh1, axis=1, keepdims=True)
    tot2 = jnp.sum(oh2, axis=1, keepdims=True)
    base = run_ref[:, 0:1]
    pos1 = jnp.sum(oh1 * (cum1 + base), axis=0, keepdims=True)
    pos2 = jnp.sum(oh2 * (cum2 + base + tot1), axis=0, keepdims=True)
    pos_ref[...] = jnp.concatenate([pos1, pos2], axis=0).astype(jnp.int32)
    new = jnp.broadcast_to(base + tot1 + tot2, run_ref.shape)
    run_ref[...] = new
    cnt_ref[...] = new


def _moe_scatter_kernel(zblk_ref, dest_ref, h_ref, xbuf_ref, zero_ref, zsem, sems):
    i = pl.program_id(0)
    rt = dest_ref.shape[1]

    @pl.when(i == 0)
    def _():
        zero_ref[...] = jnp.zeros_like(zero_ref)

        def zero_copy(j):
            row0 = pl.multiple_of(zblk_ref[j] * MOE_BLOCK, MOE_BLOCK)
            return pltpu.make_async_copy(zero_ref, xbuf_ref.at[pl.ds(row0, MOE_BLOCK)], zsem)

        def z_start(j, c):
            @pl.when(zblk_ref[j] >= 0)
            def _():
                zero_copy(j).start()
            return c

        def z_wait(j, c):
            @pl.when(zblk_ref[j] >= 0)
            def _():
                zero_copy(j).wait()
            return c

        lax.fori_loop(0, zblk_ref.shape[0], z_start, 0)
        lax.fori_loop(0, zblk_ref.shape[0], z_wait, 0)

    def row_copy(r, j, slot):
        return pltpu.make_async_copy(h_ref.at[pl.ds(i * rt + r, 1)], xbuf_ref.at[pl.ds(dest_ref[j, r], 1)],
                                     sems.at[slot])

    def issue(r, c):
        row_copy(r, 0, i % 2).start()
        row_copy(r, 1, i % 2).start()
        return c

    lax.fori_loop(0, rt, issue, 0, unroll=8)

    def drain(slot):
        def body(r, c):
            row_copy(r, 0, slot).wait()
            row_copy(r, 1, slot).wait()
            return c
        lax.fori_loop(0, rt, body, 0, unroll=8)

    @pl.when(i > 0)
    def _():
        drain((i + 1) % 2)

    @pl.when(i == pl.num_programs(0) - 1)
    def _():
        drain(i % 2)


def _moe_combine_ln_kernel(dest_ref, dnext_ref, gate_ref, h_ref, lng_ref, lnb_ref, ybuf_ref, o_ref, buf_ref, sems):
    i = pl.program_id(0)
    n = pl.num_programs(0)
    rt = h_ref.shape[0]
    cur, nxt = i % 2, (i + 1) % 2

    def row_copy(d_ref, r, j, slot):
        return pltpu.make_async_copy(ybuf_ref.at[pl.ds(d_ref[j, r], 1)], buf_ref.at[slot, j, pl.ds(r, 1)], sems.at[slot])

    def gather(d_ref, slot):
        def body(r, c):
            row_copy(d_ref, r, 0, slot).start()
            row_copy(d_ref, r, 1, slot).start()
            return c
        lax.fori_loop(0, rt, body, 0, unroll=8)

    def drain(slot):
        def body(r, c):
            row_copy(dest_ref, r, 0, slot).wait()
            row_copy(dest_ref, r, 1, slot).wait()
            return c
        lax.fori_loop(0, rt, body, 0, unroll=8)

    @pl.when(i == 0)
    def _():
        gather(dest_ref, 0)

    gather(dnext_ref, nxt)
    drain(cur)
    gate = gate_ref[...]
    ffn = gate[:, 0:1] * buf_ref[cur, 0] + gate[:, 1:2] * buf_ref[cur, 1]
    o_ref[...] = _layer_norm(DN_ALPHA * h_ref[...] + ffn, lng_ref[...], lnb_ref[...])

    @pl.when(i == n - 1)
    def _():
        drain(nxt)


def _hier_moe_ln(h, w_rg, b_rg, w_re, b_re, w1, w3, w2, ln_g, ln_b):
    M, D = h.shape
    f32, i32 = jnp.float32, jnp.int32
    E = MOE_EXPERTS
    pad_rows = MOE_ROUTE_ROWS - E - MOE_GROUPS
    wr = jnp.concatenate([w_re.T, w_rg.T, jnp.zeros((pad_rows, D), f32)], axis=0)
    br = jnp.concatenate([b_re, b_rg, jnp.zeros((pad_rows,), f32)]).reshape(MOE_ROUTE_ROWS, 1)
    tm = min(MOE_ROUTE_TM, M)
    slot_spec = pl.BlockSpec((MOE_TOPK, tm), lambda i: (0, i))
    eid, gate, pos, cnt = pl.pallas_call(
        _moe_route_kernel,
        grid=(M // tm,),
        in_specs=[pl.BlockSpec((tm, D), lambda i: (i, 0)), pl.BlockSpec((MOE_ROUTE_ROWS, D), lambda i: (0, 0)),
                  pl.BlockSpec((MOE_ROUTE_ROWS, 1), lambda i: (0, 0))],
        out_specs=[slot_spec, slot_spec, slot_spec, pl.BlockSpec((E, LANES), lambda i: (0, 0))],
        out_shape=[jax.ShapeDtypeStruct((MOE_TOPK, M), i32), jax.ShapeDtypeStruct((MOE_TOPK, M), f32),
                   jax.ShapeDtypeStruct((MOE_TOPK, M), i32), jax.ShapeDtypeStruct((E, LANES), f32)],
        scratch_shapes=[pltpu.VMEM((E, LANES), f32)],
        compiler_params=pltpu.CompilerParams(dimension_semantics=("arbitrary",), vmem_limit_bytes=VMEM_LIMIT_BYTES),
        name="moe_route",
    )(h, wr, br)

    counts = cnt[:, 0].astype(i32)
    padded = (counts + MOE_BLOCK - 1) // MOE_BLOCK * MOE_BLOCK
    ends = jnp.cumsum(padded)
    start = ends - padded
    P = M * MOE_TOPK + E * MOE_BLOCK
    n_blk = P // MOE_BLOCK
    blk_e = jnp.minimum(jnp.sum(ends[None, :] <= (jnp.arange(n_blk) * MOE_BLOCK)[:, None], axis=1), E - 1).astype(i32)
    dest = pos + jnp.sum(jnp.where(eid[:, :, None] == jnp.arange(E), start, 0), axis=-1)

    rt = min(MOE_ROW_TM, M)
    n_row = M // rt
    any_spec = pl.BlockSpec(memory_space=pl.ANY)
    last_blk = jnp.where(padded > 0, ends // MOE_BLOCK - 1, -1)
    tail_blk = ends[-1] // MOE_BLOCK + jnp.arange(E)
    zblk = jnp.concatenate([last_blk, jnp.where(tail_blk < n_blk, tail_blk, -1)]).astype(i32)
    xbuf = pl.pallas_call(
        _moe_scatter_kernel,
        grid_spec=pltpu.PrefetchScalarGridSpec(
            num_scalar_prefetch=1,
            grid=(n_row,),
            in_specs=[pl.BlockSpec((MOE_TOPK, rt), lambda i, zb: (0, i), memory_space=pltpu.SMEM), any_spec],
            out_specs=any_spec,
            scratch_shapes=[pltpu.VMEM((MOE_BLOCK, D), f32), pltpu.SemaphoreType.DMA(()), pltpu.SemaphoreType.DMA((2,))],
        ),
        out_shape=jax.ShapeDtypeStruct((P, D), f32),
        compiler_params=pltpu.CompilerParams(dimension_semantics=("arbitrary",)),
        name="moe_scatter",
    )(zblk, dest, h)

    ybuf = _moe_experts(xbuf, blk_e, w1, w3, w2)

    dest_spec = pl.BlockSpec((MOE_TOPK, rt), lambda i: (0, i), memory_space=pltpu.SMEM)
    dnext_spec = pl.BlockSpec((MOE_TOPK, rt), lambda i: (0, jnp.minimum(i + 1, n_row - 1)), memory_space=pltpu.SMEM)
    row_spec = pl.BlockSpec((rt, D), lambda i: (i, 0))
    vec_spec = pl.BlockSpec((1, D), lambda i: (0, 0))
    return pl.pallas_call(
        _moe_combine_ln_kernel,
        grid=(n_row,),
        in_specs=[dest_spec, dnext_spec, pl.BlockSpec((rt, MOE_TOPK), lambda i: (i, 0)), row_spec, vec_spec, vec_spec,
                  any_spec],
        out_specs=row_spec,
        out_shape=jax.ShapeDtypeStruct((M, D), f32),
        scratch_shapes=[pltpu.VMEM((2, MOE_TOPK, rt, D), f32), pltpu.SemaphoreType.DMA((2,))],
        compiler_params=pltpu.CompilerParams(dimension_semantics=("arbitrary",), vmem_limit_bytes=VMEM_LIMIT_BYTES),
        name="moe_combine_ln",
    )(dest, dest, gate.T, h, ln_g.reshape(1, D), ln_b.reshape(1, D), ybuf)


def _dense_res_ln_kernel(*refs):
    n = (len(refs) - 4) // 2
    x_ref, g_ref, b_ref, o_ref = refs[2 * n:]
    mix = DN_ALPHA * x_ref[...]
    for y_ref, w_ref in zip(refs[:n], refs[n:2 * n]):
        mix = mix + jnp.dot(y_ref[...].astype(jnp.bfloat16), w_ref[...], preferred_element_type=jnp.float32)
    o_ref[...] = _layer_norm(mix, g_ref[...], b_ref[...])


def _dense_res_ln(ys, w, x, ln_g, ln_b, tm=512):
    m, d = x.shape
    tm = min(tm, m)
    cuts = np.cumsum([0] + [y.shape[1] for y in ys])
    ws = [w[int(a):int(b)].astype(jnp.bfloat16) for a, b in zip(cuts[:-1], cuts[1:])]
    row_spec = lambda k: pl.BlockSpec((tm, k), lambda i: (i, 0))
    vec_spec = pl.BlockSpec((1, d), lambda i: (0, 0))
    return pl.pallas_call(
        _dense_res_ln_kernel,
        grid=(m // tm,),
        in_specs=[row_spec(y.shape[1]) for y in ys] + [pl.BlockSpec(wi.shape, lambda i: (0, 0)) for wi in ws]
        + [row_spec(d), vec_spec, vec_spec],
        out_specs=row_spec(d),
        out_shape=jax.ShapeDtypeStruct((m, d), jnp.float32),
        compiler_params=pltpu.CompilerParams(dimension_semantics=("arbitrary",), vmem_limit_bytes=VMEM_LIMIT_BYTES),
        name="dense_res_ln",
    )(*ys, *ws, x, ln_g.reshape(1, d), ln_b.reshape(1, d))


def _ple_kernel(h_ref, p_ref, wg_ref, wp_ref, o_ref):
    bf16, f32 = jnp.bfloat16, jnp.float32
    h = h_ref[...]
    gate = jax.nn.sigmoid(jnp.dot(h.astype(bf16), wg_ref[...], preferred_element_type=f32))
    o_ref[...] = h + gate * jnp.dot(p_ref[...].astype(bf16), wp_ref[...], preferred_element_type=f32)


def _ple(h, p, wg, wp, tm=512):
    m, d = h.shape
    kp = p.shape[1]
    tm = min(tm, m)
    return pl.pallas_call(
        _ple_kernel,
        grid=(m // tm,),
        in_specs=[pl.BlockSpec((tm, d), lambda i: (i, 0)), pl.BlockSpec((tm, kp), lambda i: (i, 0)),
                  pl.BlockSpec((d, d), lambda i: (0, 0)), pl.BlockSpec((kp, d), lambda i: (0, 0))],
        out_specs=pl.BlockSpec((tm, d), lambda i: (i, 0)),
        out_shape=jax.ShapeDtypeStruct((m, d), jnp.float32),
        compiler_params=pltpu.CompilerParams(dimension_semantics=("arbitrary",), vmem_limit_bytes=VMEM_LIMIT_BYTES),
        name="ple",
    )(h, p, wg.astype(jnp.bfloat16), wp.astype(jnp.bfloat16))


def kernel(x, p, positions, ev_w_in, ev_w_out, rw_mu, rw_w0, rw_w2, rw_a0, rw_a2, rw_g2, rw_k_k, rw_k_a,
           rw_r_k, rw_gn_g, rw_gn_b, nsa_cmp_pos, nsa_cmp_w1, nsa_cmp_w2, od_w_in, od_w_out, hg_lb, hg_norm_g,
           moe_w_rg, moe_b_rg, moe_w_re, moe_b_re, moe_w1, moe_w3, moe_w2, ln_g, ln_b, ple_w, ple_gate_w):
    B, T, D = x.shape
    M = B * T
    cos, sin = _rope_tables(positions, NSA_HEAD)
    lb_soft = jax.nn.softmax(hg_lb, axis=0)
    lb_all = jnp.cumsum(lb_soft, axis=0) - lb_soft[0:1]
    for li in range(DEPTH):
        j = li // 2
        if li % 2 == 0:
            z = _dense(x.reshape(M, D), ev_w_in[j], keep_pad=True).reshape(B, T, -1)
            y_rw = _rwkv7_branch(z, rw_mu[j], rw_w0[j], rw_w2[j], rw_a0[j], rw_a2[j], rw_g2[j], rw_k_k[j],
                                 rw_k_a[j], rw_r_k[j], rw_gn_g[j], rw_gn_b[j])
            y_nsa = _nsa_group(z, cos, sin, nsa_cmp_pos[j], nsa_cmp_w1[j], nsa_cmp_w2[j])
            ys, w_out = [y_rw.reshape(M, RW_WIDTH), y_nsa.reshape(M, NSA_WIDTH)], ev_w_out[j]
        else:
            ys, w_out = [_hgrn2_mixer(x, od_w_in[j], lb_all[li], hg_norm_g[j])], od_w_out[j]
        h = _dense_res_ln(ys, w_out, x.reshape(M, D), ln_g[li, 0], ln_b[li, 0])
        h = _hier_moe_ln(h, moe_w_rg[li], moe_b_rg[li], moe_w_re[li], moe_b_re[li], moe_w1[li], moe_w3[li],
                         moe_w2[li], ln_g[li, 1], ln_b[li, 1])
        x = _ple(h, p[li].reshape(M, PLE_DIM), ple_gate_w[li], ple_w[li]).reshape(B, T, D)
    return x
```

```python
import functools

import numpy as np
import jax
import jax.numpy as jnp
from jax import lax
from jax.experimental import pallas as pl
from jax.experimental.pallas import tpu as pltpu

D_MODEL = 1024
DEPTH = 2
PLE_DIM = 256
DN_ALPHA = (2 * DEPTH) ** 0.25
LN_EPS = 1e-5
ROPE_THETA = 10000.0

RW_WIDTH = D_MODEL // 2
RW_HEAD = 64
RW_HEADS = RW_WIDTH // RW_HEAD
RW_DECAY_LORA = 64
RW_AAA_LORA = 64
RW_GATE_LORA = 128
RW_GN_EPS = RW_HEAD * 1e-5
RW_SPLITS = (RW_WIDTH, RW_WIDTH, RW_WIDTH, RW_DECAY_LORA, RW_AAA_LORA, RW_GATE_LORA)
RW_COLS = sum(RW_SPLITS)

NSA_WIDTH = D_MODEL - RW_WIDTH
NSA_HEAD = 64
NSA_HEADS = NSA_WIDTH // NSA_HEAD
NSA_KV_HEADS = 2
NSA_GROUP = NSA_HEADS // NSA_KV_HEADS
NSA_KV = NSA_KV_HEADS * NSA_HEAD
CMP_LEN = 32
CMP_STRIDE = 16
CMP_HIDDEN = 128
SEL_LEN = 64
SEL_TOP = 16
WINDOW = 512
NSA_QBLOCK = 32
NSA_SPLITS = (NSA_WIDTH,) + (NSA_KV,) * 6 + (3 * NSA_HEADS,)
NSA_COLS = sum(NSA_SPLITS)
EV_COLS = RW_COLS + NSA_COLS

HG_HEAD = 128
HG_HEADS = D_MODEL // HG_HEAD
HG_CHUNK = 16
HG_SPLITS = (D_MODEL, D_MODEL, D_MODEL, D_MODEL)

MOE_GROUPS = 4
MOE_PER_GROUP = 8
MOE_EXPERTS = MOE_GROUPS * MOE_PER_GROUP
MOE_TOPK = 2
MOE_HIDDEN = 512
MOE_BLOCK = 512

LANES = 128
VMEM_LIMIT_BYTES = 56 * 1024 * 1024


def _round_up(n, m):
    return (n + m - 1) // m * m


def _dense_kernel(x_ref, w_ref, o_ref):
    o_ref[...] = jnp.dot(x_ref[...].astype(jnp.bfloat16), w_ref[...], preferred_element_type=jnp.float32)


def _dense(x2d, w, tm=512, keep_pad=False):
    m, k = x2d.shape
    n = w.shape[1]
    n_pad = _round_up(n, LANES)
    wb = w.astype(jnp.bfloat16)
    if n_pad != n:
        wb = jnp.pad(wb, ((0, 0), (0, n_pad - n)))
    tm = min(tm, m)
    assert m % tm == 0
    out = pl.pallas_call(
        _dense_kernel,
        grid=(m // tm,),
        in_specs=[pl.BlockSpec((tm, k), lambda i: (i, 0)), pl.BlockSpec((k, n_pad), lambda i: (0, 0))],
        out_specs=pl.BlockSpec((tm, n_pad), lambda i: (i, 0)),
        out_shape=jax.ShapeDtypeStruct((m, n_pad), jnp.float32),
        compiler_params=pltpu.CompilerParams(dimension_semantics=("arbitrary",), vmem_limit_bytes=VMEM_LIMIT_BYTES),
        name="dense",
    )(x2d, wb)
    return out if keep_pad or n_pad == n else out[:, :n]


def _sigmoid(x):
    return 1.0 / (1.0 + jnp.exp(-x))


def _layer_norm(x, g, b):
    xc = x - jnp.mean(x, -1, keepdims=True)
    var = jnp.mean(xc * xc, -1, keepdims=True)
    return xc * lax.rsqrt(var + LN_EPS) * g + b


def _rope_tables(positions, dim):
    inv = (1.0 / (ROPE_THETA ** (np.arange(0, dim, 2, dtype=np.float32) / dim))).astype(np.float32)
    ang = positions.astype(jnp.float32)[..., None] * inv
    return jnp.cos(ang)[:, :, None, :], jnp.sin(ang)[:, :, None, :]


RW_CHUNK = 64
RW_STEP = 128


def _split3(x):
    bf16, f32 = jnp.bfloat16, jnp.float32
    h1 = x.astype(bf16)
    r1 = x - h1.astype(f32)
    h2 = r1.astype(bf16)
    return h1, h2, (r1 - h2.astype(f32)).astype(bf16)


def _rwkv7_kernel(z_ref, mu_ref, vec_ref, w2_ref, a2_ref, g2_ref, bd_ref, o_ref, carry_ref, h_ref):
    f32, bf16 = jnp.float32, jnp.bfloat16
    S = z_ref.shape[1]
    W, N, H, C = RW_WIDTH, RW_HEAD, RW_HEADS, RW_CHUNK
    n_sub = S // C
    lora_w = RW_DECAY_LORA + RW_AAA_LORA

    @pl.when(pl.program_id(1) == 0)
    def _():
        carry_ref[...] = jnp.zeros_like(carry_ref)
        h_ref[...] = jnp.zeros_like(h_ref)

    dotf = lambda a, b: jnp.dot(a, b, preferred_element_type=f32)
    bd = bd_ref[...]
    head_sum = lambda x: sum(dotf(t, bd) for t in _split3(x))

    z = z_ref[0]
    z_prev = jnp.concatenate([carry_ref[0:1, :], z[:S - 1, :]], axis=0)
    carry_ref[0:1, :] = z[S - 1:S, :]
    zs = z + mu_ref[...] * (z_prev - z)
    r, k, v = zs[:, 0:W], zs[:, W:2 * W], zs[:, 2 * W:3 * W]
    lora = zs[:, 3 * W:3 * W + lora_w]
    gd = zs[:, 3 * W + lora_w:]
    w0, a0, k_k, k_a, r_k, gn_g, gn_b = (vec_ref[i:i + 1, :] for i in range(7))
    w_pre = -(w0 + dotf(jnp.tanh(lora).astype(bf16), w2_ref[...]))
    softplus = jnp.maximum(w_pre, 0.0) + jnp.log(1.0 + jnp.exp(-jnp.abs(w_pre)))
    lw = -jnp.exp(-softplus - 0.5)
    a = jax.nn.sigmoid(a0 + dotf(lora.astype(bf16), a2_ref[...]))
    g = dotf(jax.nn.sigmoid(gd).astype(bf16), g2_ref[...])
    kk = k * k_k
    kk = kk / jnp.maximum(jnp.sqrt(head_sum(kk * kk)), 1e-12)
    k = k * (1.0 + (a - 1.0) * k_a)
    b = a * kk

    row = lax.broadcasted_iota(jnp.int32, (S, S), 0)
    col = lax.broadcasted_iota(jnp.int32, (S, S), 1)
    same = lax.shift_right_logical(row, int(np.log2(C))) == lax.shift_right_logical(col, int(np.log2(C)))
    incl = same & (row >= col)
    strict = same & (row > col)
    tri = jnp.where(incl, 1.0, 0.0).astype(bf16)
    cs = sum(dotf(tri, t) for t in _split3(lw))
    e_neg = jnp.exp(-cs)

    stack = lambda x: jnp.stack([x[:, h * N:(h + 1) * N] for h in range(H)])

    def stack_t(x):
        parts = []
        for j in range(W // LANES):
            t = x[:, j * LANES:(j + 1) * LANES].T
            parts += [t[i * N:(i + 1) * N] for i in range(LANES // N)]
        return jnp.stack(parts)

    kks = stack(kk * jnp.exp(cs - lw))
    rs = stack(r * jnp.exp(cs))
    vh = stack(v).astype(bf16)
    bsT = stack_t(b * e_neg)
    ksT = stack_t(k * e_neg)
    csT = stack_t(cs)

    def bmm(x, y):
        return jnp.einsum('hij,hjk->hik', x.astype(bf16), y.astype(bf16), preferred_element_type=f32)

    lhs = jnp.concatenate([kks, rs], axis=1)
    mb = bmm(lhs, bsT)
    mk = bmm(lhs, ksT)
    a_b = jnp.where(strict, mb[:, :S], 0.0)
    a_k = jnp.where(strict, mk[:, :S], 0.0)
    q_b = jnp.where(incl, mb[:, S:], 0.0)
    q_k = jnp.where(incl, mk[:, S:], 0.0)

    t_inv = jnp.where(row == col, 1.0, 0.0) - a_b
    pw = a_b
    for _ in range(int(np.log2(C)) - 1):
        pw = bmm(pw, pw)
        t_inv = t_inv + bmm(t_inv, pw)

    w_mat = bmm(t_inv, kks)
    u_loc = -bmm(t_inv, bmm(a_k, vh))
    q_eff = rs - bmm(q_b, w_mat)
    o_loc = bmm(q_b, u_loc) + bmm(q_k, vh)

    eye_n = jnp.where(lax.broadcasted_iota(jnp.int32, (N, N), 0) == lax.broadcasted_iota(jnp.int32, (N, N), 1), 1.0, 0.0)
    lane_chunk = lax.shift_right_logical(lax.broadcasted_iota(jnp.int32, (N, S), 1), int(np.log2(C)))
    hc = h_ref[...]
    outs = []
    for c in range(n_sub):
        in_c = lane_chunk == c
        bs_c = jnp.where(in_c, bsT, 0.0)
        ks_c = jnp.where(in_c, ksT, 0.0)
        g_end = jnp.exp(csT[:, :, (c + 1) * C - 1:(c + 1) * C])
        g_mat = g_end * (eye_n - bmm(bs_c, w_mat))
        h_loc = g_end * (bmm(bs_c, u_loc) + bmm(ks_c, vh))
        sl = slice(c * C, (c + 1) * C)
        outs.append(bmm(q_eff[:, sl], hc) + o_loc[:, sl])
        hc = bmm(g_mat, hc) + h_loc
    h_ref[...] = hc

    o = jnp.concatenate(outs, axis=1)
    oc = o - jnp.mean(o, axis=-1, keepdims=True)
    o = oc * lax.rsqrt(jnp.mean(oc * oc, axis=-1, keepdims=True) + RW_GN_EPS)
    o = jnp.concatenate([o[h] for h in range(H)], axis=1)
    bonus = head_sum(r * k * r_k) * v
    o_ref[0] = (o * gn_g + gn_b + bonus) * g


def _rwkv7_branch(z, mu, w0, w2, a0, a2, g2, k_k, k_a, r_k, gn_g, gn_b):
    B, T, _ = z.shape
    f32, bf16 = jnp.float32, jnp.bfloat16
    S = min(RW_STEP, T)
    assert S % RW_CHUNK == 0 and T % S == 0 and RW_DECAY_LORA + RW_AAA_LORA == LANES == RW_GATE_LORA
    W = RW_WIDTH
    vec = jnp.stack([w0, a0, k_k, k_a, r_k.reshape(W), gn_g, gn_b, jnp.zeros((W,), f32)])
    w2p = jnp.concatenate([w2, jnp.zeros((RW_AAA_LORA, W), f32)]).astype(bf16)
    a2p = jnp.concatenate([jnp.zeros((RW_DECAY_LORA, W), f32), a2]).astype(bf16)
    head_of = np.arange(W) // RW_HEAD
    bd = jnp.asarray(head_of[:, None] == head_of[None, :], dtype=bf16)
    full = lambda shape: pl.BlockSpec(shape, lambda bi, ci: (0,) * len(shape))
    return pl.pallas_call(
        _rwkv7_kernel,
        grid=(B, T // S),
        in_specs=[pl.BlockSpec((1, S, RW_COLS), lambda bi, ci: (bi, ci, 0)), full((1, RW_COLS)), full((8, W)),
                  full((LANES, W)), full((LANES, W)), full((LANES, W)), full((W, W))],
        out_specs=pl.BlockSpec((1, S, W), lambda bi, ci: (bi, ci, 0)),
        out_shape=jax.ShapeDtypeStruct((B, T, W), f32),
        scratch_shapes=[pltpu.VMEM((8, RW_COLS), f32), pltpu.VMEM((RW_HEADS, RW_HEAD, RW_HEAD), f32)],
        compiler_params=pltpu.CompilerParams(dimension_semantics=("arbitrary", "arbitrary"),
                                             vmem_limit_bytes=VMEM_LIMIT_BYTES),
        name="rwkv7",
    )(z, mu.reshape(1, RW_COLS), vec, w2p, a2p, g2.astype(bf16), bd)


def _cmp_sel_overlap(n_cmp, n_sel):
    cs = np.arange(n_cmp)[:, None] * CMP_STRIDE
    ss = np.arange(n_sel)[None, :] * SEL_LEN
    ov = np.clip(np.minimum(cs + CMP_LEN, ss + SEL_LEN) - np.maximum(cs, ss), 0, None)
    return (ov / CMP_LEN).astype(np.float32)


NSA_TQ = 256
NSA_TK_SEL = 512
NSA_TK_WIN = 256
NEG_INIT = -1e30
LOG2E = float(np.log2(np.e))


def _dot_bf16x3(a, b):
    bf16, f32 = jnp.bfloat16, jnp.float32
    ah, bh = a.astype(bf16), b.astype(bf16)
    al, bl = (a - ah.astype(f32)).astype(bf16), (b - bh.astype(f32)).astype(bf16)
    d = lambda x, y: jnp.dot(x, y, preferred_element_type=f32)
    return d(ah, bh) + d(ah, bl) + d(al, bh)


def _nsa_attn_kernel(qraw_ref, qrot_ref, kcmp_ref, vcmpT_ref, ovT_ref, ksel_ref, vselT_ref, kwin_ref, vwinT_ref,
                     gl_ref, o_ref, lim_ref, key_ref, ssa_ref, ssb_ref, *, tq, n_top):
    f32, bf16 = jnp.float32, jnp.bfloat16
    G = NSA_GROUP
    R = G * tq
    t0 = pl.program_id(2) * tq
    ncp = kcmp_ref.shape[2]
    n_sel = ovT_ref.shape[0]

    qraw = jnp.concatenate([qraw_ref[0, g] for g in range(G)], axis=1)
    qrot = jnp.concatenate([qrot_ref[0, g] for g in range(G)], axis=1)

    s_c = _dot_bf16x3(kcmp_ref[0, 0], qraw)
    t_tok = t0 + lax.broadcasted_iota(jnp.int32, (1, tq), 1)
    cmp_last = lax.broadcasted_iota(jnp.int32, (ncp, tq), 0) * CMP_STRIDE + (CMP_LEN - 1)
    s_c = s_c + jnp.concatenate([jnp.where(cmp_last <= t_tok, 0.0, -jnp.inf)] * G, axis=1)
    m_c = jnp.max(s_c, axis=0, keepdims=True)
    m_c = jnp.where(m_c == -jnp.inf, 0.0, m_c)
    e_c = jnp.exp(s_c - m_c)
    p_c = e_c / jnp.maximum(jnp.sum(e_c, axis=0, keepdims=True), 1e-30)
    o_c = jnp.dot(vcmpT_ref[0, 0], p_c.astype(bf16), preferred_element_type=f32)

    p_sum = p_c[:, 0:tq]
    for g in range(1, G):
        p_sum = p_sum + p_c[:, g * tq:(g + 1) * tq]
    imp = _dot_bf16x3(ovT_ref[...], p_sum)
    j_iota = lax.broadcasted_iota(jnp.int32, (n_sel, tq), 0)
    cur = lax.shift_right_logical(t_tok, int(np.log2(SEL_LEN)))
    forced = (j_iota == 0) | (j_iota == cur) | (j_iota == cur - 1)
    cand = (j_iota >= 1) & (j_iota <= cur - 2)
    quota = n_top - 1 - jnp.minimum(cur, 2)
    key = jnp.where(cand, pltpu.bitcast(imp, jnp.int32), -1)
    key_ref[...] = key

    def rank_body(i, rank):
        row = key_ref[pl.ds(i, 1), :]
        return rank + jnp.where(row + jnp.where(j_iota > i, 1, 0) > key, 1, 0)

    i_end = jnp.maximum(lax.shift_right_logical(t0 + tq - 1, int(np.log2(SEL_LEN))) - 1, 1)
    rank = lax.fori_loop(1, i_end, rank_body, jnp.zeros((n_sel, tq), jnp.int32))
    sel = forced | (cand & (rank < quota))
    lim_ref[...] = jnp.where(sel, t_tok, -1)

    softmax_init = (jnp.full((1, R), NEG_INIT, f32), jnp.zeros((1, R), f32), jnp.zeros((NSA_HEAD, R), f32))

    def softmax_step(carry, s, vT):
        m, l, acc = carry
        m_new = jnp.maximum(m, jnp.max(s, axis=0, keepdims=True))
        alpha = jnp.exp2(m - m_new)
        p = jnp.exp2(s - m_new)
        l = alpha * l + jnp.sum(p, axis=0, keepdims=True)
        acc = alpha * acc + jnp.dot(vT, p.astype(bf16), preferred_element_type=f32)
        return m_new, l, acc

    def attend(score_fn, vT_fn, lo, hi, sa_ref, sb_ref):
        def scores(kt):
            pen = jnp.where(kt < hi, 0.0, -jnp.inf)
            return score_fn(jnp.minimum(kt, hi - 1), pen)

        def update(carry, s_ref, kt):
            return softmax_step(carry, s_ref[...], vT_fn(jnp.minimum(kt, hi - 1)))

        def body(i, carry):
            kt = lo + 2 * i
            sb_ref[...] = scores(kt + 1)
            carry = update(carry, sa_ref, kt)
            sa_ref[...] = scores(kt + 2)
            return update(carry, sb_ref, kt + 1)

        sa_ref[...] = scores(lo)
        _, l, acc = lax.fori_loop(0, lax.div(hi - lo + 1, 2), body, softmax_init)
        return acc / jnp.maximum(l, 1e-30)

    tile_g = lambda bias: jnp.concatenate([bias] * G, axis=1)

    tk = min(NSA_TK_SEL, ksel_ref.shape[2])
    nb = tk // SEL_LEN
    key_iota = lax.broadcasted_iota(jnp.int32, (tk, tq), 0)

    def sel_scores(kt, pen):
        k0 = pl.multiple_of(kt * tk, tk)
        s = jnp.dot(ksel_ref[0, 0, pl.ds(k0, tk), :], qrot, preferred_element_type=f32)
        limb = lim_ref[pl.ds(pl.multiple_of(kt * nb, nb), nb), :] - k0
        lim_t = jnp.concatenate([jnp.broadcast_to(limb[jb:jb + 1, :], (SEL_LEN, tq)) for jb in range(nb)], axis=0)
        return s + tile_g(jnp.where(key_iota <= lim_t, pen, -jnp.inf))

    o_s = attend(sel_scores, lambda kt: vselT_ref[0, 0, :, pl.ds(pl.multiple_of(kt * tk, tk), tk)],
                 0, lax.div(t0 + tq - 1, tk) + 1, ssa_ref, ssb_ref)

    tkw = min(NSA_TK_WIN, kwin_ref.shape[2])
    key_iota_w = lax.broadcasted_iota(jnp.int32, (tkw, tq), 0)

    assert WINDOW % tkw == 0 and tq % tkw == 0
    carry = softmax_init
    for i in range((WINDOW + tq) // tkw):
        kt = lax.div(t0, tkw) - WINDOW // tkw + i
        pen = jnp.where(kt >= 0, 0.0, -jnp.inf)
        k0 = pl.multiple_of(jnp.maximum(kt, 0) * tkw, tkw)
        s = jnp.dot(kwin_ref[0, 0, pl.ds(k0, tkw), :], qrot, preferred_element_type=f32)
        rel = t_tok - k0
        s = s + tile_g(jnp.where((key_iota_w <= rel) & (key_iota_w > rel - WINDOW), pen, -jnp.inf))
        carry = softmax_step(carry, s, vwinT_ref[0, 0, :, pl.ds(k0, tkw)])
    o_w = carry[2] / jnp.maximum(carry[1], 1e-30)

    gl = jnp.concatenate([gl_ref[0, g] for g in range(G)], axis=1)
    gates = jax.nn.sigmoid(gl)
    out = gates[0:1, :] * o_c + gates[1:2, :] * o_s + gates[2:3, :] * o_w
    per = LANES // NSA_HEAD
    for pair in range(G // per):
        rows = jnp.concatenate([out[:, (pair * per + j) * tq:(pair * per + j + 1) * tq] for j in range(per)], axis=0)
        o_ref[0, :, pair * LANES:(pair + 1) * LANES] = rows.T


NSA_PREP_T = 256
NSA_COL0 = RW_COLS


def _nsa_prep_kernel(qa_ref, qb_ref, ks_ref, kw_ref, gl_ref, cos_ref, sin_ref,
                     qraw_ref, qrot_ref, ksel_ref, vselT_ref, kwin_ref, vwinT_ref, glT_ref):
    bf16 = jnp.bfloat16
    d, half = NSA_HEAD, NSA_HEAD // 2
    scale = NSA_HEAD ** -0.5
    cos, sin = cos_ref[0], sin_ref[0]
    first = (lax.broadcasted_iota(jnp.int32, cos.shape, 1) & (d - 1)) < half

    def rope(x):
        partner = jnp.where(first, pltpu.roll(x, LANES - half, 1), pltpu.roll(x, half, 1))
        return x * cos + jnp.where(first, -partner, partner) * sin

    def put_heads_t(ref, first_head, x_t, dtype):
        for i in range(LANES // d):
            ref[0, first_head + i] = x_t[i * d:(i + 1) * d].astype(dtype)

    for j in range(NSA_HEADS * d // LANES):
        src = qa_ref if j < 2 else qb_ref
        piece = src[0][:, (j % 2) * LANES:(j % 2 + 1) * LANES]
        put_heads_t(qraw_ref, 2 * j, (piece * scale).T, jnp.float32)
        put_heads_t(qrot_ref, 2 * j, (rope(piece) * (scale * LOG2E)).T, bf16)
    for src, k_out, vT_out in ((ks_ref, ksel_ref, vselT_ref), (kw_ref, kwin_ref, vwinT_ref)):
        kv = src[0]
        kr = rope(kv[:, :LANES])
        for h in range(NSA_KV_HEADS):
            k_out[0, h] = kr[:, h * d:(h + 1) * d].astype(bf16)
        put_heads_t(vT_out, 0, kv[:, LANES:].T, bf16)
    gl_t = gl_ref[0].T
    for h in range(NSA_HEADS):
        glT_ref[0, h] = gl_t[3 * h:3 * h + 3]


def _nsa_prep(z, cos, sin):
    B, T, _ = z.shape
    f32, bf16 = jnp.float32, jnp.bfloat16
    t = min(NSA_PREP_T, T)
    H, Hkv, d = NSA_HEADS, NSA_KV_HEADS, NSA_HEAD
    assert NSA_COL0 % 256 == 0 and NSA_WIDTH == 512 and NSA_KV == LANES and (NSA_COL0 + NSA_WIDTH + 6 * NSA_KV) % LANES == 0
    c0 = NSA_COL0 // 256
    wide = lambda j: pl.BlockSpec((1, t, 256), lambda b, i: (b, i, c0 + j))
    gl_col = (NSA_COL0 + NSA_WIDTH + 6 * NSA_KV) // LANES
    tab = pl.BlockSpec((1, t, LANES), lambda b, i: (b, i, 0))
    tile4 = lambda c: jnp.tile(c.reshape(B, T, d // 2), (1, 1, LANES // (d // 2)))
    q_out = pl.BlockSpec((1, H, d, t), lambda b, i: (b, 0, 0, i))
    k_out = pl.BlockSpec((1, Hkv, t, d), lambda b, i: (b, 0, i, 0))
    vT_out = pl.BlockSpec((1, Hkv, d, t), lambda b, i: (b, 0, 0, i))
    return pl.pallas_call(
        _nsa_prep_kernel,
        grid=(B, T // t),
        in_specs=[wide(0), wide(1), wide(3), wide(4), pl.BlockSpec((1, t, LANES), lambda b, i: (b, i, gl_col)), tab, tab],
        out_specs=[q_out, q_out, k_out, vT_out, k_out, vT_out, pl.BlockSpec((1, H, 3, t), lambda b, i: (b, 0, 0, i))],
        out_shape=[jax.ShapeDtypeStruct((B, H, d, T), f32), jax.ShapeDtypeStruct((B, H, d, T), bf16),
                   jax.ShapeDtypeStruct((B, Hkv, T, d), bf16), jax.ShapeDtypeStruct((B, Hkv, d, T), bf16),
                   jax.ShapeDtypeStruct((B, Hkv, T, d), bf16), jax.ShapeDtypeStruct((B, Hkv, d, T), bf16),
                   jax.ShapeDtypeStruct((B, H, 3, T), f32)],
        compiler_params=pltpu.CompilerParams(dimension_semantics=("arbitrary", "arbitrary"),
                                             vmem_limit_bytes=VMEM_LIMIT_BYTES),
        name="nsa_prep",
    )(z, z, z, z, z, tile4(cos), tile4(sin))


def _nsa_group(z, cos, sin, cmp_pos, cmp_w1, cmp_w2):
    B, T, _ = z.shape
    f32, bf16 = jnp.float32, jnp.bfloat16
    qraw_T, qrot_T, ksel, vselT, kwin, vwinT, gl_T = _nsa_prep(z, cos, sin)
    kc = z[..., NSA_COL0 + NSA_WIDTH:NSA_COL0 + NSA_WIDTH + NSA_KV]
    vc = z[..., NSA_COL0 + NSA_WIDTH + NSA_KV:NSA_COL0 + NSA_WIDTH + 2 * NSA_KV]
    kv_heads = lambda t: t.reshape(B, T, NSA_KV_HEADS, NSA_HEAD)

    n_grp = T // CMP_STRIDE
    n_cmp = n_grp - 1
    half = CMP_STRIDE * NSA_HEAD
    grp = jnp.stack([kv_heads(kc), kv_heads(vc)]).transpose(0, 1, 3, 2, 4).reshape(2, B * NSA_KV_HEADS * n_grp, half)
    cmp_kv = []
    for zi in range(2):
        w1ab = jnp.concatenate([cmp_w1[zi, :half], cmp_w1[zi, half:]], axis=1)
        ab = _dense(grp[zi], w1ab).reshape(B, NSA_KV_HEADS, n_grp, 2 * CMP_HIDDEN)
        c = cmp_pos[zi].reshape(1, CMP_LEN * NSA_HEAD) @ cmp_w1[zi]
        hid = jax.nn.gelu(ab[:, :, :-1, :CMP_HIDDEN] + ab[:, :, 1:, CMP_HIDDEN:] + c)
        cmp_kv.append(jnp.pad(hid @ cmp_w2[zi], ((0, 0), (0, 0), (0, 1), (0, 0))))
    k_cmp = cmp_kv[0]
    v_cmpT = cmp_kv[1].transpose(0, 1, 3, 2).astype(bf16)

    n_sel = T // SEL_LEN
    n_top = min(SEL_TOP, n_sel)
    ovT = jnp.asarray(np.pad(_cmp_sel_overlap(n_cmp, n_sel), ((0, 1), (0, 0))).T)
    tq = min(NSA_TQ, T)
    G = NSA_GROUP
    q_spec = pl.BlockSpec((1, G, NSA_HEAD, tq), lambda b, h, i: (b, h, 0, i))
    kv_spec = lambda shape: pl.BlockSpec((1, 1) + shape, lambda b, h, i: (b, h, 0, 0))
    out = pl.pallas_call(
        functools.partial(_nsa_attn_kernel, tq=tq, n_top=n_top),
        grid=(B, NSA_KV_HEADS, T // tq),
        in_specs=[q_spec, q_spec, kv_spec((n_grp, NSA_HEAD)), kv_spec((NSA_HEAD, n_grp)),
                  pl.BlockSpec((n_sel, n_grp), lambda b, h, i: (0, 0)),
                  kv_spec((T, NSA_HEAD)), kv_spec((NSA_HEAD, T)), kv_spec((T, NSA_HEAD)), kv_spec((NSA_HEAD, T)),
                  pl.BlockSpec((1, G, 3, tq), lambda b, h, i: (b, h, 0, i))],
        out_specs=pl.BlockSpec((1, tq, G * NSA_HEAD), lambda b, h, i: (b, i, h)),
        out_shape=jax.ShapeDtypeStruct((B, T, NSA_WIDTH), f32),
        scratch_shapes=[pltpu.VMEM((n_sel, tq), jnp.int32), pltpu.VMEM((n_sel, tq), jnp.int32)]
        + [pltpu.VMEM((min(NSA_TK_SEL, T), G * tq), f32)] * 2,
        compiler_params=pltpu.CompilerParams(dimension_semantics=("arbitrary", "arbitrary", "arbitrary"),
                                             vmem_limit_bytes=VMEM_LIMIT_BYTES),
        name="nsa_attn",
    )(qraw_T, qrot_T, k_cmp, v_cmpT, ovT, ksel, vselT, kwin, vwinT, gl_T)
    return out


HG_STEP = 256
HG_HEADS_PER_STEP = 8


def _hgrn2_kernel(q_ref, f_ref, i_ref, g_ref, lb_ref, ng_ref, o_ref, st_ref):
    f32, bf16 = jnp.float32, jnp.bfloat16
    S, d = q_ref.shape[1], HG_HEAD
    hb = q_ref.shape[2] // d
    C = HG_CHUNK
    n_sub = S // C
    shift = int(np.log2(C))

    @pl.when(pl.program_id(2) == 0)
    def _():
        st_ref[...] = jnp.zeros_like(st_ref)

    heads = lambda ref: jnp.stack([ref[0][:, h * d:(h + 1) * d] for h in range(hb)])
    lb = lb_ref[...]
    forget = lb + (1.0 - lb) * _sigmoid(heads(f_ref))
    logf = jnp.log(forget)
    k = 1.0 - forget
    q = heads(q_ref)
    qs = q * _sigmoid(q)
    v = heads(i_ref)

    pn = min(LANES, S)
    n_pan = S // pn
    panels = lambda x: x.reshape(hb * n_pan, pn, d)
    row = lax.broadcasted_iota(jnp.int32, (pn, pn), 0)
    col = lax.broadcasted_iota(jnp.int32, (pn, pn), 1)
    same = lax.shift_right_logical(row, shift) == lax.shift_right_logical(col, shift)
    causal = same & (row >= col)

    def cumsum01(mask, x):
        m = jnp.broadcast_to(jnp.where(mask, 1.0, 0.0).astype(bf16), (hb * n_pan, pn, pn))
        return sum(jnp.einsum('hij,hjk->hik', m, t, preferred_element_type=f32) for t in _split3(x))

    logf_p = panels(logf)
    b = cumsum01(causal, logf_p)
    tail = cumsum01(same & (col > row), logf_p)
    q_e = (panels(qs) * jnp.exp(b)).astype(bf16)
    k_e = (panels(k) * jnp.exp(-b)).astype(bf16)
    k_tail = (panels(k) * jnp.exp(tail)).reshape(hb, n_pan, pn, d)
    v_p = panels(v)

    a = jnp.einsum('hsd,htd->hst', q_e, k_e, preferred_element_type=f32)
    o_intra = jnp.einsum('hst,htv->hsv', jnp.where(causal, a, 0.0).astype(bf16), v_p.astype(bf16),
                         preferred_element_type=f32).reshape(hb, S, d)

    vT = jnp.stack([v_p[i].T for i in range(hb * n_pan)]).astype(bf16).reshape(hb, n_pan, d, pn)
    q_e = q_e.reshape(hb, S, d)
    b = b.reshape(hb, S, d)
    row_chunk = lax.shift_right_logical(lax.broadcasted_iota(jnp.int32, (pn, d), 0), shift)
    st = st_ref[...]
    o_inter = []
    for c in range(n_sub):
        sl = slice(c * C, (c + 1) * C)
        pan, c_in = divmod(c, pn // C)
        o_inter.append(jnp.einsum('hcd,hvd->hcv', q_e[:, sl], st.astype(bf16), preferred_element_type=f32))
        k_c = jnp.where(row_chunk == c_in, k_tail[:, pan], 0.0).astype(bf16)
        d_c = jnp.exp(b[:, (c + 1) * C - 1:(c + 1) * C, :])
        st = st * d_c + jnp.einsum('hvs,hsk->hvk', vT[:, pan], k_c, preferred_element_type=f32)
    st_ref[...] = st
    o = o_intra + jnp.concatenate(o_inter, axis=1)
    o = o * lax.rsqrt(jnp.mean(o * o, axis=-1, keepdims=True) + LN_EPS) * ng_ref[...]
    g = heads(g_ref)
    o = o * (g * _sigmoid(g))
    for h in range(hb):
        o_ref[0, :, h * d:(h + 1) * d] = o[h]


def _hgrn2_mixer(x, w_in, lb, norm_g):
    B, T, D = x.shape
    z = _dense(x.reshape(B * T, D), w_in).reshape(B, T, -1)
    S = min(HG_STEP, T)
    H, d = HG_HEADS, HG_HEAD
    hb = HG_HEADS_PER_STEP
    ng = H // hb
    col = lambda j: pl.BlockSpec((1, S, hb * d), lambda b, h, t: (b, t, j * ng + h))
    vec = pl.BlockSpec((hb, 1, d), lambda b, h, t: (h, 0, 0))
    o = pl.pallas_call(
        _hgrn2_kernel,
        grid=(B, ng, T // S),
        in_specs=[col(0), col(1), col(2), col(3), vec, pl.BlockSpec((1, 1, d), lambda b, h, t: (0, 0, 0))],
        out_specs=pl.BlockSpec((1, S, hb * d), lambda b, h, t: (b, t, h)),
        out_shape=jax.ShapeDtypeStruct((B, T, D), jnp.float32),
        scratch_shapes=[pltpu.VMEM((hb, d, d), jnp.float32)],
        compiler_params=pltpu.CompilerParams(dimension_semantics=("arbitrary", "arbitrary", "arbitrary"),
                                             vmem_limit_bytes=VMEM_LIMIT_BYTES),
        name="hgrn2",
    )(z, z, z, z, lb.reshape(H, 1, d), norm_g.reshape(1, 1, d))
    return o.reshape(B * T, D)


def _moe_expert_kernel(blk_e_ref, x_ref, w1_ref, w3_ref, w2_ref, o_ref):
    del blk_e_ref
    f32, bf16 = jnp.float32, jnp.bfloat16
    x = x_ref[...].astype(bf16)
    a = jnp.dot(x, w1_ref[0], preferred_element_type=f32)
    b = jnp.dot(x, w3_ref[0], preferred_element_type=f32)
    hid = (jax.nn.silu(a) * b).astype(bf16)
    o_ref[...] = jnp.dot(hid, w2_ref[0], preferred_element_type=f32)


def _moe_experts(xbuf, blk_e, w1, w3, w2):
    P, D = xbuf.shape
    hid = w1.shape[-1]
    bf16 = jnp.bfloat16
    w_spec = lambda shape: pl.BlockSpec((1,) + shape, lambda i, be: (be[i], 0, 0))
    x_spec = pl.BlockSpec((MOE_BLOCK, D), lambda i, be: (i, 0))
    return pl.pallas_call(
        _moe_expert_kernel,
        grid_spec=pltpu.PrefetchScalarGridSpec(
            num_scalar_prefetch=1,
            grid=(P // MOE_BLOCK,),
            in_specs=[x_spec, w_spec((D, hid)), w_spec((D, hid)), w_spec((hid, D))],
            out_specs=x_spec,
        ),
        out_shape=jax.ShapeDtypeStruct((P, D), jnp.float32),
        compiler_params=pltpu.CompilerParams(dimension_semantics=("arbitrary",), vmem_limit_bytes=VMEM_LIMIT_BYTES),
        name="moe_experts",
    )(blk_e, xbuf, w1.astype(bf16), w3.astype(bf16), w2.astype(bf16))


MOE_ROUTE_TM = 512
MOE_ROW_TM = 256
MOE_ROUTE_ROWS = 40


def _moe_route_kernel(h_ref, wr_ref, br_ref, eid_ref, gate_ref, pos_ref, cnt_ref, run_ref):
    f32 = jnp.float32
    tm = h_ref.shape[0]
    E, PG, NG = MOE_EXPERTS, MOE_PER_GROUP, MOE_GROUPS

    @pl.when(pl.program_id(0) == 0)
    def _():
        run_ref[...] = jnp.zeros_like(run_ref)

    lg = lax.dot_general(wr_ref[...], h_ref[...], (((1,), (1,)), ((), ())), precision=lax.Precision.HIGHEST,
                         preferred_element_type=f32) + br_ref[...]
    grp = lg[E:E + NG]
    g_iota = lax.broadcasted_iota(jnp.int32, (NG, tm), 0)
    g_max = jnp.max(grp, axis=0, keepdims=True)
    g_sel = jnp.min(jnp.where(grp == g_max, g_iota, NG), axis=0, keepdims=True)
    p_grp = 1.0 / jnp.sum(jnp.exp(grp - g_max), axis=0, keepdims=True)
    le = lg[0:PG]
    for g in range(1, NG):
        le = jnp.where(g_sel == g, lg[g * PG:(g + 1) * PG], le)
    e_iota = lax.broadcasted_iota(jnp.int32, (PG, tm), 0)
    m1 = jnp.max(le, axis=0, keepdims=True)
    i1 = jnp.min(jnp.where(le == m1, e_iota, PG), axis=0, keepdims=True)
    le2 = jnp.where(e_iota == i1, -jnp.inf, le)
    m2 = jnp.max(le2, axis=0, keepdims=True)
    i2 = jnp.min(jnp.where(le2 == m2, e_iota, PG), axis=0, keepdims=True)
    e2 = jnp.exp(m2 - m1)
    den = 1.0 + e2
    eid1 = g_sel * PG + i1
    eid2 = g_sel * PG + i2
    eid_ref[...] = jnp.concatenate([eid1, eid2], axis=0)
    gate_ref[...] = jnp.concatenate([p_grp / den, p_grp * e2 / den], axis=0)

    x_iota = lax.broadcasted_iota(jnp.int32, (E, tm), 0)
    oh1 = jnp.where(x_iota == eid1, 1.0, 0.0)
    oh2 = jnp.where(x_iota == eid2, 1.0, 0.0)
    before = jnp.where(lax.broadcasted_iota(jnp.int32, (tm, tm), 0) < lax.broadcasted_iota(jnp.int32, (tm, tm), 1),
                       1.0, 0.0).astype(jnp.bfloat16)
    cum1 = jnp.dot(oh1.astype(jnp.bfloat16), before, preferred_element_type=f32)
    cum2 = jnp.dot(oh2.astype(jnp.bfloat16), before, preferred_element_type=f32)
    tot1 = jnp.sum(oh1, axis=1, keepdims=True)
    tot2 = jnp.sum(oh2, axis=1, keepdims=True)
    base = run_ref[:, 0:1]
    pos1 = jnp.sum(oh1 * (cum1 + base), axis=0, keepdims=True)
    pos2 = jnp.sum(oh2 * (cum2 + base + tot1), axis=0, keepdims=True)
    pos_ref[...] = jnp.concatenate([pos1, pos2], axis=0).astype(jnp.int32)
    new = jnp.broadcast_to(base + tot1 + tot2, run_ref.shape)
    run_ref[...] = new
    cnt_ref[...] = new


def _moe_scatter_kernel(zblk_ref, dest_ref, h_ref, xbuf_ref, zero_ref, zsem, sem):
    i = pl.program_id(0)
    rt = dest_ref.shape[1]

    @pl.when(i == 0)
    def _():
        zero_ref[...] = jnp.zeros_like(zero_ref)

        def zero_copy(j):
            row0 = pl.multiple_of(zblk_ref[j] * MOE_BLOCK, MOE_BLOCK)
            return pltpu.make_async_copy(zero_ref, xbuf_ref.at[pl.ds(row0, MOE_BLOCK)], zsem)

        def z_start(j, c):
            @pl.when(zblk_ref[j] >= 0)
            def _():
                zero_copy(j).start()
            return c

        def z_wait(j, c):
            @pl.when(zblk_ref[j] >= 0)
            def _():
                zero_copy(j).wait()
            return c

        lax.fori_loop(0, zblk_ref.shape[0], z_start, 0)
        lax.fori_loop(0, zblk_ref.shape[0], z_wait, 0)

    def row_copy(r, j):
        return pltpu.make_async_copy(h_ref.at[pl.ds(r, 1)], xbuf_ref.at[pl.ds(dest_ref[j, r], 1)], sem)

    def issue(r, c):
        row_copy(r, 0).start()
        row_copy(r, 1).start()
        return c

    def drain(r, c):
        row_copy(r, 0).wait()
        row_copy(r, 1).wait()
        return c

    lax.fori_loop(0, rt, issue, 0, unroll=8)
    lax.fori_loop(0, rt, drain, 0, unroll=8)


def _moe_combine_ln_kernel(dest_ref, dnext_ref, gate_ref, h_ref, lng_ref, lnb_ref, ybuf_ref, o_ref, buf_ref, sems):
    i = pl.program_id(0)
    n = pl.num_programs(0)
    rt = h_ref.shape[0]
    cur, nxt = i % 2, (i + 1) % 2

    def row_copy(d_ref, r, j, slot):
        return pltpu.make_async_copy(ybuf_ref.at[pl.ds(d_ref[j, r], 1)], buf_ref.at[slot, j, pl.ds(r, 1)], sems.at[slot])

    def gather(d_ref, slot):
        def body(r, c):
            row_copy(d_ref, r, 0, slot).start()
            row_copy(d_ref, r, 1, slot).start()
            return c
        lax.fori_loop(0, rt, body, 0, unroll=8)

    def drain(slot):
        def body(r, c):
            row_copy(dest_ref, r, 0, slot).wait()
            row_copy(dest_ref, r, 1, slot).wait()
            return c
        lax.fori_loop(0, rt, body, 0, unroll=8)

    @pl.when(i == 0)
    def _():
        gather(dest_ref, 0)

    gather(dnext_ref, nxt)
    drain(cur)
    gate = gate_ref[...]
    ffn = gate[:, 0:1] * buf_ref[cur, 0] + gate[:, 1:2] * buf_ref[cur, 1]
    o_ref[...] = _layer_norm(DN_ALPHA * h_ref[...] + ffn, lng_ref[...], lnb_ref[...])

    @pl.when(i == n - 1)
    def _():
        drain(nxt)


def _hier_moe_ln(h, w_rg, b_rg, w_re, b_re, w1, w3, w2, ln_g, ln_b):
    M, D = h.shape
    f32, i32 = jnp.float32, jnp.int32
    E = MOE_EXPERTS
    pad_rows = MOE_ROUTE_ROWS - E - MOE_GROUPS
    wr = jnp.concatenate([w_re.T, w_rg.T, jnp.zeros((pad_rows, D), f32)], axis=0)
    br = jnp.concatenate([b_re, b_rg, jnp.zeros((pad_rows,), f32)]).reshape(MOE_ROUTE_ROWS, 1)
    tm = min(MOE_ROUTE_TM, M)
    slot_spec = pl.BlockSpec((MOE_TOPK, tm), lambda i: (0, i))
    eid, gate, pos, cnt = pl.pallas_call(
        _moe_route_kernel,
        grid=(M // tm,),
        in_specs=[pl.BlockSpec((tm, D), lambda i: (i, 0)), pl.BlockSpec((MOE_ROUTE_ROWS, D), lambda i: (0, 0)),
                  pl.BlockSpec((MOE_ROUTE_ROWS, 1), lambda i: (0, 0))],
        out_specs=[slot_spec, slot_spec, slot_spec, pl.BlockSpec((E, LANES), lambda i: (0, 0))],
        out_shape=[jax.ShapeDtypeStruct((MOE_TOPK, M), i32), jax.ShapeDtypeStruct((MOE_TOPK, M), f32),
                   jax.ShapeDtypeStruct((MOE_TOPK, M), i32), jax.ShapeDtypeStruct((E, LANES), f32)],
        scratch_shapes=[pltpu.VMEM((E, LANES), f32)],
        compiler_params=pltpu.CompilerParams(dimension_semantics=("arbitrary",), vmem_limit_bytes=VMEM_LIMIT_BYTES),
        name="moe_route",
    )(h, wr, br)

    counts = cnt[:, 0].astype(i32)
    padded = (counts + MOE_BLOCK - 1) // MOE_BLOCK * MOE_BLOCK
    ends = jnp.cumsum(padded)
    start = ends - padded
    P = M * MOE_TOPK + E * MOE_BLOCK
    n_blk = P // MOE_BLOCK
    blk_e = jnp.minimum(jnp.sum(ends[None, :] <= (jnp.arange(n_blk) * MOE_BLOCK)[:, None], axis=1), E - 1).astype(i32)
    dest = pos + jnp.sum(jnp.where(eid[:, :, None] == jnp.arange(E), start, 0), axis=-1)

    rt = min(MOE_ROW_TM, M)
    n_row = M // rt
    any_spec = pl.BlockSpec(memory_space=pl.ANY)
    last_blk = jnp.where(padded > 0, ends // MOE_BLOCK - 1, -1)
    tail_blk = ends[-1] // MOE_BLOCK + jnp.arange(E)
    zblk = jnp.concatenate([last_blk, jnp.where(tail_blk < n_blk, tail_blk, -1)]).astype(i32)
    xbuf = pl.pallas_call(
        _moe_scatter_kernel,
        grid_spec=pltpu.PrefetchScalarGridSpec(
            num_scalar_prefetch=1,
            grid=(n_row,),
            in_specs=[pl.BlockSpec((MOE_TOPK, rt), lambda i, zb: (0, i), memory_space=pltpu.SMEM),
                      pl.BlockSpec((rt, D), lambda i, zb: (i, 0))],
            out_specs=any_spec,
            scratch_shapes=[pltpu.VMEM((MOE_BLOCK, D), f32), pltpu.SemaphoreType.DMA(()), pltpu.SemaphoreType.DMA(())],
        ),
        out_shape=jax.ShapeDtypeStruct((P, D), f32),
        compiler_params=pltpu.CompilerParams(dimension_semantics=("arbitrary",)),
        name="moe_scatter",
    )(zblk, dest, h)

    ybuf = _moe_experts(xbuf, blk_e, w1, w3, w2)

    dest_spec = pl.BlockSpec((MOE_TOPK, rt), lambda i: (0, i), memory_space=pltpu.SMEM)
    dnext_spec = pl.BlockSpec((MOE_TOPK, rt), lambda i: (0, jnp.minimum(i + 1, n_row - 1)), memory_space=pltpu.SMEM)
    row_spec = pl.BlockSpec((rt, D), lambda i: (i, 0))
    vec_spec = pl.BlockSpec((1, D), lambda i: (0, 0))
    return pl.pallas_call(
        _moe_combine_ln_kernel,
        grid=(n_row,),
        in_specs=[dest_spec, dnext_spec, pl.BlockSpec((rt, MOE_TOPK), lambda i: (i, 0)), row_spec, vec_spec, vec_spec,
                  any_spec],
        out_specs=row_spec,
        out_shape=jax.ShapeDtypeStruct((M, D), f32),
        scratch_shapes=[pltpu.VMEM((2, MOE_TOPK, rt, D), f32), pltpu.SemaphoreType.DMA((2,))],
        compiler_params=pltpu.CompilerParams(dimension_semantics=("arbitrary",), vmem_limit_bytes=VMEM_LIMIT_BYTES),
        name="moe_combine_ln",
    )(dest, dest, gate.T, h, ln_g.reshape(1, D), ln_b.reshape(1, D), ybuf)


def _dense_res_ln_kernel(*refs):
    n = (len(refs) - 4) // 2
    x_ref, g_ref, b_ref, o_ref = refs[2 * n:]
    mix = DN_ALPHA * x_ref[...]
    for y_ref, w_ref in zip(refs[:n], refs[n:2 * n]):
        mix = mix + jnp.dot(y_ref[...].astype(jnp.bfloat16), w_ref[...], preferred_element_type=jnp.float32)
    o_ref[...] = _layer_norm(mix, g_ref[...], b_ref[...])


def _dense_res_ln(ys, w, x, ln_g, ln_b, tm=512):
    m, d = x.shape
    tm = min(tm, m)
    cuts = np.cumsum([0] + [y.shape[1] for y in ys])
    ws = [w[int(a):int(b)].astype(jnp.bfloat16) for a, b in zip(cuts[:-1], cuts[1:])]
    row_spec = lambda k: pl.BlockSpec((tm, k), lambda i: (i, 0))
    vec_spec = pl.BlockSpec((1, d), lambda i: (0, 0))
    return pl.pallas_call(
        _dense_res_ln_kernel,
        grid=(m // tm,),
        in_specs=[row_spec(y.shape[1]) for y in ys] + [pl.BlockSpec(wi.shape, lambda i: (0, 0)) for wi in ws]
        + [row_spec(d), vec_spec, vec_spec],
        out_specs=row_spec(d),
        out_shape=jax.ShapeDtypeStruct((m, d), jnp.float32),
        compiler_params=pltpu.CompilerParams(dimension_semantics=("arbitrary",), vmem_limit_bytes=VMEM_LIMIT_BYTES),
        name="dense_res_ln",
    )(*ys, *ws, x, ln_g.reshape(1, d), ln_b.reshape(1, d))


def _ple_kernel(h_ref, p_ref, wg_ref, wp_ref, o_ref):
    bf16, f32 = jnp.bfloat16, jnp.float32
    h = h_ref[...]
    gate = jax.nn.sigmoid(jnp.dot(h.astype(bf16), wg_ref[...], preferred_element_type=f32))
    o_ref[...] = h + gate * jnp.dot(p_ref[...].astype(bf16), wp_ref[...], preferred_element_type=f32)


def _ple(h, p, wg, wp, tm=512):
    m, d = h.shape
    kp = p.shape[1]
    tm = min(tm, m)
    return pl.pallas_call(
        _ple_kernel,
        grid=(m // tm,),
        in_specs=[pl.BlockSpec((tm, d), lambda i: (i, 0)), pl.BlockSpec((tm, kp), lambda i: (i, 0)),
                  pl.BlockSpec((d, d), lambda i: (0, 0)), pl.BlockSpec((kp, d), lambda i: (0, 0))],
        out_specs=pl.BlockSpec((tm, d), lambda i: (i, 0)),
        out_shape=jax.ShapeDtypeStruct((m, d), jnp.float32),
        compiler_params=pltpu.CompilerParams(dimension_semantics=("arbitrary",), vmem_limit_bytes=VMEM_LIMIT_BYTES),
        name="ple",
    )(h, p, wg.astype(jnp.bfloat16), wp.astype(jnp.bfloat16))


def kernel(x, p, positions, ev_w_in, ev_w_out, rw_mu, rw_w0, rw_w2, rw_a0, rw_a2, rw_g2, rw_k_k, rw_k_a,
           rw_r_k, rw_gn_g, rw_gn_b, nsa_cmp_pos, nsa_cmp_w1, nsa_cmp_w2, od_w_in, od_w_out, hg_lb, hg_norm_g,
           moe_w_rg, moe_b_rg, moe_w_re, moe_b_re, moe_w1, moe_w3, moe_w2, ln_g, ln_b, ple_w, ple_gate_w):
    B, T, D = x.shape
    M = B * T
    cos, sin = _rope_tables(positions, NSA_HEAD)
    lb_soft = jax.nn.softmax(hg_lb, axis=0)
    lb_all = jnp.cumsum(lb_soft, axis=0) - lb_soft[0:1]
    for li in range(DEPTH):
        j = li // 2
        if li % 2 == 0:
            z = _dense(x.reshape(M, D), ev_w_in[j], keep_pad=True).reshape(B, T, -1)
            y_rw = _rwkv7_branch(z, rw_mu[j], rw_w0[j], rw_w2[j], rw_a0[j], rw_a2[j], rw_g2[j], rw_k_k[j],
                                 rw_k_a[j], rw_r_k[j], rw_gn_g[j], rw_gn_b[j])
            y_nsa = _nsa_group(z, cos, sin, nsa_cmp_pos[j], nsa_cmp_w1[j], nsa_cmp_w2[j])
            ys, w_out = [y_rw.reshape(M, RW_WIDTH), y_nsa.reshape(M, NSA_WIDTH)], ev_w_out[j]
        else:
            ys, w_out = [_hgrn2_mixer(x, od_w_in[j], lb_all[li], hg_norm_g[j])], od_w_out[j]
        h = _dense_res_ln(ys, w_out, x.reshape(M, D), ln_g[li, 0], ln_b[li, 0])
        h = _hier_moe_ln(h, moe_w_rg[li], moe_b_rg[li], moe_w_re[li], moe_b_re[li], moe_w1[li], moe_w3[li],
                         moe_w2[li], ln_g[li, 1], ln_b[li, 1])
        x = _ple(h, p[li].reshape(M, PLE_DIM), ple_gate_w[li], ple_w[li]).reshape(B, T, D)
    return x
```

```python
import functools

import numpy as np
import jax
import jax.numpy as jnp
from jax import lax
from jax.experimental import pallas as pl
from jax.experimental.pallas import tpu as pltpu

D_MODEL = 1024
DEPTH = 2
PLE_DIM = 256
DN_ALPHA = (2 * DEPTH) ** 0.25
LN_EPS = 1e-5
ROPE_THETA = 10000.0

RW_WIDTH = D_MODEL // 2
RW_HEAD = 64
RW_HEADS = RW_WIDTH // RW_HEAD
RW_DECAY_LORA = 64
RW_AAA_LORA = 64
RW_GATE_LORA = 128
RW_GN_EPS = RW_HEAD * 1e-5
RW_SPLITS = (RW_WIDTH, RW_WIDTH, RW_WIDTH, RW_DECAY_LORA, RW_AAA_LORA, RW_GATE_LORA)
RW_COLS = sum(RW_SPLITS)

NSA_WIDTH = D_MODEL - RW_WIDTH
NSA_HEAD = 64
NSA_HEADS = NSA_WIDTH // NSA_HEAD
NSA_KV_HEADS = 2
NSA_GROUP = NSA_HEADS // NSA_KV_HEADS
NSA_KV = NSA_KV_HEADS * NSA_HEAD
CMP_LEN = 32
CMP_STRIDE = 16
CMP_HIDDEN = 128
SEL_LEN = 64
SEL_TOP = 16
WINDOW = 512
NSA_QBLOCK = 32
NSA_SPLITS = (NSA_WIDTH,) + (NSA_KV,) * 6 + (3 * NSA_HEADS,)
NSA_COLS = sum(NSA_SPLITS)
EV_COLS = RW_COLS + NSA_COLS

HG_HEAD = 128
HG_HEADS = D_MODEL // HG_HEAD
HG_CHUNK = 16
HG_SPLITS = (D_MODEL, D_MODEL, D_MODEL, D_MODEL)

MOE_GROUPS = 4
MOE_PER_GROUP = 8
MOE_EXPERTS = MOE_GROUPS * MOE_PER_GROUP
MOE_TOPK = 2
MOE_HIDDEN = 512
MOE_BLOCK = 512

LANES = 128
VMEM_LIMIT_BYTES = 56 * 1024 * 1024


def _round_up(n, m):
    return (n + m - 1) // m * m


def _dense_kernel(x_ref, w_ref, o_ref):
    o_ref[...] = jnp.dot(x_ref[...].astype(jnp.bfloat16), w_ref[...], preferred_element_type=jnp.float32)


def _dense(x2d, w, tm=512, keep_pad=False):
    m, k = x2d.shape
    n = w.shape[1]
    n_pad = _round_up(n, LANES)
    wb = w.astype(jnp.bfloat16)
    if n_pad != n:
        wb = jnp.pad(wb, ((0, 0), (0, n_pad - n)))
    tm = min(tm, m)
    assert m % tm == 0
    out = pl.pallas_call(
        _dense_kernel,
        grid=(m // tm,),
        in_specs=[pl.BlockSpec((tm, k), lambda i: (i, 0)), pl.BlockSpec((k, n_pad), lambda i: (0, 0))],
        out_specs=pl.BlockSpec((tm, n_pad), lambda i: (i, 0)),
        out_shape=jax.ShapeDtypeStruct((m, n_pad), jnp.float32),
        compiler_params=pltpu.CompilerParams(dimension_semantics=("arbitrary",), vmem_limit_bytes=VMEM_LIMIT_BYTES),
        name="dense",
    )(x2d, wb)
    return out if keep_pad or n_pad == n else out[:, :n]


def _sigmoid(x):
    return 1.0 / (1.0 + jnp.exp(-x))


def _layer_norm(x, g, b):
    xc = x - jnp.mean(x, -1, keepdims=True)
    var = jnp.mean(xc * xc, -1, keepdims=True)
    return xc * lax.rsqrt(var + LN_EPS) * g + b


def _rope_tables(positions, dim):
    inv = (1.0 / (ROPE_THETA ** (np.arange(0, dim, 2, dtype=np.float32) / dim))).astype(np.float32)
    ang = positions.astype(jnp.float32)[..., None] * inv
    return jnp.cos(ang)[:, :, None, :], jnp.sin(ang)[:, :, None, :]


RW_CHUNK = 64
RW_STEP = 128


def _split3(x):
    bf16, f32 = jnp.bfloat16, jnp.float32
    h1 = x.astype(bf16)
    r1 = x - h1.astype(f32)
    h2 = r1.astype(bf16)
    return h1, h2, (r1 - h2.astype(f32)).astype(bf16)


def _rwkv7_kernel(z_ref, mu_ref, vec_ref, w2_ref, a2_ref, g2_ref, bd_ref, o_ref, carry_ref, h_ref):
    f32, bf16 = jnp.float32, jnp.bfloat16
    S = z_ref.shape[1]
    W, N, H, C = RW_WIDTH, RW_HEAD, RW_HEADS, RW_CHUNK
    n_sub = S // C
    lora_w = RW_DECAY_LORA + RW_AAA_LORA

    @pl.when(pl.program_id(1) == 0)
    def _():
        carry_ref[...] = jnp.zeros_like(carry_ref)
        h_ref[...] = jnp.zeros_like(h_ref)

    dotf = lambda a, b: jnp.dot(a, b, preferred_element_type=f32)
    bd = bd_ref[...]
    head_sum = lambda x: sum(dotf(t, bd) for t in _split3(x))

    z = z_ref[0]
    z_prev = jnp.concatenate([carry_ref[0:1, :], z[:S - 1, :]], axis=0)
    carry_ref[0:1, :] = z[S - 1:S, :]
    zs = z + mu_ref[...] * (z_prev - z)
    r, k, v = zs[:, 0:W], zs[:, W:2 * W], zs[:, 2 * W:3 * W]
    lora = zs[:, 3 * W:3 * W + lora_w]
    gd = zs[:, 3 * W + lora_w:]
    w0, a0, k_k, k_a, r_k, gn_g, gn_b = (vec_ref[i:i + 1, :] for i in range(7))
    w_pre = -(w0 + dotf(jnp.tanh(lora).astype(bf16), w2_ref[...]))
    softplus = jnp.maximum(w_pre, 0.0) + jnp.log(1.0 + jnp.exp(-jnp.abs(w_pre)))
    lw = -jnp.exp(-softplus - 0.5)
    a = jax.nn.sigmoid(a0 + dotf(lora.astype(bf16), a2_ref[...]))
    g = dotf(jax.nn.sigmoid(gd).astype(bf16), g2_ref[...])
    kk = k * k_k
    kk = kk / jnp.maximum(jnp.sqrt(head_sum(kk * kk)), 1e-12)
    k = k * (1.0 + (a - 1.0) * k_a)
    b = a * kk

    row = lax.broadcasted_iota(jnp.int32, (S, S), 0)
    col = lax.broadcasted_iota(jnp.int32, (S, S), 1)
    same = lax.shift_right_logical(row, int(np.log2(C))) == lax.shift_right_logical(col, int(np.log2(C)))
    incl = same & (row >= col)
    strict = same & (row > col)
    tri = jnp.where(incl, 1.0, 0.0).astype(bf16)
    cs = sum(dotf(tri, t) for t in _split3(lw))
    e_neg = jnp.exp(-cs)

    stack = lambda x: jnp.stack([x[:, h * N:(h + 1) * N] for h in range(H)])

    def stack_t(x):
        parts = []
        for j in range(W // LANES):
            t = x[:, j * LANES:(j + 1) * LANES].T
            parts += [t[i * N:(i + 1) * N] for i in range(LANES // N)]
        return jnp.stack(parts)

    kks = stack(kk * jnp.exp(cs - lw))
    rs = stack(r * jnp.exp(cs))
    vh = stack(v).astype(bf16)
    bsT = stack_t(b * e_neg)
    ksT = stack_t(k * e_neg)
    csT = stack_t(cs)

    def bmm(x, y):
        return jnp.einsum('hij,hjk->hik', x.astype(bf16), y.astype(bf16), preferred_element_type=f32)

    lhs = jnp.concatenate([kks, rs], axis=1)
    mb = bmm(lhs, bsT)
    mk = bmm(lhs, ksT)
    a_b = jnp.where(strict, mb[:, :S], 0.0)
    a_k = jnp.where(strict, mk[:, :S], 0.0)
    q_b = jnp.where(incl, mb[:, S:], 0.0)
    q_k = jnp.where(incl, mk[:, S:], 0.0)

    t_inv = jnp.where(row == col, 1.0, 0.0) - a_b
    pw = a_b
    for _ in range(int(np.log2(C)) - 1):
        pw = bmm(pw, pw)
        t_inv = t_inv + bmm(t_inv, pw)

    w_mat = bmm(t_inv, kks)
    u_loc = -bmm(t_inv, bmm(a_k, vh))
    q_eff = rs - bmm(q_b, w_mat)
    o_loc = bmm(q_b, u_loc) + bmm(q_k, vh)

    eye_n = jnp.where(lax.broadcasted_iota(jnp.int32, (N, N), 0) == lax.broadcasted_iota(jnp.int32, (N, N), 1), 1.0, 0.0)
    lane_chunk = lax.shift_right_logical(lax.broadcasted_iota(jnp.int32, (N, S), 1), int(np.log2(C)))
    hc = h_ref[...]
    outs = []
    for c in range(n_sub):
        in_c = lane_chunk == c
        bs_c = jnp.where(in_c, bsT, 0.0)
        ks_c = jnp.where(in_c, ksT, 0.0)
        g_end = jnp.exp(csT[:, :, (c + 1) * C - 1:(c + 1) * C])
        g_mat = g_end * (eye_n - bmm(bs_c, w_mat))
        h_loc = g_end * (bmm(bs_c, u_loc) + bmm(ks_c, vh))
        sl = slice(c * C, (c + 1) * C)
        outs.append(bmm(q_eff[:, sl], hc) + o_loc[:, sl])
        hc = bmm(g_mat, hc) + h_loc
    h_ref[...] = hc

    o = jnp.concatenate(outs, axis=1)
    oc = o - jnp.mean(o, axis=-1, keepdims=True)
    o = oc * lax.rsqrt(jnp.mean(oc * oc, axis=-1, keepdims=True) + RW_GN_EPS)
    o = jnp.concatenate([o[h] for h in range(H)], axis=1)
    bonus = head_sum(r * k * r_k) * v
    o_ref[0] = (o * gn_g + gn_b + bonus) * g


def _rwkv7_branch(z, mu, w0, w2, a0, a2, g2, k_k, k_a, r_k, gn_g, gn_b):
    B, T, _ = z.shape
    f32, bf16 = jnp.float32, jnp.bfloat16
    S = min(RW_STEP, T)
    assert S % RW_CHUNK == 0 and T % S == 0 and RW_DECAY_LORA + RW_AAA_LORA == LANES == RW_GATE_LORA
    W = RW_WIDTH
    vec = jnp.stack([w0, a0, k_k, k_a, r_k.reshape(W), gn_g, gn_b, jnp.zeros((W,), f32)])
    w2p = jnp.concatenate([w2, jnp.zeros((RW_AAA_LORA, W), f32)]).astype(bf16)
    a2p = jnp.concatenate([jnp.zeros((RW_DECAY_LORA, W), f32), a2]).astype(bf16)
    head_of = np.arange(W) // RW_HEAD
    bd = jnp.asarray(head_of[:, None] == head_of[None, :], dtype=bf16)
    full = lambda shape: pl.BlockSpec(shape, lambda bi, ci: (0,) * len(shape))
    return pl.pallas_call(
        _rwkv7_kernel,
        grid=(B, T // S),
        in_specs=[pl.BlockSpec((1, S, RW_COLS), lambda bi, ci: (bi, ci, 0)), full((1, RW_COLS)), full((8, W)),
                  full((LANES, W)), full((LANES, W)), full((LANES, W)), full((W, W))],
        out_specs=pl.BlockSpec((1, S, W), lambda bi, ci: (bi, ci, 0)),
        out_shape=jax.ShapeDtypeStruct((B, T, W), f32),
        scratch_shapes=[pltpu.VMEM((8, RW_COLS), f32), pltpu.VMEM((RW_HEADS, RW_HEAD, RW_HEAD), f32)],
        compiler_params=pltpu.CompilerParams(dimension_semantics=("arbitrary", "arbitrary"),
                                             vmem_limit_bytes=VMEM_LIMIT_BYTES),
        name="rwkv7",
    )(z, mu.reshape(1, RW_COLS), vec, w2p, a2p, g2.astype(bf16), bd)


def _cmp_sel_overlap(n_cmp, n_sel):
    cs = np.arange(n_cmp)[:, None] * CMP_STRIDE
    ss = np.arange(n_sel)[None, :] * SEL_LEN
    ov = np.clip(np.minimum(cs + CMP_LEN, ss + SEL_LEN) - np.maximum(cs, ss), 0, None)
    return (ov / CMP_LEN).astype(np.float32)


NSA_TQ = 256
NSA_TK_SEL = 512
NSA_TK_WIN = 256
NEG_INIT = -1e30
LOG2E = float(np.log2(np.e))


def _dot_bf16x3(a, b):
    bf16, f32 = jnp.bfloat16, jnp.float32
    ah, bh = a.astype(bf16), b.astype(bf16)
    al, bl = (a - ah.astype(f32)).astype(bf16), (b - bh.astype(f32)).astype(bf16)
    d = lambda x, y: jnp.dot(x, y, preferred_element_type=f32)
    return d(ah, bh) + d(ah, bl) + d(al, bh)


def _nsa_attn_kernel(qraw_ref, qrot_ref, kcmp_ref, vcmpT_ref, ovT_ref, ksel_ref, vselT_ref, kwin_ref, vwinT_ref,
                     gl_ref, o_ref, lim_ref, key_ref, ssa_ref, ssb_ref, *, tq, n_top):
    f32, bf16 = jnp.float32, jnp.bfloat16
    G = NSA_GROUP
    R = G * tq
    t0 = pl.program_id(2) * tq
    ncp = kcmp_ref.shape[2]
    n_sel = ovT_ref.shape[0]

    qraw = jnp.concatenate([qraw_ref[0, g] for g in range(G)], axis=1)
    qrot = jnp.concatenate([qrot_ref[0, g] for g in range(G)], axis=1)

    s_c = _dot_bf16x3(kcmp_ref[0, 0], qraw)
    t_tok = t0 + lax.broadcasted_iota(jnp.int32, (1, tq), 1)
    cmp_last = lax.broadcasted_iota(jnp.int32, (ncp, tq), 0) * CMP_STRIDE + (CMP_LEN - 1)
    s_c = s_c + jnp.concatenate([jnp.where(cmp_last <= t_tok, 0.0, -jnp.inf)] * G, axis=1)
    m_c = jnp.max(s_c, axis=0, keepdims=True)
    m_c = jnp.where(m_c == -jnp.inf, 0.0, m_c)
    e_c = jnp.exp(s_c - m_c)
    p_c = e_c / jnp.maximum(jnp.sum(e_c, axis=0, keepdims=True), 1e-30)
    o_c = jnp.dot(vcmpT_ref[0, 0], p_c.astype(bf16), preferred_element_type=f32)

    p_sum = p_c[:, 0:tq]
    for g in range(1, G):
        p_sum = p_sum + p_c[:, g * tq:(g + 1) * tq]
    imp = _dot_bf16x3(ovT_ref[...], p_sum)
    j_iota = lax.broadcasted_iota(jnp.int32, (n_sel, tq), 0)
    cur = lax.shift_right_logical(t_tok, int(np.log2(SEL_LEN)))
    forced = (j_iota == 0) | (j_iota == cur) | (j_iota == cur - 1)
    cand = (j_iota >= 1) & (j_iota <= cur - 2)
    quota = n_top - 1 - jnp.minimum(cur, 2)
    key = jnp.where(cand, pltpu.bitcast(imp, jnp.int32), -1)
    key_ref[...] = key

    def rank_body(i, rank):
        row = key_ref[pl.ds(i, 1), :]
        return rank + jnp.where(row + jnp.where(j_iota > i, 1, 0) > key, 1, 0)

    i_end = jnp.maximum(lax.shift_right_logical(t0 + tq - 1, int(np.log2(SEL_LEN))) - 1, 1)
    rank = lax.fori_loop(1, i_end, rank_body, jnp.zeros((n_sel, tq), jnp.int32))
    sel = forced | (cand & (rank < quota))
    lim_ref[...] = jnp.where(sel, t_tok, -1)

    softmax_init = (jnp.full((1, R), NEG_INIT, f32), jnp.zeros((1, R), f32), jnp.zeros((NSA_HEAD, R), f32))

    def softmax_step(carry, s, vT):
        m, l, acc = carry
        m_new = jnp.maximum(m, jnp.max(s, axis=0, keepdims=True))
        alpha = jnp.exp2(m - m_new)
        p = jnp.exp2(s - m_new)
        l = alpha * l + jnp.sum(p, axis=0, keepdims=True)
        acc = alpha * acc + jnp.dot(vT, p.astype(bf16), preferred_element_type=f32)
        return m_new, l, acc

    def attend(score_fn, vT_fn, lo, hi, sa_ref, sb_ref):
        def scores(kt):
            pen = jnp.where(kt < hi, 0.0, -jnp.inf)
            return score_fn(jnp.minimum(kt, hi - 1), pen)

        def update(carry, s_ref, kt):
            return softmax_step(carry, s_ref[...], vT_fn(jnp.minimum(kt, hi - 1)))

        def body(i, carry):
            kt = lo + 2 * i
            sb_ref[...] = scores(kt + 1)
            carry = update(carry, sa_ref, kt)
            sa_ref[...] = scores(kt + 2)
            return update(carry, sb_ref, kt + 1)

        sa_ref[...] = scores(lo)
        _, l, acc = lax.fori_loop(0, lax.div(hi - lo + 1, 2), body, softmax_init)
        return acc / jnp.maximum(l, 1e-30)

    tile_g = lambda bias: jnp.concatenate([bias] * G, axis=1)

    tk = min(NSA_TK_SEL, ksel_ref.shape[2])
    nb = tk // SEL_LEN
    key_iota = lax.broadcasted_iota(jnp.int32, (tk, tq), 0)

    def sel_scores(kt, pen):
        k0 = pl.multiple_of(kt * tk, tk)
        s = jnp.dot(ksel_ref[0, 0, pl.ds(k0, tk), :], qrot, preferred_element_type=f32)
        limb = lim_ref[pl.ds(pl.multiple_of(kt * nb, nb), nb), :] - k0
        lim_t = jnp.concatenate([jnp.broadcast_to(limb[jb:jb + 1, :], (SEL_LEN, tq)) for jb in range(nb)], axis=0)
        return s + tile_g(jnp.where(key_iota <= lim_t, pen, -jnp.inf))

    o_s = attend(sel_scores, lambda kt: vselT_ref[0, 0, :, pl.ds(pl.multiple_of(kt * tk, tk), tk)],
                 0, lax.div(t0 + tq - 1, tk) + 1, ssa_ref, ssb_ref)

    tkw = min(NSA_TK_WIN, kwin_ref.shape[2])
    key_iota_w = lax.broadcasted_iota(jnp.int32, (tkw, tq), 0)

    assert WINDOW % tkw == 0 and tq % tkw == 0
    carry = softmax_init
    for i in range((WINDOW + tq) // tkw):
        kt = lax.div(t0, tkw) - WINDOW // tkw + i
        pen = jnp.where(kt >= 0, 0.0, -jnp.inf)
        k0 = pl.multiple_of(jnp.maximum(kt, 0) * tkw, tkw)
        s = jnp.dot(kwin_ref[0, 0, pl.ds(k0, tkw), :], qrot, preferred_element_type=f32)
        rel = t_tok - k0
        s = s + tile_g(jnp.where((key_iota_w <= rel) & (key_iota_w > rel - WINDOW), pen, -jnp.inf))
        carry = softmax_step(carry, s, vwinT_ref[0, 0, :, pl.ds(k0, tkw)])
    o_w = carry[2] / jnp.maximum(carry[1], 1e-30)

    gl = jnp.concatenate([gl_ref[0, g] for g in range(G)], axis=1)
    gates = jax.nn.sigmoid(gl)
    out = gates[0:1, :] * o_c + gates[1:2, :] * o_s + gates[2:3, :] * o_w
    per = LANES // NSA_HEAD
    for pair in range(G // per):
        rows = jnp.concatenate([out[:, (pair * per + j) * tq:(pair * per + j + 1) * tq] for j in range(per)], axis=0)
        o_ref[0, :, pair * LANES:(pair + 1) * LANES] = rows.T


NSA_PREP_T = 256
NSA_COL0 = RW_COLS


def _nsa_prep_kernel(qa_ref, qb_ref, ks_ref, kw_ref, gl_ref, cos_ref, sin_ref,
                     qraw_ref, qrot_ref, ksel_ref, vselT_ref, kwin_ref, vwinT_ref, glT_ref):
    bf16 = jnp.bfloat16
    d, half = NSA_HEAD, NSA_HEAD // 2
    scale = NSA_HEAD ** -0.5
    cos, sin = cos_ref[0], sin_ref[0]
    first = (lax.broadcasted_iota(jnp.int32, cos.shape, 1) & (d - 1)) < half

    def rope(x):
        partner = jnp.where(first, pltpu.roll(x, LANES - half, 1), pltpu.roll(x, half, 1))
        return x * cos + jnp.where(first, -partner, partner) * sin

    def put_heads_t(ref, first_head, x_t, dtype):
        for i in range(LANES // d):
            ref[0, first_head + i] = x_t[i * d:(i + 1) * d].astype(dtype)

    for j in range(NSA_HEADS * d // LANES):
        src = qa_ref if j < 2 else qb_ref
        piece = src[0][:, (j % 2) * LANES:(j % 2 + 1) * LANES]
        put_heads_t(qraw_ref, 2 * j, (piece * scale).T, jnp.float32)
        put_heads_t(qrot_ref, 2 * j, (rope(piece) * (scale * LOG2E)).T, bf16)
    for src, k_out, vT_out in ((ks_ref, ksel_ref, vselT_ref), (kw_ref, kwin_ref, vwinT_ref)):
        kv = src[0]
        kr = rope(kv[:, :LANES])
        for h in range(NSA_KV_HEADS):
            k_out[0, h] = kr[:, h * d:(h + 1) * d].astype(bf16)
        put_heads_t(vT_out, 0, kv[:, LANES:].T, bf16)
    gl_t = gl_ref[0].T
    for h in range(NSA_HEADS):
        glT_ref[0, h] = gl_t[3 * h:3 * h + 3]


def _nsa_prep(z, cos, sin):
    B, T, _ = z.shape
    f32, bf16 = jnp.float32, jnp.bfloat16
    t = min(NSA_PREP_T, T)
    H, Hkv, d = NSA_HEADS, NSA_KV_HEADS, NSA_HEAD
    assert NSA_COL0 % 256 == 0 and NSA_WIDTH == 512 and NSA_KV == LANES and (NSA_COL0 + NSA_WIDTH + 6 * NSA_KV) % LANES == 0
    c0 = NSA_COL0 // 256
    wide = lambda j: pl.BlockSpec((1, t, 256), lambda b, i: (b, i, c0 + j))
    gl_col = (NSA_COL0 + NSA_WIDTH + 6 * NSA_KV) // LANES
    tab = pl.BlockSpec((1, t, LANES), lambda b, i: (b, i, 0))
    tile4 = lambda c: jnp.tile(c.reshape(B, T, d // 2), (1, 1, LANES // (d // 2)))
    q_out = pl.BlockSpec((1, H, d, t), lambda b, i: (b, 0, 0, i))
    k_out = pl.BlockSpec((1, Hkv, t, d), lambda b, i: (b, 0, i, 0))
    vT_out = pl.BlockSpec((1, Hkv, d, t), lambda b, i: (b, 0, 0, i))
    return pl.pallas_call(
        _nsa_prep_kernel,
        grid=(B, T // t),
        in_specs=[wide(0), wide(1), wide(3), wide(4), pl.BlockSpec((1, t, LANES), lambda b, i: (b, i, gl_col)), tab, tab],
        out_specs=[q_out, q_out, k_out, vT_out, k_out, vT_out, pl.BlockSpec((1, H, 3, t), lambda b, i: (b, 0, 0, i))],
        out_shape=[jax.ShapeDtypeStruct((B, H, d, T), f32), jax.ShapeDtypeStruct((B, H, d, T), bf16),
                   jax.ShapeDtypeStruct((B, Hkv, T, d), bf16), jax.ShapeDtypeStruct((B, Hkv, d, T), bf16),
                   jax.ShapeDtypeStruct((B, Hkv, T, d), bf16), jax.ShapeDtypeStruct((B, Hkv, d, T), bf16),
                   jax.ShapeDtypeStruct((B, H, 3, T), f32)],
        compiler_params=pltpu.CompilerParams(dimension_semantics=("arbitrary", "arbitrary"),
                                             vmem_limit_bytes=VMEM_LIMIT_BYTES),
        name="nsa_prep",
    )(z, z, z, z, z, tile4(cos), tile4(sin))


def _nsa_group(z, cos, sin, cmp_pos, cmp_w1, cmp_w2):
    B, T, _ = z.shape
    f32, bf16 = jnp.float32, jnp.bfloat16
    qraw_T, qrot_T, ksel, vselT, kwin, vwinT, gl_T = _nsa_prep(z, cos, sin)
    kc = z[..., NSA_COL0 + NSA_WIDTH:NSA_COL0 + NSA_WIDTH + NSA_KV]
    vc = z[..., NSA_COL0 + NSA_WIDTH + NSA_KV:NSA_COL0 + NSA_WIDTH + 2 * NSA_KV]
    kv_heads = lambda t: t.reshape(B, T, NSA_KV_HEADS, NSA_HEAD)

    n_grp = T // CMP_STRIDE
    n_cmp = n_grp - 1
    half = CMP_STRIDE * NSA_HEAD
    grp = jnp.stack([kv_heads(kc), kv_heads(vc)]).transpose(0, 1, 3, 2, 4).reshape(2, B * NSA_KV_HEADS * n_grp, half)
    cmp_kv = []
    for zi in range(2):
        w1ab = jnp.concatenate([cmp_w1[zi, :half], cmp_w1[zi, half:]], axis=1)
        ab = _dense(grp[zi], w1ab).reshape(B, NSA_KV_HEADS, n_grp, 2 * CMP_HIDDEN)
        c = cmp_pos[zi].reshape(1, CMP_LEN * NSA_HEAD) @ cmp_w1[zi]
        hid = jax.nn.gelu(ab[:, :, :-1, :CMP_HIDDEN] + ab[:, :, 1:, CMP_HIDDEN:] + c)
        cmp_kv.append(jnp.pad(hid @ cmp_w2[zi], ((0, 0), (0, 0), (0, 1), (0, 0))))
    k_cmp = cmp_kv[0]
    v_cmpT = cmp_kv[1].transpose(0, 1, 3, 2).astype(bf16)

    n_sel = T // SEL_LEN
    n_top = min(SEL_TOP, n_sel)
    ovT = jnp.asarray(np.pad(_cmp_sel_overlap(n_cmp, n_sel), ((0, 1), (0, 0))).T)
    tq = min(NSA_TQ, T)
    G = NSA_GROUP
    q_spec = pl.BlockSpec((1, G, NSA_HEAD, tq), lambda b, h, i: (b, h, 0, i))
    kv_spec = lambda shape: pl.BlockSpec((1, 1) + shape, lambda b, h, i: (b, h, 0, 0))
    out = pl.pallas_call(
        functools.partial(_nsa_attn_kernel, tq=tq, n_top=n_top),
        grid=(B, NSA_KV_HEADS, T // tq),
        in_specs=[q_spec, q_spec, kv_spec((n_grp, NSA_HEAD)), kv_spec((NSA_HEAD, n_grp)),
                  pl.BlockSpec((n_sel, n_grp), lambda b, h, i: (0, 0)),
                  kv_spec((T, NSA_HEAD)), kv_spec((NSA_HEAD, T)), kv_spec((T, NSA_HEAD)), kv_spec((NSA_HEAD, T)),
                  pl.BlockSpec((1, G, 3, tq), lambda b, h, i: (b, h, 0, i))],
        out_specs=pl.BlockSpec((1, tq, G * NSA_HEAD), lambda b, h, i: (b, i, h)),
        out_shape=jax.ShapeDtypeStruct((B, T, NSA_WIDTH), f32),
        scratch_shapes=[pltpu.VMEM((n_sel, tq), jnp.int32), pltpu.VMEM((n_sel, tq), jnp.int32)]
        + [pltpu.VMEM((min(NSA_TK_SEL, T), G * tq), f32)] * 2,
        compiler_params=pltpu.CompilerParams(dimension_semantics=("arbitrary", "arbitrary", "arbitrary"),
                                             vmem_limit_bytes=VMEM_LIMIT_BYTES),
        name="nsa_attn",
    )(qraw_T, qrot_T, k_cmp, v_cmpT, ovT, ksel, vselT, kwin, vwinT, gl_T)
    return out


HG_STEP = 256
HG_HEADS_PER_STEP = 8


def _hgrn2_kernel(q_ref, f_ref, i_ref, g_ref, lb_ref, ng_ref, o_ref, st_ref):
    f32, bf16 = jnp.float32, jnp.bfloat16
    S, d = q_ref.shape[1], HG_HEAD
    hb = q_ref.shape[2] // d
    C = HG_CHUNK
    n_sub = S // C
    shift = int(np.log2(C))

    @pl.when(pl.program_id(2) == 0)
    def _():
        st_ref[...] = jnp.zeros_like(st_ref)

    heads = lambda ref: jnp.stack([ref[0][:, h * d:(h + 1) * d] for h in range(hb)])
    lb = lb_ref[...]
    forget = lb + (1.0 - lb) * _sigmoid(heads(f_ref))
    logf = jnp.log(forget)
    k = 1.0 - forget
    q = heads(q_ref)
    qs = q * _sigmoid(q)
    v = heads(i_ref)

    pn = min(LANES, S)
    n_pan = S // pn
    panels = lambda x: x.reshape(hb * n_pan, pn, d)
    row = lax.broadcasted_iota(jnp.int32, (pn, pn), 0)
    col = lax.broadcasted_iota(jnp.int32, (pn, pn), 1)
    same = lax.shift_right_logical(row, shift) == lax.shift_right_logical(col, shift)
    causal = same & (row >= col)

    def cumsum01(mask, x):
        m = jnp.broadcast_to(jnp.where(mask, 1.0, 0.0).astype(bf16), (hb * n_pan, pn, pn))
        return sum(jnp.einsum('hij,hjk->hik', m, t, preferred_element_type=f32) for t in _split3(x))

    logf_p = panels(logf)
    b = cumsum01(causal, logf_p)
    tail = cumsum01(same & (col > row), logf_p)
    q_e = (panels(qs) * jnp.exp(b)).astype(bf16)
    k_e = (panels(k) * jnp.exp(-b)).astype(bf16)
    k_tail = (panels(k) * jnp.exp(tail)).reshape(hb, n_pan, pn, d)
    v_p = panels(v)

    a = jnp.einsum('hsd,htd->hst', q_e, k_e, preferred_element_type=f32)
    o_intra = jnp.einsum('hst,htv->hsv', jnp.where(causal, a, 0.0).astype(bf16), v_p.astype(bf16),
                         preferred_element_type=f32).reshape(hb, S, d)

    vT = jnp.stack([v_p[i].T for i in range(hb * n_pan)]).astype(bf16).reshape(hb, n_pan, d, pn)
    q_e = q_e.reshape(hb, S, d)
    b = b.reshape(hb, S, d)
    row_chunk = lax.shift_right_logical(lax.broadcasted_iota(jnp.int32, (pn, d), 0), shift)
    st = st_ref[...]
    o_inter = []
    for c in range(n_sub):
        sl = slice(c * C, (c + 1) * C)
        pan, c_in = divmod(c, pn // C)
        o_inter.append(jnp.einsum('hcd,hvd->hcv', q_e[:, sl], st.astype(bf16), preferred_element_type=f32))
        k_c = jnp.where(row_chunk == c_in, k_tail[:, pan], 0.0).astype(bf16)
        d_c = jnp.exp(b[:, (c + 1) * C - 1:(c + 1) * C, :])
        st = st * d_c + jnp.einsum('hvs,hsk->hvk', vT[:, pan], k_c, preferred_element_type=f32)
    st_ref[...] = st
    o = o_intra + jnp.concatenate(o_inter, axis=1)
    o = o * lax.rsqrt(jnp.mean(o * o, axis=-1, keepdims=True) + LN_EPS) * ng_ref[...]
    g = heads(g_ref)
    o = o * (g * _sigmoid(g))
    for h in range(hb):
        o_ref[0, :, h * d:(h + 1) * d] = o[h]


def _hgrn2_mixer(x, w_in, lb, norm_g):
    B, T, D = x.shape
    z = _dense(x.reshape(B * T, D), w_in).reshape(B, T, -1)
    S = min(HG_STEP, T)
    H, d = HG_HEADS, HG_HEAD
    hb = HG_HEADS_PER_STEP
    ng = H // hb
    col = lambda j: pl.BlockSpec((1, S, hb * d), lambda b, h, t: (b, t, j * ng + h))
    vec = pl.BlockSpec((hb, 1, d), lambda b, h, t: (h, 0, 0))
    o = pl.pallas_call(
        _hgrn2_kernel,
        grid=(B, ng, T // S),
        in_specs=[col(0), col(1), col(2), col(3), vec, pl.BlockSpec((1, 1, d), lambda b, h, t: (0, 0, 0))],
        out_specs=pl.BlockSpec((1, S, hb * d), lambda b, h, t: (b, t, h)),
        out_shape=jax.ShapeDtypeStruct((B, T, D), jnp.float32),
        scratch_shapes=[pltpu.VMEM((hb, d, d), jnp.float32)],
        compiler_params=pltpu.CompilerParams(dimension_semantics=("arbitrary", "arbitrary", "arbitrary"),
                                             vmem_limit_bytes=VMEM_LIMIT_BYTES),
        name="hgrn2",
    )(z, z, z, z, lb.reshape(H, 1, d), norm_g.reshape(1, 1, d))
    return o.reshape(B * T, D)


def _moe_expert_kernel(blk_e_ref, x_ref, w1_ref, w3_ref, w2_ref, o_ref):
    del blk_e_ref
    f32, bf16 = jnp.float32, jnp.bfloat16
    x = x_ref[...].astype(bf16)
    a = jnp.dot(x, w1_ref[0], preferred_element_type=f32)
    b = jnp.dot(x, w3_ref[0], preferred_element_type=f32)
    hid = (jax.nn.silu(a) * b).astype(bf16)
    o_ref[...] = jnp.dot(hid, w2_ref[0], preferred_element_type=f32)


def _moe_experts(xbuf, blk_e, w1, w3, w2):
    P, D = xbuf.shape
    hid = w1.shape[-1]
    bf16 = jnp.bfloat16
    w_spec = lambda shape: pl.BlockSpec((1,) + shape, lambda i, be: (be[i], 0, 0))
    x_spec = pl.BlockSpec((MOE_BLOCK, D), lambda i, be: (i, 0))
    return pl.pallas_call(
        _moe_expert_kernel,
        grid_spec=pltpu.PrefetchScalarGridSpec(
            num_scalar_prefetch=1,
            grid=(P // MOE_BLOCK,),
            in_specs=[x_spec, w_spec((D, hid)), w_spec((D, hid)), w_spec((hid, D))],
            out_specs=x_spec,
        ),
        out_shape=jax.ShapeDtypeStruct((P, D), jnp.float32),
        compiler_params=pltpu.CompilerParams(dimension_semantics=("arbitrary",), vmem_limit_bytes=VMEM_LIMIT_BYTES),
        name="moe_experts",
    )(blk_e, xbuf, w1.astype(bf16), w3.astype(bf16), w2.astype(bf16))


MOE_ROUTE_TM = 512
MOE_ROW_TM = 256
MOE_ROUTE_ROWS = 40


def _moe_route_kernel(h_ref, wr_ref, br_ref, eid_ref, gate_ref, pos_ref, cnt_ref, run_ref):
    f32 = jnp.float32
    tm = h_ref.shape[0]
    E, PG, NG = MOE_EXPERTS, MOE_PER_GROUP, MOE_GROUPS

    @pl.when(pl.program_id(0) == 0)
    def _():
        run_ref[...] = jnp.zeros_like(run_ref)

    lg = lax.dot_general(wr_ref[...], h_ref[...], (((1,), (1,)), ((), ())), precision=lax.Precision.HIGHEST,
                         preferred_element_type=f32) + br_ref[...]
    grp = lg[E:E + NG]
    g_iota = lax.broadcasted_iota(jnp.int32, (NG, tm), 0)
    g_max = jnp.max(grp, axis=0, keepdims=True)
    g_sel = jnp.min(jnp.where(grp == g_max, g_iota, NG), axis=0, keepdims=True)
    p_grp = 1.0 / jnp.sum(jnp.exp(grp - g_max), axis=0, keepdims=True)
    le = lg[0:PG]
    for g in range(1, NG):
        le = jnp.where(g_sel == g, lg[g * PG:(g + 1) * PG], le)
    e_iota = lax.broadcasted_iota(jnp.int32, (PG, tm), 0)
    m1 = jnp.max(le, axis=0, keepdims=True)
    i1 = jnp.min(jnp.where(le == m1, e_iota, PG), axis=0, keepdims=True)
    le2 = jnp.where(e_iota == i1, -jnp.inf, le)
    m2 = jnp.max(le2, axis=0, keepdims=True)
    i2 = jnp.min(jnp.where(le2 == m2, e_iota, PG), axis=0, keepdims=True)
    e2 = jnp.exp(m2 - m1)
    den = 1.0 + e2
    eid1 = g_sel * PG + i1
    eid2 = g_sel * PG + i2
    eid_ref[...] = jnp.concatenate([eid1, eid2], axis=0)
    gate_ref[...] = jnp.concatenate([p_grp / den, p_grp * e2 / den], axis=0)

    x_iota = lax.broadcasted_iota(jnp.int32, (E, tm), 0)
    oh1 = jnp.where(x_iota == eid1, 1.0, 0.0)
    oh2 = jnp.where(x_iota == eid2, 1.0, 0.0)
    before = jnp.where(lax.broadcasted_iota(jnp.int32, (tm, tm), 0) < lax.broadcasted_iota(jnp.int32, (tm, tm), 1),
                       1.0, 0.0).astype(jnp.bfloat16)
    cum1 = jnp.dot(oh1.astype(jnp.bfloat16), before, preferred_element_type=f32)
    cum2 = jnp.dot(oh2.astype(jnp.bfloat16), before, preferred_element_type=f32)
    tot1 = jnp.sum(oh1, axis=1, keepdims=True)
    tot2 = jnp.sum(oh2, axis=1, keepdims=True)
    base = run_ref[:, 0:1]
    pos1 = jnp.sum(oh1 * (cum1 + base), axis=0, keepdims=True)
    pos2 = jnp.sum(oh2 * (cum2 + base + tot1), axis=0, keepdims=True)
    pos_ref[...] = jnp.concatenate([pos1, pos2], axis=0).astype(jnp.int32)
    new = jnp.broadcast_to(base + tot1 + tot2, run_ref.shape)
    run_ref[...] = new
    cnt_ref[...] = new


def _moe_scatter_kernel(zblk_ref, dest_ref, h_ref, xbuf_ref, zero_ref, zsem, sem):
    i = pl.program_id(0)
    rt = dest_ref.shape[1]

    @pl.when(i == 0)
    def _():
        zero_ref[...] = jnp.zeros_like(zero_ref)

        def zero_copy(j):
            row0 = pl.multiple_of(zblk_ref[j] * MOE_BLOCK, MOE_BLOCK)
            return pltpu.make_async_copy(zero_ref, xbuf_ref.at[pl.ds(row0, MOE_BLOCK)], zsem)

        def z_start(j, c):
            @pl.when(zblk_ref[j] >= 0)
            def _():
                zero_copy(j).start()
            return c

        def z_wait(j, c):
            @pl.when(zblk_ref[j] >= 0)
            def _():
                zero_copy(j).wait()
            return c

        lax.fori_loop(0, zblk_ref.shape[0], z_start, 0)
        lax.fori_loop(0, zblk_ref.shape[0], z_wait, 0)

    def row_copy(r, j):
        return pltpu.make_async_copy(h_ref.at[pl.ds(r, 1)], xbuf_ref.at[pl.ds(dest_ref[j, r], 1)], sem)

    def drain(r, c):
        row_copy(r, 0).wait()
        row_copy(r, 1).wait()
        return c

    for r in range(rt):
        row_copy(r, 0).start()
        row_copy(r, 1).start()
    lax.fori_loop(0, rt, drain, 0, unroll=8)


def _moe_combine_ln_kernel(dest_ref, dnext_ref, gate_ref, h_ref, lng_ref, lnb_ref, ybuf_ref, o_ref, buf_ref, sems):
    i = pl.program_id(0)
    n = pl.num_programs(0)
    rt = h_ref.shape[0]
    cur, nxt = i % 2, (i + 1) % 2

    def row_copy(d_ref, r, j, slot):
        return pltpu.make_async_copy(ybuf_ref.at[pl.ds(d_ref[j, r], 1)], buf_ref.at[slot, j, pl.ds(r, 1)], sems.at[slot])

    def gather(d_ref, slot):
        def body(r, c):
            row_copy(d_ref, r, 0, slot).start()
            row_copy(d_ref, r, 1, slot).start()
            return c
        lax.fori_loop(0, rt, body, 0, unroll=8)

    def drain(slot):
        def body(r, c):
            row_copy(dest_ref, r, 0, slot).wait()
            row_copy(dest_ref, r, 1, slot).wait()
            return c
        lax.fori_loop(0, rt, body, 0, unroll=8)

    @pl.when(i == 0)
    def _():
        gather(dest_ref, 0)

    for slot in range(2):
        @pl.when(nxt == slot)
        def _(slot=slot):
            for r in range(rt):
                row_copy(dnext_ref, r, 0, slot).start()
                row_copy(dnext_ref, r, 1, slot).start()

    drain(cur)
    gate = gate_ref[...]
    ffn = gate[:, 0:1] * buf_ref[cur, 0] + gate[:, 1:2] * buf_ref[cur, 1]
    o_ref[...] = _layer_norm(DN_ALPHA * h_ref[...] + ffn, lng_ref[...], lnb_ref[...])

    @pl.when(i == n - 1)
    def _():
        drain(nxt)


def _hier_moe_ln(h, w_rg, b_rg, w_re, b_re, w1, w3, w2, ln_g, ln_b):
    M, D = h.shape
    f32, i32 = jnp.float32, jnp.int32
    E = MOE_EXPERTS
    pad_rows = MOE_ROUTE_ROWS - E - MOE_GROUPS
    wr = jnp.concatenate([w_re.T, w_rg.T, jnp.zeros((pad_rows, D), f32)], axis=0)
    br = jnp.concatenate([b_re, b_rg, jnp.zeros((pad_rows,), f32)]).reshape(MOE_ROUTE_ROWS, 1)
    tm = min(MOE_ROUTE_TM, M)
    slot_spec = pl.BlockSpec((MOE_TOPK, tm), lambda i: (0, i))
    eid, gate, pos, cnt = pl.pallas_call(
        _moe_route_kernel,
        grid=(M // tm,),
        in_specs=[pl.BlockSpec((tm, D), lambda i: (i, 0)), pl.BlockSpec((MOE_ROUTE_ROWS, D), lambda i: (0, 0)),
                  pl.BlockSpec((MOE_ROUTE_ROWS, 1), lambda i: (0, 0))],
        out_specs=[slot_spec, slot_spec, slot_spec, pl.BlockSpec((E, LANES), lambda i: (0, 0))],
        out_shape=[jax.ShapeDtypeStruct((MOE_TOPK, M), i32), jax.ShapeDtypeStruct((MOE_TOPK, M), f32),
                   jax.ShapeDtypeStruct((MOE_TOPK, M), i32), jax.ShapeDtypeStruct((E, LANES), f32)],
        scratch_shapes=[pltpu.VMEM((E, LANES), f32)],
        compiler_params=pltpu.CompilerParams(dimension_semantics=("arbitrary",), vmem_limit_bytes=VMEM_LIMIT_BYTES),
        name="moe_route",
    )(h, wr, br)

    counts = cnt[:, 0].astype(i32)
    padded = (counts + MOE_BLOCK - 1) // MOE_BLOCK * MOE_BLOCK
    ends = jnp.cumsum(padded)
    start = ends - padded
    P = M * MOE_TOPK + E * MOE_BLOCK
    n_blk = P // MOE_BLOCK
    blk_e = jnp.minimum(jnp.sum(ends[None, :] <= (jnp.arange(n_blk) * MOE_BLOCK)[:, None], axis=1), E - 1).astype(i32)
    dest = pos + jnp.sum(jnp.where(eid[:, :, None] == jnp.arange(E), start, 0), axis=-1)

    rt = min(MOE_ROW_TM, M)
    n_row = M // rt
    any_spec = pl.BlockSpec(memory_space=pl.ANY)
    last_blk = jnp.where(padded > 0, ends // MOE_BLOCK - 1, -1)
    tail_blk = ends[-1] // MOE_BLOCK + jnp.arange(E)
    zblk = jnp.concatenate([last_blk, jnp.where(tail_blk < n_blk, tail_blk, -1)]).astype(i32)
    xbuf = pl.pallas_call(
        _moe_scatter_kernel,
        grid_spec=pltpu.PrefetchScalarGridSpec(
            num_scalar_prefetch=1,
            grid=(n_row,),
            in_specs=[pl.BlockSpec((MOE_TOPK, rt), lambda i, zb: (0, i), memory_space=pltpu.SMEM),
                      pl.BlockSpec((rt, D), lambda i, zb: (i, 0))],
            out_specs=any_spec,
            scratch_shapes=[pltpu.VMEM((MOE_BLOCK, D), f32), pltpu.SemaphoreType.DMA(()), pltpu.SemaphoreType.DMA(())],
        ),
        out_shape=jax.ShapeDtypeStruct((P, D), f32),
        compiler_params=pltpu.CompilerParams(dimension_semantics=("arbitrary",)),
        name="moe_scatter",
    )(zblk, dest, h)

    ybuf = _moe_experts(xbuf, blk_e, w1, w3, w2)

    dest_spec = pl.BlockSpec((MOE_TOPK, rt), lambda i: (0, i), memory_space=pltpu.SMEM)
    dnext_spec = pl.BlockSpec((MOE_TOPK, rt), lambda i: (0, jnp.minimum(i + 1, n_row - 1)), memory_space=pltpu.SMEM)
    row_spec = pl.BlockSpec((rt, D), lambda i: (i, 0))
    vec_spec = pl.BlockSpec((1, D), lambda i: (0, 0))
    return pl.pallas_call(
        _moe_combine_ln_kernel,
        grid=(n_row,),
        in_specs=[dest_spec, dnext_spec, pl.BlockSpec((rt, MOE_TOPK), lambda i: (i, 0)), row_spec, vec_spec, vec_spec,
                  any_spec],
        out_specs=row_spec,
        out_shape=jax.ShapeDtypeStruct((M, D), f32),
        scratch_shapes=[pltpu.VMEM((2, MOE_TOPK, rt, D), f32), pltpu.SemaphoreType.DMA((2,))],
        compiler_params=pltpu.CompilerParams(dimension_semantics=("arbitrary",), vmem_limit_bytes=VMEM_LIMIT_BYTES),
        name="moe_combine_ln",
    )(dest, dest, gate.T, h, ln_g.reshape(1, D), ln_b.reshape(1, D), ybuf)


def _dense_res_ln_kernel(*refs):
    n = (len(refs) - 4) // 2
    x_ref, g_ref, b_ref, o_ref = refs[2 * n:]
    mix = DN_ALPHA * x_ref[...]
    for y_ref, w_ref in zip(refs[:n], refs[n:2 * n]):
        mix = mix + jnp.dot(y_ref[...].astype(jnp.bfloat16), w_ref[...], preferred_element_type=jnp.float32)
    o_ref[...] = _layer_norm(mix, g_ref[...], b_ref[...])


def _dense_res_ln(ys, w, x, ln_g, ln_b, tm=512):
    m, d = x.shape
    tm = min(tm, m)
    cuts = np.cumsum([0] + [y.shape[1] for y in ys])
    ws = [w[int(a):int(b)].astype(jnp.bfloat16) for a, b in zip(cuts[:-1], cuts[1:])]
    row_spec = lambda k: pl.BlockSpec((tm, k), lambda i: (i, 0))
    vec_spec = pl.BlockSpec((1, d), lambda i: (0, 0))
    return pl.pallas_call(
        _dense_res_ln_kernel,
        grid=(m // tm,),
        in_specs=[row_spec(y.shape[1]) for y in ys] + [pl.BlockSpec(wi.shape, lambda i: (0, 0)) for wi in ws]
        + [row_spec(d), vec_spec, vec_spec],
        out_specs=row_spec(d),
        out_shape=jax.ShapeDtypeStruct((m, d), jnp.float32),
        compiler_params=pltpu.CompilerParams(dimension_semantics=("arbitrary",), vmem_limit_bytes=VMEM_LIMIT_BYTES),
        name="dense_res_ln",
    )(*ys, *ws, x, ln_g.reshape(1, d), ln_b.reshape(1, d))


def _ple_kernel(h_ref, p_ref, wg_ref, wp_ref, o_ref):
    bf16, f32 = jnp.bfloat16, jnp.float32
    h = h_ref[...]
    gate = jax.nn.sigmoid(jnp.dot(h.astype(bf16), wg_ref[...], preferred_element_type=f32))
    o_ref[...] = h + gate * jnp.dot(p_ref[...].astype(bf16), wp_ref[...], preferred_element_type=f32)


def _ple(h, p, wg, wp, tm=512):
    m, d = h.shape
    kp = p.shape[1]
    tm = min(tm, m)
    return pl.pallas_call(
        _ple_kernel,
        grid=(m // tm,),
        in_specs=[pl.BlockSpec((tm, d), lambda i: (i, 0)), pl.BlockSpec((tm, kp), lambda i: (i, 0)),
                  pl.BlockSpec((d, d), lambda i: (0, 0)), pl.BlockSpec((kp, d), lambda i: (0, 0))],
        out_specs=pl.BlockSpec((tm, d), lambda i: (i, 0)),
        out_shape=jax.ShapeDtypeStruct((m, d), jnp.float32),
        compiler_params=pltpu.CompilerParams(dimension_semantics=("arbitrary",), vmem_limit_bytes=VMEM_LIMIT_BYTES),
        name="ple",
    )(h, p, wg.astype(jnp.bfloat16), wp.astype(jnp.bfloat16))


def kernel(x, p, positions, ev_w_in, ev_w_out, rw_mu, rw_w0, rw_w2, rw_a0, rw_a2, rw_g2, rw_k_k, rw_k_a,
           rw_r_k, rw_gn_g, rw_gn_b, nsa_cmp_pos, nsa_cmp_w1, nsa_cmp_w2, od_w_in, od_w_out, hg_lb, hg_norm_g,
           moe_w_rg, moe_b_rg, moe_w_re, moe_b_re, moe_w1, moe_w3, moe_w2, ln_g, ln_b, ple_w, ple_gate_w):
    B, T, D = x.shape
    M = B * T
    cos, sin = _rope_tables(positions, NSA_HEAD)
    lb_soft = jax.nn.softmax(hg_lb, axis=0)
    lb_all = jnp.cumsum(lb_soft, axis=0) - lb_soft[0:1]
    for li in range(DEPTH):
        j = li // 2
        if li % 2 == 0:
            z = _dense(x.reshape(M, D), ev_w_in[j], keep_pad=True).reshape(B, T, -1)
            y_rw = _rwkv7_branch(z, rw_mu[j], rw_w0[j], rw_w2[j], rw_a0[j], rw_a2[j], rw_g2[j], rw_k_k[j],
                                 rw_k_a[j], rw_r_k[j], rw_gn_g[j], rw_gn_b[j])
            y_nsa = _nsa_group(z, cos, sin, nsa_cmp_pos[j], nsa_cmp_w1[j], nsa_cmp_w2[j])
            ys, w_out = [y_rw.reshape(M, RW_WIDTH), y_nsa.reshape(M, NSA_WIDTH)], ev_w_out[j]
        else:
            ys, w_out = [_hgrn2_mixer(x, od_w_in[j], lb_all[li], hg_norm_g[j])], od_w_out[j]
        h = _dense_res_ln(ys, w_out, x.reshape(M, D), ln_g[li, 0], ln_b[li, 0])
        h = _hier_moe_ln(h, moe_w_rg[li], moe_b_rg[li], moe_w_re[li], moe_b_re[li], moe_w1[li], moe_w3[li],
                         moe_w2[li], ln_g[li, 1], ln_b[li, 1])
        x = _ple(h, p[li].reshape(M, PLE_DIM), ple_gate_w[li], ple_w[li]).reshape(B, T, D)
    return x
```

```python
import functools

import numpy as np
import jax
import jax.numpy as jnp
from jax import lax
from jax.experimental import pallas as pl
from jax.experimental.pallas import tpu as pltpu

D_MODEL = 1024
DEPTH = 2
PLE_DIM = 256
DN_ALPHA = (2 * DEPTH) ** 0.25
LN_EPS = 1e-5
ROPE_THETA = 10000.0

RW_WIDTH = D_MODEL // 2
RW_HEAD = 64
RW_HEADS = RW_WIDTH // RW_HEAD
RW_DECAY_LORA = 64
RW_AAA_LORA = 64
RW_GATE_LORA = 128
RW_GN_EPS = RW_HEAD * 1e-5
RW_SPLITS = (RW_WIDTH, RW_WIDTH, RW_WIDTH, RW_DECAY_LORA, RW_AAA_LORA, RW_GATE_LORA)
RW_COLS = sum(RW_SPLITS)

NSA_WIDTH = D_MODEL - RW_WIDTH
NSA_HEAD = 64
NSA_HEADS = NSA_WIDTH // NSA_HEAD
NSA_KV_HEADS = 2
NSA_GROUP = NSA_HEADS // NSA_KV_HEADS
NSA_KV = NSA_KV_HEADS * NSA_HEAD
CMP_LEN = 32
CMP_STRIDE = 16
CMP_HIDDEN = 128
SEL_LEN = 64
SEL_TOP = 16
WINDOW = 512
NSA_SPLITS = (NSA_WIDTH,) + (NSA_KV,) * 6 + (3 * NSA_HEADS,)
NSA_COLS = sum(NSA_SPLITS)
EV_COLS = RW_COLS + NSA_COLS

HG_HEAD = 128
HG_HEADS = D_MODEL // HG_HEAD
HG_CHUNK = 16

MOE_GROUPS = 4
MOE_PER_GROUP = 8
MOE_EXPERTS = MOE_GROUPS * MOE_PER_GROUP
MOE_TOPK = 2
MOE_BLOCK = 512

LANES = 128
VMEM_LIMIT_BYTES = 56 * 1024 * 1024


def _round_up(n, m):
    return (n + m - 1) // m * m


def _dense_kernel(x_ref, w_ref, o_ref):
    o_ref[...] = jnp.dot(x_ref[...].astype(jnp.bfloat16), w_ref[...], preferred_element_type=jnp.float32)


def _dense(x2d, w, tm=512, keep_pad=False):
    m, k = x2d.shape
    n = w.shape[1]
    n_pad = _round_up(n, LANES)
    wb = w.astype(jnp.bfloat16)
    if n_pad != n:
        wb = jnp.pad(wb, ((0, 0), (0, n_pad - n)))
    tm = min(tm, m)
    assert m % tm == 0
    out = pl.pallas_call(
        _dense_kernel,
        grid=(m // tm,),
        in_specs=[pl.BlockSpec((tm, k), lambda i: (i, 0)), pl.BlockSpec((k, n_pad), lambda i: (0, 0))],
        out_specs=pl.BlockSpec((tm, n_pad), lambda i: (i, 0)),
        out_shape=jax.ShapeDtypeStruct((m, n_pad), jnp.float32),
        compiler_params=pltpu.CompilerParams(dimension_semantics=("arbitrary",), vmem_limit_bytes=VMEM_LIMIT_BYTES),
        name="dense",
    )(x2d, wb)
    return out if keep_pad or n_pad == n else out[:, :n]


def _sigmoid(x):
    return 1.0 / (1.0 + jnp.exp(-x))


def _layer_norm(x, g, b):
    xc = x - jnp.mean(x, -1, keepdims=True)
    var = jnp.mean(xc * xc, -1, keepdims=True)
    return xc * lax.rsqrt(var + LN_EPS) * g + b


def _rope_tables(positions, dim):
    inv = (1.0 / (ROPE_THETA ** (np.arange(0, dim, 2, dtype=np.float32) / dim))).astype(np.float32)
    ang = positions.astype(jnp.float32)[..., None] * inv
    return jnp.cos(ang)[:, :, None, :], jnp.sin(ang)[:, :, None, :]


RW_CHUNK = 64
RW_STEP = 128


def _split3(x):
    bf16, f32 = jnp.bfloat16, jnp.float32
    h1 = x.astype(bf16)
    r1 = x - h1.astype(f32)
    h2 = r1.astype(bf16)
    return h1, h2, (r1 - h2.astype(f32)).astype(bf16)


def _rwkv7_kernel(z_ref, mu_ref, vec_ref, w2_ref, a2_ref, g2_ref, bd_ref, o_ref, carry_ref, h_ref):
    f32, bf16 = jnp.float32, jnp.bfloat16
    S = z_ref.shape[1]
    W, N, H, C = RW_WIDTH, RW_HEAD, RW_HEADS, RW_CHUNK
    n_sub = S // C
    lora_w = RW_DECAY_LORA + RW_AAA_LORA

    @pl.when(pl.program_id(1) == 0)
    def _():
        carry_ref[...] = jnp.zeros_like(carry_ref)
        h_ref[...] = jnp.zeros_like(h_ref)

    dotf = lambda a, b: jnp.dot(a, b, preferred_element_type=f32)
    bd = bd_ref[...]
    head_sum = lambda x: sum(dotf(t, bd) for t in _split3(x))

    z = z_ref[0]
    z_prev = jnp.concatenate([carry_ref[0:1, :], z[:S - 1, :]], axis=0)
    carry_ref[0:1, :] = z[S - 1:S, :]
    zs = z + mu_ref[...] * (z_prev - z)
    r, k, v = zs[:, 0:W], zs[:, W:2 * W], zs[:, 2 * W:3 * W]
    lora = zs[:, 3 * W:3 * W + lora_w]
    gd = zs[:, 3 * W + lora_w:]
    w0, a0, k_k, k_a, r_k, gn_g, gn_b = (vec_ref[i:i + 1, :] for i in range(7))
    w_pre = -(w0 + dotf(jnp.tanh(lora).astype(bf16), w2_ref[...]))
    softplus = jnp.maximum(w_pre, 0.0) + jnp.log(1.0 + jnp.exp(-jnp.abs(w_pre)))
    lw = -jnp.exp(-softplus - 0.5)
    a = jax.nn.sigmoid(a0 + dotf(lora.astype(bf16), a2_ref[...]))
    g = dotf(jax.nn.sigmoid(gd).astype(bf16), g2_ref[...])
    kk = k * k_k
    kk = kk / jnp.maximum(jnp.sqrt(head_sum(kk * kk)), 1e-12)
    k = k * (1.0 + (a - 1.0) * k_a)
    b = a * kk

    row = lax.broadcasted_iota(jnp.int32, (S, S), 0)
    col = lax.broadcasted_iota(jnp.int32, (S, S), 1)
    same = lax.shift_right_logical(row, int(np.log2(C))) == lax.shift_right_logical(col, int(np.log2(C)))
    incl = same & (row >= col)
    strict = same & (row > col)
    tri = jnp.where(incl, 1.0, 0.0).astype(bf16)
    cs = sum(dotf(tri, t) for t in _split3(lw))
    e_neg = jnp.exp(-cs)

    stack = lambda x: jnp.stack([x[:, h * N:(h + 1) * N] for h in range(H)])

    def stack_t(x):
        parts = []
        for j in range(W // LANES):
            t = x[:, j * LANES:(j + 1) * LANES].T
            parts += [t[i * N:(i + 1) * N] for i in range(LANES // N)]
        return jnp.stack(parts)

    kks = stack(kk * jnp.exp(cs - lw))
    rs = stack(r * jnp.exp(cs))
    vh = stack(v).astype(bf16)
    bsT = stack_t(b * e_neg)
    ksT = stack_t(k * e_neg)
    csT = stack_t(cs)

    def bmm(x, y):
        return jnp.einsum('hij,hjk->hik', x.astype(bf16), y.astype(bf16), preferred_element_type=f32)

    lhs = jnp.concatenate([kks, rs], axis=1)
    mb = bmm(lhs, bsT)
    mk = bmm(lhs, ksT)
    a_b = jnp.where(strict, mb[:, :S], 0.0)
    a_k = jnp.where(strict, mk[:, :S], 0.0)
    q_b = jnp.where(incl, mb[:, S:], 0.0)
    q_k = jnp.where(incl, mk[:, S:], 0.0)

    t_inv = jnp.where(row == col, 1.0, 0.0) - a_b
    pw = a_b
    for _ in range(int(np.log2(C)) - 1):
        pw = bmm(pw, pw)
        t_inv = t_inv + bmm(t_inv, pw)

    w_mat = bmm(t_inv, kks)
    u_loc = -bmm(t_inv, bmm(a_k, vh))
    q_eff = rs - bmm(q_b, w_mat)
    o_loc = bmm(q_b, u_loc) + bmm(q_k, vh)

    eye_n = jnp.where(lax.broadcasted_iota(jnp.int32, (N, N), 0) == lax.broadcasted_iota(jnp.int32, (N, N), 1), 1.0, 0.0)
    lane_chunk = lax.shift_right_logical(lax.broadcasted_iota(jnp.int32, (N, S), 1), int(np.log2(C)))
    hc = h_ref[...]
    outs = []
    for c in range(n_sub):
        in_c = lane_chunk == c
        bs_c = jnp.where(in_c, bsT, 0.0)
        ks_c = jnp.where(in_c, ksT, 0.0)
        g_end = jnp.exp(csT[:, :, (c + 1) * C - 1:(c + 1) * C])
        g_mat = g_end * (eye_n - bmm(bs_c, w_mat))
        h_loc = g_end * (bmm(bs_c, u_loc) + bmm(ks_c, vh))
        sl = slice(c * C, (c + 1) * C)
        outs.append(bmm(q_eff[:, sl], hc) + o_loc[:, sl])
        hc = bmm(g_mat, hc) + h_loc
    h_ref[...] = hc

    o = jnp.concatenate(outs, axis=1)
    oc = o - jnp.mean(o, axis=-1, keepdims=True)
    o = oc * lax.rsqrt(jnp.mean(oc * oc, axis=-1, keepdims=True) + RW_GN_EPS)
    o = jnp.concatenate([o[h] for h in range(H)], axis=1)
    bonus = head_sum(r * k * r_k) * v
    o_ref[0] = (o * gn_g + gn_b + bonus) * g


def _rwkv7_branch(z, mu, w0, w2, a0, a2, g2, k_k, k_a, r_k, gn_g, gn_b):
    B, T, _ = z.shape
    f32, bf16 = jnp.float32, jnp.bfloat16
    S = min(RW_STEP, T)
    assert S % RW_CHUNK == 0 and T % S == 0 and RW_DECAY_LORA + RW_AAA_LORA == LANES == RW_GATE_LORA
    W = RW_WIDTH
    vec = jnp.stack([w0, a0, k_k, k_a, r_k.reshape(W), gn_g, gn_b, jnp.zeros((W,), f32)])
    w2p = jnp.concatenate([w2, jnp.zeros((RW_AAA_LORA, W), f32)]).astype(bf16)
    a2p = jnp.concatenate([jnp.zeros((RW_DECAY_LORA, W), f32), a2]).astype(bf16)
    head_of = np.arange(W) // RW_HEAD
    bd = jnp.asarray(head_of[:, None] == head_of[None, :], dtype=bf16)
    full = lambda shape: pl.BlockSpec(shape, lambda bi, ci: (0,) * len(shape))
    return pl.pallas_call(
        _rwkv7_kernel,
        grid=(B, T // S),
        in_specs=[pl.BlockSpec((1, S, RW_COLS), lambda bi, ci: (bi, ci, 0)), full((1, RW_COLS)), full((8, W)),
                  full((LANES, W)), full((LANES, W)), full((LANES, W)), full((W, W))],
        out_specs=pl.BlockSpec((1, S, W), lambda bi, ci: (bi, ci, 0)),
        out_shape=jax.ShapeDtypeStruct((B, T, W), f32),
        scratch_shapes=[pltpu.VMEM((8, RW_COLS), f32), pltpu.VMEM((RW_HEADS, RW_HEAD, RW_HEAD), f32)],
        compiler_params=pltpu.CompilerParams(dimension_semantics=("arbitrary", "arbitrary"),
                                             vmem_limit_bytes=VMEM_LIMIT_BYTES),
        name="rwkv7",
    )(z, mu.reshape(1, RW_COLS), vec, w2p, a2p, g2.astype(bf16), bd)


def _cmp_sel_overlap(n_cmp, n_sel):
    cs = np.arange(n_cmp)[:, None] * CMP_STRIDE
    ss = np.arange(n_sel)[None, :] * SEL_LEN
    ov = np.clip(np.minimum(cs + CMP_LEN, ss + SEL_LEN) - np.maximum(cs, ss), 0, None)
    return (ov / CMP_LEN).astype(np.float32)


NSA_TQ = 256
NSA_TK_SEL = 512
NSA_TK_WIN = 256
NEG_INIT = -1e30
LOG2E = float(np.log2(np.e))


def _dot_bf16x3(a, b):
    bf16, f32 = jnp.bfloat16, jnp.float32
    ah, bh = a.astype(bf16), b.astype(bf16)
    al, bl = (a - ah.astype(f32)).astype(bf16), (b - bh.astype(f32)).astype(bf16)
    d = lambda x, y: jnp.dot(x, y, preferred_element_type=f32)
    return d(ah, bh) + d(ah, bl) + d(al, bh)


def _nsa_attn_kernel(qraw_ref, qrot_ref, kcmp_ref, vcmpT_ref, ovT_ref, ksel_ref, vselT_ref, kwin_ref, vwinT_ref,
                     gl_ref, o_ref, lim_ref, key_ref, ssa_ref, ssb_ref, *, tq, n_top):
    f32, bf16 = jnp.float32, jnp.bfloat16
    G = NSA_GROUP
    R = G * tq
    t0 = pl.program_id(2) * tq
    ncp = kcmp_ref.shape[2]
    n_sel = ovT_ref.shape[0]

    qraw = jnp.concatenate([qraw_ref[0, g] for g in range(G)], axis=1)
    qrot = jnp.concatenate([qrot_ref[0, g] for g in range(G)], axis=1)

    s_c = _dot_bf16x3(kcmp_ref[0, 0], qraw)
    t_tok = t0 + lax.broadcasted_iota(jnp.int32, (1, tq), 1)
    cmp_last = lax.broadcasted_iota(jnp.int32, (ncp, tq), 0) * CMP_STRIDE + (CMP_LEN - 1)
    s_c = s_c + jnp.concatenate([jnp.where(cmp_last <= t_tok, 0.0, -jnp.inf)] * G, axis=1)
    m_c = jnp.max(s_c, axis=0, keepdims=True)
    m_c = jnp.where(m_c == -jnp.inf, 0.0, m_c)
    e_c = jnp.exp(s_c - m_c)
    p_c = e_c / jnp.maximum(jnp.sum(e_c, axis=0, keepdims=True), 1e-30)
    o_c = jnp.dot(vcmpT_ref[0, 0], p_c.astype(bf16), preferred_element_type=f32)

    p_sum = p_c[:, 0:tq]
    for g in range(1, G):
        p_sum = p_sum + p_c[:, g * tq:(g + 1) * tq]
    imp = _dot_bf16x3(ovT_ref[...], p_sum)
    j_iota = lax.broadcasted_iota(jnp.int32, (n_sel, tq), 0)
    cur = lax.shift_right_logical(t_tok, int(np.log2(SEL_LEN)))
    forced = (j_iota == 0) | (j_iota == cur) | (j_iota == cur - 1)
    cand = (j_iota >= 1) & (j_iota <= cur - 2)
    quota = n_top - 1 - jnp.minimum(cur, 2)
    key = jnp.where(cand, pltpu.bitcast(imp, jnp.int32), -1)
    key_ref[...] = key

    def rank_body(i, rank):
        row = key_ref[pl.ds(i, 1), :]
        return rank + jnp.where(row + jnp.where(j_iota > i, 1, 0) > key, 1, 0)

    i_end = jnp.maximum(lax.shift_right_logical(t0 + tq - 1, int(np.log2(SEL_LEN))) - 1, 1)
    rank = lax.fori_loop(1, i_end, rank_body, jnp.zeros((n_sel, tq), jnp.int32))
    sel = forced | (cand & (rank < quota))
    lim_ref[...] = jnp.where(sel, t_tok, -1)

    softmax_init = (jnp.full((1, R), NEG_INIT, f32), jnp.zeros((1, R), f32), jnp.zeros((NSA_HEAD, R), f32))

    def softmax_step(carry, s, vT):
        m, l, acc = carry
        m_new = jnp.maximum(m, jnp.max(s, axis=0, keepdims=True))
        alpha = jnp.exp2(m - m_new)
        p = jnp.exp2(s - m_new)
        l = alpha * l + jnp.sum(p, axis=0, keepdims=True)
        acc = alpha * acc + jnp.dot(vT, p.astype(bf16), preferred_element_type=f32)
        return m_new, l, acc

    def attend(score_fn, vT_fn, lo, hi, sa_ref, sb_ref):
        def scores(kt):
            pen = jnp.where(kt < hi, 0.0, -jnp.inf)
            return score_fn(jnp.minimum(kt, hi - 1), pen)

        def update(carry, s_ref, kt):
            return softmax_step(carry, s_ref[...], vT_fn(jnp.minimum(kt, hi - 1)))

        def body(i, carry):
            kt = lo + 2 * i
            sb_ref[...] = scores(kt + 1)
            carry = update(carry, sa_ref, kt)
            sa_ref[...] = scores(kt + 2)
            return update(carry, sb_ref, kt + 1)

        sa_ref[...] = scores(lo)
        _, l, acc = lax.fori_loop(0, lax.div(hi - lo + 1, 2), body, softmax_init)
        return acc / jnp.maximum(l, 1e-30)

    tile_g = lambda bias: jnp.concatenate([bias] * G, axis=1)

    tk = min(NSA_TK_SEL, ksel_ref.shape[2])
    nb = tk // SEL_LEN
    key_iota = lax.broadcasted_iota(jnp.int32, (tk, tq), 0)

    def sel_scores(kt, pen):
        k0 = pl.multiple_of(kt * tk, tk)
        s = jnp.dot(ksel_ref[0, 0, pl.ds(k0, tk), :], qrot, preferred_element_type=f32)
        limb = lim_ref[pl.ds(pl.multiple_of(kt * nb, nb), nb), :] - k0
        lim_t = jnp.concatenate([jnp.broadcast_to(limb[jb:jb + 1, :], (SEL_LEN, tq)) for jb in range(nb)], axis=0)
        return s + tile_g(jnp.where(key_iota <= lim_t, pen, -jnp.inf))

    o_s = attend(sel_scores, lambda kt: vselT_ref[0, 0, :, pl.ds(pl.multiple_of(kt * tk, tk), tk)],
                 0, lax.div(t0 + tq - 1, tk) + 1, ssa_ref, ssb_ref)

    tkw = min(NSA_TK_WIN, kwin_ref.shape[2])
    key_iota_w = lax.broadcasted_iota(jnp.int32, (tkw, tq), 0)

    assert WINDOW % tkw == 0 and tq % tkw == 0
    carry = softmax_init
    for i in range((WINDOW + tq) // tkw):
        kt = lax.div(t0, tkw) - WINDOW // tkw + i
        pen = jnp.where(kt >= 0, 0.0, -jnp.inf)
        k0 = pl.multiple_of(jnp.maximum(kt, 0) * tkw, tkw)
        s = jnp.dot(kwin_ref[0, 0, pl.ds(k0, tkw), :], qrot, preferred_element_type=f32)
        rel = t_tok - k0
        s = s + tile_g(jnp.where((key_iota_w <= rel) & (key_iota_w > rel - WINDOW), pen, -jnp.inf))
        carry = softmax_step(carry, s, vwinT_ref[0, 0, :, pl.ds(k0, tkw)])
    o_w = carry[2] / jnp.maximum(carry[1], 1e-30)

    gl = jnp.concatenate([gl_ref[0, g] for g in range(G)], axis=1)
    gates = jax.nn.sigmoid(gl)
    out = gates[0:1, :] * o_c + gates[1:2, :] * o_s + gates[2:3, :] * o_w
    per = LANES // NSA_HEAD
    for pair in range(G // per):
        rows = jnp.concatenate([out[:, (pair * per + j) * tq:(pair * per + j + 1) * tq] for j in range(per)], axis=0)
        o_ref[0, :, pair * LANES:(pair + 1) * LANES] = rows.T


NSA_PREP_T = 256
NSA_COL0 = RW_COLS


def _cmp_mlp_kernel(ab_ref, c_ref, w2_ref, o_ref):
    n = ab_ref.shape[1]
    ab = ab_ref[0]
    nxt = jnp.concatenate([ab[1:, CMP_HIDDEN:], jnp.zeros((1, CMP_HIDDEN), jnp.float32)], axis=0)
    hid = jax.nn.gelu(ab[:, :CMP_HIDDEN] + nxt + c_ref[...])
    out = jnp.dot(hid.astype(jnp.bfloat16), w2_ref[...], preferred_element_type=jnp.float32)
    valid = lax.broadcasted_iota(jnp.int32, out.shape, 0) < n - 1
    o_ref[0] = jnp.where(valid, out, 0.0)


def _cmp_mlp_out(ab, c, w2):
    bh, n, _ = ab.shape
    d = w2.shape[1]
    return pl.pallas_call(
        _cmp_mlp_kernel,
        grid=(bh,),
        in_specs=[pl.BlockSpec((1, n, 2 * CMP_HIDDEN), lambda i: (i, 0, 0)), pl.BlockSpec((1, CMP_HIDDEN), lambda i: (0, 0)),
                  pl.BlockSpec((CMP_HIDDEN, d), lambda i: (0, 0))],
        out_specs=pl.BlockSpec((1, n, d), lambda i: (i, 0, 0)),
        out_shape=jax.ShapeDtypeStruct((bh, n, d), jnp.float32),
        compiler_params=pltpu.CompilerParams(dimension_semantics=("arbitrary",)),
        name="cmp_mlp",
    )(ab, c, w2.astype(jnp.bfloat16))


def _nsa_prep_kernel(qa_ref, qb_ref, kc_ref, ks_ref, kw_ref, gl_ref, cos_ref, sin_ref,
                     qraw_ref, qrot_ref, ksel_ref, vselT_ref, kwin_ref, vwinT_ref, glT_ref, kcmp_ref, vcmp_ref):
    bf16 = jnp.bfloat16
    d, half = NSA_HEAD, NSA_HEAD // 2
    scale = NSA_HEAD ** -0.5
    cos, sin = cos_ref[0], sin_ref[0]
    first = (lax.broadcasted_iota(jnp.int32, cos.shape, 1) & (d - 1)) < half

    def rope(x):
        partner = jnp.where(first, pltpu.roll(x, LANES - half, 1), pltpu.roll(x, half, 1))
        return x * cos + jnp.where(first, -partner, partner) * sin

    def put_heads_t(ref, first_head, x_t, dtype):
        for i in range(LANES // d):
            ref[0, first_head + i] = x_t[i * d:(i + 1) * d].astype(dtype)

    for j in range(NSA_HEADS * d // LANES):
        src = qa_ref if j < 2 else qb_ref
        piece = src[0][:, (j % 2) * LANES:(j % 2 + 1) * LANES]
        put_heads_t(qraw_ref, 2 * j, (piece * scale).T, jnp.float32)
        put_heads_t(qrot_ref, 2 * j, (rope(piece) * (scale * LOG2E)).T, bf16)
    for src, k_out, vT_out in ((ks_ref, ksel_ref, vselT_ref), (kw_ref, kwin_ref, vwinT_ref)):
        kv = src[0]
        kr = rope(kv[:, :LANES])
        for h in range(NSA_KV_HEADS):
            k_out[0, h] = kr[:, h * d:(h + 1) * d].astype(bf16)
        put_heads_t(vT_out, 0, kv[:, LANES:].T, bf16)
    gl_t = gl_ref[0].T
    for h in range(NSA_HEADS):
        glT_ref[0, h] = gl_t[3 * h:3 * h + 3]
    kvc = kc_ref[0]
    for h in range(NSA_KV_HEADS):
        kcmp_ref[0, h] = kvc[:, h * d:(h + 1) * d]
        vcmp_ref[0, h] = kvc[:, LANES + h * d:LANES + (h + 1) * d]


def _nsa_prep(z, cos, sin):
    B, T, _ = z.shape
    f32, bf16 = jnp.float32, jnp.bfloat16
    t = min(NSA_PREP_T, T)
    H, Hkv, d = NSA_HEADS, NSA_KV_HEADS, NSA_HEAD
    cw = 2 * NSA_KV
    gl0 = NSA_COL0 + NSA_WIDTH + 6 * NSA_KV
    assert NSA_COL0 % cw == 0 and NSA_WIDTH == 2 * cw and NSA_KV == LANES and gl0 % LANES == 0
    c0 = NSA_COL0 // cw
    wide = lambda j: pl.BlockSpec((1, t, cw), lambda b, i: (b, i, c0 + j))
    gl_col = gl0 // LANES
    tab = pl.BlockSpec((1, t, LANES), lambda b, i: (b, i, 0))
    tile4 = lambda c: jnp.tile(c.reshape(B, T, d // 2), (1, 1, LANES // (d // 2)))
    q_out = pl.BlockSpec((1, H, d, t), lambda b, i: (b, 0, 0, i))
    k_out = pl.BlockSpec((1, Hkv, t, d), lambda b, i: (b, 0, i, 0))
    vT_out = pl.BlockSpec((1, Hkv, d, t), lambda b, i: (b, 0, 0, i))
    return pl.pallas_call(
        _nsa_prep_kernel,
        grid=(B, T // t),
        in_specs=[wide(0), wide(1), wide(2), wide(3), wide(4), pl.BlockSpec((1, t, LANES), lambda b, i: (b, i, gl_col)),
                  tab, tab],
        out_specs=[q_out, q_out, k_out, vT_out, k_out, vT_out, pl.BlockSpec((1, H, 3, t), lambda b, i: (b, 0, 0, i)),
                   k_out, k_out],
        out_shape=[jax.ShapeDtypeStruct((B, H, d, T), f32), jax.ShapeDtypeStruct((B, H, d, T), bf16),
                   jax.ShapeDtypeStruct((B, Hkv, T, d), bf16), jax.ShapeDtypeStruct((B, Hkv, d, T), bf16),
                   jax.ShapeDtypeStruct((B, Hkv, T, d), bf16), jax.ShapeDtypeStruct((B, Hkv, d, T), bf16),
                   jax.ShapeDtypeStruct((B, H, 3, T), f32),
                   jax.ShapeDtypeStruct((B, Hkv, T, d), f32), jax.ShapeDtypeStruct((B, Hkv, T, d), f32)],
        compiler_params=pltpu.CompilerParams(dimension_semantics=("arbitrary", "arbitrary"),
                                             vmem_limit_bytes=VMEM_LIMIT_BYTES),
        name="nsa_prep",
    )(z, z, z, z, z, z, tile4(cos), tile4(sin))


def _nsa_group(z, cos, sin, cmp_pos, cmp_w1, cmp_w2):
    B, T, _ = z.shape
    f32, bf16 = jnp.float32, jnp.bfloat16
    qraw_T, qrot_T, ksel, vselT, kwin, vwinT, gl_T, kc, vc = _nsa_prep(z, cos, sin)

    n_grp = T // CMP_STRIDE
    n_cmp = n_grp - 1
    half = CMP_STRIDE * NSA_HEAD
    cmp_kv = []
    for zi, src in enumerate((kc, vc)):
        w1ab = jnp.concatenate([cmp_w1[zi, :half], cmp_w1[zi, half:]], axis=1)
        ab = _dense(src.reshape(B * NSA_KV_HEADS * n_grp, half), w1ab).reshape(B * NSA_KV_HEADS, n_grp, 2 * CMP_HIDDEN)
        c = cmp_pos[zi].reshape(1, CMP_LEN * NSA_HEAD) @ cmp_w1[zi]
        cmp_kv.append(_cmp_mlp_out(ab, c, cmp_w2[zi]).reshape(B, NSA_KV_HEADS, n_grp, NSA_HEAD))
    k_cmp = cmp_kv[0]
    v_cmpT = cmp_kv[1].transpose(0, 1, 3, 2).astype(bf16)

    n_sel = T // SEL_LEN
    n_top = min(SEL_TOP, n_sel)
    ovT = jnp.asarray(np.pad(_cmp_sel_overlap(n_cmp, n_sel), ((0, 1), (0, 0))).T)
    tq = min(NSA_TQ, T)
    G = NSA_GROUP
    q_spec = pl.BlockSpec((1, G, NSA_HEAD, tq), lambda b, h, i: (b, h, 0, i))
    kv_spec = lambda shape: pl.BlockSpec((1, 1) + shape, lambda b, h, i: (b, h, 0, 0))
    out = pl.pallas_call(
        functools.partial(_nsa_attn_kernel, tq=tq, n_top=n_top),
        grid=(B, NSA_KV_HEADS, T // tq),
        in_specs=[q_spec, q_spec, kv_spec((n_grp, NSA_HEAD)), kv_spec((NSA_HEAD, n_grp)),
                  pl.BlockSpec((n_sel, n_grp), lambda b, h, i: (0, 0)),
                  kv_spec((T, NSA_HEAD)), kv_spec((NSA_HEAD, T)), kv_spec((T, NSA_HEAD)), kv_spec((NSA_HEAD, T)),
                  pl.BlockSpec((1, G, 3, tq), lambda b, h, i: (b, h, 0, i))],
        out_specs=pl.BlockSpec((1, tq, G * NSA_HEAD), lambda b, h, i: (b, i, h)),
        out_shape=jax.ShapeDtypeStruct((B, T, NSA_WIDTH), f32),
        scratch_shapes=[pltpu.VMEM((n_sel, tq), jnp.int32), pltpu.VMEM((n_sel, tq), jnp.int32)]
        + [pltpu.VMEM((min(NSA_TK_SEL, T), G * tq), f32)] * 2,
        compiler_params=pltpu.CompilerParams(dimension_semantics=("arbitrary", "arbitrary", "arbitrary"),
                                             vmem_limit_bytes=VMEM_LIMIT_BYTES),
        name="nsa_attn",
    )(qraw_T, qrot_T, k_cmp, v_cmpT, ovT, ksel, vselT, kwin, vwinT, gl_T)
    return out


HG_STEP = 256
HG_HEADS_PER_STEP = 8


def _hgrn2_kernel(q_ref, f_ref, i_ref, g_ref, lb_ref, ng_ref, o_ref, st_ref):
    f32, bf16 = jnp.float32, jnp.bfloat16
    S, d = q_ref.shape[1], HG_HEAD
    hb = q_ref.shape[2] // d
    C = HG_CHUNK
    n_sub = S // C
    shift = int(np.log2(C))

    @pl.when(pl.program_id(2) == 0)
    def _():
        st_ref[...] = jnp.zeros_like(st_ref)

    heads = lambda ref: jnp.stack([ref[0][:, h * d:(h + 1) * d] for h in range(hb)])
    lb = lb_ref[...]
    forget = lb + (1.0 - lb) * _sigmoid(heads(f_ref))
    logf = jnp.log(forget)
    k = 1.0 - forget
    q = heads(q_ref)
    qs = q * _sigmoid(q)
    v = heads(i_ref)

    pn = min(LANES, S)
    n_pan = S // pn
    panels = lambda x: x.reshape(hb * n_pan, pn, d)
    row = lax.broadcasted_iota(jnp.int32, (pn, pn), 0)
    col = lax.broadcasted_iota(jnp.int32, (pn, pn), 1)
    same = lax.shift_right_logical(row, shift) == lax.shift_right_logical(col, shift)
    causal = same & (row >= col)

    def cumsum01(mask, x):
        m = jnp.broadcast_to(jnp.where(mask, 1.0, 0.0).astype(bf16), (hb * n_pan, pn, pn))
        return sum(jnp.einsum('hij,hjk->hik', m, t, preferred_element_type=f32) for t in _split3(x))

    logf_p = panels(logf)
    b = cumsum01(causal, logf_p)
    tail = cumsum01(same & (col > row), logf_p)
    q_e = (panels(qs) * jnp.exp(b)).astype(bf16)
    k_e = (panels(k) * jnp.exp(-b)).astype(bf16)
    k_tail = (panels(k) * jnp.exp(tail)).reshape(hb, n_pan, pn, d)
    v_p = panels(v)

    a = jnp.einsum('hsd,htd->hst', q_e, k_e, preferred_element_type=f32)
    o_intra = jnp.einsum('hst,htv->hsv', jnp.where(causal, a, 0.0).astype(bf16), v_p.astype(bf16),
                         preferred_element_type=f32).reshape(hb, S, d)

    vT = jnp.stack([v_p[i].T for i in range(hb * n_pan)]).astype(bf16).reshape(hb, n_pan, d, pn)
    q_e = q_e.reshape(hb, S, d)
    b = b.reshape(hb, S, d)
    row_chunk = lax.shift_right_logical(lax.broadcasted_iota(jnp.int32, (pn, d), 0), shift)
    st = st_ref[...]
    o_inter = []
    for c in range(n_sub):
        sl = slice(c * C, (c + 1) * C)
        pan, c_in = divmod(c, pn // C)
        o_inter.append(jnp.einsum('hcd,hvd->hcv', q_e[:, sl], st.astype(bf16), preferred_element_type=f32))
        k_c = jnp.where(row_chunk == c_in, k_tail[:, pan], 0.0).astype(bf16)
        d_c = jnp.exp(b[:, (c + 1) * C - 1:(c + 1) * C, :])
        st = st * d_c + jnp.einsum('hvs,hsk->hvk', vT[:, pan], k_c, preferred_element_type=f32)
    st_ref[...] = st
    o = o_intra + jnp.concatenate(o_inter, axis=1)
    o = o * lax.rsqrt(jnp.mean(o * o, axis=-1, keepdims=True) + LN_EPS) * ng_ref[...]
    g = heads(g_ref)
    o = o * (g * _sigmoid(g))
    for h in range(hb):
        o_ref[0, :, h * d:(h + 1) * d] = o[h]


def _hgrn2_mixer(x, w_in, lb, norm_g):
    B, T, D = x.shape
    z = _dense(x.reshape(B * T, D), w_in).reshape(B, T, -1)
    S = min(HG_STEP, T)
    H, d = HG_HEADS, HG_HEAD
    hb = HG_HEADS_PER_STEP
    ng = H // hb
    col = lambda j: pl.BlockSpec((1, S, hb * d), lambda b, h, t: (b, t, j * ng + h))
    vec = pl.BlockSpec((hb, 1, d), lambda b, h, t: (h, 0, 0))
    o = pl.pallas_call(
        _hgrn2_kernel,
        grid=(B, ng, T // S),
        in_specs=[col(0), col(1), col(2), col(3), vec, pl.BlockSpec((1, 1, d), lambda b, h, t: (0, 0, 0))],
        out_specs=pl.BlockSpec((1, S, hb * d), lambda b, h, t: (b, t, h)),
        out_shape=jax.ShapeDtypeStruct((B, T, D), jnp.float32),
        scratch_shapes=[pltpu.VMEM((hb, d, d), jnp.float32)],
        compiler_params=pltpu.CompilerParams(dimension_semantics=("arbitrary", "arbitrary", "arbitrary"),
                                             vmem_limit_bytes=VMEM_LIMIT_BYTES),
        name="hgrn2",
    )(z, z, z, z, lb.reshape(H, 1, d), norm_g.reshape(1, 1, d))
    return o.reshape(B * T, D)


def _moe_expert_kernel(blk_e_ref, x_ref, w1_ref, w3_ref, w2_ref, o_ref):
    del blk_e_ref
    f32, bf16 = jnp.float32, jnp.bfloat16
    x = x_ref[...].astype(bf16)
    a = jnp.dot(x, w1_ref[0], preferred_element_type=f32)
    b = jnp.dot(x, w3_ref[0], preferred_element_type=f32)
    hid = (jax.nn.silu(a) * b).astype(bf16)
    o_ref[...] = jnp.dot(hid, w2_ref[0], preferred_element_type=f32)


def _moe_experts(xbuf, blk_e, w1, w3, w2):
    P, D = xbuf.shape
    hid = w1.shape[-1]
    bf16 = jnp.bfloat16
    w_spec = lambda shape: pl.BlockSpec((1,) + shape, lambda i, be: (be[i], 0, 0))
    x_spec = pl.BlockSpec((MOE_BLOCK, D), lambda i, be: (i, 0))
    return pl.pallas_call(
        _moe_expert_kernel,
        grid_spec=pltpu.PrefetchScalarGridSpec(
            num_scalar_prefetch=1,
            grid=(P // MOE_BLOCK,),
            in_specs=[x_spec, w_spec((D, hid)), w_spec((D, hid)), w_spec((hid, D))],
            out_specs=x_spec,
        ),
        out_shape=jax.ShapeDtypeStruct((P, D), jnp.float32),
        compiler_params=pltpu.CompilerParams(dimension_semantics=("arbitrary",), vmem_limit_bytes=VMEM_LIMIT_BYTES),
        name="moe_experts",
    )(blk_e, xbuf, w1.astype(bf16), w3.astype(bf16), w2.astype(bf16))


MOE_ROUTE_TM = 512
MOE_ROW_TM = 256
MOE_ROUTE_ROWS = 40


def _moe_route_kernel(h_ref, wr_ref, br_ref, eid_ref, gate_ref, pos_ref, cnt_ref, run_ref):
    f32 = jnp.float32
    tm = h_ref.shape[0]
    E, PG, NG = MOE_EXPERTS, MOE_PER_GROUP, MOE_GROUPS

    @pl.when(pl.program_id(0) == 0)
    def _():
        run_ref[...] = jnp.zeros_like(run_ref)

    lg = lax.dot_general(wr_ref[...], h_ref[...], (((1,), (1,)), ((), ())), precision=lax.Precision.HIGHEST,
                         preferred_element_type=f32) + br_ref[...]
    grp = lg[E:E + NG]
    g_iota = lax.broadcasted_iota(jnp.int32, (NG, tm), 0)
    g_max = jnp.max(grp, axis=0, keepdims=True)
    g_sel = jnp.min(jnp.where(grp == g_max, g_iota, NG), axis=0, keepdims=True)
    p_grp = 1.0 / jnp.sum(jnp.exp(grp - g_max), axis=0, keepdims=True)
    le = lg[0:PG]
    for g in range(1, NG):
        le = jnp.where(g_sel == g, lg[g * PG:(g + 1) * PG], le)
    e_iota = lax.broadcasted_iota(jnp.int32, (PG, tm), 0)
    m1 = jnp.max(le, axis=0, keepdims=True)
    i1 = jnp.min(jnp.where(le == m1, e_iota, PG), axis=0, keepdims=True)
    le2 = jnp.where(e_iota == i1, -jnp.inf, le)
    m2 = jnp.max(le2, axis=0, keepdims=True)
    i2 = jnp.min(jnp.where(le2 == m2, e_iota, PG), axis=0, keepdims=True)
    e2 = jnp.exp(m2 - m1)
    den = 1.0 + e2
    eid1 = g_sel * PG + i1
    eid2 = g_sel * PG + i2
    eid_ref[...] = jnp.concatenate([eid1, eid2], axis=0)
    gate_ref[...] = jnp.concatenate([p_grp / den, p_grp * e2 / den], axis=0)

    x_iota = lax.broadcasted_iota(jnp.int32, (E, tm), 0)
    oh1 = jnp.where(x_iota == eid1, 1.0, 0.0)
    oh2 = jnp.where(x_iota == eid2, 1.0, 0.0)
    before = jnp.where(lax.broadcasted_iota(jnp.int32, (tm, tm), 0) < lax.broadcasted_iota(jnp.int32, (tm, tm), 1),
                       1.0, 0.0).astype(jnp.bfloat16)
    cum1 = jnp.dot(oh1.astype(jnp.bfloat16), before, preferred_element_type=f32)
    cum2 = jnp.dot(oh2.astype(jnp.bfloat16), before, preferred_element_type=f32)
    tot1 = jnp.sum(oh1, axis=1, keepdims=True)
    tot2 = jnp.sum(oh2, axis=1, keepdims=True)
    base = run_ref[:, 0:1]
    pos1 = jnp.sum(oh1 * (cum1 + base), axis=0, keepdims=True)
    pos2 = jnp.sum(oh2 * (cum2 + base + tot1), axis=0, keepdims=True)
    pos_ref[...] = jnp.concatenate([pos1, pos2], axis=0).astype(jnp.int32)
    new = jnp.broadcast_to(base + tot1 + tot2, run_ref.shape)
    run_ref[...] = new
    cnt_ref[...] = new


def _moe_scatter_kernel(zblk_ref, dest_ref, h_ref, xbuf_ref, zero_ref, zsem, sem):
    i = pl.program_id(0)
    rt = dest_ref.shape[1]

    @pl.when(i == 0)
    def _():
        zero_ref[...] = jnp.zeros_like(zero_ref)

        def zero_copy(j):
            row0 = pl.multiple_of(zblk_ref[j] * MOE_BLOCK, MOE_BLOCK)
            return pltpu.make_async_copy(zero_ref, xbuf_ref.at[pl.ds(row0, MOE_BLOCK)], zsem)

        def z_start(j, c):
            @pl.when(zblk_ref[j] >= 0)
            def _():
                zero_copy(j).start()
            return c

        def z_wait(j, c):
            @pl.when(zblk_ref[j] >= 0)
            def _():
                zero_copy(j).wait()
            return c

        lax.fori_loop(0, zblk_ref.shape[0], z_start, 0)
        lax.fori_loop(0, zblk_ref.shape[0], z_wait, 0)

    def row_copy(r, j):
        return pltpu.make_async_copy(h_ref.at[pl.ds(r, 1)], xbuf_ref.at[pl.ds(dest_ref[j, r], 1)], sem)

    def drain(r, c):
        row_copy(r, 0).wait()
        row_copy(r, 1).wait()
        return c

    for r in range(rt):
        row_copy(r, 0).start()
        row_copy(r, 1).start()
    lax.fori_loop(0, rt, drain, 0, unroll=8)


def _moe_combine_ln_kernel(dest_ref, dnext_ref, gate_ref, h_ref, lng_ref, lnb_ref, ybuf_ref, o_ref, buf_ref, sems):
    i = pl.program_id(0)
    n = pl.num_programs(0)
    rt = h_ref.shape[0]
    cur, nxt = i % 2, (i + 1) % 2

    def row_copy(d_ref, r, j, slot):
        return pltpu.make_async_copy(ybuf_ref.at[pl.ds(d_ref[j, r], 1)], buf_ref.at[slot, j, pl.ds(r, 1)], sems.at[slot])

    def gather(d_ref, slot):
        def body(r, c):
            row_copy(d_ref, r, 0, slot).start()
            row_copy(d_ref, r, 1, slot).start()
            return c
        lax.fori_loop(0, rt, body, 0, unroll=8)

    def drain(slot):
        def body(r, c):
            row_copy(dest_ref, r, 0, slot).wait()
            row_copy(dest_ref, r, 1, slot).wait()
            return c
        lax.fori_loop(0, rt, body, 0, unroll=8)

    @pl.when(i == 0)
    def _():
        gather(dest_ref, 0)

    for slot in range(2):
        @pl.when(nxt == slot)
        def _(slot=slot):
            for r in range(rt):
                row_copy(dnext_ref, r, 0, slot).start()
                row_copy(dnext_ref, r, 1, slot).start()

    drain(cur)
    gate = gate_ref[...]
    ffn = gate[:, 0:1] * buf_ref[cur, 0] + gate[:, 1:2] * buf_ref[cur, 1]
    o_ref[...] = _layer_norm(DN_ALPHA * h_ref[...] + ffn, lng_ref[...], lnb_ref[...])

    @pl.when(i == n - 1)
    def _():
        drain(nxt)


def _hier_moe_ln(h, w_rg, b_rg, w_re, b_re, w1, w3, w2, ln_g, ln_b):
    M, D = h.shape
    f32, i32 = jnp.float32, jnp.int32
    E = MOE_EXPERTS
    pad_rows = MOE_ROUTE_ROWS - E - MOE_GROUPS
    wr = jnp.concatenate([w_re.T, w_rg.T, jnp.zeros((pad_rows, D), f32)], axis=0)
    br = jnp.concatenate([b_re, b_rg, jnp.zeros((pad_rows,), f32)]).reshape(MOE_ROUTE_ROWS, 1)
    tm = min(MOE_ROUTE_TM, M)
    slot_spec = pl.BlockSpec((MOE_TOPK, tm), lambda i: (0, i))
    eid, gate, pos, cnt = pl.pallas_call(
        _moe_route_kernel,
        grid=(M // tm,),
        in_specs=[pl.BlockSpec((tm, D), lambda i: (i, 0)), pl.BlockSpec((MOE_ROUTE_ROWS, D), lambda i: (0, 0)),
                  pl.BlockSpec((MOE_ROUTE_ROWS, 1), lambda i: (0, 0))],
        out_specs=[slot_spec, slot_spec, slot_spec, pl.BlockSpec((E, LANES), lambda i: (0, 0))],
        out_shape=[jax.ShapeDtypeStruct((MOE_TOPK, M), i32), jax.ShapeDtypeStruct((MOE_TOPK, M), f32),
                   jax.ShapeDtypeStruct((MOE_TOPK, M), i32), jax.ShapeDtypeStruct((E, LANES), f32)],
        scratch_shapes=[pltpu.VMEM((E, LANES), f32)],
        compiler_params=pltpu.CompilerParams(dimension_semantics=("arbitrary",), vmem_limit_bytes=VMEM_LIMIT_BYTES),
        name="moe_route",
    )(h, wr, br)

    counts = cnt[:, 0].astype(i32)
    padded = (counts + MOE_BLOCK - 1) // MOE_BLOCK * MOE_BLOCK
    ends = jnp.cumsum(padded)
    start = ends - padded
    P = M * MOE_TOPK + E * MOE_BLOCK
    n_blk = P // MOE_BLOCK
    blk_e = jnp.minimum(jnp.sum(ends[None, :] <= (jnp.arange(n_blk) * MOE_BLOCK)[:, None], axis=1), E - 1).astype(i32)
    dest = pos + jnp.sum(jnp.where(eid[:, :, None] == jnp.arange(E), start, 0), axis=-1)

    rt = min(MOE_ROW_TM, M)
    n_row = M // rt
    any_spec = pl.BlockSpec(memory_space=pl.ANY)
    last_blk = jnp.where(padded > 0, ends // MOE_BLOCK - 1, -1)
    tail_blk = ends[-1] // MOE_BLOCK + jnp.arange(E)
    zblk = jnp.concatenate([last_blk, jnp.where(tail_blk < n_blk, tail_blk, -1)]).astype(i32)
    xbuf = pl.pallas_call(
        _moe_scatter_kernel,
        grid_spec=pltpu.PrefetchScalarGridSpec(
            num_scalar_prefetch=1,
            grid=(n_row,),
            in_specs=[pl.BlockSpec((MOE_TOPK, rt), lambda i, zb: (0, i), memory_space=pltpu.SMEM),
                      pl.BlockSpec((rt, D), lambda i, zb: (i, 0))],
            out_specs=any_spec,
            scratch_shapes=[pltpu.VMEM((MOE_BLOCK, D), f32), pltpu.SemaphoreType.DMA(()), pltpu.SemaphoreType.DMA(())],
        ),
        out_shape=jax.ShapeDtypeStruct((P, D), f32),
        compiler_params=pltpu.CompilerParams(dimension_semantics=("arbitrary",)),
        name="moe_scatter",
    )(zblk, dest, h)

    ybuf = _moe_experts(xbuf, blk_e, w1, w3, w2)

    dest_spec = pl.BlockSpec((MOE_TOPK, rt), lambda i: (0, i), memory_space=pltpu.SMEM)
    dnext_spec = pl.BlockSpec((MOE_TOPK, rt), lambda i: (0, jnp.minimum(i + 1, n_row - 1)), memory_space=pltpu.SMEM)
    row_spec = pl.BlockSpec((rt, D), lambda i: (i, 0))
    vec_spec = pl.BlockSpec((1, D), lambda i: (0, 0))
    return pl.pallas_call(
        _moe_combine_ln_kernel,
        grid=(n_row,),
        in_specs=[dest_spec, dnext_spec, pl.BlockSpec((rt, MOE_TOPK), lambda i: (i, 0)), row_spec, vec_spec, vec_spec,
                  any_spec],
        out_specs=row_spec,
        out_shape=jax.ShapeDtypeStruct((M, D), f32),
        scratch_shapes=[pltpu.VMEM((2, MOE_TOPK, rt, D), f32), pltpu.SemaphoreType.DMA((2,))],
        compiler_params=pltpu.CompilerParams(dimension_semantics=("arbitrary",), vmem_limit_bytes=VMEM_LIMIT_BYTES),
        name="moe_combine_ln",
    )(dest, dest, gate.T, h, ln_g.reshape(1, D), ln_b.reshape(1, D), ybuf)


def _dense_res_ln_kernel(*refs):
    n = (len(refs) - 4) // 2
    x_ref, g_ref, b_ref, o_ref = refs[2 * n:]
    mix = DN_ALPHA * x_ref[...]
    for y_ref, w_ref in zip(refs[:n], refs[n:2 * n]):
        mix = mix + jnp.dot(y_ref[...].astype(jnp.bfloat16), w_ref[...], preferred_element_type=jnp.float32)
    o_ref[...] = _layer_norm(mix, g_ref[...], b_ref[...])


def _dense_res_ln(ys, w, x, ln_g, ln_b, tm=512):
    m, d = x.shape
    tm = min(tm, m)
    cuts = np.cumsum([0] + [y.shape[1] for y in ys])
    ws = [w[int(a):int(b)].astype(jnp.bfloat16) for a, b in zip(cuts[:-1], cuts[1:])]
    row_spec = lambda k: pl.BlockSpec((tm, k), lambda i: (i, 0))
    vec_spec = pl.BlockSpec((1, d), lambda i: (0, 0))
    return pl.pallas_call(
        _dense_res_ln_kernel,
        grid=(m // tm,),
        in_specs=[row_spec(y.shape[1]) for y in ys] + [pl.BlockSpec(wi.shape, lambda i: (0, 0)) for wi in ws]
        + [row_spec(d), vec_spec, vec_spec],
        out_specs=row_spec(d),
        out_shape=jax.ShapeDtypeStruct((m, d), jnp.float32),
        compiler_params=pltpu.CompilerParams(dimension_semantics=("arbitrary",), vmem_limit_bytes=VMEM_LIMIT_BYTES),
        name="dense_res_ln",
    )(*ys, *ws, x, ln_g.reshape(1, d), ln_b.reshape(1, d))


def _ple_kernel(h_ref, p_ref, wg_ref, wp_ref, o_ref):
    bf16, f32 = jnp.bfloat16, jnp.float32
    h = h_ref[...]
    gate = jax.nn.sigmoid(jnp.dot(h.astype(bf16), wg_ref[...], preferred_element_type=f32))
    o_ref[...] = h + gate * jnp.dot(p_ref[...].astype(bf16), wp_ref[...], preferred_element_type=f32)


def _ple(h, p, wg, wp, tm=512):
    m, d = h.shape
    kp = p.shape[1]
    tm = min(tm, m)
    return pl.pallas_call(
        _ple_kernel,
        grid=(m // tm,),
        in_specs=[pl.BlockSpec((tm, d), lambda i: (i, 0)), pl.BlockSpec((tm, kp), lambda i: (i, 0)),
                  pl.BlockSpec((d, d), lambda i: (0, 0)), pl.BlockSpec((kp, d), lambda i: (0, 0))],
        out_specs=pl.BlockSpec((tm, d), lambda i: (i, 0)),
        out_shape=jax.ShapeDtypeStruct((m, d), jnp.float32),
        compiler_params=pltpu.CompilerParams(dimension_semantics=("arbitrary",), vmem_limit_bytes=VMEM_LIMIT_BYTES),
        name="ple",
    )(h, p, wg.astype(jnp.bfloat16), wp.astype(jnp.bfloat16))


def kernel(x, p, positions, ev_w_in, ev_w_out, rw_mu, rw_w0, rw_w2, rw_a0, rw_a2, rw_g2, rw_k_k, rw_k_a,
           rw_r_k, rw_gn_g, rw_gn_b, nsa_cmp_pos, nsa_cmp_w1, nsa_cmp_w2, od_w_in, od_w_out, hg_lb, hg_norm_g,
           moe_w_rg, moe_b_rg, moe_w_re, moe_b_re, moe_w1, moe_w3, moe_w2, ln_g, ln_b, ple_w, ple_gate_w):
    B, T, D = x.shape
    M = B * T
    cos, sin = _rope_tables(positions, NSA_HEAD)
    lb_soft = jax.nn.softmax(hg_lb, axis=0)
    lb_all = jnp.cumsum(lb_soft, axis=0) - lb_soft[0:1]
    for li in range(DEPTH):
        j = li // 2
        if li % 2 == 0:
            z = _dense(x.reshape(M, D), ev_w_in[j], keep_pad=True).reshape(B, T, -1)
            y_rw = _rwkv7_branch(z, rw_mu[j], rw_w0[j], rw_w2[j], rw_a0[j], rw_a2[j], rw_g2[j], rw_k_k[j],
                                 rw_k_a[j], rw_r_k[j], rw_gn_g[j], rw_gn_b[j])
            y_nsa = _nsa_group(z, cos, sin, nsa_cmp_pos[j], nsa_cmp_w1[j], nsa_cmp_w2[j])
            ys, w_out = [y_rw.reshape(M, RW_WIDTH), y_nsa.reshape(M, NSA_WIDTH)], ev_w_out[j]
        else:
            ys, w_out = [_hgrn2_mixer(x, od_w_in[j], lb_all[li], hg_norm_g[j])], od_w_out[j]
        h = _dense_res_ln(ys, w_out, x.reshape(M, D), ln_g[li, 0], ln_b[li, 0])
        h = _hier_moe_ln(h, moe_w_rg[li], moe_b_rg[li], moe_w_re[li], moe_b_re[li], moe_w1[li], moe_w3[li],
                         moe_w2[li], ln_g[li, 1], ln_b[li, 1])
        x = _ple(h, p[li].reshape(M, PLE_DIM), ple_gate_w[li], ple_w[li]).reshape(B, T, D)
    return x
```

```python
import functools

import numpy as np
import jax
import jax.numpy as jnp
from jax import lax
from jax.experimental import pallas as pl
from jax.experimental.pallas import tpu as pltpu

D_MODEL = 1024
DEPTH = 2
PLE_DIM = 256
DN_ALPHA = (2 * DEPTH) ** 0.25
LN_EPS = 1e-5
ROPE_THETA = 10000.0

RW_WIDTH = D_MODEL // 2
RW_HEAD = 64
RW_HEADS = RW_WIDTH // RW_HEAD
RW_DECAY_LORA = 64
RW_AAA_LORA = 64
RW_GATE_LORA = 128
RW_GN_EPS = RW_HEAD * 1e-5
RW_SPLITS = (RW_WIDTH, RW_WIDTH, RW_WIDTH, RW_DECAY_LORA, RW_AAA_LORA, RW_GATE_LORA)
RW_COLS = sum(RW_SPLITS)

NSA_WIDTH = D_MODEL - RW_WIDTH
NSA_HEAD = 64
NSA_HEADS = NSA_WIDTH // NSA_HEAD
NSA_KV_HEADS = 2
NSA_GROUP = NSA_HEADS // NSA_KV_HEADS
NSA_KV = NSA_KV_HEADS * NSA_HEAD
CMP_LEN = 32
CMP_STRIDE = 16
CMP_HIDDEN = 128
SEL_LEN = 64
SEL_TOP = 16
WINDOW = 512
NSA_SPLITS = (NSA_WIDTH,) + (NSA_KV,) * 6 + (3 * NSA_HEADS,)
NSA_COLS = sum(NSA_SPLITS)
EV_COLS = RW_COLS + NSA_COLS

HG_HEAD = 128
HG_HEADS = D_MODEL // HG_HEAD
HG_CHUNK = 16

MOE_GROUPS = 4
MOE_PER_GROUP = 8
MOE_EXPERTS = MOE_GROUPS * MOE_PER_GROUP
MOE_TOPK = 2
MOE_BLOCK = 512

LANES = 128
VMEM_LIMIT_BYTES = 56 * 1024 * 1024


def _round_up(n, m):
    return (n + m - 1) // m * m


def _dense_kernel(x_ref, w_ref, o_ref):
    o_ref[...] = jnp.dot(x_ref[...].astype(jnp.bfloat16), w_ref[...], preferred_element_type=jnp.float32)


def _dense(x2d, w, tm=512, keep_pad=False):
    m, k = x2d.shape
    n = w.shape[1]
    n_pad = _round_up(n, LANES)
    wb = w.astype(jnp.bfloat16)
    if n_pad != n:
        wb = jnp.pad(wb, ((0, 0), (0, n_pad - n)))
    tm = min(tm, m)
    assert m % tm == 0
    out = pl.pallas_call(
        _dense_kernel,
        grid=(m // tm,),
        in_specs=[pl.BlockSpec((tm, k), lambda i: (i, 0)), pl.BlockSpec((k, n_pad), lambda i: (0, 0))],
        out_specs=pl.BlockSpec((tm, n_pad), lambda i: (i, 0)),
        out_shape=jax.ShapeDtypeStruct((m, n_pad), jnp.float32),
        compiler_params=pltpu.CompilerParams(dimension_semantics=("arbitrary",), vmem_limit_bytes=VMEM_LIMIT_BYTES),
        name="dense",
    )(x2d, wb)
    return out if keep_pad or n_pad == n else out[:, :n]


def _sigmoid(x):
    return 1.0 / (1.0 + jnp.exp(-x))


def _layer_norm(x, g, b):
    xc = x - jnp.mean(x, -1, keepdims=True)
    var = jnp.mean(xc * xc, -1, keepdims=True)
    return xc * lax.rsqrt(var + LN_EPS) * g + b


def _rope_tables(positions, dim):
    inv = (1.0 / (ROPE_THETA ** (np.arange(0, dim, 2, dtype=np.float32) / dim))).astype(np.float32)
    ang = positions.astype(jnp.float32)[..., None] * inv
    return jnp.cos(ang)[:, :, None, :], jnp.sin(ang)[:, :, None, :]


RW_CHUNK = 64
RW_STEP = 128


def _split3(x):
    bf16, f32 = jnp.bfloat16, jnp.float32
    h1 = x.astype(bf16)
    r1 = x - h1.astype(f32)
    h2 = r1.astype(bf16)
    return h1, h2, (r1 - h2.astype(f32)).astype(bf16)


def _rwkv7_kernel(z_ref, mu_ref, vec_ref, w2_ref, a2_ref, g2_ref, bd_ref, o_ref, carry_ref, h_ref):
    f32, bf16 = jnp.float32, jnp.bfloat16
    S = z_ref.shape[1]
    W, N, H, C = RW_WIDTH, RW_HEAD, RW_HEADS, RW_CHUNK
    n_sub = S // C
    lora_w = RW_DECAY_LORA + RW_AAA_LORA

    @pl.when(pl.program_id(1) == 0)
    def _():
        carry_ref[...] = jnp.zeros_like(carry_ref)
        h_ref[...] = jnp.zeros_like(h_ref)

    dotf = lambda a, b: jnp.dot(a, b, preferred_element_type=f32)
    bd = bd_ref[...]
    head_sum = lambda x: sum(dotf(t, bd) for t in _split3(x))

    z = z_ref[0]
    z_prev = jnp.concatenate([carry_ref[0:1, :], z[:S - 1, :]], axis=0)
    carry_ref[0:1, :] = z[S - 1:S, :]
    zs = z + mu_ref[...] * (z_prev - z)
    r, k, v = zs[:, 0:W], zs[:, W:2 * W], zs[:, 2 * W:3 * W]
    lora = zs[:, 3 * W:3 * W + lora_w]
    gd = zs[:, 3 * W + lora_w:]
    w0, a0, k_k, k_a, r_k, gn_g, gn_b = (vec_ref[i:i + 1, :] for i in range(7))
    w_pre = -(w0 + dotf(jnp.tanh(lora).astype(bf16), w2_ref[...]))
    softplus = jnp.maximum(w_pre, 0.0) + jnp.log(1.0 + jnp.exp(-jnp.abs(w_pre)))
    lw = -jnp.exp(-softplus - 0.5)
    a = jax.nn.sigmoid(a0 + dotf(lora.astype(bf16), a2_ref[...]))
    g = dotf(jax.nn.sigmoid(gd).astype(bf16), g2_ref[...])
    kk = k * k_k
    kk = kk / jnp.maximum(jnp.sqrt(head_sum(kk * kk)), 1e-12)
    k = k * (1.0 + (a - 1.0) * k_a)
    b = a * kk

    row = lax.broadcasted_iota(jnp.int32, (S, S), 0)
    col = lax.broadcasted_iota(jnp.int32, (S, S), 1)
    same = lax.shift_right_logical(row, int(np.log2(C))) == lax.shift_right_logical(col, int(np.log2(C)))
    incl = same & (row >= col)
    strict = same & (row > col)
    tri = jnp.where(incl, 1.0, 0.0).astype(bf16)
    cs = sum(dotf(tri, t) for t in _split3(lw))
    e_neg = jnp.exp(-cs)

    stack = lambda x: jnp.stack([x[:, h * N:(h + 1) * N] for h in range(H)])

    def stack_t(x):
        parts = []
        for j in range(W // LANES):
            t = x[:, j * LANES:(j + 1) * LANES].T
            parts += [t[i * N:(i + 1) * N] for i in range(LANES // N)]
        return jnp.stack(parts)

    kks = stack(kk * jnp.exp(cs - lw))
    rs = stack(r * jnp.exp(cs))
    vh = stack(v).astype(bf16)
    bsT = stack_t(b * e_neg)
    ksT = stack_t(k * e_neg)
    csT = stack_t(cs)

    def bmm(x, y):
        return jnp.einsum('hij,hjk->hik', x.astype(bf16), y.astype(bf16), preferred_element_type=f32)

    lhs = jnp.concatenate([kks, rs], axis=1)
    mb = bmm(lhs, bsT)
    mk = bmm(lhs, ksT)
    a_b = jnp.where(strict, mb[:, :S], 0.0)
    a_k = jnp.where(strict, mk[:, :S], 0.0)
    q_b = jnp.where(incl, mb[:, S:], 0.0)
    q_k = jnp.where(incl, mk[:, S:], 0.0)

    t_inv = jnp.where(row == col, 1.0, 0.0) - a_b
    pw = a_b
    for _ in range(int(np.log2(C)) - 1):
        pw = bmm(pw, pw)
        t_inv = t_inv + bmm(t_inv, pw)

    w_mat = bmm(t_inv, kks)
    u_loc = -bmm(t_inv, bmm(a_k, vh))
    q_eff = rs - bmm(q_b, w_mat)
    o_loc = bmm(q_b, u_loc) + bmm(q_k, vh)

    eye_n = jnp.where(lax.broadcasted_iota(jnp.int32, (N, N), 0) == lax.broadcasted_iota(jnp.int32, (N, N), 1), 1.0, 0.0)
    lane_chunk = lax.shift_right_logical(lax.broadcasted_iota(jnp.int32, (N, S), 1), int(np.log2(C)))
    hc = h_ref[...]
    outs = []
    for c in range(n_sub):
        in_c = lane_chunk == c
        bs_c = jnp.where(in_c, bsT, 0.0)
        ks_c = jnp.where(in_c, ksT, 0.0)
        g_end = jnp.exp(csT[:, :, (c + 1) * C - 1:(c + 1) * C])
        g_mat = g_end * (eye_n - bmm(bs_c, w_mat))
        h_loc = g_end * (bmm(bs_c, u_loc) + bmm(ks_c, vh))
        sl = slice(c * C, (c + 1) * C)
        outs.append(bmm(q_eff[:, sl], hc) + o_loc[:, sl])
        hc = bmm(g_mat, hc) + h_loc
    h_ref[...] = hc

    o = jnp.concatenate(outs, axis=1)
    oc = o - jnp.mean(o, axis=-1, keepdims=True)
    o = oc * lax.rsqrt(jnp.mean(oc * oc, axis=-1, keepdims=True) + RW_GN_EPS)
    o = jnp.concatenate([o[h] for h in range(H)], axis=1)
    bonus = head_sum(r * k * r_k) * v
    o_ref[0] = (o * gn_g + gn_b + bonus) * g


def _rwkv7_branch(z, mu, w0, w2, a0, a2, g2, k_k, k_a, r_k, gn_g, gn_b):
    B, T, _ = z.shape
    f32, bf16 = jnp.float32, jnp.bfloat16
    S = min(RW_STEP, T)
    assert S % RW_CHUNK == 0 and T % S == 0 and RW_DECAY_LORA + RW_AAA_LORA == LANES == RW_GATE_LORA
    W = RW_WIDTH
    vec = jnp.stack([w0, a0, k_k, k_a, r_k.reshape(W), gn_g, gn_b, jnp.zeros((W,), f32)])
    w2p = jnp.concatenate([w2, jnp.zeros((RW_AAA_LORA, W), f32)]).astype(bf16)
    a2p = jnp.concatenate([jnp.zeros((RW_DECAY_LORA, W), f32), a2]).astype(bf16)
    head_of = np.arange(W) // RW_HEAD
    bd = jnp.asarray(head_of[:, None] == head_of[None, :], dtype=bf16)
    full = lambda shape: pl.BlockSpec(shape, lambda bi, ci: (0,) * len(shape))
    return pl.pallas_call(
        _rwkv7_kernel,
        grid=(B, T // S),
        in_specs=[pl.BlockSpec((1, S, RW_COLS), lambda bi, ci: (bi, ci, 0)), full((1, RW_COLS)), full((8, W)),
                  full((LANES, W)), full((LANES, W)), full((LANES, W)), full((W, W))],
        out_specs=pl.BlockSpec((1, S, W), lambda bi, ci: (bi, ci, 0)),
        out_shape=jax.ShapeDtypeStruct((B, T, W), f32),
        scratch_shapes=[pltpu.VMEM((8, RW_COLS), f32), pltpu.VMEM((RW_HEADS, RW_HEAD, RW_HEAD), f32)],
        compiler_params=pltpu.CompilerParams(dimension_semantics=("arbitrary", "arbitrary"),
                                             vmem_limit_bytes=VMEM_LIMIT_BYTES),
        name="rwkv7",
    )(z, mu.reshape(1, RW_COLS), vec, w2p, a2p, g2.astype(bf16), bd)


def _cmp_sel_overlap(n_cmp, n_sel):
    cs = np.arange(n_cmp)[:, None] * CMP_STRIDE
    ss = np.arange(n_sel)[None, :] * SEL_LEN
    ov = np.clip(np.minimum(cs + CMP_LEN, ss + SEL_LEN) - np.maximum(cs, ss), 0, None)
    return (ov / CMP_LEN).astype(np.float32)


NSA_TQ = 256
NSA_TK_SEL = 512
NSA_TK_WIN = 256
NEG_INIT = -1e30
LOG2E = float(np.log2(np.e))


def _dot_bf16x3(a, b, transpose_b=False):
    bf16, f32 = jnp.bfloat16, jnp.float32
    ah, bh = a.astype(bf16), b.astype(bf16)
    al, bl = (a - ah.astype(f32)).astype(bf16), (b - bh.astype(f32)).astype(bf16)
    dims = (((1,), (1 if transpose_b else 0,)), ((), ()))
    d = lambda x, y: lax.dot_general(x, y, dims, preferred_element_type=f32)
    return d(ah, bh) + d(ah, bl) + d(al, bh)


def _nsa_attn_kernel(qraw_ref, qrot_ref, kcmp_ref, vcmpT_ref, ovT_ref, ksel_ref, vselT_ref, kwin_ref, vwinT_ref,
                     gl_ref, o_ref, lim_ref, key_ref, ssa_ref, ssb_ref, *, tq, n_top):
    f32, bf16 = jnp.float32, jnp.bfloat16
    G = NSA_GROUP
    R = G * tq
    t0 = pl.program_id(2) * tq
    ncp = kcmp_ref.shape[2]
    n_sel = ovT_ref.shape[0]

    qraw = jnp.concatenate([qraw_ref[0, g] for g in range(G)], axis=1)
    qrot = jnp.concatenate([qrot_ref[0, g] for g in range(G)], axis=1)

    s_c = _dot_bf16x3(kcmp_ref[0, 0], qraw)
    t_tok = t0 + lax.broadcasted_iota(jnp.int32, (1, tq), 1)
    cmp_last = lax.broadcasted_iota(jnp.int32, (ncp, tq), 0) * CMP_STRIDE + (CMP_LEN - 1)
    s_c = s_c + jnp.concatenate([jnp.where(cmp_last <= t_tok, 0.0, -jnp.inf)] * G, axis=1)
    m_c = jnp.max(s_c, axis=0, keepdims=True)
    m_c = jnp.where(m_c == -jnp.inf, 0.0, m_c)
    e_c = jnp.exp(s_c - m_c)
    p_c = e_c / jnp.maximum(jnp.sum(e_c, axis=0, keepdims=True), 1e-30)
    o_c = jnp.dot(vcmpT_ref[0, 0], p_c.astype(bf16), preferred_element_type=f32)

    p_sum = p_c[:, 0:tq]
    for g in range(1, G):
        p_sum = p_sum + p_c[:, g * tq:(g + 1) * tq]
    imp = _dot_bf16x3(ovT_ref[...], p_sum)
    j_iota = lax.broadcasted_iota(jnp.int32, (n_sel, tq), 0)
    cur = lax.shift_right_logical(t_tok, int(np.log2(SEL_LEN)))
    forced = (j_iota == 0) | (j_iota == cur) | (j_iota == cur - 1)
    cand = (j_iota >= 1) & (j_iota <= cur - 2)
    quota = n_top - 1 - jnp.minimum(cur, 2)
    key = jnp.where(cand, pltpu.bitcast(imp, jnp.int32), -1)
    key_ref[...] = key

    def rank_body(i, rank):
        row = key_ref[pl.ds(i, 1), :]
        return rank + jnp.where(row + jnp.where(j_iota > i, 1, 0) > key, 1, 0)

    i_end = jnp.maximum(lax.shift_right_logical(t0 + tq - 1, int(np.log2(SEL_LEN))) - 1, 1)
    rank = lax.fori_loop(1, i_end, rank_body, jnp.zeros((n_sel, tq), jnp.int32))
    sel = forced | (cand & (rank < quota))
    lim_ref[...] = jnp.where(sel, t_tok, -1)

    softmax_init = (jnp.full((1, R), NEG_INIT, f32), jnp.zeros((1, R), f32), jnp.zeros((NSA_HEAD, R), f32))

    def softmax_step(carry, s, vT):
        m, l, acc = carry
        m_new = jnp.maximum(m, jnp.max(s, axis=0, keepdims=True))
        alpha = jnp.exp2(m - m_new)
        p = jnp.exp2(s - m_new)
        l = alpha * l + jnp.sum(p, axis=0, keepdims=True)
        acc = alpha * acc + jnp.dot(vT, p.astype(bf16), preferred_element_type=f32)
        return m_new, l, acc

    def attend(score_fn, vT_fn, lo, hi, sa_ref, sb_ref):
        def scores(kt):
            pen = jnp.where(kt < hi, 0.0, -jnp.inf)
            return score_fn(jnp.minimum(kt, hi - 1), pen)

        def update(carry, s_ref, kt):
            return softmax_step(carry, s_ref[...], vT_fn(jnp.minimum(kt, hi - 1)))

        def body(i, carry):
            kt = lo + 2 * i
            sb_ref[...] = scores(kt + 1)
            carry = update(carry, sa_ref, kt)
            sa_ref[...] = scores(kt + 2)
            return update(carry, sb_ref, kt + 1)

        sa_ref[...] = scores(lo)
        _, l, acc = lax.fori_loop(0, lax.div(hi - lo + 1, 2), body, softmax_init)
        return acc / jnp.maximum(l, 1e-30)

    tile_g = lambda bias: jnp.concatenate([bias] * G, axis=1)

    tk = min(NSA_TK_SEL, ksel_ref.shape[2])
    nb = tk // SEL_LEN
    key_iota = lax.broadcasted_iota(jnp.int32, (tk, tq), 0)

    def sel_scores(kt, pen):
        k0 = pl.multiple_of(kt * tk, tk)
        s = jnp.dot(ksel_ref[0, 0, pl.ds(k0, tk), :], qrot, preferred_element_type=f32)
        limb = lim_ref[pl.ds(pl.multiple_of(kt * nb, nb), nb), :] - k0
        lim_t = jnp.concatenate([jnp.broadcast_to(limb[jb:jb + 1, :], (SEL_LEN, tq)) for jb in range(nb)], axis=0)
        return s + tile_g(jnp.where(key_iota <= lim_t, pen, -jnp.inf))

    o_s = attend(sel_scores, lambda kt: vselT_ref[0, 0, :, pl.ds(pl.multiple_of(kt * tk, tk), tk)],
                 0, lax.div(t0 + tq - 1, tk) + 1, ssa_ref, ssb_ref)

    tkw = min(NSA_TK_WIN, kwin_ref.shape[2])
    key_iota_w = lax.broadcasted_iota(jnp.int32, (tkw, tq), 0)

    assert WINDOW % tkw == 0 and tq % tkw == 0
    carry = softmax_init
    for i in range((WINDOW + tq) // tkw):
        kt = lax.div(t0, tkw) - WINDOW // tkw + i
        pen = jnp.where(kt >= 0, 0.0, -jnp.inf)
        k0 = pl.multiple_of(jnp.maximum(kt, 0) * tkw, tkw)
        s = jnp.dot(kwin_ref[0, 0, pl.ds(k0, tkw), :], qrot, preferred_element_type=f32)
        rel = t_tok - k0
        s = s + tile_g(jnp.where((key_iota_w <= rel) & (key_iota_w > rel - WINDOW), pen, -jnp.inf))
        carry = softmax_step(carry, s, vwinT_ref[0, 0, :, pl.ds(k0, tkw)])
    o_w = carry[2] / jnp.maximum(carry[1], 1e-30)

    gl = jnp.concatenate([gl_ref[0, g] for g in range(G)], axis=1)
    gates = jax.nn.sigmoid(gl)
    out = gates[0:1, :] * o_c + gates[1:2, :] * o_s + gates[2:3, :] * o_w
    per = LANES // NSA_HEAD
    for pair in range(G // per):
        rows = jnp.concatenate([out[:, (pair * per + j) * tq:(pair * per + j + 1) * tq] for j in range(per)], axis=0)
        o_ref[0, :, pair * LANES:(pair + 1) * LANES] = rows.T


NSA_PREP_T = 256
NSA_COL0 = RW_COLS


def _cmp_mlp_kernel(ab_ref, c_ref, w2_ref, o_ref):
    n = ab_ref.shape[1]
    ab = ab_ref[0]
    nxt = jnp.concatenate([ab[1:, CMP_HIDDEN:], jnp.zeros((1, CMP_HIDDEN), jnp.float32)], axis=0)
    hid = jax.nn.gelu(ab[:, :CMP_HIDDEN] + nxt + c_ref[...])
    out = jnp.dot(hid.astype(jnp.bfloat16), w2_ref[...], preferred_element_type=jnp.float32)
    valid = lax.broadcasted_iota(jnp.int32, out.shape, 0) < n - 1
    o_ref[0] = jnp.where(valid, out, 0.0)


def _cmp_mlp_out(ab, c, w2):
    bh, n, _ = ab.shape
    d = w2.shape[1]
    return pl.pallas_call(
        _cmp_mlp_kernel,
        grid=(bh,),
        in_specs=[pl.BlockSpec((1, n, 2 * CMP_HIDDEN), lambda i: (i, 0, 0)), pl.BlockSpec((1, CMP_HIDDEN), lambda i: (0, 0)),
                  pl.BlockSpec((CMP_HIDDEN, d), lambda i: (0, 0))],
        out_specs=pl.BlockSpec((1, n, d), lambda i: (i, 0, 0)),
        out_shape=jax.ShapeDtypeStruct((bh, n, d), jnp.float32),
        compiler_params=pltpu.CompilerParams(dimension_semantics=("arbitrary",)),
        name="cmp_mlp",
    )(ab, c, w2.astype(jnp.bfloat16))


def _nsa_prep_kernel(qa_ref, qb_ref, kc_ref, ks_ref, kw_ref, gl_ref, cos_ref, sin_ref,
                     qraw_ref, qrot_ref, ksel_ref, vselT_ref, kwin_ref, vwinT_ref, glT_ref, kcmp_ref, vcmp_ref):
    bf16 = jnp.bfloat16
    d, half = NSA_HEAD, NSA_HEAD // 2
    scale = NSA_HEAD ** -0.5
    cos, sin = cos_ref[0], sin_ref[0]
    first = (lax.broadcasted_iota(jnp.int32, cos.shape, 1) & (d - 1)) < half

    def rope(x):
        partner = jnp.where(first, pltpu.roll(x, LANES - half, 1), pltpu.roll(x, half, 1))
        return x * cos + jnp.where(first, -partner, partner) * sin

    def put_heads_t(ref, first_head, x_t, dtype):
        for i in range(LANES // d):
            ref[0, first_head + i] = x_t[i * d:(i + 1) * d].astype(dtype)

    for j in range(NSA_HEADS * d // LANES):
        src = qa_ref if j < 2 else qb_ref
        piece = src[0][:, (j % 2) * LANES:(j % 2 + 1) * LANES]
        put_heads_t(qraw_ref, 2 * j, (piece * scale).T, jnp.float32)
        put_heads_t(qrot_ref, 2 * j, (rope(piece) * (scale * LOG2E)).T, bf16)
    for src, k_out, vT_out in ((ks_ref, ksel_ref, vselT_ref), (kw_ref, kwin_ref, vwinT_ref)):
        kv = src[0]
        kr = rope(kv[:, :LANES])
        for h in range(NSA_KV_HEADS):
            k_out[0, h] = kr[:, h * d:(h + 1) * d].astype(bf16)
        put_heads_t(vT_out, 0, kv[:, LANES:].T, bf16)
    gl_t = gl_ref[0].T
    for h in range(NSA_HEADS):
        glT_ref[0, h] = gl_t[3 * h:3 * h + 3]
    kvc = kc_ref[0]
    for h in range(NSA_KV_HEADS):
        kcmp_ref[0, h] = kvc[:, h * d:(h + 1) * d]
        vcmp_ref[0, h] = kvc[:, LANES + h * d:LANES + (h + 1) * d]


def _nsa_prep(z, cos, sin):
    B, T, _ = z.shape
    f32, bf16 = jnp.float32, jnp.bfloat16
    t = min(NSA_PREP_T, T)
    H, Hkv, d = NSA_HEADS, NSA_KV_HEADS, NSA_HEAD
    cw = 2 * NSA_KV
    gl0 = NSA_COL0 + NSA_WIDTH + 6 * NSA_KV
    assert NSA_COL0 % cw == 0 and NSA_WIDTH == 2 * cw and NSA_KV == LANES and gl0 % LANES == 0
    c0 = NSA_COL0 // cw
    wide = lambda j: pl.BlockSpec((1, t, cw), lambda b, i: (b, i, c0 + j))
    gl_col = gl0 // LANES
    tab = pl.BlockSpec((1, t, LANES), lambda b, i: (b, i, 0))
    tile4 = lambda c: jnp.tile(c.reshape(B, T, d // 2), (1, 1, LANES // (d // 2)))
    q_out = pl.BlockSpec((1, H, d, t), lambda b, i: (b, 0, 0, i))
    k_out = pl.BlockSpec((1, Hkv, t, d), lambda b, i: (b, 0, i, 0))
    vT_out = pl.BlockSpec((1, Hkv, d, t), lambda b, i: (b, 0, 0, i))
    return pl.pallas_call(
        _nsa_prep_kernel,
        grid=(B, T // t),
        in_specs=[wide(0), wide(1), wide(2), wide(3), wide(4), pl.BlockSpec((1, t, LANES), lambda b, i: (b, i, gl_col)),
                  tab, tab],
        out_specs=[q_out, q_out, k_out, vT_out, k_out, vT_out, pl.BlockSpec((1, H, 3, t), lambda b, i: (b, 0, 0, i)),
                   k_out, k_out],
        out_shape=[jax.ShapeDtypeStruct((B, H, d, T), f32), jax.ShapeDtypeStruct((B, H, d, T), bf16),
                   jax.ShapeDtypeStruct((B, Hkv, T, d), bf16), jax.ShapeDtypeStruct((B, Hkv, d, T), bf16),
                   jax.ShapeDtypeStruct((B, Hkv, T, d), bf16), jax.ShapeDtypeStruct((B, Hkv, d, T), bf16),
                   jax.ShapeDtypeStruct((B, H, 3, T), f32),
                   jax.ShapeDtypeStruct((B, Hkv, T, d), f32), jax.ShapeDtypeStruct((B, Hkv, T, d), f32)],
        compiler_params=pltpu.CompilerParams(dimension_semantics=("arbitrary", "arbitrary"),
                                             vmem_limit_bytes=VMEM_LIMIT_BYTES),
        name="nsa_prep",
    )(z, z, z, z, z, z, tile4(cos), tile4(sin))


def _nsa_group(z, cos, sin, cmp_pos, cmp_w1, cmp_w2):
    B, T, _ = z.shape
    f32, bf16 = jnp.float32, jnp.bfloat16
    qraw_T, qrot_T, ksel, vselT, kwin, vwinT, gl_T, kc, vc = _nsa_prep(z, cos, sin)

    n_grp = T // CMP_STRIDE
    n_cmp = n_grp - 1
    half = CMP_STRIDE * NSA_HEAD
    cmp_kv = []
    for zi, src in enumerate((kc, vc)):
        w1ab = jnp.concatenate([cmp_w1[zi, :half], cmp_w1[zi, half:]], axis=1)
        ab = _dense(src.reshape(B * NSA_KV_HEADS * n_grp, half), w1ab).reshape(B * NSA_KV_HEADS, n_grp, 2 * CMP_HIDDEN)
        c = cmp_pos[zi].reshape(1, CMP_LEN * NSA_HEAD) @ cmp_w1[zi]
        cmp_kv.append(_cmp_mlp_out(ab, c, cmp_w2[zi]).reshape(B, NSA_KV_HEADS, n_grp, NSA_HEAD))
    k_cmp = cmp_kv[0]
    v_cmpT = cmp_kv[1].transpose(0, 1, 3, 2).astype(bf16)

    n_sel = T // SEL_LEN
    n_top = min(SEL_TOP, n_sel)
    ovT = jnp.asarray(np.pad(_cmp_sel_overlap(n_cmp, n_sel), ((0, 1), (0, 0))).T)
    tq = min(NSA_TQ, T)
    G = NSA_GROUP
    q_spec = pl.BlockSpec((1, G, NSA_HEAD, tq), lambda b, h, i: (b, h, 0, i))
    kv_spec = lambda shape: pl.BlockSpec((1, 1) + shape, lambda b, h, i: (b, h, 0, 0))
    out = pl.pallas_call(
        functools.partial(_nsa_attn_kernel, tq=tq, n_top=n_top),
        grid=(B, NSA_KV_HEADS, T // tq),
        in_specs=[q_spec, q_spec, kv_spec((n_grp, NSA_HEAD)), kv_spec((NSA_HEAD, n_grp)),
                  pl.BlockSpec((n_sel, n_grp), lambda b, h, i: (0, 0)),
                  kv_spec((T, NSA_HEAD)), kv_spec((NSA_HEAD, T)), kv_spec((T, NSA_HEAD)), kv_spec((NSA_HEAD, T)),
                  pl.BlockSpec((1, G, 3, tq), lambda b, h, i: (b, h, 0, i))],
        out_specs=pl.BlockSpec((1, tq, G * NSA_HEAD), lambda b, h, i: (b, i, h)),
        out_shape=jax.ShapeDtypeStruct((B, T, NSA_WIDTH), f32),
        scratch_shapes=[pltpu.VMEM((n_sel, tq), jnp.int32), pltpu.VMEM((n_sel, tq), jnp.int32)]
        + [pltpu.VMEM((min(NSA_TK_SEL, T), G * tq), f32)] * 2,
        compiler_params=pltpu.CompilerParams(dimension_semantics=("arbitrary", "arbitrary", "arbitrary"),
                                             vmem_limit_bytes=VMEM_LIMIT_BYTES),
        name="nsa_attn",
    )(qraw_T, qrot_T, k_cmp, v_cmpT, ovT, ksel, vselT, kwin, vwinT, gl_T)
    return out


HG_STEP = 256
HG_HEADS_PER_STEP = 8


def _hgrn2_kernel(q_ref, f_ref, i_ref, g_ref, lb_ref, ng_ref, o_ref, st_ref):
    f32, bf16 = jnp.float32, jnp.bfloat16
    S, d = q_ref.shape[1], HG_HEAD
    hb = q_ref.shape[2] // d
    C = HG_CHUNK
    n_sub = S // C
    shift = int(np.log2(C))

    @pl.when(pl.program_id(2) == 0)
    def _():
        st_ref[...] = jnp.zeros_like(st_ref)

    heads = lambda ref: jnp.stack([ref[0][:, h * d:(h + 1) * d] for h in range(hb)])
    lb = lb_ref[...]
    forget = lb + (1.0 - lb) * _sigmoid(heads(f_ref))
    logf = jnp.log(forget)
    k = 1.0 - forget
    q = heads(q_ref)
    qs = q * _sigmoid(q)
    v = heads(i_ref)

    pn = min(LANES, S)
    n_pan = S // pn
    panels = lambda x: x.reshape(hb * n_pan, pn, d)
    row = lax.broadcasted_iota(jnp.int32, (pn, pn), 0)
    col = lax.broadcasted_iota(jnp.int32, (pn, pn), 1)
    same = lax.shift_right_logical(row, shift) == lax.shift_right_logical(col, shift)
    causal = same & (row >= col)

    def cumsum01(mask, x):
        m = jnp.broadcast_to(jnp.where(mask, 1.0, 0.0).astype(bf16), (hb * n_pan, pn, pn))
        return sum(jnp.einsum('hij,hjk->hik', m, t, preferred_element_type=f32) for t in _split3(x))

    logf_p = panels(logf)
    b = cumsum01(causal, logf_p)
    tail = cumsum01(same & (col > row), logf_p)
    q_e = (panels(qs) * jnp.exp(b)).astype(bf16)
    k_e = (panels(k) * jnp.exp(-b)).astype(bf16)
    k_tail = (panels(k) * jnp.exp(tail)).reshape(hb, n_pan, pn, d)
    v_p = panels(v)

    a = jnp.einsum('hsd,htd->hst', q_e, k_e, preferred_element_type=f32)
    o_intra = jnp.einsum('hst,htv->hsv', jnp.where(causal, a, 0.0).astype(bf16), v_p.astype(bf16),
                         preferred_element_type=f32).reshape(hb, S, d)

    vT = jnp.stack([v_p[i].T for i in range(hb * n_pan)]).astype(bf16).reshape(hb, n_pan, d, pn)
    q_e = q_e.reshape(hb, S, d)
    b = b.reshape(hb, S, d)
    row_chunk = lax.shift_right_logical(lax.broadcasted_iota(jnp.int32, (pn, d), 0), shift)
    st = st_ref[...]
    o_inter = []
    for c in range(n_sub):
        sl = slice(c * C, (c + 1) * C)
        pan, c_in = divmod(c, pn // C)
        o_inter.append(jnp.einsum('hcd,hvd->hcv', q_e[:, sl], st.astype(bf16), preferred_element_type=f32))
        k_c = jnp.where(row_chunk == c_in, k_tail[:, pan], 0.0).astype(bf16)
        d_c = jnp.exp(b[:, (c + 1) * C - 1:(c + 1) * C, :])
        st = st * d_c + jnp.einsum('hvs,hsk->hvk', vT[:, pan], k_c, preferred_element_type=f32)
    st_ref[...] = st
    o = o_intra + jnp.concatenate(o_inter, axis=1)
    o = o * lax.rsqrt(jnp.mean(o * o, axis=-1, keepdims=True) + LN_EPS) * ng_ref[...]
    g = heads(g_ref)
    o = o * (g * _sigmoid(g))
    for h in range(hb):
        o_ref[0, :, h * d:(h + 1) * d] = o[h]


def _hgrn2_mixer(x, w_in, lb, norm_g):
    B, T, D = x.shape
    z = _dense(x.reshape(B * T, D), w_in).reshape(B, T, -1)
    S = min(HG_STEP, T)
    H, d = HG_HEADS, HG_HEAD
    hb = HG_HEADS_PER_STEP
    ng = H // hb
    col = lambda j: pl.BlockSpec((1, S, hb * d), lambda b, h, t: (b, t, j * ng + h))
    vec = pl.BlockSpec((hb, 1, d), lambda b, h, t: (h, 0, 0))
    o = pl.pallas_call(
        _hgrn2_kernel,
        grid=(B, ng, T // S),
        in_specs=[col(0), col(1), col(2), col(3), vec, pl.BlockSpec((1, 1, d), lambda b, h, t: (0, 0, 0))],
        out_specs=pl.BlockSpec((1, S, hb * d), lambda b, h, t: (b, t, h)),
        out_shape=jax.ShapeDtypeStruct((B, T, D), jnp.float32),
        scratch_shapes=[pltpu.VMEM((hb, d, d), jnp.float32)],
        compiler_params=pltpu.CompilerParams(dimension_semantics=("arbitrary", "arbitrary", "arbitrary"),
                                             vmem_limit_bytes=VMEM_LIMIT_BYTES),
        name="hgrn2",
    )(z, z, z, z, lb.reshape(H, 1, d), norm_g.reshape(1, 1, d))
    return o.reshape(B * T, D)


def _moe_expert_kernel(blk_e_ref, x_ref, w1_ref, w3_ref, w2_ref, o_ref, w1b_ref, w3b_ref, w2b_ref):
    f32, bf16 = jnp.float32, jnp.bfloat16
    i = pl.program_id(0)

    @pl.when((i == 0) | (blk_e_ref[i] != blk_e_ref[jnp.maximum(i - 1, 0)]))
    def _():
        w1b_ref[...] = w1_ref[0].astype(bf16)
        w3b_ref[...] = w3_ref[0].astype(bf16)
        w2b_ref[...] = w2_ref[0].astype(bf16)

    x = x_ref[...].astype(bf16)
    a = jnp.dot(x, w1b_ref[...], preferred_element_type=f32)
    b = jnp.dot(x, w3b_ref[...], preferred_element_type=f32)
    hid = (jax.nn.silu(a) * b).astype(bf16)
    o_ref[...] = jnp.dot(hid, w2b_ref[...], preferred_element_type=f32)


def _moe_experts(xbuf, blk_e, w1, w3, w2):
    P, D = xbuf.shape
    hid = w1.shape[-1]
    bf16 = jnp.bfloat16
    w_spec = lambda shape: pl.BlockSpec((1,) + shape, lambda i, be: (be[i], 0, 0))
    x_spec = pl.BlockSpec((MOE_BLOCK, D), lambda i, be: (i, 0))
    return pl.pallas_call(
        _moe_expert_kernel,
        grid_spec=pltpu.PrefetchScalarGridSpec(
            num_scalar_prefetch=1,
            grid=(P // MOE_BLOCK,),
            in_specs=[x_spec, w_spec((D, hid)), w_spec((D, hid)), w_spec((hid, D))],
            out_specs=x_spec,
            scratch_shapes=[pltpu.VMEM((D, hid), bf16), pltpu.VMEM((D, hid), bf16), pltpu.VMEM((hid, D), bf16)],
        ),
        out_shape=jax.ShapeDtypeStruct((P, D), jnp.float32),
        compiler_params=pltpu.CompilerParams(dimension_semantics=("arbitrary",), vmem_limit_bytes=VMEM_LIMIT_BYTES),
        name="moe_experts",
    )(blk_e, xbuf, w1, w3, w2)


MOE_ROUTE_TM = 512
MOE_ROW_TM = 256
MOE_ROUTE_ROWS = 40


def _moe_route_kernel(h_ref, wr_ref, br_ref, eid_ref, gate_ref, pos_ref, cnt_ref, run_ref):
    f32 = jnp.float32
    tm = h_ref.shape[0]
    E, PG, NG = MOE_EXPERTS, MOE_PER_GROUP, MOE_GROUPS

    @pl.when(pl.program_id(0) == 0)
    def _():
        run_ref[...] = jnp.zeros_like(run_ref)

    lg = _dot_bf16x3(wr_ref[...], h_ref[...], transpose_b=True) + br_ref[...]
    grp = lg[E:E + NG]
    g_iota = lax.broadcasted_iota(jnp.int32, (NG, tm), 0)
    g_max = jnp.max(grp, axis=0, keepdims=True)
    g_sel = jnp.min(jnp.where(grp == g_max, g_iota, NG), axis=0, keepdims=True)
    p_grp = 1.0 / jnp.sum(jnp.exp(grp - g_max), axis=0, keepdims=True)
    le = lg[0:PG]
    for g in range(1, NG):
        le = jnp.where(g_sel == g, lg[g * PG:(g + 1) * PG], le)
    e_iota = lax.broadcasted_iota(jnp.int32, (PG, tm), 0)
    m1 = jnp.max(le, axis=0, keepdims=True)
    i1 = jnp.min(jnp.where(le == m1, e_iota, PG), axis=0, keepdims=True)
    le2 = jnp.where(e_iota == i1, -jnp.inf, le)
    m2 = jnp.max(le2, axis=0, keepdims=True)
    i2 = jnp.min(jnp.where(le2 == m2, e_iota, PG), axis=0, keepdims=True)
    e2 = jnp.exp(m2 - m1)
    den = 1.0 + e2
    eid1 = g_sel * PG + i1
    eid2 = g_sel * PG + i2
    eid_ref[...] = jnp.concatenate([eid1, eid2], axis=0)
    gate_ref[...] = jnp.concatenate([p_grp / den, p_grp * e2 / den], axis=0)

    x_iota = lax.broadcasted_iota(jnp.int32, (E, tm), 0)
    oh1 = jnp.where(x_iota == eid1, 1.0, 0.0)
    oh2 = jnp.where(x_iota == eid2, 1.0, 0.0)
    before = jnp.where(lax.broadcasted_iota(jnp.int32, (tm, tm), 0) < lax.broadcasted_iota(jnp.int32, (tm, tm), 1),
                       1.0, 0.0).astype(jnp.bfloat16)
    cum1 = jnp.dot(oh1.astype(jnp.bfloat16), before, preferred_element_type=f32)
    cum2 = jnp.dot(oh2.astype(jnp.bfloat16), before, preferred_element_type=f32)
    tot1 = jnp.sum(oh1, axis=1, keepdims=True)
    tot2 = jnp.sum(oh2, axis=1, keepdims=True)
    base = run_ref[:, 0:1]
    pos1 = jnp.sum(oh1 * (cum1 + base), axis=0, keepdims=True)
    pos2 = jnp.sum(oh2 * (cum2 + base + tot1), axis=0, keepdims=True)
    pos_ref[...] = jnp.concatenate([pos1, pos2], axis=0).astype(jnp.int32)
    new = jnp.broadcast_to(base + tot1 + tot2, run_ref.shape)
    run_ref[...] = new
    cnt_ref[...] = new


def _moe_scatter_kernel(zblk_ref, dest_ref, h_ref, xbuf_ref, zero_ref, zsem, sem):
    i = pl.program_id(0)
    rt = dest_ref.shape[1]

    @pl.when(i == 0)
    def _():
        zero_ref[...] = jnp.zeros_like(zero_ref)

        def zero_copy(j):
            row0 = pl.multiple_of(zblk_ref[j] * MOE_BLOCK, MOE_BLOCK)
            return pltpu.make_async_copy(zero_ref, xbuf_ref.at[pl.ds(row0, MOE_BLOCK)], zsem)

        def z_start(j, c):
            @pl.when(zblk_ref[j] >= 0)
            def _():
                zero_copy(j).start()
            return c

        def z_wait(j, c):
            @pl.when(zblk_ref[j] >= 0)
            def _():
                zero_copy(j).wait()
            return c

        lax.fori_loop(0, zblk_ref.shape[0], z_start, 0)
        lax.fori_loop(0, zblk_ref.shape[0], z_wait, 0)

    def row_copy(r, j):
        return pltpu.make_async_copy(h_ref.at[pl.ds(r, 1)], xbuf_ref.at[pl.ds(dest_ref[j, r], 1)], sem)

    def drain(r, c):
        row_copy(r, 0).wait()
        row_copy(r, 1).wait()
        return c

    for r in range(rt):
        row_copy(r, 0).start()
        row_copy(r, 1).start()
    lax.fori_loop(0, rt, drain, 0, unroll=8)


def _moe_combine_ln_kernel(dest_ref, dnext_ref, gate_ref, h_ref, lng_ref, lnb_ref, ybuf_ref, o_ref, buf_ref, sems):
    i = pl.program_id(0)
    n = pl.num_programs(0)
    rt = h_ref.shape[0]
    cur, nxt = i % 2, (i + 1) % 2

    def row_copy(d_ref, r, j, slot):
        return pltpu.make_async_copy(ybuf_ref.at[pl.ds(d_ref[j, r], 1)], buf_ref.at[slot, j, pl.ds(r, 1)], sems.at[slot])

    def gather(d_ref, slot):
        def body(r, c):
            row_copy(d_ref, r, 0, slot).start()
            row_copy(d_ref, r, 1, slot).start()
            return c
        lax.fori_loop(0, rt, body, 0, unroll=8)

    def drain(slot):
        def body(r, c):
            row_copy(dest_ref, r, 0, slot).wait()
            row_copy(dest_ref, r, 1, slot).wait()
            return c
        lax.fori_loop(0, rt, body, 0, unroll=8)

    @pl.when(i == 0)
    def _():
        gather(dest_ref, 0)

    for slot in range(2):
        @pl.when(nxt == slot)
        def _(slot=slot):
            for r in range(rt):
                row_copy(dnext_ref, r, 0, slot).start()
                row_copy(dnext_ref, r, 1, slot).start()

    drain(cur)
    gate = gate_ref[...]
    ffn = gate[:, 0:1] * buf_ref[cur, 0] + gate[:, 1:2] * buf_ref[cur, 1]
    o_ref[...] = _layer_norm(DN_ALPHA * h_ref[...] + ffn, lng_ref[...], lnb_ref[...])

    @pl.when(i == n - 1)
    def _():
        drain(nxt)


def _hier_moe_ln(h, w_rg, b_rg, w_re, b_re, w1, w3, w2, ln_g, ln_b):
    M, D = h.shape
    f32, i32 = jnp.float32, jnp.int32
    E = MOE_EXPERTS
    pad_rows = MOE_ROUTE_ROWS - E - MOE_GROUPS
    wr = jnp.concatenate([w_re.T, w_rg.T, jnp.zeros((pad_rows, D), f32)], axis=0)
    br = jnp.concatenate([b_re, b_rg, jnp.zeros((pad_rows,), f32)]).reshape(MOE_ROUTE_ROWS, 1)
    tm = min(MOE_ROUTE_TM, M)
    slot_spec = pl.BlockSpec((MOE_TOPK, tm), lambda i: (0, i))
    eid, gate, pos, cnt = pl.pallas_call(
        _moe_route_kernel,
        grid=(M // tm,),
        in_specs=[pl.BlockSpec((tm, D), lambda i: (i, 0)), pl.BlockSpec((MOE_ROUTE_ROWS, D), lambda i: (0, 0)),
                  pl.BlockSpec((MOE_ROUTE_ROWS, 1), lambda i: (0, 0))],
        out_specs=[slot_spec, slot_spec, slot_spec, pl.BlockSpec((E, LANES), lambda i: (0, 0))],
        out_shape=[jax.ShapeDtypeStruct((MOE_TOPK, M), i32), jax.ShapeDtypeStruct((MOE_TOPK, M), f32),
                   jax.ShapeDtypeStruct((MOE_TOPK, M), i32), jax.ShapeDtypeStruct((E, LANES), f32)],
        scratch_shapes=[pltpu.VMEM((E, LANES), f32)],
        compiler_params=pltpu.CompilerParams(dimension_semantics=("arbitrary",), vmem_limit_bytes=VMEM_LIMIT_BYTES),
        name="moe_route",
    )(h, wr, br)

    counts = cnt[:, 0].astype(i32)
    padded = (counts + MOE_BLOCK - 1) // MOE_BLOCK * MOE_BLOCK
    ends = jnp.cumsum(padded)
    start = ends - padded
    P = M * MOE_TOPK + E * MOE_BLOCK
    n_blk = P // MOE_BLOCK
    blk_e = jnp.minimum(jnp.sum(ends[None, :] <= (jnp.arange(n_blk) * MOE_BLOCK)[:, None], axis=1), E - 1).astype(i32)
    dest = pos + jnp.sum(jnp.where(eid[:, :, None] == jnp.arange(E), start, 0), axis=-1)

    rt = min(MOE_ROW_TM, M)
    n_row = M // rt
    any_spec = pl.BlockSpec(memory_space=pl.ANY)
    last_blk = jnp.where(padded > 0, ends // MOE_BLOCK - 1, -1)
    tail_blk = ends[-1] // MOE_BLOCK + jnp.arange(E)
    zblk = jnp.concatenate([last_blk, jnp.where(tail_blk < n_blk, tail_blk, -1)]).astype(i32)
    xbuf = pl.pallas_call(
        _moe_scatter_kernel,
        grid_spec=pltpu.PrefetchScalarGridSpec(
            num_scalar_prefetch=1,
            grid=(n_row,),
            in_specs=[pl.BlockSpec((MOE_TOPK, rt), lambda i, zb: (0, i), memory_space=pltpu.SMEM),
                      pl.BlockSpec((rt, D), lambda i, zb: (i, 0))],
            out_specs=any_spec,
            scratch_shapes=[pltpu.VMEM((MOE_BLOCK, D), f32), pltpu.SemaphoreType.DMA(()), pltpu.SemaphoreType.DMA(())],
        ),
        out_shape=jax.ShapeDtypeStruct((P, D), f32),
        compiler_params=pltpu.CompilerParams(dimension_semantics=("arbitrary",)),
        name="moe_scatter",
    )(zblk, dest, h)

    ybuf = _moe_experts(xbuf, blk_e, w1, w3, w2)

    dest_spec = pl.BlockSpec((MOE_TOPK, rt), lambda i: (0, i), memory_space=pltpu.SMEM)
    dnext_spec = pl.BlockSpec((MOE_TOPK, rt), lambda i: (0, jnp.minimum(i + 1, n_row - 1)), memory_space=pltpu.SMEM)
    row_spec = pl.BlockSpec((rt, D), lambda i: (i, 0))
    vec_spec = pl.BlockSpec((1, D), lambda i: (0, 0))
    return pl.pallas_call(
        _moe_combine_ln_kernel,
        grid=(n_row,),
        in_specs=[dest_spec, dnext_spec, pl.BlockSpec((rt, MOE_TOPK), lambda i: (i, 0)), row_spec, vec_spec, vec_spec,
                  any_spec],
        out_specs=row_spec,
        out_shape=jax.ShapeDtypeStruct((M, D), f32),
        scratch_shapes=[pltpu.VMEM((2, MOE_TOPK, rt, D), f32), pltpu.SemaphoreType.DMA((2,))],
        compiler_params=pltpu.CompilerParams(dimension_semantics=("arbitrary",), vmem_limit_bytes=VMEM_LIMIT_BYTES),
        name="moe_combine_ln",
    )(dest, dest, gate.T, h, ln_g.reshape(1, D), ln_b.reshape(1, D), ybuf)


def _dense_res_ln_kernel(*refs):
    n = (len(refs) - 4) // 2
    x_ref, g_ref, b_ref, o_ref = refs[2 * n:]
    mix = DN_ALPHA * x_ref[...]
    for y_ref, w_ref in zip(refs[:n], refs[n:2 * n]):
        mix = mix + jnp.dot(y_ref[...].astype(jnp.bfloat16), w_ref[...], preferred_element_type=jnp.float32)
    o_ref[...] = _layer_norm(mix, g_ref[...], b_ref[...])


def _dense_res_ln(ys, w, x, ln_g, ln_b, tm=512):
    m, d = x.shape
    tm = min(tm, m)
    cuts = np.cumsum([0] + [y.shape[1] for y in ys])
    ws = [w[int(a):int(b)].astype(jnp.bfloat16) for a, b in zip(cuts[:-1], cuts[1:])]
    row_spec = lambda k: pl.BlockSpec((tm, k), lambda i: (i, 0))
    vec_spec = pl.BlockSpec((1, d), lambda i: (0, 0))
    return pl.pallas_call(
        _dense_res_ln_kernel,
        grid=(m // tm,),
        in_specs=[row_spec(y.shape[1]) for y in ys] + [pl.BlockSpec(wi.shape, lambda i: (0, 0)) for wi in ws]
        + [row_spec(d), vec_spec, vec_spec],
        out_specs=row_spec(d),
        out_shape=jax.ShapeDtypeStruct((m, d), jnp.float32),
        compiler_params=pltpu.CompilerParams(dimension_semantics=("arbitrary",), vmem_limit_bytes=VMEM_LIMIT_BYTES),
        name="dense_res_ln",
    )(*ys, *ws, x, ln_g.reshape(1, d), ln_b.reshape(1, d))


def _ple_kernel(h_ref, p_ref, wg_ref, wp_ref, o_ref):
    bf16, f32 = jnp.bfloat16, jnp.float32
    h = h_ref[...]
    gate = jax.nn.sigmoid(jnp.dot(h.astype(bf16), wg_ref[...], preferred_element_type=f32))
    o_ref[...] = h + gate * jnp.dot(p_ref[...].astype(bf16), wp_ref[...], preferred_element_type=f32)


def _ple(h, p, wg, wp, tm=512):
    m, d = h.shape
    kp = p.shape[1]
    tm = min(tm, m)
    return pl.pallas_call(
        _ple_kernel,
        grid=(m // tm,),
        in_specs=[pl.BlockSpec((tm, d), lambda i: (i, 0)), pl.BlockSpec((tm, kp), lambda i: (i, 0)),
                  pl.BlockSpec((d, d), lambda i: (0, 0)), pl.BlockSpec((kp, d), lambda i: (0, 0))],
        out_specs=pl.BlockSpec((tm, d), lambda i: (i, 0)),
        out_shape=jax.ShapeDtypeStruct((m, d), jnp.float32),
        compiler_params=pltpu.CompilerParams(dimension_semantics=("arbitrary",), vmem_limit_bytes=VMEM_LIMIT_BYTES),
        name="ple",
    )(h, p, wg.astype(jnp.bfloat16), wp.astype(jnp.bfloat16))


def kernel(x, p, positions, ev_w_in, ev_w_out, rw_mu, rw_w0, rw_w2, rw_a0, rw_a2, rw_g2, rw_k_k, rw_k_a,
           rw_r_k, rw_gn_g, rw_gn_b, nsa_cmp_pos, nsa_cmp_w1, nsa_cmp_w2, od_w_in, od_w_out, hg_lb, hg_norm_g,
           moe_w_rg, moe_b_rg, moe_w_re, moe_b_re, moe_w1, moe_w3, moe_w2, ln_g, ln_b, ple_w, ple_gate_w):
    B, T, D = x.shape
    M = B * T
    cos, sin = _rope_tables(positions, NSA_HEAD)
    lb_soft = jax.nn.softmax(hg_lb, axis=0)
    lb_all = jnp.cumsum(lb_soft, axis=0) - lb_soft[0:1]
    for li in range(DEPTH):
        j = li // 2
        if li % 2 == 0:
            z = _dense(x.reshape(M, D), ev_w_in[j], keep_pad=True).reshape(B, T, -1)
            y_rw = _rwkv7_branch(z, rw_mu[j], rw_w0[j], rw_w2[j], rw_a0[j], rw_a2[j], rw_g2[j], rw_k_k[j],
                                 rw_k_a[j], rw_r_k[j], rw_gn_g[j], rw_gn_b[j])
            y_nsa = _nsa_group(z, cos, sin, nsa_cmp_pos[j], nsa_cmp_w1[j], nsa_cmp_w2[j])
            ys, w_out = [y_rw.reshape(M, RW_WIDTH), y_nsa.reshape(M, NSA_WIDTH)], ev_w_out[j]
        else:
            ys, w_out = [_hgrn2_mixer(x, od_w_in[j], lb_all[li], hg_norm_g[j])], od_w_out[j]
        h = _dense_res_ln(ys, w_out, x.reshape(M, D), ln_g[li, 0], ln_b[li, 0])
        h = _hier_moe_ln(h, moe_w_rg[li], moe_b_rg[li], moe_w_re[li], moe_b_re[li], moe_w1[li], moe_w3[li],
                         moe_w2[li], ln_g[li, 1], ln_b[li, 1])
        x = _ple(h, p[li].reshape(M, PLE_DIM), ple_gate_w[li], ple_w[li]).reshape(B, T, D)
    return x
```

```python
import functools

import numpy as np
import jax
import jax.numpy as jnp
from jax import lax
from jax.experimental import pallas as pl
from jax.experimental.pallas import tpu as pltpu

D_MODEL = 1024
DEPTH = 2
PLE_DIM = 256
DN_ALPHA = (2 * DEPTH) ** 0.25
LN_EPS = 1e-5
ROPE_THETA = 10000.0

RW_WIDTH = D_MODEL // 2
RW_HEAD = 64
RW_HEADS = RW_WIDTH // RW_HEAD
RW_DECAY_LORA = 64
RW_AAA_LORA = 64
RW_GATE_LORA = 128
RW_GN_EPS = RW_HEAD * 1e-5
RW_SPLITS = (RW_WIDTH, RW_WIDTH, RW_WIDTH, RW_DECAY_LORA, RW_AAA_LORA, RW_GATE_LORA)
RW_COLS = sum(RW_SPLITS)

NSA_WIDTH = D_MODEL - RW_WIDTH
NSA_HEAD = 64
NSA_HEADS = NSA_WIDTH // NSA_HEAD
NSA_KV_HEADS = 2
NSA_GROUP = NSA_HEADS // NSA_KV_HEADS
NSA_KV = NSA_KV_HEADS * NSA_HEAD
CMP_LEN = 32
CMP_STRIDE = 16
CMP_HIDDEN = 128
SEL_LEN = 64
SEL_TOP = 16
WINDOW = 512
NSA_SPLITS = (NSA_WIDTH,) + (NSA_KV,) * 6 + (3 * NSA_HEADS,)
NSA_COLS = sum(NSA_SPLITS)
EV_COLS = RW_COLS + NSA_COLS

HG_HEAD = 128
HG_HEADS = D_MODEL // HG_HEAD
HG_CHUNK = 16

MOE_GROUPS = 4
MOE_PER_GROUP = 8
MOE_EXPERTS = MOE_GROUPS * MOE_PER_GROUP
MOE_TOPK = 2
MOE_BLOCK = 512

LANES = 128
VMEM_LIMIT_BYTES = 56 * 1024 * 1024


def _round_up(n, m):
    return (n + m - 1) // m * m


def _dense_kernel(x_ref, w_ref, o_ref):
    o_ref[...] = jnp.dot(x_ref[...].astype(jnp.bfloat16), w_ref[...], preferred_element_type=jnp.float32)


def _dense(x2d, w, tm=512, keep_pad=False):
    m, k = x2d.shape
    n = w.shape[1]
    n_pad = _round_up(n, LANES)
    wb = w.astype(jnp.bfloat16)
    if n_pad != n:
        wb = jnp.pad(wb, ((0, 0), (0, n_pad - n)))
    tm = min(tm, m)
    assert m % tm == 0
    out = pl.pallas_call(
        _dense_kernel,
        grid=(m // tm,),
        in_specs=[pl.BlockSpec((tm, k), lambda i: (i, 0)), pl.BlockSpec((k, n_pad), lambda i: (0, 0))],
        out_specs=pl.BlockSpec((tm, n_pad), lambda i: (i, 0)),
        out_shape=jax.ShapeDtypeStruct((m, n_pad), jnp.float32),
        compiler_params=pltpu.CompilerParams(dimension_semantics=("arbitrary",), vmem_limit_bytes=VMEM_LIMIT_BYTES),
        name="dense",
    )(x2d, wb)
    return out if keep_pad or n_pad == n else out[:, :n]


def _sigmoid(x):
    return 1.0 / (1.0 + jnp.exp(-x))


def _layer_norm(x, g, b):
    xc = x - jnp.mean(x, -1, keepdims=True)
    var = jnp.mean(xc * xc, -1, keepdims=True)
    return xc * lax.rsqrt(var + LN_EPS) * g + b


def _rope_tables(positions, dim):
    inv = (1.0 / (ROPE_THETA ** (np.arange(0, dim, 2, dtype=np.float32) / dim))).astype(np.float32)
    ang = positions.astype(jnp.float32)[..., None] * inv
    return jnp.cos(ang)[:, :, None, :], jnp.sin(ang)[:, :, None, :]


RW_CHUNK = 64
RW_STEP = 128


def _split3(x):
    bf16, f32 = jnp.bfloat16, jnp.float32
    h1 = x.astype(bf16)
    r1 = x - h1.astype(f32)
    h2 = r1.astype(bf16)
    return h1, h2, (r1 - h2.astype(f32)).astype(bf16)


def _rwkv7_kernel(z_ref, mu_ref, vec_ref, w2_ref, a2_ref, g2_ref, bd_ref, o_ref, carry_ref, h_ref):
    f32, bf16 = jnp.float32, jnp.bfloat16
    S = z_ref.shape[1]
    W, N, H, C = RW_WIDTH, RW_HEAD, RW_HEADS, RW_CHUNK
    n_sub = S // C
    lora_w = RW_DECAY_LORA + RW_AAA_LORA

    @pl.when(pl.program_id(1) == 0)
    def _():
        carry_ref[...] = jnp.zeros_like(carry_ref)
        h_ref[...] = jnp.zeros_like(h_ref)

    dotf = lambda a, b: jnp.dot(a, b, preferred_element_type=f32)
    bd = bd_ref[...]
    head_sum = lambda x: sum(dotf(t, bd) for t in _split3(x))

    z = z_ref[0]
    z_prev = jnp.concatenate([carry_ref[0:1, :], z[:S - 1, :]], axis=0)
    carry_ref[0:1, :] = z[S - 1:S, :]
    zs = z + mu_ref[...] * (z_prev - z)
    r, k, v = zs[:, 0:W], zs[:, W:2 * W], zs[:, 2 * W:3 * W]
    lora = zs[:, 3 * W:3 * W + lora_w]
    gd = zs[:, 3 * W + lora_w:]
    w0, a0, k_k, k_a, r_k, gn_g, gn_b = (vec_ref[i:i + 1, :] for i in range(7))
    w_pre = -(w0 + dotf(jnp.tanh(lora).astype(bf16), w2_ref[...]))
    softplus = jnp.maximum(w_pre, 0.0) + jnp.log(1.0 + jnp.exp(-jnp.abs(w_pre)))
    lw = -jnp.exp(-softplus - 0.5)
    a = jax.nn.sigmoid(a0 + dotf(lora.astype(bf16), a2_ref[...]))
    g = dotf(jax.nn.sigmoid(gd).astype(bf16), g2_ref[...])
    kk = k * k_k
    kk = kk / jnp.maximum(jnp.sqrt(head_sum(kk * kk)), 1e-12)
    k = k * (1.0 + (a - 1.0) * k_a)
    b = a * kk

    row = lax.broadcasted_iota(jnp.int32, (S, S), 0)
    col = lax.broadcasted_iota(jnp.int32, (S, S), 1)
    same = lax.shift_right_logical(row, int(np.log2(C))) == lax.shift_right_logical(col, int(np.log2(C)))
    incl = same & (row >= col)
    strict = same & (row > col)
    tri = jnp.where(incl, 1.0, 0.0).astype(bf16)
    cs = sum(dotf(tri, t) for t in _split3(lw))
    e_neg = jnp.exp(-cs)

    stack = lambda x: jnp.stack([x[:, h * N:(h + 1) * N] for h in range(H)])

    def stack_t(x):
        parts = []
        for j in range(W // LANES):
            t = x[:, j * LANES:(j + 1) * LANES].T
            parts += [t[i * N:(i + 1) * N] for i in range(LANES // N)]
        return jnp.stack(parts)

    kks = stack(kk * jnp.exp(cs - lw))
    rs = stack(r * jnp.exp(cs))
    vh = stack(v).astype(bf16)
    bsT = stack_t(b * e_neg)
    ksT = stack_t(k * e_neg)
    csT = stack_t(cs)

    def bmm(x, y):
        return jnp.einsum('hij,hjk->hik', x.astype(bf16), y.astype(bf16), preferred_element_type=f32)

    lhs = jnp.concatenate([kks, rs], axis=1)
    mb = bmm(lhs, bsT)
    mk = bmm(lhs, ksT)
    a_b = jnp.where(strict, mb[:, :S], 0.0)
    a_k = jnp.where(strict, mk[:, :S], 0.0)
    q_b = jnp.where(incl, mb[:, S:], 0.0)
    q_k = jnp.where(incl, mk[:, S:], 0.0)

    t_inv = jnp.where(row == col, 1.0, 0.0) - a_b
    pw = a_b
    for _ in range(int(np.log2(C)) - 1):
        pw = bmm(pw, pw)
        t_inv = t_inv + bmm(t_inv, pw)

    w_mat = bmm(t_inv, kks)
    u_loc = -bmm(t_inv, bmm(a_k, vh))
    q_eff = rs - bmm(q_b, w_mat)
    o_loc = bmm(q_b, u_loc) + bmm(q_k, vh)

    eye_n = jnp.where(lax.broadcasted_iota(jnp.int32, (N, N), 0) == lax.broadcasted_iota(jnp.int32, (N, N), 1), 1.0, 0.0)
    lane_chunk = lax.shift_right_logical(lax.broadcasted_iota(jnp.int32, (N, S), 1), int(np.log2(C)))
    hc = h_ref[...]
    outs = []
    for c in range(n_sub):
        in_c = lane_chunk == c
        bs_c = jnp.where(in_c, bsT, 0.0)
        ks_c = jnp.where(in_c, ksT, 0.0)
        g_end = jnp.exp(csT[:, :, (c + 1) * C - 1:(c + 1) * C])
        g_mat = g_end * (eye_n - bmm(bs_c, w_mat))
        h_loc = g_end * (bmm(bs_c, u_loc) + bmm(ks_c, vh))
        sl = slice(c * C, (c + 1) * C)
        outs.append(bmm(q_eff[:, sl], hc) + o_loc[:, sl])
        hc = bmm(g_mat, hc) + h_loc
    h_ref[...] = hc

    o = jnp.concatenate(outs, axis=1)
    oc = o - jnp.mean(o, axis=-1, keepdims=True)
    o = oc * lax.rsqrt(jnp.mean(oc * oc, axis=-1, keepdims=True) + RW_GN_EPS)
    o = jnp.concatenate([o[h] for h in range(H)], axis=1)
    bonus = head_sum(r * k * r_k) * v
    o_ref[0] = (o * gn_g + gn_b + bonus) * g


def _rwkv7_branch(z, mu, w0, w2, a0, a2, g2, k_k, k_a, r_k, gn_g, gn_b):
    B, T, _ = z.shape
    f32, bf16 = jnp.float32, jnp.bfloat16
    S = min(RW_STEP, T)
    assert S % RW_CHUNK == 0 and T % S == 0 and RW_DECAY_LORA + RW_AAA_LORA == LANES == RW_GATE_LORA
    W = RW_WIDTH
    vec = jnp.stack([w0, a0, k_k, k_a, r_k.reshape(W), gn_g, gn_b, jnp.zeros((W,), f32)])
    w2p = jnp.concatenate([w2, jnp.zeros((RW_AAA_LORA, W), f32)]).astype(bf16)
    a2p = jnp.concatenate([jnp.zeros((RW_DECAY_LORA, W), f32), a2]).astype(bf16)
    head_of = np.arange(W) // RW_HEAD
    bd = jnp.asarray(head_of[:, None] == head_of[None, :], dtype=bf16)
    full = lambda shape: pl.BlockSpec(shape, lambda bi, ci: (0,) * len(shape))
    return pl.pallas_call(
        _rwkv7_kernel,
        grid=(B, T // S),
        in_specs=[pl.BlockSpec((1, S, RW_COLS), lambda bi, ci: (bi, ci, 0)), full((1, RW_COLS)), full((8, W)),
                  full((LANES, W)), full((LANES, W)), full((LANES, W)), full((W, W))],
        out_specs=pl.BlockSpec((1, S, W), lambda bi, ci: (bi, ci, 0)),
        out_shape=jax.ShapeDtypeStruct((B, T, W), f32),
        scratch_shapes=[pltpu.VMEM((8, RW_COLS), f32), pltpu.VMEM((RW_HEADS, RW_HEAD, RW_HEAD), f32)],
        compiler_params=pltpu.CompilerParams(dimension_semantics=("arbitrary", "arbitrary"),
                                             vmem_limit_bytes=VMEM_LIMIT_BYTES),
        name="rwkv7",
    )(z, mu.reshape(1, RW_COLS), vec, w2p, a2p, g2.astype(bf16), bd)


def _cmp_sel_overlap(n_cmp, n_sel):
    cs = np.arange(n_cmp)[:, None] * CMP_STRIDE
    ss = np.arange(n_sel)[None, :] * SEL_LEN
    ov = np.clip(np.minimum(cs + CMP_LEN, ss + SEL_LEN) - np.maximum(cs, ss), 0, None)
    return (ov / CMP_LEN).astype(np.float32)


NSA_TQ = 256
NSA_TK_SEL = 512
NSA_TK_WIN = 256
NEG_INIT = -1e30
LOG2E = float(np.log2(np.e))


def _dot_bf16x3(a, b, transpose_b=False):
    bf16, f32 = jnp.bfloat16, jnp.float32
    ah, bh = a.astype(bf16), b.astype(bf16)
    al, bl = (a - ah.astype(f32)).astype(bf16), (b - bh.astype(f32)).astype(bf16)
    dims = (((1,), (1 if transpose_b else 0,)), ((), ()))
    d = lambda x, y: lax.dot_general(x, y, dims, preferred_element_type=f32)
    return d(ah, bh) + d(ah, bl) + d(al, bh)


def _nsa_attn_kernel(qraw_ref, qrot_ref, kcmp_ref, vcmpT_ref, ovT_ref, ksel_ref, vselT_ref, kwin_ref, vwinT_ref,
                     gl_ref, o_ref, lim_ref, key_ref, ssa_ref, ssb_ref, *, tq, n_top):
    f32, bf16 = jnp.float32, jnp.bfloat16
    G = NSA_GROUP
    R = G * tq
    t0 = pl.program_id(2) * tq
    ncp = kcmp_ref.shape[2]
    n_sel = ovT_ref.shape[0]

    qraw = jnp.concatenate([qraw_ref[0, g] for g in range(G)], axis=1)
    qrot = jnp.concatenate([qrot_ref[0, g] for g in range(G)], axis=1)

    s_c = _dot_bf16x3(kcmp_ref[0, 0], qraw)
    t_tok = t0 + lax.broadcasted_iota(jnp.int32, (1, tq), 1)
    cmp_last = lax.broadcasted_iota(jnp.int32, (ncp, tq), 0) * CMP_STRIDE + (CMP_LEN - 1)
    s_c = s_c + jnp.concatenate([jnp.where(cmp_last <= t_tok, 0.0, -jnp.inf)] * G, axis=1)
    m_c = jnp.max(s_c, axis=0, keepdims=True)
    m_c = jnp.where(m_c == -jnp.inf, 0.0, m_c)
    e_c = jnp.exp(s_c - m_c)
    p_c = e_c / jnp.maximum(jnp.sum(e_c, axis=0, keepdims=True), 1e-30)
    o_c = jnp.dot(vcmpT_ref[0, 0], p_c.astype(bf16), preferred_element_type=f32)

    p_sum = p_c[:, 0:tq]
    for g in range(1, G):
        p_sum = p_sum + p_c[:, g * tq:(g + 1) * tq]
    imp = _dot_bf16x3(ovT_ref[...], p_sum)
    j_iota = lax.broadcasted_iota(jnp.int32, (n_sel, tq), 0)
    cur = lax.shift_right_logical(t_tok, int(np.log2(SEL_LEN)))
    forced = (j_iota == 0) | (j_iota == cur) | (j_iota == cur - 1)
    cand = (j_iota >= 1) & (j_iota <= cur - 2)
    quota = n_top - 1 - jnp.minimum(cur, 2)
    key = jnp.where(cand, pltpu.bitcast(imp, jnp.int32), -1)
    key_ref[...] = key

    def rank_body(i, rank):
        row = key_ref[pl.ds(i, 1), :]
        return rank + jnp.where(row + jnp.where(j_iota > i, 1, 0) > key, 1, 0)

    i_end = jnp.maximum(lax.shift_right_logical(t0 + tq - 1, int(np.log2(SEL_LEN))) - 1, 1)
    rank = lax.fori_loop(1, i_end, rank_body, jnp.zeros((n_sel, tq), jnp.int32))
    sel = forced | (cand & (rank < quota))
    lim_ref[...] = jnp.where(sel, t_tok, -1)

    softmax_init = (jnp.full((1, R), NEG_INIT, f32), jnp.zeros((1, R), f32), jnp.zeros((NSA_HEAD, R), f32))

    def softmax_step(carry, s, vT):
        m, l, acc = carry
        m_new = jnp.maximum(m, jnp.max(s, axis=0, keepdims=True))
        alpha = jnp.exp2(m - m_new)
        p = jnp.exp2(s - m_new)
        l = alpha * l + jnp.sum(p, axis=0, keepdims=True)
        acc = alpha * acc + jnp.dot(vT, p.astype(bf16), preferred_element_type=f32)
        return m_new, l, acc

    def attend(score_fn, vT_fn, lo, hi, sa_ref, sb_ref):
        def scores(kt):
            pen = jnp.where(kt < hi, 0.0, -jnp.inf)
            return score_fn(jnp.minimum(kt, hi - 1), pen)

        def update(carry, s_ref, kt):
            return softmax_step(carry, s_ref[...], vT_fn(jnp.minimum(kt, hi - 1)))

        def body(i, carry):
            kt = lo + 2 * i
            sb_ref[...] = scores(kt + 1)
            carry = update(carry, sa_ref, kt)
            sa_ref[...] = scores(kt + 2)
            return update(carry, sb_ref, kt + 1)

        sa_ref[...] = scores(lo)
        _, l, acc = lax.fori_loop(0, lax.div(hi - lo + 1, 2), body, softmax_init)
        return acc / jnp.maximum(l, 1e-30)

    tile_g = lambda bias: jnp.concatenate([bias] * G, axis=1)

    tk = min(NSA_TK_SEL, ksel_ref.shape[2])
    nb = tk // SEL_LEN
    key_iota = lax.broadcasted_iota(jnp.int32, (tk, tq), 0)

    def sel_scores(kt, pen):
        k0 = pl.multiple_of(kt * tk, tk)
        s = jnp.dot(ksel_ref[0, 0, pl.ds(k0, tk), :], qrot, preferred_element_type=f32)
        limb = lim_ref[pl.ds(pl.multiple_of(kt * nb, nb), nb), :] - k0
        lim_t = jnp.concatenate([jnp.broadcast_to(limb[jb:jb + 1, :], (SEL_LEN, tq)) for jb in range(nb)], axis=0)
        return s + tile_g(jnp.where(key_iota <= lim_t, pen, -jnp.inf))

    o_s = attend(sel_scores, lambda kt: vselT_ref[0, 0, :, pl.ds(pl.multiple_of(kt * tk, tk), tk)],
                 0, lax.div(t0 + tq - 1, tk) + 1, ssa_ref, ssb_ref)

    tkw = min(NSA_TK_WIN, kwin_ref.shape[2])
    key_iota_w = lax.broadcasted_iota(jnp.int32, (tkw, tq), 0)

    assert WINDOW % tkw == 0 and tq % tkw == 0
    carry = softmax_init
    for i in range((WINDOW + tq) // tkw):
        kt = lax.div(t0, tkw) - WINDOW // tkw + i
        pen = jnp.where(kt >= 0, 0.0, -jnp.inf)
        k0 = pl.multiple_of(jnp.maximum(kt, 0) * tkw, tkw)
        s = jnp.dot(kwin_ref[0, 0, pl.ds(k0, tkw), :], qrot, preferred_element_type=f32)
        rel = t_tok - k0
        s = s + tile_g(jnp.where((key_iota_w <= rel) & (key_iota_w > rel - WINDOW), pen, -jnp.inf))
        carry = softmax_step(carry, s, vwinT_ref[0, 0, :, pl.ds(k0, tkw)])
    o_w = carry[2] / jnp.maximum(carry[1], 1e-30)

    gl = jnp.concatenate([gl_ref[0, g] for g in range(G)], axis=1)
    gates = jax.nn.sigmoid(gl)
    out = gates[0:1, :] * o_c + gates[1:2, :] * o_s + gates[2:3, :] * o_w
    per = LANES // NSA_HEAD
    for pair in range(G // per):
        rows = jnp.concatenate([out[:, (pair * per + j) * tq:(pair * per + j + 1) * tq] for j in range(per)], axis=0)
        o_ref[0, :, pair * LANES:(pair + 1) * LANES] = rows.T


NSA_PREP_T = 256
NSA_COL0 = RW_COLS


def _cmp_mlp_kernel(ab_ref, c_ref, w2_ref, o_ref):
    n = ab_ref.shape[1]
    ab = ab_ref[0]
    nxt = jnp.concatenate([ab[1:, CMP_HIDDEN:], jnp.zeros((1, CMP_HIDDEN), jnp.float32)], axis=0)
    hid = jax.nn.gelu(ab[:, :CMP_HIDDEN] + nxt + c_ref[...])
    out = jnp.dot(hid.astype(jnp.bfloat16), w2_ref[...], preferred_element_type=jnp.float32)
    valid = lax.broadcasted_iota(jnp.int32, out.shape, 0) < n - 1
    o_ref[0] = jnp.where(valid, out, 0.0)


def _cmp_mlp_out(ab, c, w2):
    bh, n, _ = ab.shape
    d = w2.shape[1]
    return pl.pallas_call(
        _cmp_mlp_kernel,
        grid=(bh,),
        in_specs=[pl.BlockSpec((1, n, 2 * CMP_HIDDEN), lambda i: (i, 0, 0)), pl.BlockSpec((1, CMP_HIDDEN), lambda i: (0, 0)),
                  pl.BlockSpec((CMP_HIDDEN, d), lambda i: (0, 0))],
        out_specs=pl.BlockSpec((1, n, d), lambda i: (i, 0, 0)),
        out_shape=jax.ShapeDtypeStruct((bh, n, d), jnp.float32),
        compiler_params=pltpu.CompilerParams(dimension_semantics=("arbitrary",)),
        name="cmp_mlp",
    )(ab, c, w2.astype(jnp.bfloat16))


def _nsa_prep_kernel(qa_ref, qb_ref, kc_ref, ks_ref, kw_ref, gl_ref, cos_ref, sin_ref,
                     qraw_ref, qrot_ref, ksel_ref, vselT_ref, kwin_ref, vwinT_ref, glT_ref, kcmp_ref, vcmp_ref):
    bf16 = jnp.bfloat16
    d, half = NSA_HEAD, NSA_HEAD // 2
    scale = NSA_HEAD ** -0.5
    cos, sin = cos_ref[0], sin_ref[0]
    first = (lax.broadcasted_iota(jnp.int32, cos.shape, 1) & (d - 1)) < half

    def rope(x):
        partner = jnp.where(first, pltpu.roll(x, LANES - half, 1), pltpu.roll(x, half, 1))
        return x * cos + jnp.where(first, -partner, partner) * sin

    def put_heads_t(ref, first_head, x_t, dtype):
        for i in range(LANES // d):
            ref[0, first_head + i] = x_t[i * d:(i + 1) * d].astype(dtype)

    for j in range(NSA_HEADS * d // LANES):
        src = qa_ref if j < 2 else qb_ref
        piece = src[0][:, (j % 2) * LANES:(j % 2 + 1) * LANES]
        put_heads_t(qraw_ref, 2 * j, (piece * scale).T, jnp.float32)
        put_heads_t(qrot_ref, 2 * j, (rope(piece) * (scale * LOG2E)).T, bf16)
    for src, k_out, vT_out in ((ks_ref, ksel_ref, vselT_ref), (kw_ref, kwin_ref, vwinT_ref)):
        kv = src[0]
        kr = rope(kv[:, :LANES])
        for h in range(NSA_KV_HEADS):
            k_out[0, h] = kr[:, h * d:(h + 1) * d].astype(bf16)
        put_heads_t(vT_out, 0, kv[:, LANES:].T, bf16)
    gl_t = gl_ref[0].T
    for h in range(NSA_HEADS):
        glT_ref[0, h] = gl_t[3 * h:3 * h + 3]
    kvc = kc_ref[0]
    for h in range(NSA_KV_HEADS):
        kcmp_ref[0, h] = kvc[:, h * d:(h + 1) * d]
        vcmp_ref[0, h] = kvc[:, LANES + h * d:LANES + (h + 1) * d]


def _nsa_prep(z, cos, sin):
    B, T, _ = z.shape
    f32, bf16 = jnp.float32, jnp.bfloat16
    t = min(NSA_PREP_T, T)
    H, Hkv, d = NSA_HEADS, NSA_KV_HEADS, NSA_HEAD
    cw = 2 * NSA_KV
    gl0 = NSA_COL0 + NSA_WIDTH + 6 * NSA_KV
    assert NSA_COL0 % cw == 0 and NSA_WIDTH == 2 * cw and NSA_KV == LANES and gl0 % LANES == 0
    c0 = NSA_COL0 // cw
    wide = lambda j: pl.BlockSpec((1, t, cw), lambda b, i: (b, i, c0 + j))
    gl_col = gl0 // LANES
    tab = pl.BlockSpec((1, t, LANES), lambda b, i: (b, i, 0))
    tile4 = lambda c: jnp.tile(c.reshape(B, T, d // 2), (1, 1, LANES // (d // 2)))
    q_out = pl.BlockSpec((1, H, d, t), lambda b, i: (b, 0, 0, i))
    k_out = pl.BlockSpec((1, Hkv, t, d), lambda b, i: (b, 0, i, 0))
    vT_out = pl.BlockSpec((1, Hkv, d, t), lambda b, i: (b, 0, 0, i))
    return pl.pallas_call(
        _nsa_prep_kernel,
        grid=(B, T // t),
        in_specs=[wide(0), wide(1), wide(2), wide(3), wide(4), pl.BlockSpec((1, t, LANES), lambda b, i: (b, i, gl_col)),
                  tab, tab],
        out_specs=[q_out, q_out, k_out, vT_out, k_out, vT_out, pl.BlockSpec((1, H, 3, t), lambda b, i: (b, 0, 0, i)),
                   k_out, k_out],
        out_shape=[jax.ShapeDtypeStruct((B, H, d, T), f32), jax.ShapeDtypeStruct((B, H, d, T), bf16),
                   jax.ShapeDtypeStruct((B, Hkv, T, d), bf16), jax.ShapeDtypeStruct((B, Hkv, d, T), bf16),
                   jax.ShapeDtypeStruct((B, Hkv, T, d), bf16), jax.ShapeDtypeStruct((B, Hkv, d, T), bf16),
                   jax.ShapeDtypeStruct((B, H, 3, T), f32),
                   jax.ShapeDtypeStruct((B, Hkv, T, d), f32), jax.ShapeDtypeStruct((B, Hkv, T, d), f32)],
        compiler_params=pltpu.CompilerParams(dimension_semantics=("arbitrary", "arbitrary"),
                                             vmem_limit_bytes=VMEM_LIMIT_BYTES),
        name="nsa_prep",
    )(z, z, z, z, z, z, tile4(cos), tile4(sin))


def _nsa_group(z, cos, sin, cmp_pos, cmp_w1, cmp_w2):
    B, T, _ = z.shape
    f32, bf16 = jnp.float32, jnp.bfloat16
    qraw_T, qrot_T, ksel, vselT, kwin, vwinT, gl_T, kc, vc = _nsa_prep(z, cos, sin)

    n_grp = T // CMP_STRIDE
    n_cmp = n_grp - 1
    half = CMP_STRIDE * NSA_HEAD
    cmp_kv = []
    for zi, src in enumerate((kc, vc)):
        w1ab = jnp.concatenate([cmp_w1[zi, :half], cmp_w1[zi, half:]], axis=1)
        ab = _dense(src.reshape(B * NSA_KV_HEADS * n_grp, half), w1ab).reshape(B * NSA_KV_HEADS, n_grp, 2 * CMP_HIDDEN)
        c = cmp_pos[zi].reshape(1, CMP_LEN * NSA_HEAD) @ cmp_w1[zi]
        cmp_kv.append(_cmp_mlp_out(ab, c, cmp_w2[zi]).reshape(B, NSA_KV_HEADS, n_grp, NSA_HEAD))
    k_cmp = cmp_kv[0]
    v_cmpT = cmp_kv[1].transpose(0, 1, 3, 2).astype(bf16)

    n_sel = T // SEL_LEN
    n_top = min(SEL_TOP, n_sel)
    ovT = jnp.asarray(np.pad(_cmp_sel_overlap(n_cmp, n_sel), ((0, 1), (0, 0))).T)
    tq = min(NSA_TQ, T)
    G = NSA_GROUP
    q_spec = pl.BlockSpec((1, G, NSA_HEAD, tq), lambda b, h, i: (b, h, 0, i))
    kv_spec = lambda shape: pl.BlockSpec((1, 1) + shape, lambda b, h, i: (b, h, 0, 0))
    out = pl.pallas_call(
        functools.partial(_nsa_attn_kernel, tq=tq, n_top=n_top),
        grid=(B, NSA_KV_HEADS, T // tq),
        in_specs=[q_spec, q_spec, kv_spec((n_grp, NSA_HEAD)), kv_spec((NSA_HEAD, n_grp)),
                  pl.BlockSpec((n_sel, n_grp), lambda b, h, i: (0, 0)),
                  kv_spec((T, NSA_HEAD)), kv_spec((NSA_HEAD, T)), kv_spec((T, NSA_HEAD)), kv_spec((NSA_HEAD, T)),
                  pl.BlockSpec((1, G, 3, tq), lambda b, h, i: (b, h, 0, i))],
        out_specs=pl.BlockSpec((1, tq, G * NSA_HEAD), lambda b, h, i: (b, i, h)),
        out_shape=jax.ShapeDtypeStruct((B, T, NSA_WIDTH), f32),
        scratch_shapes=[pltpu.VMEM((n_sel, tq), jnp.int32), pltpu.VMEM((n_sel, tq), jnp.int32)]
        + [pltpu.VMEM((min(NSA_TK_SEL, T), G * tq), f32)] * 2,
        compiler_params=pltpu.CompilerParams(dimension_semantics=("arbitrary", "arbitrary", "arbitrary"),
                                             vmem_limit_bytes=VMEM_LIMIT_BYTES),
        name="nsa_attn",
    )(qraw_T, qrot_T, k_cmp, v_cmpT, ovT, ksel, vselT, kwin, vwinT, gl_T)
    return out


HG_STEP = 256
HG_HEADS_PER_STEP = 8


def _hgrn2_kernel(q_ref, f_ref, i_ref, g_ref, lb_ref, ng_ref, o_ref, st_ref):
    f32, bf16 = jnp.float32, jnp.bfloat16
    S, d = q_ref.shape[1], HG_HEAD
    hb = q_ref.shape[2] // d
    C = HG_CHUNK
    n_sub = S // C
    shift = int(np.log2(C))

    @pl.when(pl.program_id(2) == 0)
    def _():
        st_ref[...] = jnp.zeros_like(st_ref)

    heads = lambda ref: jnp.stack([ref[0][:, h * d:(h + 1) * d] for h in range(hb)])
    lb = lb_ref[...]
    forget = lb + (1.0 - lb) * _sigmoid(heads(f_ref))
    logf = jnp.log(forget)
    k = 1.0 - forget
    q = heads(q_ref)
    qs = q * _sigmoid(q)
    v = heads(i_ref)

    pn = min(LANES, S)
    n_pan = S // pn
    panels = lambda x: x.reshape(hb * n_pan, pn, d)
    row = lax.broadcasted_iota(jnp.int32, (pn, pn), 0)
    col = lax.broadcasted_iota(jnp.int32, (pn, pn), 1)
    same = lax.shift_right_logical(row, shift) == lax.shift_right_logical(col, shift)
    causal = same & (row >= col)

    def cumsum01(mask, x):
        m = jnp.broadcast_to(jnp.where(mask, 1.0, 0.0).astype(bf16), (hb * n_pan, pn, pn))
        return sum(jnp.einsum('hij,hjk->hik', m, t, preferred_element_type=f32) for t in _split3(x))

    logf_p = panels(logf)
    b = cumsum01(causal, logf_p)
    tail = cumsum01(same & (col > row), logf_p)
    q_e = (panels(qs) * jnp.exp(b)).astype(bf16)
    k_e = (panels(k) * jnp.exp(-b)).astype(bf16)
    k_tail = (panels(k) * jnp.exp(tail)).reshape(hb, n_pan, pn, d)
    v_p = panels(v)

    a = jnp.einsum('hsd,htd->hst', q_e, k_e, preferred_element_type=f32)
    o_intra = jnp.einsum('hst,htv->hsv', jnp.where(causal, a, 0.0).astype(bf16), v_p.astype(bf16),
                         preferred_element_type=f32).reshape(hb, S, d)

    vT = jnp.stack([v_p[i].T for i in range(hb * n_pan)]).astype(bf16).reshape(hb, n_pan, d, pn)
    q_e = q_e.reshape(hb, S, d)
    b = b.reshape(hb, S, d)
    row_chunk = lax.shift_right_logical(lax.broadcasted_iota(jnp.int32, (pn, d), 0), shift)
    st = st_ref[...]
    o_inter = []
    for c in range(n_sub):
        sl = slice(c * C, (c + 1) * C)
        pan, c_in = divmod(c, pn // C)
        o_inter.append(jnp.einsum('hcd,hvd->hcv', q_e[:, sl], st.astype(bf16), preferred_element_type=f32))
        k_c = jnp.where(row_chunk == c_in, k_tail[:, pan], 0.0).astype(bf16)
        d_c = jnp.exp(b[:, (c + 1) * C - 1:(c + 1) * C, :])
        st = st * d_c + jnp.einsum('hvs,hsk->hvk', vT[:, pan], k_c, preferred_element_type=f32)
    st_ref[...] = st
    o = o_intra + jnp.concatenate(o_inter, axis=1)
    o = o * lax.rsqrt(jnp.mean(o * o, axis=-1, keepdims=True) + LN_EPS) * ng_ref[...]
    g = heads(g_ref)
    o = o * (g * _sigmoid(g))
    for h in range(hb):
        o_ref[0, :, h * d:(h + 1) * d] = o[h]


def _hgrn2_mixer(x, w_in, lb, norm_g):
    B, T, D = x.shape
    z = _dense(x.reshape(B * T, D), w_in).reshape(B, T, -1)
    S = min(HG_STEP, T)
    H, d = HG_HEADS, HG_HEAD
    hb = HG_HEADS_PER_STEP
    ng = H // hb
    col = lambda j: pl.BlockSpec((1, S, hb * d), lambda b, h, t: (b, t, j * ng + h))
    vec = pl.BlockSpec((hb, 1, d), lambda b, h, t: (h, 0, 0))
    o = pl.pallas_call(
        _hgrn2_kernel,
        grid=(B, ng, T // S),
        in_specs=[col(0), col(1), col(2), col(3), vec, pl.BlockSpec((1, 1, d), lambda b, h, t: (0, 0, 0))],
        out_specs=pl.BlockSpec((1, S, hb * d), lambda b, h, t: (b, t, h)),
        out_shape=jax.ShapeDtypeStruct((B, T, D), jnp.float32),
        scratch_shapes=[pltpu.VMEM((hb, d, d), jnp.float32)],
        compiler_params=pltpu.CompilerParams(dimension_semantics=("arbitrary", "arbitrary", "arbitrary"),
                                             vmem_limit_bytes=VMEM_LIMIT_BYTES),
        name="hgrn2",
    )(z, z, z, z, lb.reshape(H, 1, d), norm_g.reshape(1, 1, d))
    return o.reshape(B * T, D)


def _moe_expert_kernel(blk_e_ref, x_ref, w1_ref, w3_ref, w2_ref, o_ref, w1b_ref, w3b_ref, w2b_ref):
    f32, bf16 = jnp.float32, jnp.bfloat16
    i = pl.program_id(0)

    @pl.when((i == 0) | (blk_e_ref[i] != blk_e_ref[jnp.maximum(i - 1, 0)]))
    def _():
        w1b_ref[...] = w1_ref[0, 0].astype(bf16)
        w3b_ref[...] = w3_ref[0, 0].astype(bf16)
        w2b_ref[...] = w2_ref[0, 0].astype(bf16)

    x = x_ref[...].astype(bf16)
    a = jnp.dot(x, w1b_ref[...], preferred_element_type=f32)
    b = jnp.dot(x, w3b_ref[...], preferred_element_type=f32)
    hid = (jax.nn.silu(a) * b).astype(bf16)
    o_ref[...] = jnp.dot(hid, w2b_ref[...], preferred_element_type=f32)


def _moe_experts(xbuf, blk_e, w1, w3, w2, layer):
    P, D = xbuf.shape
    hid = w1.shape[-1]
    bf16 = jnp.bfloat16
    w_spec = lambda shape: pl.BlockSpec((1, 1) + shape, lambda i, be: (layer, be[i], 0, 0))
    x_spec = pl.BlockSpec((MOE_BLOCK, D), lambda i, be: (i, 0))
    return pl.pallas_call(
        _moe_expert_kernel,
        grid_spec=pltpu.PrefetchScalarGridSpec(
            num_scalar_prefetch=1,
            grid=(P // MOE_BLOCK,),
            in_specs=[x_spec, w_spec((D, hid)), w_spec((D, hid)), w_spec((hid, D))],
            out_specs=x_spec,
            scratch_shapes=[pltpu.VMEM((D, hid), bf16), pltpu.VMEM((D, hid), bf16), pltpu.VMEM((hid, D), bf16)],
        ),
        out_shape=jax.ShapeDtypeStruct((P, D), jnp.float32),
        compiler_params=pltpu.CompilerParams(dimension_semantics=("arbitrary",), vmem_limit_bytes=VMEM_LIMIT_BYTES),
        name="moe_experts",
    )(blk_e, xbuf, w1, w3, w2)


MOE_ROUTE_TM = 512
MOE_ROW_TM = 256
MOE_ROUTE_ROWS = 40


def _moe_route_kernel(h_ref, wr_ref, br_ref, eid_ref, gate_ref, pos_ref, cnt_ref, run_ref):
    f32 = jnp.float32
    tm = h_ref.shape[0]
    E, PG, NG = MOE_EXPERTS, MOE_PER_GROUP, MOE_GROUPS

    @pl.when(pl.program_id(0) == 0)
    def _():
        run_ref[...] = jnp.zeros_like(run_ref)

    lg = _dot_bf16x3(wr_ref[...], h_ref[...], transpose_b=True) + br_ref[...]
    grp = lg[E:E + NG]
    g_iota = lax.broadcasted_iota(jnp.int32, (NG, tm), 0)
    g_max = jnp.max(grp, axis=0, keepdims=True)
    g_sel = jnp.min(jnp.where(grp == g_max, g_iota, NG), axis=0, keepdims=True)
    p_grp = 1.0 / jnp.sum(jnp.exp(grp - g_max), axis=0, keepdims=True)
    le = lg[0:PG]
    for g in range(1, NG):
        le = jnp.where(g_sel == g, lg[g * PG:(g + 1) * PG], le)
    e_iota = lax.broadcasted_iota(jnp.int32, (PG, tm), 0)
    m1 = jnp.max(le, axis=0, keepdims=True)
    i1 = jnp.min(jnp.where(le == m1, e_iota, PG), axis=0, keepdims=True)
    le2 = jnp.where(e_iota == i1, -jnp.inf, le)
    m2 = jnp.max(le2, axis=0, keepdims=True)
    i2 = jnp.min(jnp.where(le2 == m2, e_iota, PG), axis=0, keepdims=True)
    e2 = jnp.exp(m2 - m1)
    den = 1.0 + e2
    eid1 = g_sel * PG + i1
    eid2 = g_sel * PG + i2
    eid_ref[...] = jnp.concatenate([eid1, eid2], axis=0)
    gate_ref[...] = jnp.concatenate([p_grp / den, p_grp * e2 / den], axis=0)

    x_iota = lax.broadcasted_iota(jnp.int32, (E, tm), 0)
    oh1 = jnp.where(x_iota == eid1, 1.0, 0.0)
    oh2 = jnp.where(x_iota == eid2, 1.0, 0.0)
    before = jnp.where(lax.broadcasted_iota(jnp.int32, (tm, tm), 0) < lax.broadcasted_iota(jnp.int32, (tm, tm), 1),
                       1.0, 0.0).astype(jnp.bfloat16)
    cum1 = jnp.dot(oh1.astype(jnp.bfloat16), before, preferred_element_type=f32)
    cum2 = jnp.dot(oh2.astype(jnp.bfloat16), before, preferred_element_type=f32)
    tot1 = jnp.sum(oh1, axis=1, keepdims=True)
    tot2 = jnp.sum(oh2, axis=1, keepdims=True)
    base = run_ref[:, 0:1]
    pos1 = jnp.sum(oh1 * (cum1 + base), axis=0, keepdims=True)
    pos2 = jnp.sum(oh2 * (cum2 + base + tot1), axis=0, keepdims=True)
    pos_ref[...] = jnp.concatenate([pos1, pos2], axis=0).astype(jnp.int32)
    new = jnp.broadcast_to(base + tot1 + tot2, run_ref.shape)
    run_ref[...] = new
    cnt_ref[...] = new


def _moe_scatter_kernel(zblk_ref, dest_ref, h_ref, xbuf_ref, zero_ref, zsem, sem):
    i = pl.program_id(0)
    rt = dest_ref.shape[1]

    @pl.when(i == 0)
    def _():
        zero_ref[...] = jnp.zeros_like(zero_ref)

        def zero_copy(j):
            row0 = pl.multiple_of(zblk_ref[j] * MOE_BLOCK, MOE_BLOCK)
            return pltpu.make_async_copy(zero_ref, xbuf_ref.at[pl.ds(row0, MOE_BLOCK)], zsem)

        def z_start(j, c):
            @pl.when(zblk_ref[j] >= 0)
            def _():
                zero_copy(j).start()
            return c

        def z_wait(j, c):
            @pl.when(zblk_ref[j] >= 0)
            def _():
                zero_copy(j).wait()
            return c

        lax.fori_loop(0, zblk_ref.shape[0], z_start, 0)
        lax.fori_loop(0, zblk_ref.shape[0], z_wait, 0)

    def row_copy(r, j):
        return pltpu.make_async_copy(h_ref.at[pl.ds(r, 1)], xbuf_ref.at[pl.ds(dest_ref[j, r], 1)], sem)

    def drain(r, c):
        row_copy(r, 0).wait()
        row_copy(r, 1).wait()
        return c

    for r in range(rt):
        row_copy(r, 0).start()
        row_copy(r, 1).start()
    lax.fori_loop(0, rt, drain, 0, unroll=8)


def _moe_combine_ln_kernel(dest_ref, dnext_ref, gate_ref, h_ref, lng_ref, lnb_ref, ybuf_ref, o_ref, buf_ref, sems):
    i = pl.program_id(0)
    n = pl.num_programs(0)
    rt = h_ref.shape[0]
    cur, nxt = i % 2, (i + 1) % 2

    def row_copy(d_ref, r, j, slot):
        return pltpu.make_async_copy(ybuf_ref.at[pl.ds(d_ref[j, r], 1)], buf_ref.at[slot, j, pl.ds(r, 1)], sems.at[slot])

    def gather(d_ref, slot):
        def body(r, c):
            row_copy(d_ref, r, 0, slot).start()
            row_copy(d_ref, r, 1, slot).start()
            return c
        lax.fori_loop(0, rt, body, 0, unroll=8)

    def drain(slot):
        def body(r, c):
            row_copy(dest_ref, r, 0, slot).wait()
            row_copy(dest_ref, r, 1, slot).wait()
            return c
        lax.fori_loop(0, rt, body, 0, unroll=8)

    @pl.when(i == 0)
    def _():
        gather(dest_ref, 0)

    for slot in range(2):
        @pl.when(nxt == slot)
        def _(slot=slot):
            for r in range(rt):
                row_copy(dnext_ref, r, 0, slot).start()
                row_copy(dnext_ref, r, 1, slot).start()

    drain(cur)
    gate = gate_ref[...]
    ffn = gate[:, 0:1] * buf_ref[cur, 0] + gate[:, 1:2] * buf_ref[cur, 1]
    o_ref[...] = _layer_norm(DN_ALPHA * h_ref[...] + ffn, lng_ref[...], lnb_ref[...])

    @pl.when(i == n - 1)
    def _():
        drain(nxt)


def _hier_moe_ln(h, w_rg, b_rg, w_re, b_re, w1, w3, w2, ln_g, ln_b, layer=0):
    if w1.ndim == 3:
        w1, w3, w2 = w1[None], w3[None], w2[None]
    M, D = h.shape
    f32, i32 = jnp.float32, jnp.int32
    E = MOE_EXPERTS
    pad_rows = MOE_ROUTE_ROWS - E - MOE_GROUPS
    wr = jnp.concatenate([w_re.T, w_rg.T, jnp.zeros((pad_rows, D), f32)], axis=0)
    br = jnp.concatenate([b_re, b_rg, jnp.zeros((pad_rows,), f32)]).reshape(MOE_ROUTE_ROWS, 1)
    tm = min(MOE_ROUTE_TM, M)
    slot_spec = pl.BlockSpec((MOE_TOPK, tm), lambda i: (0, i))
    eid, gate, pos, cnt = pl.pallas_call(
        _moe_route_kernel,
        grid=(M // tm,),
        in_specs=[pl.BlockSpec((tm, D), lambda i: (i, 0)), pl.BlockSpec((MOE_ROUTE_ROWS, D), lambda i: (0, 0)),
                  pl.BlockSpec((MOE_ROUTE_ROWS, 1), lambda i: (0, 0))],
        out_specs=[slot_spec, slot_spec, slot_spec, pl.BlockSpec((E, LANES), lambda i: (0, 0))],
        out_shape=[jax.ShapeDtypeStruct((MOE_TOPK, M), i32), jax.ShapeDtypeStruct((MOE_TOPK, M), f32),
                   jax.ShapeDtypeStruct((MOE_TOPK, M), i32), jax.ShapeDtypeStruct((E, LANES), f32)],
        scratch_shapes=[pltpu.VMEM((E, LANES), f32)],
        compiler_params=pltpu.CompilerParams(dimension_semantics=("arbitrary",), vmem_limit_bytes=VMEM_LIMIT_BYTES),
        name="moe_route",
    )(h, wr, br)

    counts = cnt[:, 0].astype(i32)
    padded = (counts + MOE_BLOCK - 1) // MOE_BLOCK * MOE_BLOCK
    ends = jnp.cumsum(padded)
    start = ends - padded
    P = M * MOE_TOPK + E * MOE_BLOCK
    n_blk = P // MOE_BLOCK
    blk_e = jnp.minimum(jnp.sum(ends[None, :] <= (jnp.arange(n_blk) * MOE_BLOCK)[:, None], axis=1), E - 1).astype(i32)
    dest = pos + jnp.sum(jnp.where(eid[:, :, None] == jnp.arange(E), start, 0), axis=-1)

    rt = min(MOE_ROW_TM, M)
    n_row = M // rt
    any_spec = pl.BlockSpec(memory_space=pl.ANY)
    last_blk = jnp.where(padded > 0, ends // MOE_BLOCK - 1, -1)
    tail_blk = ends[-1] // MOE_BLOCK + jnp.arange(E)
    zblk = jnp.concatenate([last_blk, jnp.where(tail_blk < n_blk, tail_blk, -1)]).astype(i32)
    xbuf = pl.pallas_call(
        _moe_scatter_kernel,
        grid_spec=pltpu.PrefetchScalarGridSpec(
            num_scalar_prefetch=1,
            grid=(n_row,),
            in_specs=[pl.BlockSpec((MOE_TOPK, rt), lambda i, zb: (0, i), memory_space=pltpu.SMEM),
                      pl.BlockSpec((rt, D), lambda i, zb: (i, 0))],
            out_specs=any_spec,
            scratch_shapes=[pltpu.VMEM((MOE_BLOCK, D), f32), pltpu.SemaphoreType.DMA(()), pltpu.SemaphoreType.DMA(())],
        ),
        out_shape=jax.ShapeDtypeStruct((P, D), f32),
        compiler_params=pltpu.CompilerParams(dimension_semantics=("arbitrary",)),
        name="moe_scatter",
    )(zblk, dest, h)

    ybuf = _moe_experts(xbuf, blk_e, w1, w3, w2, layer)

    dest_spec = pl.BlockSpec((MOE_TOPK, rt), lambda i: (0, i), memory_space=pltpu.SMEM)
    dnext_spec = pl.BlockSpec((MOE_TOPK, rt), lambda i: (0, jnp.minimum(i + 1, n_row - 1)), memory_space=pltpu.SMEM)
    row_spec = pl.BlockSpec((rt, D), lambda i: (i, 0))
    vec_spec = pl.BlockSpec((1, D), lambda i: (0, 0))
    return pl.pallas_call(
        _moe_combine_ln_kernel,
        grid=(n_row,),
        in_specs=[dest_spec, dnext_spec, pl.BlockSpec((rt, MOE_TOPK), lambda i: (i, 0)), row_spec, vec_spec, vec_spec,
                  any_spec],
        out_specs=row_spec,
        out_shape=jax.ShapeDtypeStruct((M, D), f32),
        scratch_shapes=[pltpu.VMEM((2, MOE_TOPK, rt, D), f32), pltpu.SemaphoreType.DMA((2,))],
        compiler_params=pltpu.CompilerParams(dimension_semantics=("arbitrary",), vmem_limit_bytes=VMEM_LIMIT_BYTES),
        name="moe_combine_ln",
    )(dest, dest, gate.T, h, ln_g.reshape(1, D), ln_b.reshape(1, D), ybuf)


def _dense_res_ln_kernel(*refs):
    n = (len(refs) - 4) // 2
    x_ref, g_ref, b_ref, o_ref = refs[2 * n:]
    mix = DN_ALPHA * x_ref[...]
    for y_ref, w_ref in zip(refs[:n], refs[n:2 * n]):
        mix = mix + jnp.dot(y_ref[...].astype(jnp.bfloat16), w_ref[...], preferred_element_type=jnp.float32)
    o_ref[...] = _layer_norm(mix, g_ref[...], b_ref[...])


def _dense_res_ln(ys, w, x, ln_g, ln_b, tm=512):
    m, d = x.shape
    tm = min(tm, m)
    cuts = np.cumsum([0] + [y.shape[1] for y in ys])
    ws = [w[int(a):int(b)].astype(jnp.bfloat16) for a, b in zip(cuts[:-1], cuts[1:])]
    row_spec = lambda k: pl.BlockSpec((tm, k), lambda i: (i, 0))
    vec_spec = pl.BlockSpec((1, d), lambda i: (0, 0))
    return pl.pallas_call(
        _dense_res_ln_kernel,
        grid=(m // tm,),
        in_specs=[row_spec(y.shape[1]) for y in ys] + [pl.BlockSpec(wi.shape, lambda i: (0, 0)) for wi in ws]
        + [row_spec(d), vec_spec, vec_spec],
        out_specs=row_spec(d),
        out_shape=jax.ShapeDtypeStruct((m, d), jnp.float32),
        compiler_params=pltpu.CompilerParams(dimension_semantics=("arbitrary",), vmem_limit_bytes=VMEM_LIMIT_BYTES),
        name="dense_res_ln",
    )(*ys, *ws, x, ln_g.reshape(1, d), ln_b.reshape(1, d))


def _ple_kernel(h_ref, p_ref, wg_ref, wp_ref, o_ref):
    bf16, f32 = jnp.bfloat16, jnp.float32
    h = h_ref[...]
    gate = jax.nn.sigmoid(jnp.dot(h.astype(bf16), wg_ref[...], preferred_element_type=f32))
    o_ref[...] = h + gate * jnp.dot(p_ref[...].astype(bf16), wp_ref[...], preferred_element_type=f32)


def _ple(h, p, wg, wp, tm=512):
    m, d = h.shape
    kp = p.shape[1]
    tm = min(tm, m)
    return pl.pallas_call(
        _ple_kernel,
        grid=(m // tm,),
        in_specs=[pl.BlockSpec((tm, d), lambda i: (i, 0)), pl.BlockSpec((tm, kp), lambda i: (i, 0)),
                  pl.BlockSpec((d, d), lambda i: (0, 0)), pl.BlockSpec((kp, d), lambda i: (0, 0))],
        out_specs=pl.BlockSpec((tm, d), lambda i: (i, 0)),
        out_shape=jax.ShapeDtypeStruct((m, d), jnp.float32),
        compiler_params=pltpu.CompilerParams(dimension_semantics=("arbitrary",), vmem_limit_bytes=VMEM_LIMIT_BYTES),
        name="ple",
    )(h, p, wg.astype(jnp.bfloat16), wp.astype(jnp.bfloat16))


def kernel(x, p, positions, ev_w_in, ev_w_out, rw_mu, rw_w0, rw_w2, rw_a0, rw_a2, rw_g2, rw_k_k, rw_k_a,
           rw_r_k, rw_gn_g, rw_gn_b, nsa_cmp_pos, nsa_cmp_w1, nsa_cmp_w2, od_w_in, od_w_out, hg_lb, hg_norm_g,
           moe_w_rg, moe_b_rg, moe_w_re, moe_b_re, moe_w1, moe_w3, moe_w2, ln_g, ln_b, ple_w, ple_gate_w):
    B, T, D = x.shape
    M = B * T
    cos, sin = _rope_tables(positions, NSA_HEAD)
    lb_soft = jax.nn.softmax(hg_lb, axis=0)
    lb_all = jnp.cumsum(lb_soft, axis=0) - lb_soft[0:1]
    for li in range(DEPTH):
        j = li // 2
        if li % 2 == 0:
            z = _dense(x.reshape(M, D), ev_w_in[j], keep_pad=True).reshape(B, T, -1)
            y_rw = _rwkv7_branch(z, rw_mu[j], rw_w0[j], rw_w2[j], rw_a0[j], rw_a2[j], rw_g2[j], rw_k_k[j],
                                 rw_k_a[j], rw_r_k[j], rw_gn_g[j], rw_gn_b[j])
            y_nsa = _nsa_group(z, cos, sin, nsa_cmp_pos[j], nsa_cmp_w1[j], nsa_cmp_w2[j])
            ys, w_out = [y_rw.reshape(M, RW_WIDTH), y_nsa.reshape(M, NSA_WIDTH)], ev_w_out[j]
        else:
            ys, w_out = [_hgrn2_mixer(x, od_w_in[j], lb_all[li], hg_norm_g[j])], od_w_out[j]
        h = _dense_res_ln(ys, w_out, x.reshape(M, D), ln_g[li, 0], ln_b[li, 0])
        h = _hier_moe_ln(h, moe_w_rg[li], moe_b_rg[li], moe_w_re[li], moe_b_re[li], moe_w1, moe_w3, moe_w2,
                         ln_g[li, 1], ln_b[li, 1], layer=li)
        x = _ple(h, p[li].reshape(M, PLE_DIM), ple_gate_w[li], ple_w[li]).reshape(B, T, D)
    return x
```

```python
import functools

import numpy as np
import jax
import jax.numpy as jnp
from jax import lax
from jax.experimental import pallas as pl
from jax.experimental.pallas import tpu as pltpu

D_MODEL = 1024
DEPTH = 2
PLE_DIM = 256
DN_ALPHA = (2 * DEPTH) ** 0.25
LN_EPS = 1e-5
ROPE_THETA = 10000.0

RW_WIDTH = D_MODEL // 2
RW_HEAD = 64
RW_HEADS = RW_WIDTH // RW_HEAD
RW_DECAY_LORA = 64
RW_AAA_LORA = 64
RW_GATE_LORA = 128
RW_GN_EPS = RW_HEAD * 1e-5
RW_SPLITS = (RW_WIDTH, RW_WIDTH, RW_WIDTH, RW_DECAY_LORA, RW_AAA_LORA, RW_GATE_LORA)
RW_COLS = sum(RW_SPLITS)

NSA_WIDTH = D_MODEL - RW_WIDTH
NSA_HEAD = 64
NSA_HEADS = NSA_WIDTH // NSA_HEAD
NSA_KV_HEADS = 2
NSA_GROUP = NSA_HEADS // NSA_KV_HEADS
NSA_KV = NSA_KV_HEADS * NSA_HEAD
CMP_LEN = 32
CMP_STRIDE = 16
CMP_HIDDEN = 128
SEL_LEN = 64
SEL_TOP = 16
WINDOW = 512
NSA_SPLITS = (NSA_WIDTH,) + (NSA_KV,) * 6 + (3 * NSA_HEADS,)
NSA_COLS = sum(NSA_SPLITS)
EV_COLS = RW_COLS + NSA_COLS

HG_HEAD = 128
HG_HEADS = D_MODEL // HG_HEAD
HG_CHUNK = 16

MOE_GROUPS = 4
MOE_PER_GROUP = 8
MOE_EXPERTS = MOE_GROUPS * MOE_PER_GROUP
MOE_TOPK = 2
MOE_BLOCK = 512

LANES = 128
VMEM_LIMIT_BYTES = 56 * 1024 * 1024


def _round_up(n, m):
    return (n + m - 1) // m * m


def _dense_kernel(x_ref, w_ref, o_ref):
    o_ref[...] = jnp.dot(x_ref[...].astype(jnp.bfloat16), w_ref[...], preferred_element_type=jnp.float32)


def _dense(x2d, w, tm=512, keep_pad=False):
    m, k = x2d.shape
    n = w.shape[1]
    n_pad = _round_up(n, LANES)
    wb = w.astype(jnp.bfloat16)
    if n_pad != n:
        wb = jnp.pad(wb, ((0, 0), (0, n_pad - n)))
    tm = min(tm, m)
    assert m % tm == 0
    out = pl.pallas_call(
        _dense_kernel,
        grid=(m // tm,),
        in_specs=[pl.BlockSpec((tm, k), lambda i: (i, 0)), pl.BlockSpec((k, n_pad), lambda i: (0, 0))],
        out_specs=pl.BlockSpec((tm, n_pad), lambda i: (i, 0)),
        out_shape=jax.ShapeDtypeStruct((m, n_pad), jnp.float32),
        compiler_params=pltpu.CompilerParams(dimension_semantics=("arbitrary",), vmem_limit_bytes=VMEM_LIMIT_BYTES),
        name="dense",
    )(x2d, wb)
    return out if keep_pad or n_pad == n else out[:, :n]


def _sigmoid(x):
    return 1.0 / (1.0 + jnp.exp(-x))


def _layer_norm(x, g, b):
    xc = x - jnp.mean(x, -1, keepdims=True)
    var = jnp.mean(xc * xc, -1, keepdims=True)
    return xc * lax.rsqrt(var + LN_EPS) * g + b


def _rope_tables(positions, dim):
    inv = (1.0 / (ROPE_THETA ** (np.arange(0, dim, 2, dtype=np.float32) / dim))).astype(np.float32)
    ang = positions.astype(jnp.float32)[..., None] * inv
    return jnp.cos(ang)[:, :, None, :], jnp.sin(ang)[:, :, None, :]


RW_CHUNK = 64
RW_STEP = 128


def _split3(x):
    bf16, f32 = jnp.bfloat16, jnp.float32
    h1 = x.astype(bf16)
    r1 = x - h1.astype(f32)
    h2 = r1.astype(bf16)
    return h1, h2, (r1 - h2.astype(f32)).astype(bf16)


def _rwkv7_kernel(z_ref, mu_ref, vec_ref, w2_ref, a2_ref, g2_ref, bd_ref, o_ref, carry_ref, h_ref):
    f32, bf16 = jnp.float32, jnp.bfloat16
    S = z_ref.shape[1]
    W, N, H, C = RW_WIDTH, RW_HEAD, RW_HEADS, RW_CHUNK
    n_sub = S // C
    lora_w = RW_DECAY_LORA + RW_AAA_LORA

    @pl.when(pl.program_id(1) == 0)
    def _():
        carry_ref[...] = jnp.zeros_like(carry_ref)
        h_ref[...] = jnp.zeros_like(h_ref)

    dotf = lambda a, b: jnp.dot(a, b, preferred_element_type=f32)
    bd = bd_ref[...]
    head_sum = lambda x: sum(dotf(t, bd) for t in _split3(x))

    z = z_ref[0]
    z_prev = jnp.concatenate([carry_ref[0:1, :], z[:S - 1, :]], axis=0)
    carry_ref[0:1, :] = z[S - 1:S, :]
    zs = z + mu_ref[...] * (z_prev - z)
    r, k, v = zs[:, 0:W], zs[:, W:2 * W], zs[:, 2 * W:3 * W]
    lora = zs[:, 3 * W:3 * W + lora_w]
    gd = zs[:, 3 * W + lora_w:]
    w0, a0, k_k, k_a, r_k, gn_g, gn_b = (vec_ref[i:i + 1, :] for i in range(7))
    w_pre = -(w0 + dotf(jnp.tanh(lora).astype(bf16), w2_ref[...]))
    softplus = jnp.maximum(w_pre, 0.0) + jnp.log(1.0 + jnp.exp(-jnp.abs(w_pre)))
    lw = -jnp.exp(-softplus - 0.5)
    a = jax.nn.sigmoid(a0 + dotf(lora.astype(bf16), a2_ref[...]))
    g = dotf(jax.nn.sigmoid(gd).astype(bf16), g2_ref[...])
    kk = k * k_k
    kk = kk / jnp.maximum(jnp.sqrt(head_sum(kk * kk)), 1e-12)
    k = k * (1.0 + (a - 1.0) * k_a)
    b = a * kk

    row = lax.broadcasted_iota(jnp.int32, (S, S), 0)
    col = lax.broadcasted_iota(jnp.int32, (S, S), 1)
    same = lax.shift_right_logical(row, int(np.log2(C))) == lax.shift_right_logical(col, int(np.log2(C)))
    incl = same & (row >= col)
    strict = same & (row > col)
    tri = jnp.where(incl, 1.0, 0.0).astype(bf16)
    cs = sum(dotf(tri, t) for t in _split3(lw))
    e_neg = jnp.exp(-cs)

    stack = lambda x: jnp.stack([x[:, h * N:(h + 1) * N] for h in range(H)])

    def stack_t(x):
        parts = []
        for j in range(W // LANES):
            t = x[:, j * LANES:(j + 1) * LANES].T
            parts += [t[i * N:(i + 1) * N] for i in range(LANES // N)]
        return jnp.stack(parts)

    kks = stack(kk * jnp.exp(cs - lw))
    rs = stack(r * jnp.exp(cs))
    vh = stack(v).astype(bf16)
    bsT = stack_t(b * e_neg)
    ksT = stack_t(k * e_neg)
    csT = stack_t(cs)

    def bmm(x, y):
        return jnp.einsum('hij,hjk->hik', x.astype(bf16), y.astype(bf16), preferred_element_type=f32)

    lhs = jnp.concatenate([kks, rs], axis=1)
    mb = bmm(lhs, bsT)
    mk = bmm(lhs, ksT)
    a_b = jnp.where(strict, mb[:, :S], 0.0)
    a_k = jnp.where(strict, mk[:, :S], 0.0)
    q_b = jnp.where(incl, mb[:, S:], 0.0)
    q_k = jnp.where(incl, mk[:, S:], 0.0)

    t_inv = jnp.where(row == col, 1.0, 0.0) - a_b
    pw = a_b
    for _ in range(int(np.log2(C)) - 1):
        pw = bmm(pw, pw)
        t_inv = t_inv + bmm(t_inv, pw)

    w_mat = bmm(t_inv, kks)
    u_loc = -bmm(t_inv, bmm(a_k, vh))
    q_eff = rs - bmm(q_b, w_mat)
    o_loc = bmm(q_b, u_loc) + bmm(q_k, vh)

    eye_n = jnp.where(lax.broadcasted_iota(jnp.int32, (N, N), 0) == lax.broadcasted_iota(jnp.int32, (N, N), 1), 1.0, 0.0)
    lane_chunk = lax.shift_right_logical(lax.broadcasted_iota(jnp.int32, (N, S), 1), int(np.log2(C)))
    hc = h_ref[...]
    outs = []
    for c in range(n_sub):
        in_c = lane_chunk == c
        bs_c = jnp.where(in_c, bsT, 0.0)
        ks_c = jnp.where(in_c, ksT, 0.0)
        g_end = jnp.exp(csT[:, :, (c + 1) * C - 1:(c + 1) * C])
        g_mat = g_end * (eye_n - bmm(bs_c, w_mat))
        h_loc = g_end * (bmm(bs_c, u_loc) + bmm(ks_c, vh))
        sl = slice(c * C, (c + 1) * C)
        outs.append(bmm(q_eff[:, sl], hc) + o_loc[:, sl])
        hc = bmm(g_mat, hc) + h_loc
    h_ref[...] = hc

    o = jnp.concatenate(outs, axis=1)
    oc = o - jnp.mean(o, axis=-1, keepdims=True)
    o = oc * lax.rsqrt(jnp.mean(oc * oc, axis=-1, keepdims=True) + RW_GN_EPS)
    o = jnp.concatenate([o[h] for h in range(H)], axis=1)
    bonus = head_sum(r * k * r_k) * v
    o_ref[0] = (o * gn_g + gn_b + bonus) * g


def _rwkv7_branch(z, mu, w0, w2, a0, a2, g2, k_k, k_a, r_k, gn_g, gn_b):
    B, T, _ = z.shape
    f32, bf16 = jnp.float32, jnp.bfloat16
    S = min(RW_STEP, T)
    assert S % RW_CHUNK == 0 and T % S == 0 and RW_DECAY_LORA + RW_AAA_LORA == LANES == RW_GATE_LORA
    W = RW_WIDTH
    vec = jnp.stack([w0, a0, k_k, k_a, r_k.reshape(W), gn_g, gn_b, jnp.zeros((W,), f32)])
    w2p = jnp.concatenate([w2, jnp.zeros((RW_AAA_LORA, W), f32)]).astype(bf16)
    a2p = jnp.concatenate([jnp.zeros((RW_DECAY_LORA, W), f32), a2]).astype(bf16)
    head_of = np.arange(W) // RW_HEAD
    bd = jnp.asarray(head_of[:, None] == head_of[None, :], dtype=bf16)
    full = lambda shape: pl.BlockSpec(shape, lambda bi, ci: (0,) * len(shape))
    return pl.pallas_call(
        _rwkv7_kernel,
        grid=(B, T // S),
        in_specs=[pl.BlockSpec((1, S, RW_COLS), lambda bi, ci: (bi, ci, 0)), full((1, RW_COLS)), full((8, W)),
                  full((LANES, W)), full((LANES, W)), full((LANES, W)), full((W, W))],
        out_specs=pl.BlockSpec((1, S, W), lambda bi, ci: (bi, ci, 0)),
        out_shape=jax.ShapeDtypeStruct((B, T, W), f32),
        scratch_shapes=[pltpu.VMEM((8, RW_COLS), f32), pltpu.VMEM((RW_HEADS, RW_HEAD, RW_HEAD), f32)],
        compiler_params=pltpu.CompilerParams(dimension_semantics=("arbitrary", "arbitrary"),
                                             vmem_limit_bytes=VMEM_LIMIT_BYTES),
        name="rwkv7",
    )(z, mu.reshape(1, RW_COLS), vec, w2p, a2p, g2.astype(bf16), bd)


def _cmp_sel_overlap(n_cmp, n_sel):
    cs = np.arange(n_cmp)[:, None] * CMP_STRIDE
    ss = np.arange(n_sel)[None, :] * SEL_LEN
    ov = np.clip(np.minimum(cs + CMP_LEN, ss + SEL_LEN) - np.maximum(cs, ss), 0, None)
    return (ov / CMP_LEN).astype(np.float32)


NSA_TQ = 256
NSA_TK_SEL = 512
NSA_TK_WIN = 256
NEG_INIT = -1e30
LOG2E = float(np.log2(np.e))


def _dot_bf16x3(a, b, transpose_b=False):
    bf16, f32 = jnp.bfloat16, jnp.float32
    ah, bh = a.astype(bf16), b.astype(bf16)
    al, bl = (a - ah.astype(f32)).astype(bf16), (b - bh.astype(f32)).astype(bf16)
    dims = (((1,), (1 if transpose_b else 0,)), ((), ()))
    d = lambda x, y: lax.dot_general(x, y, dims, preferred_element_type=f32)
    return d(ah, bh) + d(ah, bl) + d(al, bh)


def _nsa_attn_kernel(qraw_ref, qrot_ref, kcmp_ref, vcmpT_ref, ovT_ref, ksel_ref, vselT_ref, kwin_ref, vwinT_ref,
                     gl_ref, o_ref, lim_ref, key_ref, ssa_ref, ssb_ref, *, tq, n_top):
    f32, bf16 = jnp.float32, jnp.bfloat16
    G = NSA_GROUP
    R = G * tq
    t0 = pl.program_id(2) * tq
    ncp = kcmp_ref.shape[2]
    n_sel = ovT_ref.shape[0]

    qraw = jnp.concatenate([qraw_ref[0, g] for g in range(G)], axis=1)
    qrot = jnp.concatenate([qrot_ref[0, g] for g in range(G)], axis=1)

    s_c = _dot_bf16x3(kcmp_ref[0, 0], qraw)
    t_tok = t0 + lax.broadcasted_iota(jnp.int32, (1, tq), 1)
    cmp_last = lax.broadcasted_iota(jnp.int32, (ncp, tq), 0) * CMP_STRIDE + (CMP_LEN - 1)
    s_c = s_c + jnp.concatenate([jnp.where(cmp_last <= t_tok, 0.0, -jnp.inf)] * G, axis=1)
    m_c = jnp.max(s_c, axis=0, keepdims=True)
    m_c = jnp.where(m_c == -jnp.inf, 0.0, m_c)
    e_c = jnp.exp(s_c - m_c)
    p_c = e_c / jnp.maximum(jnp.sum(e_c, axis=0, keepdims=True), 1e-30)
    o_c = jnp.dot(vcmpT_ref[0, 0], p_c.astype(bf16), preferred_element_type=f32)

    p_sum = p_c[:, 0:tq]
    for g in range(1, G):
        p_sum = p_sum + p_c[:, g * tq:(g + 1) * tq]
    imp = _dot_bf16x3(ovT_ref[...], p_sum)
    j_iota = lax.broadcasted_iota(jnp.int32, (n_sel, tq), 0)
    cur = lax.shift_right_logical(t_tok, int(np.log2(SEL_LEN)))
    forced = (j_iota == 0) | (j_iota == cur) | (j_iota == cur - 1)
    cand = (j_iota >= 1) & (j_iota <= cur - 2)
    quota = n_top - 1 - jnp.minimum(cur, 2)
    key = jnp.where(cand, pltpu.bitcast(imp, jnp.int32), -1)
    key_ref[...] = key

    def rank_body(i, rank):
        row = key_ref[pl.ds(i, 1), :]
        return rank + jnp.where(row + jnp.where(j_iota > i, 1, 0) > key, 1, 0)

    i_end = jnp.maximum(lax.shift_right_logical(t0 + tq - 1, int(np.log2(SEL_LEN))) - 1, 1)
    rank = lax.fori_loop(1, i_end, rank_body, jnp.zeros((n_sel, tq), jnp.int32))
    sel = forced | (cand & (rank < quota))
    lim_ref[...] = jnp.where(sel, t_tok, -1)

    softmax_init = (jnp.full((1, R), NEG_INIT, f32), jnp.zeros((1, R), f32), jnp.zeros((NSA_HEAD, R), f32))

    def softmax_step(carry, s, vT):
        m, l, acc = carry
        m_new = jnp.maximum(m, jnp.max(s, axis=0, keepdims=True))
        alpha = jnp.exp2(m - m_new)
        p = jnp.exp2(s - m_new)
        l = alpha * l + jnp.sum(p, axis=0, keepdims=True)
        acc = alpha * acc + jnp.dot(vT, p.astype(bf16), preferred_element_type=f32)
        return m_new, l, acc

    def attend(score_fn, vT_fn, lo, hi, sa_ref, sb_ref):
        def scores(kt):
            pen = jnp.where(kt < hi, 0.0, -jnp.inf)
            return score_fn(jnp.minimum(kt, hi - 1), pen)

        def update(carry, s_ref, kt):
            return softmax_step(carry, s_ref[...], vT_fn(jnp.minimum(kt, hi - 1)))

        def body(i, carry):
            kt = lo + 2 * i
            sb_ref[...] = scores(kt + 1)
            carry = update(carry, sa_ref, kt)
            sa_ref[...] = scores(kt + 2)
            return update(carry, sb_ref, kt + 1)

        sa_ref[...] = scores(lo)
        _, l, acc = lax.fori_loop(0, lax.div(hi - lo + 1, 2), body, softmax_init)
        return acc / jnp.maximum(l, 1e-30)

    tile_g = lambda bias: jnp.concatenate([bias] * G, axis=1)

    tk = min(NSA_TK_SEL, ksel_ref.shape[2])
    nb = tk // SEL_LEN
    key_iota = lax.broadcasted_iota(jnp.int32, (tk, tq), 0)

    def sel_scores(kt, pen):
        k0 = pl.multiple_of(kt * tk, tk)
        s = jnp.dot(ksel_ref[0, 0, pl.ds(k0, tk), :], qrot, preferred_element_type=f32)
        limb = lim_ref[pl.ds(pl.multiple_of(kt * nb, nb), nb), :] - k0
        lim_t = jnp.concatenate([jnp.broadcast_to(limb[jb:jb + 1, :], (SEL_LEN, tq)) for jb in range(nb)], axis=0)
        return s + tile_g(jnp.where(key_iota <= lim_t, pen, -jnp.inf))

    o_s = attend(sel_scores, lambda kt: vselT_ref[0, 0, :, pl.ds(pl.multiple_of(kt * tk, tk), tk)],
                 0, lax.div(t0 + tq - 1, tk) + 1, ssa_ref, ssb_ref)

    tkw = min(NSA_TK_WIN, kwin_ref.shape[2])
    key_iota_w = lax.broadcasted_iota(jnp.int32, (tkw, tq), 0)

    assert WINDOW % tkw == 0 and tq % tkw == 0
    carry = softmax_init
    for i in range((WINDOW + tq) // tkw):
        kt = lax.div(t0, tkw) - WINDOW // tkw + i
        pen = jnp.where(kt >= 0, 0.0, -jnp.inf)
        k0 = pl.multiple_of(jnp.maximum(kt, 0) * tkw, tkw)
        s = jnp.dot(kwin_ref[0, 0, pl.ds(k0, tkw), :], qrot, preferred_element_type=f32)
        rel = t_tok - k0
        s = s + tile_g(jnp.where((key_iota_w <= rel) & (key_iota_w > rel - WINDOW), pen, -jnp.inf))
        carry = softmax_step(carry, s, vwinT_ref[0, 0, :, pl.ds(k0, tkw)])
    o_w = carry[2] / jnp.maximum(carry[1], 1e-30)

    gl = jnp.concatenate([gl_ref[0, g] for g in range(G)], axis=1)
    gates = jax.nn.sigmoid(gl)
    out = gates[0:1, :] * o_c + gates[1:2, :] * o_s + gates[2:3, :] * o_w
    per = LANES // NSA_HEAD
    for pair in range(G // per):
        rows = jnp.concatenate([out[:, (pair * per + j) * tq:(pair * per + j + 1) * tq] for j in range(per)], axis=0)
        o_ref[0, :, pair * LANES:(pair + 1) * LANES] = rows.T


NSA_PREP_T = 256
NSA_COL0 = RW_COLS


def _cmp_mlp_kernel(ab_ref, c_ref, w2_ref, o_ref):
    n = ab_ref.shape[1]
    ab = ab_ref[0]
    nxt = jnp.concatenate([ab[1:, CMP_HIDDEN:], jnp.zeros((1, CMP_HIDDEN), jnp.float32)], axis=0)
    hid = jax.nn.gelu(ab[:, :CMP_HIDDEN] + nxt + c_ref[...])
    out = jnp.dot(hid.astype(jnp.bfloat16), w2_ref[...], preferred_element_type=jnp.float32)
    valid = lax.broadcasted_iota(jnp.int32, out.shape, 0) < n - 1
    o_ref[0] = jnp.where(valid, out, 0.0)


def _cmp_mlp_out(ab, c, w2):
    bh, n, _ = ab.shape
    d = w2.shape[1]
    return pl.pallas_call(
        _cmp_mlp_kernel,
        grid=(bh,),
        in_specs=[pl.BlockSpec((1, n, 2 * CMP_HIDDEN), lambda i: (i, 0, 0)), pl.BlockSpec((1, CMP_HIDDEN), lambda i: (0, 0)),
                  pl.BlockSpec((CMP_HIDDEN, d), lambda i: (0, 0))],
        out_specs=pl.BlockSpec((1, n, d), lambda i: (i, 0, 0)),
        out_shape=jax.ShapeDtypeStruct((bh, n, d), jnp.float32),
        compiler_params=pltpu.CompilerParams(dimension_semantics=("arbitrary",)),
        name="cmp_mlp",
    )(ab, c, w2.astype(jnp.bfloat16))


def _nsa_prep_kernel(qa_ref, qb_ref, kc_ref, ks_ref, kw_ref, gl_ref, cos_ref, sin_ref,
                     qraw_ref, qrot_ref, ksel_ref, vselT_ref, kwin_ref, vwinT_ref, glT_ref, kcmp_ref, vcmp_ref):
    bf16 = jnp.bfloat16
    d, half = NSA_HEAD, NSA_HEAD // 2
    scale = NSA_HEAD ** -0.5
    cos, sin = cos_ref[0], sin_ref[0]
    first = (lax.broadcasted_iota(jnp.int32, cos.shape, 1) & (d - 1)) < half

    def rope(x):
        partner = jnp.where(first, pltpu.roll(x, LANES - half, 1), pltpu.roll(x, half, 1))
        return x * cos + jnp.where(first, -partner, partner) * sin

    def put_heads_t(ref, first_head, x_t, dtype):
        for i in range(LANES // d):
            ref[0, first_head + i] = x_t[i * d:(i + 1) * d].astype(dtype)

    for j in range(NSA_HEADS * d // LANES):
        src = qa_ref if j < 2 else qb_ref
        piece = src[0][:, (j % 2) * LANES:(j % 2 + 1) * LANES]
        put_heads_t(qraw_ref, 2 * j, (piece * scale).T, jnp.float32)
        put_heads_t(qrot_ref, 2 * j, (rope(piece) * (scale * LOG2E)).T, bf16)
    for src, k_out, vT_out in ((ks_ref, ksel_ref, vselT_ref), (kw_ref, kwin_ref, vwinT_ref)):
        kv = src[0]
        kr = rope(kv[:, :LANES])
        for h in range(NSA_KV_HEADS):
            k_out[0, h] = kr[:, h * d:(h + 1) * d].astype(bf16)
        put_heads_t(vT_out, 0, kv[:, LANES:].T, bf16)
    gl_t = gl_ref[0].T
    for h in range(NSA_HEADS):
        glT_ref[0, h] = gl_t[3 * h:3 * h + 3]
    kvc = kc_ref[0]
    for h in range(NSA_KV_HEADS):
        kcmp_ref[0, h] = kvc[:, h * d:(h + 1) * d]
        vcmp_ref[0, h] = kvc[:, LANES + h * d:LANES + (h + 1) * d]


def _nsa_prep(z, cos, sin):
    B, T, _ = z.shape
    f32, bf16 = jnp.float32, jnp.bfloat16
    t = min(NSA_PREP_T, T)
    H, Hkv, d = NSA_HEADS, NSA_KV_HEADS, NSA_HEAD
    cw = 2 * NSA_KV
    gl0 = NSA_COL0 + NSA_WIDTH + 6 * NSA_KV
    assert NSA_COL0 % cw == 0 and NSA_WIDTH == 2 * cw and NSA_KV == LANES and gl0 % LANES == 0
    c0 = NSA_COL0 // cw
    wide = lambda j: pl.BlockSpec((1, t, cw), lambda b, i: (b, i, c0 + j))
    gl_col = gl0 // LANES
    tab = pl.BlockSpec((1, t, LANES), lambda b, i: (b, i, 0))
    tile4 = lambda c: jnp.tile(c.reshape(B, T, d // 2), (1, 1, LANES // (d // 2)))
    q_out = pl.BlockSpec((1, H, d, t), lambda b, i: (b, 0, 0, i))
    k_out = pl.BlockSpec((1, Hkv, t, d), lambda b, i: (b, 0, i, 0))
    vT_out = pl.BlockSpec((1, Hkv, d, t), lambda b, i: (b, 0, 0, i))
    return pl.pallas_call(
        _nsa_prep_kernel,
        grid=(B, T // t),
        in_specs=[wide(0), wide(1), wide(2), wide(3), wide(4), pl.BlockSpec((1, t, LANES), lambda b, i: (b, i, gl_col)),
                  tab, tab],
        out_specs=[q_out, q_out, k_out, vT_out, k_out, vT_out, pl.BlockSpec((1, H, 3, t), lambda b, i: (b, 0, 0, i)),
                   k_out, k_out],
        out_shape=[jax.ShapeDtypeStruct((B, H, d, T), f32), jax.ShapeDtypeStruct((B, H, d, T), bf16),
                   jax.ShapeDtypeStruct((B, Hkv, T, d), bf16), jax.ShapeDtypeStruct((B, Hkv, d, T), bf16),
                   jax.ShapeDtypeStruct((B, Hkv, T, d), bf16), jax.ShapeDtypeStruct((B, Hkv, d, T), bf16),
                   jax.ShapeDtypeStruct((B, H, 3, T), f32),
                   jax.ShapeDtypeStruct((B, Hkv, T, d), f32), jax.ShapeDtypeStruct((B, Hkv, T, d), f32)],
        compiler_params=pltpu.CompilerParams(dimension_semantics=("arbitrary", "arbitrary"),
                                             vmem_limit_bytes=VMEM_LIMIT_BYTES),
        name="nsa_prep",
    )(z, z, z, z, z, z, tile4(cos), tile4(sin))


def _nsa_group(z, cos, sin, cmp_pos, cmp_w1, cmp_w2):
    B, T, _ = z.shape
    f32, bf16 = jnp.float32, jnp.bfloat16
    qraw_T, qrot_T, ksel, vselT, kwin, vwinT, gl_T, kc, vc = _nsa_prep(z, cos, sin)

    n_grp = T // CMP_STRIDE
    n_cmp = n_grp - 1
    half = CMP_STRIDE * NSA_HEAD
    cmp_kv = []
    for zi, src in enumerate((kc, vc)):
        w1ab = jnp.concatenate([cmp_w1[zi, :half], cmp_w1[zi, half:]], axis=1)
        ab = _dense(src.reshape(B * NSA_KV_HEADS * n_grp, half), w1ab).reshape(B * NSA_KV_HEADS, n_grp, 2 * CMP_HIDDEN)
        c = cmp_pos[zi].reshape(1, CMP_LEN * NSA_HEAD) @ cmp_w1[zi]
        cmp_kv.append(_cmp_mlp_out(ab, c, cmp_w2[zi]).reshape(B, NSA_KV_HEADS, n_grp, NSA_HEAD))
    k_cmp = cmp_kv[0]
    v_cmpT = cmp_kv[1].transpose(0, 1, 3, 2).astype(bf16)

    n_sel = T // SEL_LEN
    n_top = min(SEL_TOP, n_sel)
    ovT = jnp.asarray(np.pad(_cmp_sel_overlap(n_cmp, n_sel), ((0, 1), (0, 0))).T)
    tq = min(NSA_TQ, T)
    G = NSA_GROUP
    q_spec = pl.BlockSpec((1, G, NSA_HEAD, tq), lambda b, h, i: (b, h, 0, i))
    kv_spec = lambda shape: pl.BlockSpec((1, 1) + shape, lambda b, h, i: (b, h, 0, 0))
    out = pl.pallas_call(
        functools.partial(_nsa_attn_kernel, tq=tq, n_top=n_top),
        grid=(B, NSA_KV_HEADS, T // tq),
        in_specs=[q_spec, q_spec, kv_spec((n_grp, NSA_HEAD)), kv_spec((NSA_HEAD, n_grp)),
                  pl.BlockSpec((n_sel, n_grp), lambda b, h, i: (0, 0)),
                  kv_spec((T, NSA_HEAD)), kv_spec((NSA_HEAD, T)), kv_spec((T, NSA_HEAD)), kv_spec((NSA_HEAD, T)),
                  pl.BlockSpec((1, G, 3, tq), lambda b, h, i: (b, h, 0, i))],
        out_specs=pl.BlockSpec((1, tq, G * NSA_HEAD), lambda b, h, i: (b, i, h)),
        out_shape=jax.ShapeDtypeStruct((B, T, NSA_WIDTH), f32),
        scratch_shapes=[pltpu.VMEM((n_sel, tq), jnp.int32), pltpu.VMEM((n_sel, tq), jnp.int32)]
        + [pltpu.VMEM((min(NSA_TK_SEL, T), G * tq), f32)] * 2,
        compiler_params=pltpu.CompilerParams(dimension_semantics=("arbitrary", "arbitrary", "arbitrary"),
                                             vmem_limit_bytes=VMEM_LIMIT_BYTES),
        name="nsa_attn",
    )(qraw_T, qrot_T, k_cmp, v_cmpT, ovT, ksel, vselT, kwin, vwinT, gl_T)
    return out


HG_STEP = 256
HG_HEADS_PER_STEP = 8


def _hgrn2_kernel(q_ref, f_ref, i_ref, g_ref, lb_ref, ng_ref, o_ref, st_ref):
    f32, bf16 = jnp.float32, jnp.bfloat16
    S, d = q_ref.shape[1], HG_HEAD
    hb = q_ref.shape[2] // d
    C = HG_CHUNK
    n_sub = S // C
    shift = int(np.log2(C))

    @pl.when(pl.program_id(2) == 0)
    def _():
        st_ref[...] = jnp.zeros_like(st_ref)

    heads = lambda ref: jnp.stack([ref[0][:, h * d:(h + 1) * d] for h in range(hb)])
    lb = lb_ref[...]
    forget = lb + (1.0 - lb) * _sigmoid(heads(f_ref))
    logf = jnp.log(forget)
    k = 1.0 - forget
    q = heads(q_ref)
    qs = q * _sigmoid(q)
    v = heads(i_ref)

    pn = min(LANES, S)
    n_pan = S // pn
    panels = lambda x: x.reshape(hb * n_pan, pn, d)
    row = lax.broadcasted_iota(jnp.int32, (pn, pn), 0)
    col = lax.broadcasted_iota(jnp.int32, (pn, pn), 1)
    same = lax.shift_right_logical(row, shift) == lax.shift_right_logical(col, shift)
    causal = same & (row >= col)

    def cumsum01(mask, x):
        m = jnp.broadcast_to(jnp.where(mask, 1.0, 0.0).astype(bf16), (hb * n_pan, pn, pn))
        return sum(jnp.einsum('hij,hjk->hik', m, t, preferred_element_type=f32) for t in _split3(x))

    logf_p = panels(logf)
    b = cumsum01(causal, logf_p)
    tail = cumsum01(same & (col > row), logf_p)
    q_e = (panels(qs) * jnp.exp(b)).astype(bf16)
    k_e = (panels(k) * jnp.exp(-b)).astype(bf16)
    k_tail = (panels(k) * jnp.exp(tail)).reshape(hb, n_pan, pn, d)
    v_p = panels(v)

    a = jnp.einsum('hsd,htd->hst', q_e, k_e, preferred_element_type=f32)
    o_intra = jnp.einsum('hst,htv->hsv', jnp.where(causal, a, 0.0).astype(bf16), v_p.astype(bf16),
                         preferred_element_type=f32).reshape(hb, S, d)

    vT = jnp.stack([v_p[i].T for i in range(hb * n_pan)]).astype(bf16).reshape(hb, n_pan, d, pn)
    q_e = q_e.reshape(hb, S, d)
    b = b.reshape(hb, S, d)
    row_chunk = lax.shift_right_logical(lax.broadcasted_iota(jnp.int32, (pn, d), 0), shift)
    st = st_ref[...]
    o_inter = []
    for c in range(n_sub):
        sl = slice(c * C, (c + 1) * C)
        pan, c_in = divmod(c, pn // C)
        o_inter.append(jnp.einsum('hcd,hvd->hcv', q_e[:, sl], st.astype(bf16), preferred_element_type=f32))
        k_c = jnp.where(row_chunk == c_in, k_tail[:, pan], 0.0).astype(bf16)
        d_c = jnp.exp(b[:, (c + 1) * C - 1:(c + 1) * C, :])
        st = st * d_c + jnp.einsum('hvs,hsk->hvk', vT[:, pan], k_c, preferred_element_type=f32)
    st_ref[...] = st
    o = o_intra + jnp.concatenate(o_inter, axis=1)
    o = o * lax.rsqrt(jnp.mean(o * o, axis=-1, keepdims=True) + LN_EPS) * ng_ref[...]
    g = heads(g_ref)
    o = o * (g * _sigmoid(g))
    for h in range(hb):
        o_ref[0, :, h * d:(h + 1) * d] = o[h]


def _hgrn2_mixer(x, w_in, lb, norm_g):
    B, T, D = x.shape
    z = _dense(x.reshape(B * T, D), w_in).reshape(B, T, -1)
    S = min(HG_STEP, T)
    H, d = HG_HEADS, HG_HEAD
    hb = HG_HEADS_PER_STEP
    ng = H // hb
    col = lambda j: pl.BlockSpec((1, S, hb * d), lambda b, h, t: (b, t, j * ng + h))
    vec = pl.BlockSpec((hb, 1, d), lambda b, h, t: (h, 0, 0))
    o = pl.pallas_call(
        _hgrn2_kernel,
        grid=(B, ng, T // S),
        in_specs=[col(0), col(1), col(2), col(3), vec, pl.BlockSpec((1, 1, d), lambda b, h, t: (0, 0, 0))],
        out_specs=pl.BlockSpec((1, S, hb * d), lambda b, h, t: (b, t, h)),
        out_shape=jax.ShapeDtypeStruct((B, T, D), jnp.float32),
        scratch_shapes=[pltpu.VMEM((hb, d, d), jnp.float32)],
        compiler_params=pltpu.CompilerParams(dimension_semantics=("arbitrary", "arbitrary", "arbitrary"),
                                             vmem_limit_bytes=VMEM_LIMIT_BYTES),
        name="hgrn2",
    )(z, z, z, z, lb.reshape(H, 1, d), norm_g.reshape(1, 1, d))
    return o.reshape(B * T, D)


MOE_X_SLOTS = 3


def _moe_expert_kernel(blk_e_ref, x_hbm, w1_ref, w3_ref, w2_ref, o_ref, w1b_ref, w3b_ref, w2b_ref, xr_ref, xsem):
    f32, bf16 = jnp.float32, jnp.bfloat16
    i = pl.program_id(0)
    n = pl.num_programs(0)
    ahead = MOE_X_SLOTS - 1

    def tile_copy(t):
        slot = t % MOE_X_SLOTS
        row0 = pl.multiple_of(t * MOE_BLOCK, MOE_BLOCK)
        return pltpu.make_async_copy(x_hbm.at[pl.ds(row0, MOE_BLOCK)], xr_ref.at[slot], xsem.at[slot])

    @pl.when(i == 0)
    def _():
        for t in range(ahead):
            @pl.when(t < n)
            def _(t=t):
                tile_copy(t).start()

    @pl.when(i + ahead < n)
    def _():
        tile_copy(i + ahead).start()

    tile_copy(i).wait()
    x_ref = xr_ref.at[i % MOE_X_SLOTS]

    @pl.when((i == 0) | (blk_e_ref[i] != blk_e_ref[jnp.maximum(i - 1, 0)]))
    def _():
        w1b_ref[...] = w1_ref[0, 0].astype(bf16)
        w3b_ref[...] = w3_ref[0, 0].astype(bf16)
        w2b_ref[...] = w2_ref[0, 0].astype(bf16)

    x = x_ref[...].astype(bf16)
    a = jnp.dot(x, w1b_ref[...], preferred_element_type=f32)
    b = jnp.dot(x, w3b_ref[...], preferred_element_type=f32)
    hid = (jax.nn.silu(a) * b).astype(bf16)
    o_ref[...] = jnp.dot(hid, w2b_ref[...], preferred_element_type=f32)


def _moe_experts(xbuf, blk_e, w1, w3, w2, layer):
    P, D = xbuf.shape
    hid = w1.shape[-1]
    bf16 = jnp.bfloat16
    w_spec = lambda shape: pl.BlockSpec((1, 1) + shape, lambda i, be: (layer, be[i], 0, 0))
    x_spec = pl.BlockSpec((MOE_BLOCK, D), lambda i, be: (i, 0))
    return pl.pallas_call(
        _moe_expert_kernel,
        grid_spec=pltpu.PrefetchScalarGridSpec(
            num_scalar_prefetch=1,
            grid=(P // MOE_BLOCK,),
            in_specs=[pl.BlockSpec(memory_space=pl.ANY), w_spec((D, hid)), w_spec((D, hid)), w_spec((hid, D))],
            out_specs=x_spec,
            scratch_shapes=[pltpu.VMEM((D, hid), bf16), pltpu.VMEM((D, hid), bf16), pltpu.VMEM((hid, D), bf16),
                            pltpu.VMEM((MOE_X_SLOTS, MOE_BLOCK, D), jnp.float32), pltpu.SemaphoreType.DMA((MOE_X_SLOTS,))],
        ),
        out_shape=jax.ShapeDtypeStruct((P, D), jnp.float32),
        compiler_params=pltpu.CompilerParams(dimension_semantics=("arbitrary",), vmem_limit_bytes=VMEM_LIMIT_BYTES),
        name="moe_experts",
    )(blk_e, xbuf, w1, w3, w2)


MOE_ROUTE_TM = 512
MOE_ROW_TM = 256
MOE_ROUTE_ROWS = 40


def _moe_route_kernel(h_ref, wr_ref, br_ref, eid_ref, gate_ref, pos_ref, cnt_ref, run_ref):
    f32 = jnp.float32
    tm = h_ref.shape[0]
    E, PG, NG = MOE_EXPERTS, MOE_PER_GROUP, MOE_GROUPS

    @pl.when(pl.program_id(0) == 0)
    def _():
        run_ref[...] = jnp.zeros_like(run_ref)

    lg = _dot_bf16x3(wr_ref[...], h_ref[...], transpose_b=True) + br_ref[...]
    grp = lg[E:E + NG]
    g_iota = lax.broadcasted_iota(jnp.int32, (NG, tm), 0)
    g_max = jnp.max(grp, axis=0, keepdims=True)
    g_sel = jnp.min(jnp.where(grp == g_max, g_iota, NG), axis=0, keepdims=True)
    p_grp = 1.0 / jnp.sum(jnp.exp(grp - g_max), axis=0, keepdims=True)
    le = lg[0:PG]
    for g in range(1, NG):
        le = jnp.where(g_sel == g, lg[g * PG:(g + 1) * PG], le)
    e_iota = lax.broadcasted_iota(jnp.int32, (PG, tm), 0)
    m1 = jnp.max(le, axis=0, keepdims=True)
    i1 = jnp.min(jnp.where(le == m1, e_iota, PG), axis=0, keepdims=True)
    le2 = jnp.where(e_iota == i1, -jnp.inf, le)
    m2 = jnp.max(le2, axis=0, keepdims=True)
    i2 = jnp.min(jnp.where(le2 == m2, e_iota, PG), axis=0, keepdims=True)
    e2 = jnp.exp(m2 - m1)
    den = 1.0 + e2
    eid1 = g_sel * PG + i1
    eid2 = g_sel * PG + i2
    eid_ref[...] = jnp.concatenate([eid1, eid2], axis=0)
    gate_ref[...] = jnp.concatenate([p_grp / den, p_grp * e2 / den], axis=0)

    x_iota = lax.broadcasted_iota(jnp.int32, (E, tm), 0)
    oh1 = jnp.where(x_iota == eid1, 1.0, 0.0)
    oh2 = jnp.where(x_iota == eid2, 1.0, 0.0)
    before = jnp.where(lax.broadcasted_iota(jnp.int32, (tm, tm), 0) < lax.broadcasted_iota(jnp.int32, (tm, tm), 1),
                       1.0, 0.0).astype(jnp.bfloat16)
    cum1 = jnp.dot(oh1.astype(jnp.bfloat16), before, preferred_element_type=f32)
    cum2 = jnp.dot(oh2.astype(jnp.bfloat16), before, preferred_element_type=f32)
    tot1 = jnp.sum(oh1, axis=1, keepdims=True)
    tot2 = jnp.sum(oh2, axis=1, keepdims=True)
    base = run_ref[:, 0:1]
    pos1 = jnp.sum(oh1 * (cum1 + base), axis=0, keepdims=True)
    pos2 = jnp.sum(oh2 * (cum2 + base + tot1), axis=0, keepdims=True)
    pos_ref[...] = jnp.concatenate([pos1, pos2], axis=0).astype(jnp.int32)
    new = jnp.broadcast_to(base + tot1 + tot2, run_ref.shape)
    run_ref[...] = new
    cnt_ref[...] = new


def _moe_scatter_kernel(zblk_ref, dest_ref, h_ref, xbuf_ref, zero_ref, zsem, sem):
    i = pl.program_id(0)
    rt = dest_ref.shape[1]

    @pl.when(i == 0)
    def _():
        zero_ref[...] = jnp.zeros_like(zero_ref)

        def zero_copy(j):
            row0 = pl.multiple_of(zblk_ref[j] * MOE_BLOCK, MOE_BLOCK)
            return pltpu.make_async_copy(zero_ref, xbuf_ref.at[pl.ds(row0, MOE_BLOCK)], zsem)

        def z_start(j, c):
            @pl.when(zblk_ref[j] >= 0)
            def _():
                zero_copy(j).start()
            return c

        def z_wait(j, c):
            @pl.when(zblk_ref[j] >= 0)
            def _():
                zero_copy(j).wait()
            return c

        lax.fori_loop(0, zblk_ref.shape[0], z_start, 0)
        lax.fori_loop(0, zblk_ref.shape[0], z_wait, 0)

    def row_copy(r, j):
        return pltpu.make_async_copy(h_ref.at[pl.ds(r, 1)], xbuf_ref.at[pl.ds(dest_ref[j, r], 1)], sem)

    def drain(r, c):
        row_copy(r, 0).wait()
        row_copy(r, 1).wait()
        return c

    for r in range(rt):
        row_copy(r, 0).start()
        row_copy(r, 1).start()
    lax.fori_loop(0, rt, drain, 0, unroll=8)


def _moe_combine_ln_kernel(dest_ref, dnext_ref, gate_ref, h_ref, lng_ref, lnb_ref, ybuf_ref, o_ref, buf_ref, sems):
    i = pl.program_id(0)
    n = pl.num_programs(0)
    rt = h_ref.shape[0]
    cur, nxt = i % 2, (i + 1) % 2

    def row_copy(d_ref, r, j, slot):
        return pltpu.make_async_copy(ybuf_ref.at[pl.ds(d_ref[j, r], 1)], buf_ref.at[slot, j, pl.ds(r, 1)], sems.at[slot])

    def gather(d_ref, slot):
        def body(r, c):
            row_copy(d_ref, r, 0, slot).start()
            row_copy(d_ref, r, 1, slot).start()
            return c
        lax.fori_loop(0, rt, body, 0, unroll=8)

    def drain(slot):
        def body(r, c):
            row_copy(dest_ref, r, 0, slot).wait()
            row_copy(dest_ref, r, 1, slot).wait()
            return c
        lax.fori_loop(0, rt, body, 0, unroll=8)

    @pl.when(i == 0)
    def _():
        gather(dest_ref, 0)

    for slot in range(2):
        @pl.when(nxt == slot)
        def _(slot=slot):
            for r in range(rt):
                row_copy(dnext_ref, r, 0, slot).start()
                row_copy(dnext_ref, r, 1, slot).start()

    drain(cur)
    gate = gate_ref[...]
    ffn = gate[:, 0:1] * buf_ref[cur, 0] + gate[:, 1:2] * buf_ref[cur, 1]
    o_ref[...] = _layer_norm(DN_ALPHA * h_ref[...] + ffn, lng_ref[...], lnb_ref[...])

    @pl.when(i == n - 1)
    def _():
        drain(nxt)


def _hier_moe_ln(h, w_rg, b_rg, w_re, b_re, w1, w3, w2, ln_g, ln_b, layer=0):
    if w1.ndim == 3:
        w1, w3, w2 = w1[None], w3[None], w2[None]
    M, D = h.shape
    f32, i32 = jnp.float32, jnp.int32
    E = MOE_EXPERTS
    pad_rows = MOE_ROUTE_ROWS - E - MOE_GROUPS
    wr = jnp.concatenate([w_re.T, w_rg.T, jnp.zeros((pad_rows, D), f32)], axis=0)
    br = jnp.concatenate([b_re, b_rg, jnp.zeros((pad_rows,), f32)]).reshape(MOE_ROUTE_ROWS, 1)
    tm = min(MOE_ROUTE_TM, M)
    slot_spec = pl.BlockSpec((MOE_TOPK, tm), lambda i: (0, i))
    eid, gate, pos, cnt = pl.pallas_call(
        _moe_route_kernel,
        grid=(M // tm,),
        in_specs=[pl.BlockSpec((tm, D), lambda i: (i, 0)), pl.BlockSpec((MOE_ROUTE_ROWS, D), lambda i: (0, 0)),
                  pl.BlockSpec((MOE_ROUTE_ROWS, 1), lambda i: (0, 0))],
        out_specs=[slot_spec, slot_spec, slot_spec, pl.BlockSpec((E, LANES), lambda i: (0, 0))],
        out_shape=[jax.ShapeDtypeStruct((MOE_TOPK, M), i32), jax.ShapeDtypeStruct((MOE_TOPK, M), f32),
                   jax.ShapeDtypeStruct((MOE_TOPK, M), i32), jax.ShapeDtypeStruct((E, LANES), f32)],
        scratch_shapes=[pltpu.VMEM((E, LANES), f32)],
        compiler_params=pltpu.CompilerParams(dimension_semantics=("arbitrary",), vmem_limit_bytes=VMEM_LIMIT_BYTES),
        name="moe_route",
    )(h, wr, br)

    counts = cnt[:, 0].astype(i32)
    padded = (counts + MOE_BLOCK - 1) // MOE_BLOCK * MOE_BLOCK
    ends = jnp.cumsum(padded)
    start = ends - padded
    P = M * MOE_TOPK + E * MOE_BLOCK
    n_blk = P // MOE_BLOCK
    blk_e = jnp.minimum(jnp.sum(ends[None, :] <= (jnp.arange(n_blk) * MOE_BLOCK)[:, None], axis=1), E - 1).astype(i32)
    dest = pos + jnp.sum(jnp.where(eid[:, :, None] == jnp.arange(E), start, 0), axis=-1)

    rt = min(MOE_ROW_TM, M)
    n_row = M // rt
    any_spec = pl.BlockSpec(memory_space=pl.ANY)
    last_blk = jnp.where(padded > 0, ends // MOE_BLOCK - 1, -1)
    tail_blk = ends[-1] // MOE_BLOCK + jnp.arange(E)
    zblk = jnp.concatenate([last_blk, jnp.where(tail_blk < n_blk, tail_blk, -1)]).astype(i32)
    xbuf = pl.pallas_call(
        _moe_scatter_kernel,
        grid_spec=pltpu.PrefetchScalarGridSpec(
            num_scalar_prefetch=1,
            grid=(n_row,),
            in_specs=[pl.BlockSpec((MOE_TOPK, rt), lambda i, zb: (0, i), memory_space=pltpu.SMEM),
                      pl.BlockSpec((rt, D), lambda i, zb: (i, 0))],
            out_specs=any_spec,
            scratch_shapes=[pltpu.VMEM((MOE_BLOCK, D), f32), pltpu.SemaphoreType.DMA(()), pltpu.SemaphoreType.DMA(())],
        ),
        out_shape=jax.ShapeDtypeStruct((P, D), f32),
        compiler_params=pltpu.CompilerParams(dimension_semantics=("arbitrary",)),
        name="moe_scatter",
    )(zblk, dest, h)

    ybuf = _moe_experts(xbuf, blk_e, w1, w3, w2, layer)

    dest_spec = pl.BlockSpec((MOE_TOPK, rt), lambda i: (0, i), memory_space=pltpu.SMEM)
    dnext_spec = pl.BlockSpec((MOE_TOPK, rt), lambda i: (0, jnp.minimum(i + 1, n_row - 1)), memory_space=pltpu.SMEM)
    row_spec = pl.BlockSpec((rt, D), lambda i: (i, 0))
    vec_spec = pl.BlockSpec((1, D), lambda i: (0, 0))
    return pl.pallas_call(
        _moe_combine_ln_kernel,
        grid=(n_row,),
        in_specs=[dest_spec, dnext_spec, pl.BlockSpec((rt, MOE_TOPK), lambda i: (i, 0)), row_spec, vec_spec, vec_spec,
                  any_spec],
        out_specs=row_spec,
        out_shape=jax.ShapeDtypeStruct((M, D), f32),
        scratch_shapes=[pltpu.VMEM((2, MOE_TOPK, rt, D), f32), pltpu.SemaphoreType.DMA((2,))],
        compiler_params=pltpu.CompilerParams(dimension_semantics=("arbitrary",), vmem_limit_bytes=VMEM_LIMIT_BYTES),
        name="moe_combine_ln",
    )(dest, dest, gate.T, h, ln_g.reshape(1, D), ln_b.reshape(1, D), ybuf)


def _dense_res_ln_kernel(*refs):
    n = (len(refs) - 4) // 2
    x_ref, g_ref, b_ref, o_ref = refs[2 * n:]
    mix = DN_ALPHA * x_ref[...]
    for y_ref, w_ref in zip(refs[:n], refs[n:2 * n]):
        mix = mix + jnp.dot(y_ref[...].astype(jnp.bfloat16), w_ref[...], preferred_element_type=jnp.float32)
    o_ref[...] = _layer_norm(mix, g_ref[...], b_ref[...])


def _dense_res_ln(ys, w, x, ln_g, ln_b, tm=512):
    m, d = x.shape
    tm = min(tm, m)
    cuts = np.cumsum([0] + [y.shape[1] for y in ys])
    ws = [w[int(a):int(b)].astype(jnp.bfloat16) for a, b in zip(cuts[:-1], cuts[1:])]
    row_spec = lambda k: pl.BlockSpec((tm, k), lambda i: (i, 0))
    vec_spec = pl.BlockSpec((1, d), lambda i: (0, 0))
    return pl.pallas_call(
        _dense_res_ln_kernel,
        grid=(m // tm,),
        in_specs=[row_spec(y.shape[1]) for y in ys] + [pl.BlockSpec(wi.shape, lambda i: (0, 0)) for wi in ws]
        + [row_spec(d), vec_spec, vec_spec],
        out_specs=row_spec(d),
        out_shape=jax.ShapeDtypeStruct((m, d), jnp.float32),
        compiler_params=pltpu.CompilerParams(dimension_semantics=("arbitrary",), vmem_limit_bytes=VMEM_LIMIT_BYTES),
        name="dense_res_ln",
    )(*ys, *ws, x, ln_g.reshape(1, d), ln_b.reshape(1, d))


def _ple_kernel(h_ref, p_ref, wg_ref, wp_ref, o_ref):
    bf16, f32 = jnp.bfloat16, jnp.float32
    h = h_ref[...]
    gate = jax.nn.sigmoid(jnp.dot(h.astype(bf16), wg_ref[...], preferred_element_type=f32))
    o_ref[...] = h + gate * jnp.dot(p_ref[...].astype(bf16), wp_ref[...], preferred_element_type=f32)


def _ple(h, p, wg, wp, tm=512):
    m, d = h.shape
    kp = p.shape[1]
    tm = min(tm, m)
    return pl.pallas_call(
        _ple_kernel,
        grid=(m // tm,),
        in_specs=[pl.BlockSpec((tm, d), lambda i: (i, 0)), pl.BlockSpec((tm, kp), lambda i: (i, 0)),
                  pl.BlockSpec((d, d), lambda i: (0, 0)), pl.BlockSpec((kp, d), lambda i: (0, 0))],
        out_specs=pl.BlockSpec((tm, d), lambda i: (i, 0)),
        out_shape=jax.ShapeDtypeStruct((m, d), jnp.float32),
        compiler_params=pltpu.CompilerParams(dimension_semantics=("arbitrary",), vmem_limit_bytes=VMEM_LIMIT_BYTES),
        name="ple",
    )(h, p, wg.astype(jnp.bfloat16), wp.astype(jnp.bfloat16))


def kernel(x, p, positions, ev_w_in, ev_w_out, rw_mu, rw_w0, rw_w2, rw_a0, rw_a2, rw_g2, rw_k_k, rw_k_a,
           rw_r_k, rw_gn_g, rw_gn_b, nsa_cmp_pos, nsa_cmp_w1, nsa_cmp_w2, od_w_in, od_w_out, hg_lb, hg_norm_g,
           moe_w_rg, moe_b_rg, moe_w_re, moe_b_re, moe_w1, moe_w3, moe_w2, ln_g, ln_b, ple_w, ple_gate_w):
    B, T, D = x.shape
    M = B * T
    cos, sin = _rope_tables(positions, NSA_HEAD)
    lb_soft = jax.nn.softmax(hg_lb, axis=0)
    lb_all = jnp.cumsum(lb_soft, axis=0) - lb_soft[0:1]
    for li in range(DEPTH):
        j = li // 2
        if li % 2 == 0:
            z = _dense(x.reshape(M, D), ev_w_in[j], keep_pad=True).reshape(B, T, -1)
            y_rw = _rwkv7_branch(z, rw_mu[j], rw_w0[j], rw_w2[j], rw_a0[j], rw_a2[j], rw_g2[j], rw_k_k[j],
                                 rw_k_a[j], rw_r_k[j], rw_gn_g[j], rw_gn_b[j])
            y_nsa = _nsa_group(z, cos, sin, nsa_cmp_pos[j], nsa_cmp_w1[j], nsa_cmp_w2[j])
            ys, w_out = [y_rw.reshape(M, RW_WIDTH), y_nsa.reshape(M, NSA_WIDTH)], ev_w_out[j]
        else:
            ys, w_out = [_hgrn2_mixer(x, od_w_in[j], lb_all[li], hg_norm_g[j])], od_w_out[j]
        h = _dense_res_ln(ys, w_out, x.reshape(M, D), ln_g[li, 0], ln_b[li, 0])
        h = _hier_moe_ln(h, moe_w_rg[li], moe_b_rg[li], moe_w_re[li], moe_b_re[li], moe_w1, moe_w3, moe_w2,
                         ln_g[li, 1], ln_b[li, 1], layer=li)
        x = _ple(h, p[li].reshape(M, PLE_DIM), ple_gate_w[li], ple_w[li]).reshape(B, T, D)
    return x
```
